```python
import jax, jax.numpy as jnp
from jax import lax
import numpy as np

D_MODEL = 1024
BATCH = 4
SEQ = 4096
DEPTH = 1
DEC_BATCH = 128
DEC_SEQ = 4
PAST_LEN = 16384
PAGE_SIZE = 128

HEAD_DIM = 64
N_Q_HEADS = 16
N_KV_HEADS = 2
GQA_GROUP = N_Q_HEADS // N_KV_HEADS
WINDOW = 128
ROPE_THETA = 10000.0
POOL_WIDTH = D_MODEL
POOL_WINDOWS = (2, 4, 8, 16)
N_POOL_GROUPS = len(POOL_WINDOWS)
POOL_GROUP_WIDTH = POOL_WIDTH // N_POOL_GROUPS
POOL_STATE_LEN = max(POOL_WINDOWS) - 1
N_EXPERT_GROUPS = 4
EXPERTS_PER_GROUP = 8
N_EXPERTS = N_EXPERT_GROUPS * EXPERTS_PER_GROUP
TOP_K_INNER = 2
D_EXPERT = 512
MOE_BLOCK = 128
RMS_EPS = 1e-6
Q_W = N_Q_HEADS * HEAD_DIM
KV_W = N_KV_HEADS * HEAD_DIM
OFF_K = Q_W
OFF_V = OFF_K + KV_W
OFF_U = OFF_V + KV_W
OFF_GA = OFF_U + POOL_WIDTH
OFF_GP = OFF_GA + D_MODEL
IN_W = OFF_GP + D_MODEL

kernel_name = 'hybrid_swa_sink_pool_hmoe_step'


def rms_norm(x, g):
    xf = x.astype(jnp.float32)
    y = xf * lax.rsqrt(jnp.mean(xf * xf, axis=-1, keepdims=True) + RMS_EPS)
    return (y * g.astype(jnp.float32)).astype(x.dtype)


def rope(x, pos):
    half = HEAD_DIM // 2
    inv = ROPE_THETA ** (-jnp.arange(half, dtype=jnp.float32) * (2.0 / HEAD_DIM))
    ang = pos.astype(jnp.float32)[:, None] * inv[None, :]
    cos = jnp.cos(ang)[None, :, None, :]
    sin = jnp.sin(ang)[None, :, None, :]
    xf = x.astype(jnp.float32)
    x1, x2 = xf[..., :half], xf[..., half:]
    return jnp.concatenate([x1 * cos - x2 * sin, x2 * cos + x1 * sin], axis=-1).astype(x.dtype)


def sink_window_attention(q, k, v, q_pos, k_pos, sinks):
    s = jnp.einsum('bnqhgd,bnkhd->bnhgqk', q.astype(jnp.float32), k.astype(jnp.float32)) * (HEAD_DIM ** -0.5)
    rel = q_pos[:, :, None] - k_pos[:, None, :]
    mask = (rel >= 0) & (rel < WINDOW) & (k_pos[:, None, :] >= 0)
    s = jnp.where(mask[None, :, None, None], s, -jnp.inf)
    sink = sinks.astype(jnp.float32).reshape(N_KV_HEADS, GQA_GROUP)[None, None, :, :, None, None]
    m = jnp.maximum(jnp.max(s, axis=-1, keepdims=True), sink)
    p = jnp.exp(s - m)
    denom = jnp.sum(p, axis=-1, keepdims=True) + jnp.exp(sink - m)
    return jnp.einsum('bnhgqk,bnkhd->bnqhgd', p / denom, v.astype(jnp.float32))


def multiscale_pool(ext, first_pos, n_out):
    B, R, _ = ext.shape
    e = ext.astype(jnp.float32).reshape(B, R, N_POOL_GROUPS, POOL_GROUP_WIDTH)
    csum = jnp.concatenate([jnp.zeros_like(e[:, :1]), jnp.cumsum(e, axis=1)], axis=1)
    rows = jnp.arange(R - n_out, R, dtype=jnp.int32)
    pos = first_pos + rows
    means = []
    for g, w in enumerate(POOL_WINDOWS):
        lo = jnp.maximum(rows + 1 - w, 0)
        c = csum[:, :, g]
        window_sum = c[:, R - n_out + 1:] - jnp.take(c, lo, axis=1)
        count = jnp.minimum(w, pos + 1).astype(jnp.float32)
        means.append(window_sum / count[None, :, None])
    return jnp.stack(means, axis=2) - e[:, R - n_out:]


def hier_moe(h, w_route_group, b_route_group, w_route_expert, b_route_expert, w_gate, w_up, w_down):
    N, D = h.shape
    hf = h.astype(jnp.float32)
    g_prob = jax.nn.softmax(hf @ w_route_group.astype(jnp.float32) + b_route_group.astype(jnp.float32), axis=-1)
    g_w, g_idx = lax.top_k(g_prob, 1)
    e_logits = (hf @ w_route_expert.astype(jnp.float32) + b_route_expert.astype(jnp.float32)).reshape(N, N_EXPERT_GROUPS, EXPERTS_PER_GROUP)
    e_logits = e_logits[jnp.arange(N), g_idx[:, 0]]
    top_l, top_i = lax.top_k(e_logits, TOP_K_INNER)
    comb = g_w * jax.nn.softmax(top_l, axis=-1)
    eid = g_idx * EXPERTS_PER_GROUP + top_i
    A = N * TOP_K_INNER
    flat_e = eid.reshape(A)
    flat_tok = jnp.arange(A, dtype=jnp.int32) // TOP_K_INNER
    flat_w = comb.reshape(A)
    order = jnp.argsort(flat_e)
    se, stok, sw = flat_e[order], flat_tok[order], flat_w[order]
    counts = jnp.bincount(flat_e, length=N_EXPERTS)
    starts = jnp.cumsum(counts) - counts
    pcounts = (counts + MOE_BLOCK - 1) // MOE_BLOCK * MOE_BLOCK
    pends = jnp.cumsum(pcounts)
    pstarts = pends - pcounts
    dest = pstarts[se] + jnp.arange(A, dtype=jnp.int32) - starts[se]
    n_blocks = -(-(A + N_EXPERTS * (MOE_BLOCK - 1)) // MOE_BLOCK)
    xp = jnp.zeros((n_blocks * MOE_BLOCK, D), h.dtype).at[dest].set(h[stok])
    block_e = jnp.minimum(jnp.searchsorted(pends, jnp.arange(n_blocks, dtype=jnp.int32) * MOE_BLOCK, side='right'), N_EXPERTS - 1)

    def expert_block(args):
        xb, e = args
        return (jax.nn.silu(xb @ w_gate[e]) * (xb @ w_up[e])) @ w_down[e]

    yp = lax.map(expert_block, (xp.reshape(n_blocks, MOE_BLOCK, D), block_e)).reshape(n_blocks * MOE_BLOCK, D)
    out = jnp.zeros((N, D), jnp.float32).at[stok].add(yp[dest].astype(jnp.float32) * sw[:, None])
    return out.astype(h.dtype)


def decoder_layer(x, start, past_k, past_v, past_u, w):
    B, S, _ = x.shape
    pos = start + jnp.arange(S, dtype=jnp.int32)
    h = rms_norm(x, w['norm_mix_g'])
    z = h @ w['w_in']
    q = z[..., :OFF_K].reshape(B, S, N_Q_HEADS, HEAD_DIM)
    k = z[..., OFF_K:OFF_V].reshape(B, S, N_KV_HEADS, HEAD_DIM)
    v = z[..., OFF_V:OFF_U].reshape(B, S, N_KV_HEADS, HEAD_DIM)
    u = z[..., OFF_U:OFF_GA]
    ga = z[..., OFF_GA:OFF_GP]
    gp = z[..., OFF_GP:]
    q = rope(rms_norm(q, w['q_norm_g']), pos)
    k = rope(rms_norm(k, w['k_norm_g']), pos)
    if past_k is None:
        nb = S // WINDOW
        qb = q.reshape(B, nb, WINDOW, N_KV_HEADS, GQA_GROUP, HEAD_DIM)
        kb = k.reshape(B, nb, WINDOW, N_KV_HEADS, HEAD_DIM)
        vb = v.reshape(B, nb, WINDOW, N_KV_HEADS, HEAD_DIM)
        shift = lambda t: jnp.concatenate([jnp.zeros_like(t[:, :1]), t[:, :-1]], axis=1)
        k_band = jnp.concatenate([shift(kb), kb], axis=2)
        v_band = jnp.concatenate([shift(vb), vb], axis=2)
        q_pos = pos.reshape(nb, WINDOW)
        k_pos = jnp.concatenate([q_pos - WINDOW, q_pos], axis=1)
        new_k, new_v = k[:, -WINDOW:], v[:, -WINDOW:]
        ext, first = u, start
        new_u = u[:, -POOL_STATE_LEN:]
    else:
        k_all = jnp.concatenate([past_k.astype(k.dtype), k], axis=1)
        v_all = jnp.concatenate([past_v.astype(v.dtype), v], axis=1)
        qb = q.reshape(B, 1, S, N_KV_HEADS, GQA_GROUP, HEAD_DIM)
        k_band, v_band = k_all[:, None], v_all[:, None]
        q_pos = pos[None]
        k_pos = jnp.concatenate([start - WINDOW + jnp.arange(WINDOW, dtype=jnp.int32), pos])[None]
        new_k, new_v = k_all[:, -WINDOW:], v_all[:, -WINDOW:]
        ext = jnp.concatenate([past_u.astype(u.dtype), u], axis=1)
        first = start - POOL_STATE_LEN
        new_u = ext[:, -POOL_STATE_LEN:]
    attn = sink_window_attention(qb, k_band, v_band, q_pos, k_pos, w['attn_sinks']).reshape(B, S, Q_W).astype(x.dtype)
    pooled = multiscale_pool(ext, first, S)
    pooled = jnp.einsum('bsgc,gcd->bsgd', pooled, w['pool_mix_w'].astype(jnp.float32)).reshape(B, S, POOL_WIDTH)
    pooled = (pooled * w['pool_scale'].astype(jnp.float32)).astype(x.dtype)
    merged = jax.nn.sigmoid(ga) * (attn @ w['w_attn_branch']) + jax.nn.sigmoid(gp) * (pooled @ w['w_pool_branch'])
    x = x + merged @ w['w_out']
    hn = rms_norm(x, w['norm_ffn_g']).reshape(B * S, D_MODEL)
    x = x + hier_moe(hn, w['w_route_group'], w['b_route_group'], w['w_route_expert'], w['b_route_expert'],
                     w['w_expert_gate'], w['w_expert_up'], w['w_expert_down']).reshape(B, S, D_MODEL)
    return x, new_k, new_v, new_u


def setup_inputs(seed: int = 0) -> dict:
    key = jax.random.key(seed)
    ks = jax.random.split(key, 24)
    L = DEPTH
    C = POOL_GROUP_WIDTH
    nrm = lambda k, shape, scale: jax.random.normal(k, shape, jnp.float32) * scale
    return {
        'x_prompt': nrm(ks[0], (BATCH, SEQ, D_MODEL), 1.0),
        'x_sample': nrm(ks[1], (DEC_BATCH, DEC_SEQ, D_MODEL), 1.0),
        'cache_k': nrm(ks[2], (L, DEC_BATCH, WINDOW, N_KV_HEADS, HEAD_DIM), 1.0),
        'cache_v': nrm(ks[3], (L, DEC_BATCH, WINDOW, N_KV_HEADS, HEAD_DIM), 1.0),
        'state_pool': nrm(ks[4], (L, DEC_BATCH, POOL_STATE_LEN, POOL_WIDTH), 1.0),
        'norm_mix_g': 1.0 + nrm(ks[5], (L, D_MODEL), 0.05),
        'w_in': nrm(ks[6], (L, D_MODEL, IN_W), D_MODEL ** -0.5),
        'q_norm_g': 1.0 + nrm(ks[7], (L, HEAD_DIM), 0.05),
        'k_norm_g': 1.0 + nrm(ks[8], (L, HEAD_DIM), 0.05),
        'attn_sinks': nrm(ks[9], (L, N_Q_HEADS), 1.0),
        'w_attn_branch': nrm(ks[10], (L, Q_W, D_MODEL), Q_W ** -0.5),
        'pool_mix_w': nrm(ks[11], (L, N_POOL_GROUPS, C, C), C ** -0.5),
        'pool_scale': 1.0 + nrm(ks[12], (L, POOL_WIDTH), 0.1),
        'w_pool_branch': nrm(ks[13], (L, POOL_WIDTH, D_MODEL), POOL_WIDTH ** -0.5),
        'w_out': nrm(ks[14], (L, D_MODEL, D_MODEL), D_MODEL ** -0.5),
        'norm_ffn_g': 1.0 + nrm(ks[15], (L, D_MODEL), 0.05),
        'w_route_group': nrm(ks[16], (L, D_MODEL, N_EXPERT_GROUPS), D_MODEL ** -0.5),
        'b_route_group': nrm(ks[17], (L, N_EXPERT_GROUPS), 0.01),
        'w_route_expert': nrm(ks[18], (L, D_MODEL, N_EXPERTS), D_MODEL ** -0.5),
        'b_route_expert': nrm(ks[19], (L, N_EXPERTS), 0.01),
        'w_expert_gate': nrm(ks[20], (L, N_EXPERTS, D_MODEL, D_EXPERT), D_MODEL ** -0.5),
        'w_expert_up': nrm(ks[21], (L, N_EXPERTS, D_MODEL, D_EXPERT), D_MODEL ** -0.5),
        'w_expert_down': nrm(ks[22], (L, N_EXPERTS, D_EXPERT, D_MODEL), D_EXPERT ** -0.5),
    }


def reference(x_prompt, x_sample, cache_k, cache_v, state_pool, norm_mix_g, w_in, q_norm_g, k_norm_g,
              attn_sinks, w_attn_branch, pool_mix_w, pool_scale, w_pool_branch, w_out, norm_ffn_g,
              w_route_group, b_route_group, w_route_expert, b_route_expert, w_expert_gate, w_expert_up,
              w_expert_down):
    xp, xs = x_prompt, x_sample
    kp, vp, up, ksl, vsl, usl = [], [], [], [], [], []
    for l in range(DEPTH):
        w = dict(norm_mix_g=norm_mix_g[l], w_in=w_in[l], q_norm_g=q_norm_g[l], k_norm_g=k_norm_g[l],
                 attn_sinks=attn_sinks[l], w_attn_branch=w_attn_branch[l], pool_mix_w=pool_mix_w[l],
                 pool_scale=pool_scale[l], w_pool_branch=w_pool_branch[l], w_out=w_out[l],
                 norm_ffn_g=norm_ffn_g[l], w_route_group=w_route_group[l], b_route_group=b_route_group[l],
                 w_route_expert=w_route_expert[l], b_route_expert=b_route_expert[l],
                 w_expert_gate=w_expert_gate[l], w_expert_up=w_expert_up[l], w_expert_down=w_expert_down[l])
        xp, nk, nv, nu = decoder_layer(xp, 0, None, None, None, w)
        kp.append(nk); vp.append(nv); up.append(nu)
        xs, nk, nv, nu = decoder_layer(xs, PAST_LEN, cache_k[l], cache_v[l], state_pool[l], w)
        ksl.append(nk); vsl.append(nv); usl.append(nu)
    return (xp, xs, jnp.stack(kp), jnp.stack(vp), jnp.stack(up), jnp.stack(ksl), jnp.stack(vsl), jnp.stack(usl))
```

```python
import functools

import jax
import jax.numpy as jnp
from jax import lax
from jax.experimental import pallas as pl
from jax.experimental.pallas import tpu as pltpu

F32 = jnp.float32
BF16 = jnp.bfloat16

D_MODEL = 1024
HEAD_DIM = 64
N_Q_HEADS = 16
N_KV_HEADS = 2
WINDOW = 128
ROPE_THETA = 10000.0
POOL_WINDOWS = (2, 4, 8, 16)
POOL_GROUP_WIDTH = D_MODEL // len(POOL_WINDOWS)
POOL_STATE_LEN = max(POOL_WINDOWS) - 1
N_EXPERT_GROUPS = 4
EXPERTS_PER_GROUP = 8
N_EXPERTS = N_EXPERT_GROUPS * EXPERTS_PER_GROUP
D_EXPERT = 512
RMS_EPS = 1e-6
Q_W = N_Q_HEADS * HEAD_DIM
KV_W = N_KV_HEADS * HEAD_DIM
OFF_K = Q_W
OFF_V = OFF_K + KV_W
OFF_U = OFF_V + KV_W
OFF_GA = OFF_U + D_MODEL
OFF_GP = OFF_GA + D_MODEL
IN_W = OFF_GP + D_MODEL

LANES = 128
ROW_BLOCK = 512
EXPERT_LANE0 = 32
VMEM_LIMIT = 56 * 1024 * 1024
NEG_INF = float("-inf")


def _rms(x, g):
    return x * lax.rsqrt(jnp.mean(x * x, axis=-1, keepdims=True) + RMS_EPS) * g


def _bdot(a, b):
    return jnp.dot(a, b, preferred_element_type=F32)


def _inproj_body(x_ref, g_ref, w_ref, cos_ref, sin_ref, gq_ref, gk_ref,
                 q_ref, k_ref, v_ref, kk_ref, vv_ref, u_ref, sga_ref, sgp_ref):
    rows = x_ref.shape[0]
    hb = _rms(x_ref[...], g_ref[...]).astype(BF16)
    cos = cos_ref[...]
    sin = sin_ref[...]
    lane = lax.broadcasted_iota(jnp.int32, (rows, LANES), 1)
    lo = lane < HEAD_DIM
    first_half = (lane % HEAD_DIM) < (HEAD_DIM // 2)

    def head_norm_rope(zc, gain):
        sq = zc * zc
        ss_lo = jnp.sum(jnp.where(lo, sq, 0.0), axis=-1, keepdims=True)
        ss_hi = jnp.sum(jnp.where(lo, 0.0, sq), axis=-1, keepdims=True)
        r = lax.rsqrt(jnp.where(lo, ss_lo, ss_hi) * (1.0 / HEAD_DIM) + RMS_EPS)
        y = zc * r * gain
        partner = jnp.where(first_half, pltpu.roll(y, LANES - HEAD_DIM // 2, 1),
                            pltpu.roll(y, HEAD_DIM // 2, 1))
        return y * cos + partner * sin

    gq = gq_ref[...]
    for j in range(Q_W // 256):
        z = _bdot(hb, w_ref[:, 256 * j:256 * (j + 1)])
        for c in range(2):
            qn = head_norm_rope(z[:, LANES * c:LANES * (c + 1)], gq) * (HEAD_DIM ** -0.5)
            q_ref[:, 256 * j + LANES * c:256 * j + LANES * (c + 1)] = qn.astype(BF16)

    z = _bdot(hb, w_ref[:, OFF_K:OFF_U])
    kn = head_norm_rope(z[:, :KV_W], gk_ref[...])
    vr = z[:, KV_W:]
    k_ref[...] = kn
    v_ref[...] = vr
    kr = pltpu.roll(kn, HEAD_DIM, 1)
    vrr = pltpu.roll(vr, HEAD_DIM, 1)
    kk_ref[:, :LANES] = jnp.where(lo, kn, kr).astype(BF16)
    kk_ref[:, LANES:] = jnp.where(lo, kr, kn).astype(BF16)
    vv_ref[:, :LANES] = jnp.where(lo, vr, vrr).astype(BF16)
    vv_ref[:, LANES:] = jnp.where(lo, vrr, vr).astype(BF16)

    for j in range(D_MODEL // 256):
        u_ref[:, 256 * j:256 * (j + 1)] = _bdot(hb, w_ref[:, OFF_U + 256 * j:OFF_U + 256 * (j + 1)])
        sga_ref[:, 256 * j:256 * (j + 1)] = jax.nn.sigmoid(
            _bdot(hb, w_ref[:, OFF_GA + 256 * j:OFF_GA + 256 * (j + 1)]))
        sgp_ref[:, 256 * j:256 * (j + 1)] = jax.nn.sigmoid(
            _bdot(hb, w_ref[:, OFF_GP + 256 * j:OFF_GP + 256 * (j + 1)]))


def _inproj(x, g, w_in_b, cos, sin, gq, gk):
    n = x.shape[0]
    nb = n // ROW_BLOCK
    ncos = cos.shape[0] // ROW_BLOCK
    row = lambda w: pl.BlockSpec((ROW_BLOCK, w), lambda i: (i, 0))
    full = lambda a: pl.BlockSpec(a.shape, lambda i: (0,) * a.ndim)
    tab = pl.BlockSpec((ROW_BLOCK, LANES), lambda i: (i % ncos, 0))
    out_shapes = (
        jax.ShapeDtypeStruct((n, Q_W), BF16),
        jax.ShapeDtypeStruct((n, KV_W), F32),
        jax.ShapeDtypeStruct((n, KV_W), F32),
        jax.ShapeDtypeStruct((n, 2 * LANES), BF16),
        jax.ShapeDtypeStruct((n, 2 * LANES), BF16),
        jax.ShapeDtypeStruct((n, D_MODEL), F32),
        jax.ShapeDtypeStruct((n, D_MODEL), F32),
        jax.ShapeDtypeStruct((n, D_MODEL), F32),
    )
    return pl.pallas_call(
        _inproj_body,
        grid=(nb,),
        in_specs=[row(D_MODEL), full(g), full(w_in_b), tab, tab, full(gq), full(gk)],
        out_specs=tuple(row(s.shape[1]) for s in out_shapes),
        out_shape=out_shapes,
        compiler_params=pltpu.CompilerParams(dimension_semantics=("arbitrary",),
                                             vmem_limit_bytes=VMEM_LIMIT),
        name="inproj",
    )(x, g, w_in_b, cos, sin, gq, gk)


def _attend(q2, kg, vg, cur_valid, sink_col, prev_dead):
    s = lax.dot_general(q2, kg, (((1,), (1,)), ((), ())), preferred_element_type=F32)
    s_prev = s[:, :WINDOW]
    if prev_dead is not None:
        s_prev = jnp.where(prev_dead, NEG_INF, s_prev)
    sc = jnp.where(cur_valid, s[:, WINDOW:], s_prev)
    m = jnp.maximum(jnp.max(sc, axis=-1, keepdims=True), sink_col)
    p = jnp.exp(sc - m)
    den = jnp.sum(p, axis=-1, keepdims=True) + jnp.exp(sink_col - m)
    pn = p * (1.0 / den)
    p2 = jnp.concatenate([jnp.where(cur_valid, 0.0, pn), jnp.where(cur_valid, pn, 0.0)], axis=1)
    return _bdot(p2.astype(BF16), vg)


def _mixer_prompt_body(sink_ref, q_ref, kkc_ref, kkp_ref, vvc_ref, vvp_ref, uc_ref, up_ref,
                       attn_ref, pd_ref, ext_ref):
    i = pl.program_id(1)
    first = i == 0
    m2 = 2 * WINDOW
    row = lax.broadcasted_iota(jnp.int32, (m2, LANES), 0)
    col = lax.broadcasted_iota(jnp.int32, (m2, LANES), 1)
    cur_valid = (row % WINDOW) >= col
    upper = lax.broadcasted_iota(jnp.int32, (m2, 1), 0) < WINDOW
    lo = lax.broadcasted_iota(jnp.int32, (WINDOW, LANES), 1) < HEAD_DIM

    for j in range(ROW_BLOCK // WINDOW):
        r0 = WINDOW * j
        if j == 0:
            kprev, vprev = kkp_ref[...], vvp_ref[...]
            prev_dead = first
        else:
            kprev, vprev = kkc_ref[r0 - WINDOW:r0, :], vvc_ref[r0 - WINDOW:r0, :]
            prev_dead = None
        kband = jnp.concatenate([kprev, kkc_ref[r0:r0 + WINDOW, :]], axis=0)
        vband = jnp.concatenate([vprev, vvc_ref[r0:r0 + WINDOW, :]], axis=0)
        for g in range(N_KV_HEADS):
            kg = kband[:, LANES * g:LANES * (g + 1)]
            vg = vband[:, LANES * g:LANES * (g + 1)]
            for c in range(4):
                cc = 4 * g + c
                qc = q_ref[r0:r0 + WINDOW, LANES * cc:LANES * (cc + 1)]
                zero = jnp.zeros_like(qc)
                q2 = jnp.concatenate([jnp.where(lo, qc, zero), jnp.where(lo, zero, qc)], axis=0)
                sink_col = jnp.where(upper, sink_ref[2 * cc], sink_ref[2 * cc + 1])
                o = _attend(q2, kg, vg, cur_valid, sink_col, prev_dead)
                attn_ref[r0:r0 + WINDOW, LANES * cc:LANES * (cc + 1)] = jnp.where(
                    lo, o[:WINDOW], o[WINDOW:]).astype(BF16)

    hist = 16
    ext_ref[0:hist, :] = jnp.where(first, 0.0, up_ref[...])
    ext_ref[hist:, :] = uc_ref[...]
    pos = i * ROW_BLOCK + lax.broadcasted_iota(jnp.int32, (ROW_BLOCK, 1), 0)
    for g, w in enumerate(POOL_WINDOWS):
        c0, c1 = POOL_GROUP_WIDTH * g, POOL_GROUP_WIDTH * (g + 1)
        acc = uc_ref[:, c0:c1]
        for k in range(1, w):
            acc = acc + ext_ref[hist - k:hist - k + ROW_BLOCK, c0:c1]
        cnt = jnp.minimum(w, pos + 1).astype(F32)
        pd_ref[:, c0:c1] = (acc / cnt - uc_ref[:, c0:c1]).astype(BF16)


def _mixer_prompt(sinks, q, kk, vv, u, batch, seq):
    nb = seq // ROW_BLOCK
    sub = ROW_BLOCK // WINDOW
    cur = lambda w: pl.BlockSpec((ROW_BLOCK, w), lambda b, i: (b * nb + i, 0))
    prev_kv = pl.BlockSpec((WINDOW, 2 * LANES),
                           lambda b, i: (jnp.maximum((b * nb + i) * sub - 1, 0), 0))
    prev_u = pl.BlockSpec((16, D_MODEL),
                          lambda b, i: (jnp.maximum((b * nb + i) * (ROW_BLOCK // 16) - 1, 0), 0))
    n = batch * seq
    return pl.pallas_call(
        _mixer_prompt_body,
        grid=(batch, nb),
        in_specs=[pl.BlockSpec(memory_space=pltpu.SMEM), cur(Q_W), cur(2 * LANES), prev_kv,
                  cur(2 * LANES), prev_kv, cur(D_MODEL), prev_u],
        out_specs=(cur(Q_W), cur(D_MODEL)),
        out_shape=(jax.ShapeDtypeStruct((n, Q_W), BF16), jax.ShapeDtypeStruct((n, D_MODEL), BF16)),
        scratch_shapes=[pltpu.VMEM((16 + ROW_BLOCK, D_MODEL), F32)],
        compiler_params=pltpu.CompilerParams(dimension_semantics=("arbitrary", "arbitrary"),
                                             vmem_limit_bytes=VMEM_LIMIT),
        name="mixer_prompt",
    )(sinks, q, kk, kk, vv, vv, u, u)


SEQ_PAIR_ROWS = 8


def _mixer_sample_body(sink_ref, q_ref, k_ref, v_ref, u_ref, ck_ref, cv_ref, st_ref,
                       attn_ref, pd_ref, kb_ref, vb_ref, ext_ref):
    m = SEQ_PAIR_ROWS
    half = m // 2
    row1 = lax.broadcasted_iota(jnp.int32, (m, 1), 0)
    row = lax.broadcasted_iota(jnp.int32, (2 * m, LANES), 0)
    col = lax.broadcasted_iota(jnp.int32, (2 * m, LANES), 1)
    cur_valid = (row % half) >= col
    lane8 = lax.broadcasted_iota(jnp.int32, (m, LANES), 1)
    lo8 = lane8 < HEAD_DIM
    top8 = lax.broadcasted_iota(jnp.int32, (m, LANES), 0) < half
    lob = lax.broadcasted_iota(jnp.int32, (2 * WINDOW, LANES), 1) < HEAD_DIM
    pad = jnp.zeros((m, LANES), F32)

    k8 = k_ref[...]
    v8 = v_ref[...]
    u8 = u_ref[...]
    kb_ref[WINDOW + m:, :] = jnp.zeros((WINDOW - m, LANES), F32)
    vb_ref[WINDOW + m:, :] = jnp.zeros((WINDOW - m, LANES), F32)

    attn_parts = []
    pd_parts = []
    for s in range(2):
        shift = lambda a: a if s == 0 else pltpu.roll(a, half, 0)
        kb_ref[:WINDOW, :] = ck_ref[s]
        vb_ref[:WINDOW, :] = cv_ref[s]
        kb_ref[WINDOW:WINDOW + m, :] = shift(k8)
        vb_ref[WINDOW:WINDOW + m, :] = shift(v8)
        kb = kb_ref[...]
        vb = vb_ref[...]
        kbr = pltpu.roll(kb, HEAD_DIM, 1)
        vbr = pltpu.roll(vb, HEAD_DIM, 1)
        chunks = []
        for g in range(N_KV_HEADS):
            kg = (jnp.where(lob, kb, kbr) if g == 0 else jnp.where(lob, kbr, kb)).astype(BF16)
            vg = (jnp.where(lob, vb, vbr) if g == 0 else jnp.where(lob, vbr, vb)).astype(BF16)
            for c in range(4):
                cc = 4 * g + c
                qa = shift(q_ref[:, LANES * cc:LANES * (cc + 1)].astype(F32))
                qb = pltpu.roll(qa, half, 0)
                q2 = jnp.where(top8, jnp.where(lo8, qa, 0.0), jnp.where(lo8, 0.0, qb))
                q2 = jnp.concatenate([q2, pad], axis=0).astype(BF16)
                sink_col = jnp.where(lax.broadcasted_iota(jnp.int32, (2 * m, 1), 0) < half,
                                     sink_ref[2 * cc], sink_ref[2 * cc + 1])
                o = _attend(q2, kg, vg, cur_valid, sink_col, None)[:m]
                chunks.append(jnp.where(lo8, o, pltpu.roll(o, half, 0)))
        attn_parts.append(jnp.concatenate(chunks, axis=1))

        ext_ref[0:16, :] = st_ref[s]
        ext_ref[16:16 + m, :] = shift(u8)
        cols = []
        for g, w in enumerate(POOL_WINDOWS):
            c0, c1 = POOL_GROUP_WIDTH * g, POOL_GROUP_WIDTH * (g + 1)
            acc = ext_ref[16:16 + m, c0:c1]
            for k in range(1, w):
                acc = acc + ext_ref[16 - k:16 - k + m, c0:c1]
            cols.append(acc / float(w) - ext_ref[16:16 + m, c0:c1])
        pd_parts.append(jnp.concatenate(cols, axis=1))

    top = row1 < half
    attn_ref[...] = jnp.where(top, attn_parts[0], pltpu.roll(attn_parts[1], half, 0))
    pd_ref[...] = jnp.where(top, pd_parts[0], pltpu.roll(pd_parts[1], half, 0))


def _mixer_sample(sinks, q, k, v, u, cache_k, cache_v, state16):
    n = q.shape[0]
    m = SEQ_PAIR_ROWS
    rows = lambda w: pl.BlockSpec((m, w), lambda i: (i, 0))
    seqs = lambda a: pl.BlockSpec((2,) + a.shape[1:], lambda i: (i, 0, 0))
    return pl.pallas_call(
        _mixer_sample_body,
        grid=(n // m,),
        in_specs=[pl.BlockSpec(memory_space=pltpu.SMEM), rows(Q_W), rows(KV_W), rows(KV_W),
                  rows(D_MODEL), seqs(cache_k), seqs(cache_v), seqs(state16)],
        out_specs=(rows(Q_W), rows(D_MODEL)),
        out_shape=(jax.ShapeDtypeStruct((n, Q_W), F32), jax.ShapeDtypeStruct((n, D_MODEL), F32)),
        scratch_shapes=[pltpu.VMEM((2 * WINDOW, LANES), F32), pltpu.VMEM((2 * WINDOW, LANES), F32),
                        pltpu.VMEM((16 + SEQ_PAIR_ROWS, D_MODEL), F32)],
        compiler_params=pltpu.CompilerParams(dimension_semantics=("arbitrary",),
                                             vmem_limit_bytes=VMEM_LIMIT),
        name="mixer_sample",
    )(sinks, q, k, v, u, cache_k, cache_v, state16)


def _dense_body(x_ref, attn_ref, pd_ref, sga_ref, sgp_ref, mix_ref, ps_ref, wa_ref, wp_ref,
                wo_ref, gf_ref, wrh_ref, wrl_ref, br_ref, x1_ref, hn_ref, rw_ref):
    rows = x_ref.shape[0]
    pd = pd_ref[...].astype(BF16)
    pooled = []
    for g in range(len(POOL_WINDOWS)):
        c0, c1 = POOL_GROUP_WIDTH * g, POOL_GROUP_WIDTH * (g + 1)
        pooled.append((_bdot(pd[:, c0:c1], mix_ref[g]) * ps_ref[:, c0:c1]).astype(BF16))
    pooled = jnp.concatenate(pooled, axis=1)
    merged = (sga_ref[...] * _bdot(attn_ref[...].astype(BF16), wa_ref[...])
              + sgp_ref[...] * _bdot(pooled, wp_ref[...]))
    x1 = x_ref[...] + _bdot(merged.astype(BF16), wo_ref[...])
    x1_ref[...] = x1
    hn = _rms(x1, gf_ref[...])
    hi = hn.astype(BF16)
    hn_ref[...] = hi
    lo = (hn - hi.astype(F32)).astype(BF16)
    logits = (_bdot(hi, wrh_ref[...]) + _bdot(hi, wrl_ref[...]) + _bdot(lo, wrh_ref[...])
              + br_ref[...])

    lane = lax.broadcasted_iota(jnp.int32, (rows, LANES), 1)
    big = jnp.int32(LANES)
    gl = jnp.where(lane < N_EXPERT_GROUPS, logits, NEG_INF)
    gmax = jnp.max(gl, axis=-1, keepdims=True)
    gidx = jnp.min(jnp.where(gl == gmax, lane, big), axis=-1, keepdims=True)
    g_w = 1.0 / jnp.sum(jnp.exp(gl - gmax), axis=-1, keepdims=True)
    e0 = EXPERT_LANE0 + gidx * EXPERTS_PER_GROUP
    el = jnp.where((lane >= e0) & (lane < e0 + EXPERTS_PER_GROUP), logits, NEG_INF)
    l1 = jnp.max(el, axis=-1, keepdims=True)
    i1 = jnp.min(jnp.where(el == l1, lane, big), axis=-1, keepdims=True)
    el2 = jnp.where(lane == i1, NEG_INF, el)
    l2 = jnp.max(el2, axis=-1, keepdims=True)
    i2 = jnp.min(jnp.where(el2 == l2, lane, big), axis=-1, keepdims=True)
    e = jnp.exp(l2 - l1)
    w1 = 1.0 / (1.0 + e)
    w2 = e * w1
    rw_ref[...] = jnp.where(lane == i1, g_w * w1, 0.0) + jnp.where(lane == i2, g_w * w2, 0.0)


def _dense(x, attn, pd, sga, sgp, mix_b, pool_scale, wa_b, wp_b, wo_b, gf, wr_hi, wr_lo, br):
    n = x.shape[0]
    row = lambda w: pl.BlockSpec((ROW_BLOCK, w), lambda i: (i, 0))
    full = lambda a: pl.BlockSpec(a.shape, lambda i: (0,) * a.ndim)
    return pl.pallas_call(
        _dense_body,
        grid=(n // ROW_BLOCK,),
        in_specs=[row(D_MODEL)] * 5 + [full(a) for a in
                                       (mix_b, pool_scale, wa_b, wp_b, wo_b, gf, wr_hi, wr_lo, br)],
        out_specs=(row(D_MODEL), row(D_MODEL), row(LANES)),
        out_shape=(jax.ShapeDtypeStruct((n, D_MODEL), F32),
                   jax.ShapeDtypeStruct((n, D_MODEL), BF16),
                   jax.ShapeDtypeStruct((n, LANES), F32)),
        compiler_params=pltpu.CompilerParams(dimension_semantics=("arbitrary",),
                                             vmem_limit_bytes=VMEM_LIMIT),
        name="dense",
    )(x, attn, pd, sga, sgp, mix_b, pool_scale, wa_b, wp_b, wo_b, gf, wr_hi, wr_lo, br)


def _experts_body(x1_ref, hn_ref, rw_ref, wg_ref, wu_ref, wd_ref, y_ref):
    e = pl.program_id(1)

    @pl.when(e == 0)
    def _():
        y_ref[...] = x1_ref[...]

    rw = rw_ref[...]
    lane = lax.broadcasted_iota(jnp.int32, rw.shape, 1)
    wcol = jnp.sum(jnp.where(lane == EXPERT_LANE0 + e, rw, 0.0), axis=-1, keepdims=True)
    hn = hn_ref[...]
    gate = _bdot(hn, wg_ref[0].astype(BF16))
    up = _bdot(hn, wu_ref[0].astype(BF16))
    act = (jax.nn.silu(gate) * up).astype(BF16)
    y_ref[...] += wcol * _bdot(act, wd_ref[0].astype(BF16))


def _experts(x1, hn, rw, w_gate, w_up, w_down, tile):
    n = x1.shape[0]
    row = lambda w: pl.BlockSpec((tile, w), lambda t, e: (t, 0))
    return pl.pallas_call(
        _experts_body,
        grid=(n // tile, N_EXPERTS),
        in_specs=[row(D_MODEL), row(D_MODEL), row(LANES),
                  pl.BlockSpec((1, D_MODEL, D_EXPERT), lambda t, e: (e, 0, 0)),
                  pl.BlockSpec((1, D_MODEL, D_EXPERT), lambda t, e: (e, 0, 0)),
                  pl.BlockSpec((1, D_EXPERT, D_MODEL), lambda t, e: (e, 0, 0))],
        out_specs=row(D_MODEL),
        out_shape=jax.ShapeDtypeStruct((n, D_MODEL), F32),
        compiler_params=pltpu.CompilerParams(dimension_semantics=("arbitrary", "arbitrary"),
                                             vmem_limit_bytes=VMEM_LIMIT),
        name="experts",
    )(x1, hn, rw, w_gate, w_up, w_down)


def _rope_tables(pos):
    half = HEAD_DIM // 2
    inv = ROPE_THETA ** (-jnp.arange(half, dtype=F32) * (2.0 / HEAD_DIM))
    ang = pos.astype(F32)[:, None] * inv[None, :]
    cos = jnp.tile(jnp.cos(ang), (1, LANES // half))
    sin = jnp.sin(ang)
    sin = jnp.tile(jnp.concatenate([-sin, sin], axis=1), (1, LANES // HEAD_DIM))
    return cos, sin


def _split_bf16(w):
    hi = w.astype(BF16)
    return hi, (w - hi.astype(F32)).astype(BF16)


def kernel(x_prompt, x_sample, cache_k, cache_v, state_pool, norm_mix_g, w_in, q_norm_g, k_norm_g,
           attn_sinks, w_attn_branch, pool_mix_w, pool_scale, w_pool_branch, w_out, norm_ffn_g,
           w_route_group, b_route_group, w_route_expert, b_route_expert, w_expert_gate,
           w_expert_up, w_expert_down):
    batch, seq, d = x_prompt.shape
    dec_batch, dec_seq, _ = x_sample.shape
    past_len = 16384
    assert x_prompt.shape == (4, 4096, D_MODEL) and x_sample.shape == (128, 4, D_MODEL)
    assert w_in.shape[0] == 1, "single layer"
    n_p, n_s = batch * seq, dec_batch * dec_seq

    g_mix = norm_mix_g[0][None, :]
    w_in_b = w_in[0].astype(BF16)
    gq = jnp.tile(q_norm_g[0], LANES // HEAD_DIM)[None, :]
    gk = jnp.tile(k_norm_g[0], LANES // HEAD_DIM)[None, :]
    sinks = attn_sinks[0]
    mix_b = pool_mix_w[0].astype(BF16)
    ps = pool_scale[0][None, :]
    wa_b = w_attn_branch[0].astype(BF16)
    wp_b = w_pool_branch[0].astype(BF16)
    wo_b = w_out[0].astype(BF16)
    gf = norm_ffn_g[0][None, :]
    wr = jnp.zeros((D_MODEL, LANES), F32)
    wr = wr.at[:, :N_EXPERT_GROUPS].set(w_route_group[0])
    wr = wr.at[:, EXPERT_LANE0:EXPERT_LANE0 + N_EXPERTS].set(w_route_expert[0])
    wr_hi, wr_lo = _split_bf16(wr)
    br = jnp.zeros((1, LANES), F32)
    br = br.at[0, :N_EXPERT_GROUPS].set(b_route_group[0])
    br = br.at[0, EXPERT_LANE0:EXPERT_LANE0 + N_EXPERTS].set(b_route_expert[0])

    cos_p, sin_p = _rope_tables(jnp.arange(seq, dtype=jnp.int32))
    pos_s = past_len + (jnp.arange(n_s, dtype=jnp.int32) % dec_seq)
    cos_s, sin_s = _rope_tables(pos_s)

    def tail(x1, hn, rw, tile):
        return _experts(x1, hn, rw, w_expert_gate[0], w_expert_up[0], w_expert_down[0], tile)

    xp = x_prompt.reshape(n_p, d)
    q, k, v, kk, vv, u, sga, sgp = _inproj(xp, g_mix, w_in_b, cos_p, sin_p, gq, gk)
    attn, pd = _mixer_prompt(sinks, q, kk, vv, u, batch, seq)
    x1, hn, rw = _dense(xp, attn, pd, sga, sgp, mix_b, ps, wa_b, wp_b, wo_b, gf, wr_hi, wr_lo, br)
    y_prompt = tail(x1, hn, rw, 1024).reshape(batch, seq, d)
    new_k_p = k.reshape(batch, seq, N_KV_HEADS, HEAD_DIM)[:, -WINDOW:][None]
    new_v_p = v.reshape(batch, seq, N_KV_HEADS, HEAD_DIM)[:, -WINDOW:][None]
    new_u_p = u.reshape(batch, seq, d)[:, -POOL_STATE_LEN:][None]

    xs = x_sample.reshape(n_s, d)
    q, k, v, kk, vv, u, sga, sgp = _inproj(xs, g_mix, w_in_b, cos_s, sin_s, gq, gk)
    ck = cache_k[0].reshape(dec_batch, WINDOW, KV_W)
    cv = cache_v[0].reshape(dec_batch, WINDOW, KV_W)
    state16 = jnp.pad(state_pool[0], ((0, 0), (1, 0), (0, 0)))
    attn, pd = _mixer_sample(sinks, q.astype(F32), k, v, u, ck, cv, state16)
    x1, hn, rw = _dense(xs, attn, pd, sga, sgp, mix_b, ps, wa_b, wp_b, wo_b, gf, wr_hi, wr_lo, br)
    y_sample = tail(x1, hn, rw, n_s).reshape(dec_batch, dec_seq, d)
    k_new = k.reshape(dec_batch, dec_seq, KV_W)
    v_new = v.reshape(dec_batch, dec_seq, KV_W)
    u_new = u.reshape(dec_batch, dec_seq, d)
    new_k_s = jnp.concatenate([ck, k_new], axis=1)[:, -WINDOW:].reshape(
        1, dec_batch, WINDOW, N_KV_HEADS, HEAD_DIM)
    new_v_s = jnp.concatenate([cv, v_new], axis=1)[:, -WINDOW:].reshape(
        1, dec_batch, WINDOW, N_KV_HEADS, HEAD_DIM)
    new_u_s = jnp.concatenate([state_pool[0], u_new], axis=1)[:, -POOL_STATE_LEN:][None]

    return (y_prompt, y_sample, new_k_p, new_v_p, new_u_p, new_k_s, new_v_s, new_u_s)
```

```python
import functools

import jax
import jax.numpy as jnp
from jax import lax
from jax.experimental import pallas as pl
from jax.experimental.pallas import tpu as pltpu

F32 = jnp.float32
BF16 = jnp.bfloat16

D_MODEL = 1024
HEAD_DIM = 64
N_Q_HEADS = 16
N_KV_HEADS = 2
WINDOW = 128
ROPE_THETA = 10000.0
POOL_WINDOWS = (2, 4, 8, 16)
POOL_GROUP_WIDTH = D_MODEL // len(POOL_WINDOWS)
POOL_STATE_LEN = max(POOL_WINDOWS) - 1
N_EXPERT_GROUPS = 4
EXPERTS_PER_GROUP = 8
N_EXPERTS = N_EXPERT_GROUPS * EXPERTS_PER_GROUP
D_EXPERT = 512
RMS_EPS = 1e-6
Q_W = N_Q_HEADS * HEAD_DIM
KV_W = N_KV_HEADS * HEAD_DIM
OFF_K = Q_W
OFF_V = OFF_K + KV_W
OFF_U = OFF_V + KV_W
OFF_GA = OFF_U + D_MODEL
OFF_GP = OFF_GA + D_MODEL
IN_W = OFF_GP + D_MODEL

LANES = 128
ROW_BLOCK = 512
EXPERT_LANE0 = 32
VMEM_LIMIT = 56 * 1024 * 1024
EXPERTS_VMEM_LIMIT = 60 * 1024 * 1024
NEG_INF = float("-inf")

N_TILES = 4
TILE_BLOCKS = 8
TILE_PROMPT = TILE_BLOCKS * ROW_BLOCK
TILE_SAMPLE = 128
TILE_TOKENS = TILE_PROMPT + TILE_SAMPLE
MOE_ROWS = 128
DUMMY_ROWS = 8
SCATTER_BATCH = 4
LIST_CAP = 2 * TILE_TOKENS + N_EXPERTS * (MOE_ROWS - 1) + 8


def _rms(x, g):
    return x * lax.rsqrt(jnp.mean(x * x, axis=-1, keepdims=True) + RMS_EPS) * g


def _bdot(a, b):
    return jnp.dot(a, b, preferred_element_type=F32)


def _inproj_body(x_ref, g_ref, w_ref, cos_ref, sin_ref, gq_ref, gk_ref,
                 q_ref, k_ref, v_ref, kk_ref, vv_ref, u_ref, sga_ref, sgp_ref):
    rows = x_ref.shape[0]
    hb = _rms(x_ref[...], g_ref[...]).astype(BF16)
    cos = cos_ref[...]
    sin = sin_ref[...]
    lane = lax.broadcasted_iota(jnp.int32, (rows, LANES), 1)
    lo = lane < HEAD_DIM
    first_half = (lane % HEAD_DIM) < (HEAD_DIM // 2)

    def head_norm_rope(zc, gain):
        sq = zc * zc
        ss_lo = jnp.sum(jnp.where(lo, sq, 0.0), axis=-1, keepdims=True)
        ss_hi = jnp.sum(jnp.where(lo, 0.0, sq), axis=-1, keepdims=True)
        r = lax.rsqrt(jnp.where(lo, ss_lo, ss_hi) * (1.0 / HEAD_DIM) + RMS_EPS)
        y = zc * r * gain
        partner = jnp.where(first_half, pltpu.roll(y, LANES - HEAD_DIM // 2, 1),
                            pltpu.roll(y, HEAD_DIM // 2, 1))
        return y * cos + partner * sin

    gq = gq_ref[...]
    for j in range(Q_W // 256):
        z = _bdot(hb, w_ref[:, 256 * j:256 * (j + 1)])
        for c in range(2):
            qn = head_norm_rope(z[:, LANES * c:LANES * (c + 1)], gq) * (HEAD_DIM ** -0.5)
            q_ref[:, 256 * j + LANES * c:256 * j + LANES * (c + 1)] = qn.astype(BF16)

    z = _bdot(hb, w_ref[:, OFF_K:OFF_U])
    kn = head_norm_rope(z[:, :KV_W], gk_ref[...])
    vr = z[:, KV_W:]
    k_ref[...] = kn
    v_ref[...] = vr
    kr = pltpu.roll(kn, HEAD_DIM, 1)
    vrr = pltpu.roll(vr, HEAD_DIM, 1)
    kk_ref[:, :LANES] = jnp.where(lo, kn, kr).astype(BF16)
    kk_ref[:, LANES:] = jnp.where(lo, kr, kn).astype(BF16)
    vv_ref[:, :LANES] = jnp.where(lo, vr, vrr).astype(BF16)
    vv_ref[:, LANES:] = jnp.where(lo, vrr, vr).astype(BF16)

    for j in range(D_MODEL // 256):
        u_ref[:, 256 * j:256 * (j + 1)] = _bdot(hb, w_ref[:, OFF_U + 256 * j:OFF_U + 256 * (j + 1)])
        sga_ref[:, 256 * j:256 * (j + 1)] = jax.nn.sigmoid(
            _bdot(hb, w_ref[:, OFF_GA + 256 * j:OFF_GA + 256 * (j + 1)]))
        sgp_ref[:, 256 * j:256 * (j + 1)] = jax.nn.sigmoid(
            _bdot(hb, w_ref[:, OFF_GP + 256 * j:OFF_GP + 256 * (j + 1)]))


def _inproj(x, g, w_in_b, cos, sin, gq, gk):
    n = x.shape[0]
    nb = n // ROW_BLOCK
    ncos = cos.shape[0] // ROW_BLOCK
    row = lambda w: pl.BlockSpec((ROW_BLOCK, w), lambda i: (i, 0))
    full = lambda a: pl.BlockSpec(a.shape, lambda i: (0,) * a.ndim)
    tab = pl.BlockSpec((ROW_BLOCK, LANES), lambda i: (i % ncos, 0))
    out_shapes = (
        jax.ShapeDtypeStruct((n, Q_W), BF16),
        jax.ShapeDtypeStruct((n, KV_W), F32),
        jax.ShapeDtypeStruct((n, KV_W), F32),
        jax.ShapeDtypeStruct((n, 2 * LANES), BF16),
        jax.ShapeDtypeStruct((n, 2 * LANES), BF16),
        jax.ShapeDtypeStruct((n, D_MODEL), F32),
        jax.ShapeDtypeStruct((n, D_MODEL), F32),
        jax.ShapeDtypeStruct((n, D_MODEL), F32),
    )
    return pl.pallas_call(
        _inproj_body,
        grid=(nb,),
        in_specs=[row(D_MODEL), full(g), full(w_in_b), tab, tab, full(gq), full(gk)],
        out_specs=tuple(row(s.shape[1]) for s in out_shapes),
        out_shape=out_shapes,
        compiler_params=pltpu.CompilerParams(dimension_semantics=("arbitrary",),
                                             vmem_limit_bytes=VMEM_LIMIT),
        name="inproj",
    )(x, g, w_in_b, cos, sin, gq, gk)


def _attend(q2, kg, vg, cur_valid, sink_col, prev_dead):
    s = lax.dot_general(q2, kg, (((1,), (1,)), ((), ())), preferred_element_type=F32)
    s_prev = s[:, :WINDOW]
    if prev_dead is not None:
        s_prev = jnp.where(prev_dead, NEG_INF, s_prev)
    sc = jnp.where(cur_valid, s[:, WINDOW:], s_prev)
    m = jnp.maximum(jnp.max(sc, axis=-1, keepdims=True), sink_col)
    p = jnp.exp(sc - m)
    den = jnp.sum(p, axis=-1, keepdims=True) + jnp.exp(sink_col - m)
    pn = p * (1.0 / den)
    p2 = jnp.concatenate([jnp.where(cur_valid, 0.0, pn), jnp.where(cur_valid, pn, 0.0)], axis=1)
    return _bdot(p2.astype(BF16), vg)


def _mixer_prompt_body(sink_ref, q_ref, kkc_ref, kkp_ref, vvc_ref, vvp_ref, uc_ref, up_ref,
                       attn_ref, pd_ref, ext_ref):
    i = pl.program_id(1)
    first = i == 0
    m2 = 2 * WINDOW
    row = lax.broadcasted_iota(jnp.int32, (m2, LANES), 0)
    col = lax.broadcasted_iota(jnp.int32, (m2, LANES), 1)
    cur_valid = (row % WINDOW) >= col
    upper = lax.broadcasted_iota(jnp.int32, (m2, 1), 0) < WINDOW
    lo = lax.broadcasted_iota(jnp.int32, (WINDOW, LANES), 1) < HEAD_DIM

    for j in range(ROW_BLOCK // WINDOW):
        r0 = WINDOW * j
        if j == 0:
            kprev, vprev = kkp_ref[...], vvp_ref[...]
            prev_dead = first
        else:
            kprev, vprev = kkc_ref[r0 - WINDOW:r0, :], vvc_ref[r0 - WINDOW:r0, :]
            prev_dead = None
        kband = jnp.concatenate([kprev, kkc_ref[r0:r0 + WINDOW, :]], axis=0)
        vband = jnp.concatenate([vprev, vvc_ref[r0:r0 + WINDOW, :]], axis=0)
        for g in range(N_KV_HEADS):
            kg = kband[:, LANES * g:LANES * (g + 1)]
            vg = vband[:, LANES * g:LANES * (g + 1)]
            for c in range(4):
                cc = 4 * g + c
                qc = q_ref[r0:r0 + WINDOW, LANES * cc:LANES * (cc + 1)]
                zero = jnp.zeros_like(qc)
                q2 = jnp.concatenate([jnp.where(lo, qc, zero), jnp.where(lo, zero, qc)], axis=0)
                sink_col = jnp.where(upper, sink_ref[2 * cc], sink_ref[2 * cc + 1])
                o = _attend(q2, kg, vg, cur_valid, sink_col, prev_dead)
                attn_ref[r0:r0 + WINDOW, LANES * cc:LANES * (cc + 1)] = jnp.where(
                    lo, o[:WINDOW], o[WINDOW:]).astype(BF16)

    hist = 16
    ext_ref[0:hist, :] = jnp.where(first, 0.0, up_ref[...])
    ext_ref[hist:, :] = uc_ref[...]
    pos = i * ROW_BLOCK + lax.broadcasted_iota(jnp.int32, (ROW_BLOCK, 1), 0)
    for g, w in enumerate(POOL_WINDOWS):
        c0, c1 = POOL_GROUP_WIDTH * g, POOL_GROUP_WIDTH * (g + 1)
        acc = uc_ref[:, c0:c1]
        for k in range(1, w):
            acc = acc + ext_ref[hist - k:hist - k + ROW_BLOCK, c0:c1]
        cnt = jnp.minimum(w, pos + 1).astype(F32)
        pd_ref[:, c0:c1] = (acc / cnt - uc_ref[:, c0:c1]).astype(BF16)


def _mixer_prompt(sinks, q, kk, vv, u, batch, seq):
    nb = seq // ROW_BLOCK
    sub = ROW_BLOCK // WINDOW
    cur = lambda w: pl.BlockSpec((ROW_BLOCK, w), lambda b, i: (b * nb + i, 0))
    prev_kv = pl.BlockSpec((WINDOW, 2 * LANES),
                           lambda b, i: (jnp.maximum((b * nb + i) * sub - 1, 0), 0))
    prev_u = pl.BlockSpec((16, D_MODEL),
                          lambda b, i: (jnp.maximum((b * nb + i) * (ROW_BLOCK // 16) - 1, 0), 0))
    n = batch * seq
    return pl.pallas_call(
        _mixer_prompt_body,
        grid=(batch, nb),
        in_specs=[pl.BlockSpec(memory_space=pltpu.SMEM), cur(Q_W), cur(2 * LANES), prev_kv,
                  cur(2 * LANES), prev_kv, cur(D_MODEL), prev_u],
        out_specs=(cur(Q_W), cur(D_MODEL)),
        out_shape=(jax.ShapeDtypeStruct((n, Q_W), BF16), jax.ShapeDtypeStruct((n, D_MODEL), BF16)),
        scratch_shapes=[pltpu.VMEM((16 + ROW_BLOCK, D_MODEL), F32)],
        compiler_params=pltpu.CompilerParams(dimension_semantics=("arbitrary", "arbitrary"),
                                             vmem_limit_bytes=VMEM_LIMIT),
        name="mixer_prompt",
    )(sinks, q, kk, kk, vv, vv, u, u)


SEQ_PAIR_ROWS = 8
SAMPLE_PAIRS_PER_STEP = 8


def _mixer_sample_body(sink_ref, q_ref, k_ref, v_ref, u_ref, ck_ref, cv_ref, st_ref,
                       attn_ref, pd_ref, ext_ref):
    m = SEQ_PAIR_ROWS
    half = m // 2
    n_chunks = 4
    mq = n_chunks * m
    row1 = lax.broadcasted_iota(jnp.int32, (m, 1), 0)
    row = lax.broadcasted_iota(jnp.int32, (mq, LANES), 0)
    col = lax.broadcasted_iota(jnp.int32, (mq, LANES), 1)
    cur_valid = (row % half) >= col
    lane8 = lax.broadcasted_iota(jnp.int32, (m, LANES), 1)
    lo8 = lane8 < HEAD_DIM
    top8 = lax.broadcasted_iota(jnp.int32, (m, LANES), 0) < half
    lob = lax.broadcasted_iota(jnp.int32, (2 * WINDOW, LANES), 1) < HEAD_DIM
    tail = jnp.zeros((WINDOW - m, LANES), F32)
    top = row1 < half
    sink_cols = [jnp.concatenate([jnp.where(top, sink_ref[2 * (4 * g + c)], sink_ref[2 * (4 * g + c) + 1])
                                  for c in range(n_chunks)], axis=0) for g in range(N_KV_HEADS)]

    def pair(p, carry):
        r0 = pl.multiple_of(p * m, m)
        q8 = q_ref[pl.ds(r0, m), :]
        k8 = k_ref[pl.ds(r0, m), :]
        v8 = v_ref[pl.ds(r0, m), :]
        u8 = u_ref[pl.ds(r0, m), :]
        attn_parts = []
        pd_parts = []
        for s in range(2):
            shift = lambda a: a if s == 0 else pltpu.roll(a, half, 0)
            kb = jnp.concatenate([ck_ref[2 * p + s], shift(k8), tail], axis=0)
            vb = jnp.concatenate([cv_ref[2 * p + s], shift(v8), tail], axis=0)
            kbr = pltpu.roll(kb, HEAD_DIM, 1)
            vbr = pltpu.roll(vb, HEAD_DIM, 1)
            qs = shift(q8)
            qr = pltpu.roll(qs, half, 0)
            chunks = []
            for g in range(N_KV_HEADS):
                kg = (jnp.where(lob, kb, kbr) if g == 0 else jnp.where(lob, kbr, kb)).astype(BF16)
                vg = (jnp.where(lob, vb, vbr) if g == 0 else jnp.where(lob, vbr, vb)).astype(BF16)
                q2 = []
                for c in range(n_chunks):
                    cc = n_chunks * g + c
                    qa = qs[:, LANES * cc:LANES * (cc + 1)]
                    qb = qr[:, LANES * cc:LANES * (cc + 1)]
                    q2.append(jnp.where(top8, jnp.where(lo8, qa, 0.0), jnp.where(lo8, 0.0, qb)))
                q2 = jnp.concatenate(q2, axis=0).astype(BF16)
                o = _attend(q2, kg, vg, cur_valid, sink_cols[g], None)
                for c in range(n_chunks):
                    oc = o[m * c:m * (c + 1)]
                    chunks.append(jnp.where(lo8, oc, pltpu.roll(oc, half, 0)))
            attn_parts.append(jnp.concatenate(chunks, axis=1))

            ext_ref[0:16, :] = st_ref[2 * p + s]
            ext_ref[16:16 + m, :] = shift(u8)
            cols = []
            for g, w in enumerate(POOL_WINDOWS):
                c0, c1 = POOL_GROUP_WIDTH * g, POOL_GROUP_WIDTH * (g + 1)
                acc = ext_ref[16:16 + m, c0:c1]
                for k in range(1, w):
                    acc = acc + ext_ref[16 - k:16 - k + m, c0:c1]
                cols.append(acc / float(w) - ext_ref[16:16 + m, c0:c1])
            pd_parts.append(jnp.concatenate(cols, axis=1))

        attn_ref[pl.ds(r0, m), :] = jnp.where(top, attn_parts[0], pltpu.roll(attn_parts[1], half, 0))
        pd_ref[pl.ds(r0, m), :] = jnp.where(top, pd_parts[0], pltpu.roll(pd_parts[1], half, 0))
        return carry

    lax.fori_loop(0, SAMPLE_PAIRS_PER_STEP, pair, 0)


def _mixer_sample(sinks, q, k, v, u, cache_k, cache_v, state16):
    n = q.shape[0]
    m = SEQ_PAIR_ROWS * SAMPLE_PAIRS_PER_STEP
    rows = lambda w: pl.BlockSpec((m, w), lambda i: (i, 0))
    seqs = lambda a: pl.BlockSpec((2 * SAMPLE_PAIRS_PER_STEP,) + a.shape[1:], lambda i: (i, 0, 0))
    return pl.pallas_call(
        _mixer_sample_body,
        grid=(n // m,),
        in_specs=[pl.BlockSpec(memory_space=pltpu.SMEM), rows(Q_W), rows(KV_W), rows(KV_W),
                  rows(D_MODEL), seqs(cache_k), seqs(cache_v), seqs(state16)],
        out_specs=(rows(Q_W), rows(D_MODEL)),
        out_shape=(jax.ShapeDtypeStruct((n, Q_W), F32), jax.ShapeDtypeStruct((n, D_MODEL), F32)),
        scratch_shapes=[pltpu.VMEM((16 + SEQ_PAIR_ROWS, D_MODEL), F32)],
        compiler_params=pltpu.CompilerParams(dimension_semantics=("arbitrary",),
                                             vmem_limit_bytes=VMEM_LIMIT),
        name="mixer_sample",
    )(sinks, q, k, v, u, cache_k, cache_v, state16)


def _dense_body(x_ref, attn_ref, pd_ref, sga_ref, sgp_ref, mix_ref, ps_ref, wa_ref, wp_ref,
                wo_ref, gf_ref, wrh_ref, wrl_ref, br_ref, ltri_ref, ssel_ref, crow_ref,
                x1_ref, hn_ref, rt_ref, cnt_ref, seg_ref, carry_ref):
    rows = x_ref.shape[0]
    step = pl.program_id(0)

    @pl.when(step % TILE_BLOCKS == 0)
    def _():
        carry_ref[...] = jnp.zeros_like(carry_ref)

    pd = pd_ref[...].astype(BF16)
    pooled = []
    for g in range(len(POOL_WINDOWS)):
        c0, c1 = POOL_GROUP_WIDTH * g, POOL_GROUP_WIDTH * (g + 1)
        pooled.append((_bdot(pd[:, c0:c1], mix_ref[g]) * ps_ref[:, c0:c1]).astype(BF16))
    pooled = jnp.concatenate(pooled, axis=1)
    merged = (sga_ref[...] * _bdot(attn_ref[...].astype(BF16), wa_ref[...])
              + sgp_ref[...] * _bdot(pooled, wp_ref[...]))
    x1 = x_ref[...] + _bdot(merged.astype(BF16), wo_ref[...])
    x1_ref[...] = x1
    hn = _rms(x1, gf_ref[...])
    hn_ref[...] = hn
    hi = hn.astype(BF16)
    lo = (hn - hi.astype(F32)).astype(BF16)
    logits = (_bdot(hi, wrh_ref[...]) + _bdot(hi, wrl_ref[...]) + _bdot(lo, wrh_ref[...])
              + br_ref[...])

    lane = lax.broadcasted_iota(jnp.int32, (rows, LANES), 1)
    big = jnp.int32(LANES)
    gl = jnp.where(lane < N_EXPERT_GROUPS, logits, NEG_INF)
    gmax = jnp.max(gl, axis=-1, keepdims=True)
    gidx = jnp.min(jnp.where(gl == gmax, lane, big), axis=-1, keepdims=True)
    g_w = 1.0 / jnp.sum(jnp.exp(gl - gmax), axis=-1, keepdims=True)
    e0 = EXPERT_LANE0 + gidx * EXPERTS_PER_GROUP
    el = jnp.where((lane >= e0) & (lane < e0 + EXPERTS_PER_GROUP), logits, NEG_INF)
    l1 = jnp.max(el, axis=-1, keepdims=True)
    i1 = jnp.min(jnp.where(el == l1, lane, big), axis=-1, keepdims=True)
    el2 = jnp.where(lane == i1, NEG_INF, el)
    l2 = jnp.max(el2, axis=-1, keepdims=True)
    i2 = jnp.min(jnp.where(el2 == l2, lane, big), axis=-1, keepdims=True)
    e = jnp.exp(l2 - l1)
    w1 = 1.0 / (1.0 + e)
    w2 = e * w1

    sel = (lane == i1) | (lane == i2)
    onehot = jnp.where(sel, 1.0, 0.0).astype(BF16)
    rank = _bdot(ltri_ref[...], onehot) + carry_ref[0:1, :] + crow_ref[...]
    segsum = _bdot(ssel_ref[...], onehot)
    seg_ref[0] = segsum
    carry_ref[...] = carry_ref[...] + segsum[0:1, :]
    cnt_ref[0] = carry_ref[...]
    r1 = jnp.sum(jnp.where(lane == i1, rank, 0.0), axis=-1, keepdims=True)
    r2 = jnp.sum(jnp.where(lane == i2, rank, 0.0), axis=-1, keepdims=True)
    cols = ((i1 - EXPERT_LANE0).astype(F32), (i2 - EXPERT_LANE0).astype(F32), r1, r2,
            g_w * w1, g_w * w2)
    tile = jnp.zeros((rows, LANES), F32)
    for c, val in enumerate(cols):
        tile = jnp.where(lane == c, val, tile)
    rt_ref[...] = tile


def _dense(x, attn, pd, sga, sgp, mix_b, pool_scale, wa_b, wp_b, wo_b, gf, wr_hi, wr_lo, br,
           ltri, ssel, crow):
    n = x.shape[0]
    nb = n // ROW_BLOCK
    row = lambda w: pl.BlockSpec((ROW_BLOCK, w), lambda i: (i, 0))
    full = lambda a: pl.BlockSpec(a.shape, lambda i: (0,) * a.ndim)
    stat = pl.BlockSpec((1, 8, LANES), lambda i: (i, 0, 0))
    consts = (mix_b, pool_scale, wa_b, wp_b, wo_b, gf, wr_hi, wr_lo, br, ltri, ssel, crow)
    return pl.pallas_call(
        _dense_body,
        grid=(nb,),
        in_specs=[row(D_MODEL)] * 5 + [full(a) for a in consts],
        out_specs=(row(D_MODEL), row(D_MODEL), row(LANES), stat, stat),
        out_shape=(jax.ShapeDtypeStruct((n, D_MODEL), F32),
                   jax.ShapeDtypeStruct((n, D_MODEL), F32),
                   jax.ShapeDtypeStruct((n, LANES), F32),
                   jax.ShapeDtypeStruct((nb, 8, LANES), F32),
                   jax.ShapeDtypeStruct((nb, 8, LANES), F32)),
        scratch_shapes=[pltpu.VMEM((8, LANES), F32)],
        compiler_params=pltpu.CompilerParams(dimension_semantics=("arbitrary",),
                                             vmem_limit_bytes=VMEM_LIMIT),
        name="dense",
    )(x, attn, pd, sga, sgp, *consts)


def _experts_body(nblk_ref, seg_ref, cnt_ref, slots_ref, wts_ref, hn_p, hn_s, x1_p, x1_s,
                  wg_ref, wu_ref, wd_ref, y_p, y_s,
                  hn_t, acc, xg, yb, wgb, wub, wdb, tok, sem_in, sem_out):
    tau = pl.program_id(0)
    e = pl.program_id(1)
    t_rows = TILE_TOKENS
    dummy = 2 * t_rows
    p_rows = pl.ds(pl.multiple_of(tau * TILE_PROMPT, TILE_PROMPT), TILE_PROMPT)
    s_rows = pl.ds(pl.multiple_of(tau * TILE_SAMPLE, TILE_SAMPLE), TILE_SAMPLE)
    tile_p = pl.ds(0, TILE_PROMPT)
    tile_s = pl.ds(TILE_PROMPT, TILE_SAMPLE)

    def in_copies():
        return [pltpu.make_async_copy(hn_p.at[p_rows], hn_t.at[tile_p], sem_in.at[0]),
                pltpu.make_async_copy(hn_s.at[s_rows], hn_t.at[tile_s], sem_in.at[1]),
                pltpu.make_async_copy(x1_p.at[p_rows], acc.at[tile_p], sem_in.at[2]),
                pltpu.make_async_copy(x1_s.at[s_rows], acc.at[tile_s], sem_in.at[3])]

    def out_copies():
        return [pltpu.make_async_copy(acc.at[tile_p], y_p.at[p_rows], sem_out.at[0]),
                pltpu.make_async_copy(acc.at[tile_s], y_s.at[s_rows], sem_out.at[1])]

    @pl.when(e == 0)
    def _load_tile():
        for cp in in_copies():
            cp.start()

        hn_t[t_rows:, :] = jnp.zeros((DUMMY_ROWS, D_MODEL), F32)
        xg[...] = jnp.zeros_like(xg)

        def pad_fill(ei, c):
            p = seg_ref[tau, ei] + cnt_ref[tau, ei]
            for i in range(8):
                tok[p + i] = dummy
            return c
        lax.fori_loop(0, N_EXPERTS, pad_fill, 0)

        def invert(g, c):
            t0 = g * 8
            for i in range(8):
                for k in range(2):
                    tok[slots_ref[0, k, t0 + i]] = 2 * (t0 + i) + k
            return c
        lax.fori_loop(0, t_rows // 8, invert, 0)

        for cp in in_copies():
            cp.wait()
        acc[t_rows:, :] = jnp.zeros((DUMMY_ROWS, D_MODEL), F32)

    wgb[...] = wg_ref[0].astype(BF16)
    wub[...] = wu_ref[0].astype(BF16)
    wdb[...] = wd_ref[0].astype(BF16)

    seg = seg_ref[tau, e]
    cnt = cnt_ref[tau, e]
    half = SCATTER_BATCH

    def block(jb, c):
        base = seg + jb * MOE_ROWS
        nrows = jnp.minimum(MOE_ROWS, cnt - jb * MOE_ROWS)
        ngrp = lax.shift_right_logical(nrows + 7, 3)

        def gather(g, c2):
            for i in range(8):
                t = lax.shift_right_logical(tok[base + g * 8 + i], 1)
                xg[g, pl.ds(i, 1), :] = hn_t[pl.ds(t, 1), :]
            return c2
        lax.fori_loop(0, ngrp, gather, 0)

        xb = xg[...].reshape(MOE_ROWS, D_MODEL).astype(BF16)
        act = (jax.nn.silu(_bdot(xb, wgb[...])) * _bdot(xb, wub[...])).astype(BF16)
        yb[...] = _bdot(act, wdb[...]).reshape(MOE_ROWS // 8, 8, D_MODEL)

        def scatter(g, c2):
            for i0 in range(0, 8, half):
                ents = [tok[base + g * 8 + i0 + i] for i in range(half)]
                rows = [lax.shift_right_logical(en, 1) for en in ents]
                vals = [acc[pl.ds(rows[i], 1), :]
                        + wts_ref[0, 0, ents[i]] * yb[g, pl.ds(i0 + i, 1), :] for i in range(half)]
                for i in range(half):
                    acc[pl.ds(rows[i], 1), :] = vals[i]
            return c2
        lax.fori_loop(0, ngrp, scatter, 0)
        return c
    lax.fori_loop(0, nblk_ref[tau, e], block, 0)

    @pl.when(e == N_EXPERTS - 1)
    def _store_tile():
        cps = out_copies()
        for cp in cps:
            cp.start()
        for cp in cps:
            cp.wait()


def _experts(nblk, seg, cnt, slots, wts, hn_p, hn_s, x1_p, x1_s, w_gate, w_up, w_down):
    any_spec = pl.BlockSpec(memory_space=pl.ANY)
    wspec = lambda a: pl.BlockSpec((1,) + a.shape[1:], lambda t, e, *_: (e, 0, 0))
    smem = lambda a: pl.BlockSpec((1,) + a.shape[1:], lambda t, e, *_: (t, 0, 0),
                                  memory_space=pltpu.SMEM)
    grid_spec = pltpu.PrefetchScalarGridSpec(
        num_scalar_prefetch=3,
        grid=(N_TILES, N_EXPERTS),
        in_specs=[smem(slots), smem(wts), any_spec, any_spec, any_spec, any_spec,
                  wspec(w_gate), wspec(w_up), wspec(w_down)],
        out_specs=(any_spec, any_spec),
        scratch_shapes=[
            pltpu.VMEM((TILE_TOKENS + DUMMY_ROWS, D_MODEL), F32),
            pltpu.VMEM((TILE_TOKENS + DUMMY_ROWS, D_MODEL), F32),
            pltpu.VMEM((MOE_ROWS // 8, 8, D_MODEL), F32),
            pltpu.VMEM((MOE_ROWS // 8, 8, D_MODEL), F32),
            pltpu.VMEM((D_MODEL, D_EXPERT), BF16),
            pltpu.VMEM((D_MODEL, D_EXPERT), BF16),
            pltpu.VMEM((D_EXPERT, D_MODEL), BF16),
            pltpu.SMEM((LIST_CAP,), jnp.int32),
            pltpu.SemaphoreType.DMA((4,)),
            pltpu.SemaphoreType.DMA((2,)),
        ],
    )
    return pl.pallas_call(
        _experts_body,
        grid_spec=grid_spec,
        out_shape=(jax.ShapeDtypeStruct(x1_p.shape, F32), jax.ShapeDtypeStruct(x1_s.shape, F32)),
        compiler_params=pltpu.CompilerParams(dimension_semantics=("arbitrary", "arbitrary"),
                                             vmem_limit_bytes=EXPERTS_VMEM_LIMIT),
        name="experts",
    )(nblk, seg, cnt, slots, wts, hn_p, hn_s, x1_p, x1_s, w_gate, w_up, w_down)


def _rope_tables(pos):
    half = HEAD_DIM // 2
    inv = ROPE_THETA ** (-jnp.arange(half, dtype=F32) * (2.0 / HEAD_DIM))
    ang = pos.astype(F32)[:, None] * inv[None, :]
    cos = jnp.tile(jnp.cos(ang), (1, LANES // half))
    sin = jnp.sin(ang)
    sin = jnp.tile(jnp.concatenate([-sin, sin], axis=1), (1, LANES // HEAD_DIM))
    return cos, sin


def _split_bf16(w):
    hi = w.astype(BF16)
    return hi, (w - hi.astype(F32)).astype(BF16)


def kernel(x_prompt, x_sample, cache_k, cache_v, state_pool, norm_mix_g, w_in, q_norm_g, k_norm_g,
           attn_sinks, w_attn_branch, pool_mix_w, pool_scale, w_pool_branch, w_out, norm_ffn_g,
           w_route_group, b_route_group, w_route_expert, b_route_expert, w_expert_gate,
           w_expert_up, w_expert_down):
    batch, seq, d = x_prompt.shape
    dec_batch, dec_seq, _ = x_sample.shape
    past_len = 16384
    assert x_prompt.shape == (4, 4096, D_MODEL) and x_sample.shape == (128, 4, D_MODEL)
    assert w_in.shape[0] == 1, "single layer"
    n_p, n_s = batch * seq, dec_batch * dec_seq

    g_mix = norm_mix_g[0][None, :]
    w_in_b = w_in[0].astype(BF16)
    gq = jnp.tile(q_norm_g[0], LANES // HEAD_DIM)[None, :]
    gk = jnp.tile(k_norm_g[0], LANES // HEAD_DIM)[None, :]
    sinks = attn_sinks[0]
    mix_b = pool_mix_w[0].astype(BF16)
    ps = pool_scale[0][None, :]
    wa_b = w_attn_branch[0].astype(BF16)
    wp_b = w_pool_branch[0].astype(BF16)
    wo_b = w_out[0].astype(BF16)
    gf = norm_ffn_g[0][None, :]
    wr = jnp.zeros((D_MODEL, LANES), F32)
    wr = wr.at[:, :N_EXPERT_GROUPS].set(w_route_group[0])
    wr = wr.at[:, EXPERT_LANE0:EXPERT_LANE0 + N_EXPERTS].set(w_route_expert[0])
    wr_hi, wr_lo = _split_bf16(wr)
    br = jnp.zeros((1, LANES), F32)
    br = br.at[0, :N_EXPERT_GROUPS].set(b_route_group[0])
    br = br.at[0, EXPERT_LANE0:EXPERT_LANE0 + N_EXPERTS].set(b_route_expert[0])

    cos_p, sin_p = _rope_tables(jnp.arange(seq, dtype=jnp.int32))
    pos_s = past_len + (jnp.arange(n_s, dtype=jnp.int32) % dec_seq)
    cos_s, sin_s = _rope_tables(pos_s)

    dense_consts = (mix_b, ps, wa_b, wp_b, wo_b, gf, wr_hi, wr_lo, br)
    assert n_s == ROW_BLOCK == N_TILES * TILE_SAMPLE and n_p == N_TILES * TILE_PROMPT
    ridx = jnp.arange(ROW_BLOCK, dtype=jnp.int32)
    lower = ridx[:, None] > ridx[None, :]
    seg_of = ridx // TILE_SAMPLE
    ltri_p = lower.astype(BF16)
    ltri_s = (lower & (seg_of[:, None] == seg_of[None, :])).astype(BF16)
    ssel_p = (jnp.arange(8, dtype=jnp.int32)[:, None] == 0) & (ridx[None, :] >= 0)
    ssel_s = jnp.arange(8, dtype=jnp.int32)[:, None] == seg_of[None, :]

    xp = x_prompt.reshape(n_p, d)
    q, k, v, kk, vv, u, sga, sgp = _inproj(xp, g_mix, w_in_b, cos_p, sin_p, gq, gk)
    attn, pd = _mixer_prompt(sinks, q, kk, vv, u, batch, seq)
    x1_p, hn_p, rt_p, cnt_p, _ = _dense(xp, attn, pd, sga, sgp, *dense_consts, ltri_p,
                                        ssel_p.astype(BF16), jnp.zeros((ROW_BLOCK, LANES), F32))
    new_k_p = k.reshape(batch, seq, N_KV_HEADS, HEAD_DIM)[:, -WINDOW:][None]
    new_v_p = v.reshape(batch, seq, N_KV_HEADS, HEAD_DIM)[:, -WINDOW:][None]
    new_u_p = u.reshape(batch, seq, d)[:, -POOL_STATE_LEN:][None]

    xs = x_sample.reshape(n_s, d)
    q, k, v, kk, vv, u, sga, sgp = _inproj(xs, g_mix, w_in_b, cos_s, sin_s, gq, gk)
    ck = cache_k[0].reshape(dec_batch, WINDOW, KV_W)
    cv = cache_v[0].reshape(dec_batch, WINDOW, KV_W)
    state16 = jnp.pad(state_pool[0], ((0, 0), (1, 0), (0, 0)))
    attn, pd = _mixer_sample(sinks, q.astype(F32), k, v, u, ck, cv, state16)
    cnt_tiles_p = cnt_p[TILE_BLOCKS - 1::TILE_BLOCKS, 0, :]
    crow = jnp.repeat(cnt_tiles_p, TILE_SAMPLE, axis=0)
    x1_s, hn_s, rt_s, _, seg_s = _dense(xs, attn, pd, sga, sgp, *dense_consts, ltri_s,
                                        ssel_s.astype(BF16), crow)
    k_new = k.reshape(dec_batch, dec_seq, KV_W)
    v_new = v.reshape(dec_batch, dec_seq, KV_W)
    u_new = u.reshape(dec_batch, dec_seq, d)
    new_k_s = jnp.concatenate([ck, k_new], axis=1)[:, -WINDOW:].reshape(
        1, dec_batch, WINDOW, N_KV_HEADS, HEAD_DIM)
    new_v_s = jnp.concatenate([cv, v_new], axis=1)[:, -WINDOW:].reshape(
        1, dec_batch, WINDOW, N_KV_HEADS, HEAD_DIM)
    new_u_s = jnp.concatenate([state_pool[0], u_new], axis=1)[:, -POOL_STATE_LEN:][None]

    ex = slice(EXPERT_LANE0, EXPERT_LANE0 + N_EXPERTS)
    cnt = (cnt_tiles_p[:, ex] + seg_s[0, :N_TILES, ex]).astype(jnp.int32)
    nblk = (cnt + (MOE_ROWS - 1)) // MOE_ROWS
    seg = MOE_ROWS * (jnp.cumsum(nblk, axis=1) - nblk)
    rt = jnp.concatenate([rt_p[:, :6].reshape(N_TILES, TILE_PROMPT, 6),
                          rt_s[:, :6].reshape(N_TILES, TILE_SAMPLE, 6)], axis=1)
    expert_hit = rt[:, :, 0:2].astype(jnp.int32)[..., None] == jnp.arange(N_EXPERTS, dtype=jnp.int32)
    slot = (rt[:, :, 2:4].astype(jnp.int32)
            + jnp.sum(jnp.where(expert_hit, seg[:, None, None, :], 0), axis=-1))
    slots = slot.transpose(0, 2, 1)
    wts = rt[:, :, 4:6].reshape(N_TILES, 1, 2 * TILE_TOKENS)
    wts = jnp.pad(wts, ((0, 0), (0, 0), (0, 8)))
    y_p, y_s = _experts(nblk, seg, cnt, slots, wts, hn_p, hn_s, x1_p, x1_s,
                        w_expert_gate[0], w_expert_up[0], w_expert_down[0])
    y_prompt = y_p.reshape(batch, seq, d)
    y_sample = y_s.reshape(dec_batch, dec_seq, d)

    return (y_prompt, y_sample, new_k_p, new_v_p, new_u_p, new_k_s, new_v_s, new_u_s)
```

```python
import functools

import jax
import jax.numpy as jnp
from jax import lax
from jax.experimental import pallas as pl
from jax.experimental.pallas import tpu as pltpu

F32 = jnp.float32
BF16 = jnp.bfloat16

D_MODEL = 1024
HEAD_DIM = 64
N_Q_HEADS = 16
N_KV_HEADS = 2
WINDOW = 128
ROPE_THETA = 10000.0
POOL_WINDOWS = (2, 4, 8, 16)
POOL_GROUP_WIDTH = D_MODEL // len(POOL_WINDOWS)
POOL_STATE_LEN = max(POOL_WINDOWS) - 1
N_EXPERT_GROUPS = 4
EXPERTS_PER_GROUP = 8
N_EXPERTS = N_EXPERT_GROUPS * EXPERTS_PER_GROUP
D_EXPERT = 512
RMS_EPS = 1e-6
Q_W = N_Q_HEADS * HEAD_DIM
KV_W = N_KV_HEADS * HEAD_DIM
OFF_K = Q_W
OFF_V = OFF_K + KV_W
OFF_U = OFF_V + KV_W
OFF_GA = OFF_U + D_MODEL
OFF_GP = OFF_GA + D_MODEL
IN_W = OFF_GP + D_MODEL

LANES = 128
ROW_BLOCK = 512
EXPERT_LANE0 = 32
VMEM_LIMIT = 56 * 1024 * 1024
EXPERTS_VMEM_LIMIT = 60 * 1024 * 1024
NEG_INF = float("-inf")

N_TILES = 4
TILE_BLOCKS = 8
TILE_PROMPT = TILE_BLOCKS * ROW_BLOCK
TILE_SAMPLE = 128
TILE_TOKENS = TILE_PROMPT + TILE_SAMPLE
MOE_ROWS = 128
DUMMY_ROWS = 8
SCATTER_BATCH = 4
LIST_CAP = 2 * TILE_TOKENS + N_EXPERTS * (MOE_ROWS - 1) + 2 * MOE_ROWS + 1


def _rms(x, g):
    return x * lax.rsqrt(jnp.mean(x * x, axis=-1, keepdims=True) + RMS_EPS) * g


def _bdot(a, b):
    return jnp.dot(a, b, preferred_element_type=F32)


def _inproj_body(x_ref, g_ref, w_ref, cos_ref, sin_ref, gq_ref, gk_ref,
                 q_ref, k_ref, v_ref, kk_ref, vv_ref, u_ref, sga_ref, sgp_ref):
    rows = x_ref.shape[0]
    hb = _rms(x_ref[...], g_ref[...]).astype(BF16)
    cos = cos_ref[...]
    sin = sin_ref[...]
    lane = lax.broadcasted_iota(jnp.int32, (rows, LANES), 1)
    lo = lane < HEAD_DIM
    first_half = (lane % HEAD_DIM) < (HEAD_DIM // 2)

    def head_norm_rope(zc, gain):
        sq = zc * zc
        ss_lo = jnp.sum(jnp.where(lo, sq, 0.0), axis=-1, keepdims=True)
        ss_hi = jnp.sum(jnp.where(lo, 0.0, sq), axis=-1, keepdims=True)
        r = lax.rsqrt(jnp.where(lo, ss_lo, ss_hi) * (1.0 / HEAD_DIM) + RMS_EPS)
        y = zc * r * gain
        partner = jnp.where(first_half, pltpu.roll(y, LANES - HEAD_DIM // 2, 1),
                            pltpu.roll(y, HEAD_DIM // 2, 1))
        return y * cos + partner * sin

    gq = gq_ref[...]
    for j in range(Q_W // 256):
        z = _bdot(hb, w_ref[:, 256 * j:256 * (j + 1)])
        for c in range(2):
            qn = head_norm_rope(z[:, LANES * c:LANES * (c + 1)], gq) * (HEAD_DIM ** -0.5)
            q_ref[:, 256 * j + LANES * c:256 * j + LANES * (c + 1)] = qn.astype(BF16)

    z = _bdot(hb, w_ref[:, OFF_K:OFF_U])
    kn = head_norm_rope(z[:, :KV_W], gk_ref[...])
    vr = z[:, KV_W:]
    k_ref[...] = kn
    v_ref[...] = vr
    kr = pltpu.roll(kn, HEAD_DIM, 1)
    vrr = pltpu.roll(vr, HEAD_DIM, 1)
    kk_ref[:, :LANES] = jnp.where(lo, kn, kr).astype(BF16)
    kk_ref[:, LANES:] = jnp.where(lo, kr, kn).astype(BF16)
    vv_ref[:, :LANES] = jnp.where(lo, vr, vrr).astype(BF16)
    vv_ref[:, LANES:] = jnp.where(lo, vrr, vr).astype(BF16)

    for j in range(D_MODEL // 256):
        u_ref[:, 256 * j:256 * (j + 1)] = _bdot(hb, w_ref[:, OFF_U + 256 * j:OFF_U + 256 * (j + 1)])
        sga_ref[:, 256 * j:256 * (j + 1)] = jax.nn.sigmoid(
            _bdot(hb, w_ref[:, OFF_GA + 256 * j:OFF_GA + 256 * (j + 1)]))
        sgp_ref[:, 256 * j:256 * (j + 1)] = jax.nn.sigmoid(
            _bdot(hb, w_ref[:, OFF_GP + 256 * j:OFF_GP + 256 * (j + 1)]))


def _inproj(x, g, w_in_b, cos, sin, gq, gk):
    n = x.shape[0]
    nb = n // ROW_BLOCK
    ncos = cos.shape[0] // ROW_BLOCK
    row = lambda w: pl.BlockSpec((ROW_BLOCK, w), lambda i: (i, 0))
    full = lambda a: pl.BlockSpec(a.shape, lambda i: (0,) * a.ndim)
    tab = pl.BlockSpec((ROW_BLOCK, LANES), lambda i: (i % ncos, 0))
    out_shapes = (
        jax.ShapeDtypeStruct((n, Q_W), BF16),
        jax.ShapeDtypeStruct((n, KV_W), F32),
        jax.ShapeDtypeStruct((n, KV_W), F32),
        jax.ShapeDtypeStruct((n, 2 * LANES), BF16),
        jax.ShapeDtypeStruct((n, 2 * LANES), BF16),
        jax.ShapeDtypeStruct((n, D_MODEL), F32),
        jax.ShapeDtypeStruct((n, D_MODEL), F32),
        jax.ShapeDtypeStruct((n, D_MODEL), F32),
    )
    return pl.pallas_call(
        _inproj_body,
        grid=(nb,),
        in_specs=[row(D_MODEL), full(g), full(w_in_b), tab, tab, full(gq), full(gk)],
        out_specs=tuple(row(s.shape[1]) for s in out_shapes),
        out_shape=out_shapes,
        compiler_params=pltpu.CompilerParams(dimension_semantics=("arbitrary",),
                                             vmem_limit_bytes=VMEM_LIMIT),
        name="inproj",
    )(x, g, w_in_b, cos, sin, gq, gk)


def _attend(q2, kg, vg, cur_valid, sink_col, prev_dead):
    s = lax.dot_general(q2, kg, (((1,), (1,)), ((), ())), preferred_element_type=F32)
    s_prev = s[:, :WINDOW]
    if prev_dead is not None:
        s_prev = jnp.where(prev_dead, NEG_INF, s_prev)
    sc = jnp.where(cur_valid, s[:, WINDOW:], s_prev)
    m = jnp.maximum(jnp.max(sc, axis=-1, keepdims=True), sink_col)
    p = jnp.exp(sc - m)
    den = jnp.sum(p, axis=-1, keepdims=True) + jnp.exp(sink_col - m)
    pn = p * (1.0 / den)
    p2 = jnp.concatenate([jnp.where(cur_valid, 0.0, pn), jnp.where(cur_valid, pn, 0.0)], axis=1)
    return _bdot(p2.astype(BF16), vg)


def _mixer_prompt_body(sink_ref, q_ref, kkc_ref, kkp_ref, vvc_ref, vvp_ref, uc_ref, up_ref,
                       attn_ref, pd_ref, ext_ref):
    i = pl.program_id(1)
    first = i == 0
    m2 = 2 * WINDOW
    row = lax.broadcasted_iota(jnp.int32, (m2, LANES), 0)
    col = lax.broadcasted_iota(jnp.int32, (m2, LANES), 1)
    cur_valid = (row % WINDOW) >= col
    upper = lax.broadcasted_iota(jnp.int32, (m2, 1), 0) < WINDOW
    lo = lax.broadcasted_iota(jnp.int32, (WINDOW, LANES), 1) < HEAD_DIM

    for j in range(ROW_BLOCK // WINDOW):
        r0 = WINDOW * j
        if j == 0:
            kprev, vprev = kkp_ref[...], vvp_ref[...]
            prev_dead = first
        else:
            kprev, vprev = kkc_ref[r0 - WINDOW:r0, :], vvc_ref[r0 - WINDOW:r0, :]
            prev_dead = None
        kband = jnp.concatenate([kprev, kkc_ref[r0:r0 + WINDOW, :]], axis=0)
        vband = jnp.concatenate([vprev, vvc_ref[r0:r0 + WINDOW, :]], axis=0)
        for g in range(N_KV_HEADS):
            kg = kband[:, LANES * g:LANES * (g + 1)]
            vg = vband[:, LANES * g:LANES * (g + 1)]
            for c in range(4):
                cc = 4 * g + c
                qc = q_ref[r0:r0 + WINDOW, LANES * cc:LANES * (cc + 1)]
                zero = jnp.zeros_like(qc)
                q2 = jnp.concatenate([jnp.where(lo, qc, zero), jnp.where(lo, zero, qc)], axis=0)
                sink_col = jnp.where(upper, sink_ref[2 * cc], sink_ref[2 * cc + 1])
                o = _attend(q2, kg, vg, cur_valid, sink_col, prev_dead)
                attn_ref[r0:r0 + WINDOW, LANES * cc:LANES * (cc + 1)] = jnp.where(
                    lo, o[:WINDOW], o[WINDOW:]).astype(BF16)

    hist = 16
    ext_ref[0:hist, :] = jnp.where(first, 0.0, up_ref[...])
    ext_ref[hist:, :] = uc_ref[...]
    pos = i * ROW_BLOCK + lax.broadcasted_iota(jnp.int32, (ROW_BLOCK, 1), 0)
    for g, w in enumerate(POOL_WINDOWS):
        c0, c1 = POOL_GROUP_WIDTH * g, POOL_GROUP_WIDTH * (g + 1)
        acc = uc_ref[:, c0:c1]
        for k in range(1, w):
            acc = acc + ext_ref[hist - k:hist - k + ROW_BLOCK, c0:c1]
        cnt = jnp.minimum(w, pos + 1).astype(F32)
        pd_ref[:, c0:c1] = (acc / cnt - uc_ref[:, c0:c1]).astype(BF16)


def _mixer_prompt(sinks, q, kk, vv, u, batch, seq):
    nb = seq // ROW_BLOCK
    sub = ROW_BLOCK // WINDOW
    cur = lambda w: pl.BlockSpec((ROW_BLOCK, w), lambda b, i: (b * nb + i, 0))
    prev_kv = pl.BlockSpec((WINDOW, 2 * LANES),
                           lambda b, i: (jnp.maximum((b * nb + i) * sub - 1, 0), 0))
    prev_u = pl.BlockSpec((16, D_MODEL),
                          lambda b, i: (jnp.maximum((b * nb + i) * (ROW_BLOCK // 16) - 1, 0), 0))
    n = batch * seq
    return pl.pallas_call(
        _mixer_prompt_body,
        grid=(batch, nb),
        in_specs=[pl.BlockSpec(memory_space=pltpu.SMEM), cur(Q_W), cur(2 * LANES), prev_kv,
                  cur(2 * LANES), prev_kv, cur(D_MODEL), prev_u],
        out_specs=(cur(Q_W), cur(D_MODEL)),
        out_shape=(jax.ShapeDtypeStruct((n, Q_W), BF16), jax.ShapeDtypeStruct((n, D_MODEL), BF16)),
        scratch_shapes=[pltpu.VMEM((16 + ROW_BLOCK, D_MODEL), F32)],
        compiler_params=pltpu.CompilerParams(dimension_semantics=("arbitrary", "arbitrary"),
                                             vmem_limit_bytes=VMEM_LIMIT),
        name="mixer_prompt",
    )(sinks, q, kk, kk, vv, vv, u, u)


SEQ_PAIR_ROWS = 8
SAMPLE_PAIRS_PER_STEP = 8


def _mixer_sample_body(sink_ref, q_ref, k_ref, v_ref, u_ref, ck_ref, cv_ref, st_ref,
                       attn_ref, pd_ref, ext_ref):
    m = SEQ_PAIR_ROWS
    half = m // 2
    n_chunks = 4
    mq = n_chunks * m
    row1 = lax.broadcasted_iota(jnp.int32, (m, 1), 0)
    row = lax.broadcasted_iota(jnp.int32, (mq, LANES), 0)
    col = lax.broadcasted_iota(jnp.int32, (mq, LANES), 1)
    cur_valid = (row % half) >= col
    lane8 = lax.broadcasted_iota(jnp.int32, (m, LANES), 1)
    lo8 = lane8 < HEAD_DIM
    top8 = lax.broadcasted_iota(jnp.int32, (m, LANES), 0) < half
    lob = lax.broadcasted_iota(jnp.int32, (2 * WINDOW, LANES), 1) < HEAD_DIM
    tail = jnp.zeros((WINDOW - m, LANES), F32)
    top = row1 < half
    sink_cols = [jnp.concatenate([jnp.where(top, sink_ref[2 * (4 * g + c)], sink_ref[2 * (4 * g + c) + 1])
                                  for c in range(n_chunks)], axis=0) for g in range(N_KV_HEADS)]

    def pair(p, carry):
        r0 = pl.multiple_of(p * m, m)
        q8 = q_ref[pl.ds(r0, m), :]
        k8 = k_ref[pl.ds(r0, m), :]
        v8 = v_ref[pl.ds(r0, m), :]
        u8 = u_ref[pl.ds(r0, m), :]
        attn_parts = []
        pd_parts = []
        for s in range(2):
            shift = lambda a: a if s == 0 else pltpu.roll(a, half, 0)
            kb = jnp.concatenate([ck_ref[2 * p + s], shift(k8), tail], axis=0)
            vb = jnp.concatenate([cv_ref[2 * p + s], shift(v8), tail], axis=0)
            kbr = pltpu.roll(kb, HEAD_DIM, 1)
            vbr = pltpu.roll(vb, HEAD_DIM, 1)
            qs = shift(q8)
            qr = pltpu.roll(qs, half, 0)
            chunks = []
            for g in range(N_KV_HEADS):
                kg = (jnp.where(lob, kb, kbr) if g == 0 else jnp.where(lob, kbr, kb)).astype(BF16)
                vg = (jnp.where(lob, vb, vbr) if g == 0 else jnp.where(lob, vbr, vb)).astype(BF16)
                q2 = []
                for c in range(n_chunks):
                    cc = n_chunks * g + c
                    qa = qs[:, LANES * cc:LANES * (cc + 1)]
                    qb = qr[:, LANES * cc:LANES * (cc + 1)]
                    q2.append(jnp.where(top8, jnp.where(lo8, qa, 0.0), jnp.where(lo8, 0.0, qb)))
                q2 = jnp.concatenate(q2, axis=0).astype(BF16)
                o = _attend(q2, kg, vg, cur_valid, sink_cols[g], None)
                for c in range(n_chunks):
                    oc = o[m * c:m * (c + 1)]
                    chunks.append(jnp.where(lo8, oc, pltpu.roll(oc, half, 0)))
            attn_parts.append(jnp.concatenate(chunks, axis=1))

            ext_ref[0:16, :] = st_ref[2 * p + s]
            ext_ref[16:16 + m, :] = shift(u8)
            cols = []
            for g, w in enumerate(POOL_WINDOWS):
                c0, c1 = POOL_GROUP_WIDTH * g, POOL_GROUP_WIDTH * (g + 1)
                acc = ext_ref[16:16 + m, c0:c1]
                for k in range(1, w):
                    acc = acc + ext_ref[16 - k:16 - k + m, c0:c1]
                cols.append(acc / float(w) - ext_ref[16:16 + m, c0:c1])
            pd_parts.append(jnp.concatenate(cols, axis=1))

        attn_ref[pl.ds(r0, m), :] = jnp.where(top, attn_parts[0], pltpu.roll(attn_parts[1], half, 0))
        pd_ref[pl.ds(r0, m), :] = jnp.where(top, pd_parts[0], pltpu.roll(pd_parts[1], half, 0))
        return carry

    lax.fori_loop(0, SAMPLE_PAIRS_PER_STEP, pair, 0)


def _mixer_sample(sinks, q, k, v, u, cache_k, cache_v, state16):
    n = q.shape[0]
    m = SEQ_PAIR_ROWS * SAMPLE_PAIRS_PER_STEP
    rows = lambda w: pl.BlockSpec((m, w), lambda i: (i, 0))
    seqs = lambda a: pl.BlockSpec((2 * SAMPLE_PAIRS_PER_STEP,) + a.shape[1:], lambda i: (i, 0, 0))
    return pl.pallas_call(
        _mixer_sample_body,
        grid=(n // m,),
        in_specs=[pl.BlockSpec(memory_space=pltpu.SMEM), rows(Q_W), rows(KV_W), rows(KV_W),
                  rows(D_MODEL), seqs(cache_k), seqs(cache_v), seqs(state16)],
        out_specs=(rows(Q_W), rows(D_MODEL)),
        out_shape=(jax.ShapeDtypeStruct((n, Q_W), F32), jax.ShapeDtypeStruct((n, D_MODEL), F32)),
        scratch_shapes=[pltpu.VMEM((16 + SEQ_PAIR_ROWS, D_MODEL), F32)],
        compiler_params=pltpu.CompilerParams(dimension_semantics=("arbitrary",),
                                             vmem_limit_bytes=VMEM_LIMIT),
        name="mixer_sample",
    )(sinks, q, k, v, u, cache_k, cache_v, state16)


def _dense_body(x_ref, attn_ref, pd_ref, sga_ref, sgp_ref, mix_ref, ps_ref, wa_ref, wp_ref,
                wo_ref, gf_ref, wrh_ref, wrl_ref, br_ref, ltri_ref, ssel_ref, crow_ref,
                x1_ref, hn_ref, rt_ref, cnt_ref, seg_ref, carry_ref):
    rows = x_ref.shape[0]
    step = pl.program_id(0)

    @pl.when(step % TILE_BLOCKS == 0)
    def _():
        carry_ref[...] = jnp.zeros_like(carry_ref)

    pd = pd_ref[...].astype(BF16)
    pooled = []
    for g in range(len(POOL_WINDOWS)):
        c0, c1 = POOL_GROUP_WIDTH * g, POOL_GROUP_WIDTH * (g + 1)
        pooled.append((_bdot(pd[:, c0:c1], mix_ref[g]) * ps_ref[:, c0:c1]).astype(BF16))
    pooled = jnp.concatenate(pooled, axis=1)
    merged = (sga_ref[...] * _bdot(attn_ref[...].astype(BF16), wa_ref[...])
              + sgp_ref[...] * _bdot(pooled, wp_ref[...]))
    x1 = x_ref[...] + _bdot(merged.astype(BF16), wo_ref[...])
    x1_ref[...] = x1
    hn = _rms(x1, gf_ref[...])
    hn_ref[...] = hn
    hi = hn.astype(BF16)
    lo = (hn - hi.astype(F32)).astype(BF16)
    logits = (_bdot(hi, wrh_ref[...]) + _bdot(hi, wrl_ref[...]) + _bdot(lo, wrh_ref[...])
              + br_ref[...])

    lane = lax.broadcasted_iota(jnp.int32, (rows, LANES), 1)
    big = jnp.int32(LANES)
    gl = jnp.where(lane < N_EXPERT_GROUPS, logits, NEG_INF)
    gmax = jnp.max(gl, axis=-1, keepdims=True)
    gidx = jnp.min(jnp.where(gl == gmax, lane, big), axis=-1, keepdims=True)
    g_w = 1.0 / jnp.sum(jnp.exp(gl - gmax), axis=-1, keepdims=True)
    e0 = EXPERT_LANE0 + gidx * EXPERTS_PER_GROUP
    el = jnp.where((lane >= e0) & (lane < e0 + EXPERTS_PER_GROUP), logits, NEG_INF)
    l1 = jnp.max(el, axis=-1, keepdims=True)
    i1 = jnp.min(jnp.where(el == l1, lane, big), axis=-1, keepdims=True)
    el2 = jnp.where(lane == i1, NEG_INF, el)
    l2 = jnp.max(el2, axis=-1, keepdims=True)
    i2 = jnp.min(jnp.where(el2 == l2, lane, big), axis=-1, keepdims=True)
    e = jnp.exp(l2 - l1)
    w1 = 1.0 / (1.0 + e)
    w2 = e * w1

    sel = (lane == i1) | (lane == i2)
    onehot = jnp.where(sel, 1.0, 0.0).astype(BF16)
    rank = _bdot(ltri_ref[...], onehot) + carry_ref[0:1, :] + crow_ref[...]
    segsum = _bdot(ssel_ref[...], onehot)
    seg_ref[0] = segsum
    carry_ref[...] = carry_ref[...] + segsum[0:1, :]
    cnt_ref[0] = carry_ref[...]
    r1 = jnp.sum(jnp.where(lane == i1, rank, 0.0), axis=-1, keepdims=True)
    r2 = jnp.sum(jnp.where(lane == i2, rank, 0.0), axis=-1, keepdims=True)
    cols = ((i1 - EXPERT_LANE0).astype(F32), (i2 - EXPERT_LANE0).astype(F32), r1, r2,
            g_w * w1, g_w * w2)
    tile = jnp.zeros((rows, LANES), F32)
    for c, val in enumerate(cols):
        tile = jnp.where(lane == c, val, tile)
    rt_ref[...] = tile


def _dense(x, attn, pd, sga, sgp, mix_b, pool_scale, wa_b, wp_b, wo_b, gf, wr_hi, wr_lo, br,
           ltri, ssel, crow):
    n = x.shape[0]
    nb = n // ROW_BLOCK
    row = lambda w: pl.BlockSpec((ROW_BLOCK, w), lambda i: (i, 0))
    full = lambda a: pl.BlockSpec(a.shape, lambda i: (0,) * a.ndim)
    stat = pl.BlockSpec((1, 8, LANES), lambda i: (i, 0, 0))
    consts = (mix_b, pool_scale, wa_b, wp_b, wo_b, gf, wr_hi, wr_lo, br, ltri, ssel, crow)
    return pl.pallas_call(
        _dense_body,
        grid=(nb,),
        in_specs=[row(D_MODEL)] * 5 + [full(a) for a in consts],
        out_specs=(row(D_MODEL), row(D_MODEL), row(LANES), stat, stat),
        out_shape=(jax.ShapeDtypeStruct((n, D_MODEL), F32),
                   jax.ShapeDtypeStruct((n, D_MODEL), F32),
                   jax.ShapeDtypeStruct((n, LANES), F32),
                   jax.ShapeDtypeStruct((nb, 8, LANES), F32),
                   jax.ShapeDtypeStruct((nb, 8, LANES), F32)),
        scratch_shapes=[pltpu.VMEM((8, LANES), F32)],
        compiler_params=pltpu.CompilerParams(dimension_semantics=("arbitrary",),
                                             vmem_limit_bytes=VMEM_LIMIT),
        name="dense",
    )(x, attn, pd, sga, sgp, *consts)


def _experts_body(nblk_ref, seg_ref, cnt_ref, slots_ref, wts_ref, hn_p, hn_s, x1_p, x1_s,
                  wg_ref, wu_ref, wd_ref, y_p, y_s,
                  hn_t, acc, xg0, xg1, yb0, yb1, wgb, wub, wdb, tok, sem_in, sem_out):
    tau = pl.program_id(0)
    e = pl.program_id(1)
    t_rows = TILE_TOKENS
    dummy = 2 * t_rows
    p_rows = pl.ds(pl.multiple_of(tau * TILE_PROMPT, TILE_PROMPT), TILE_PROMPT)
    s_rows = pl.ds(pl.multiple_of(tau * TILE_SAMPLE, TILE_SAMPLE), TILE_SAMPLE)
    tile_p = pl.ds(0, TILE_PROMPT)
    tile_s = pl.ds(TILE_PROMPT, TILE_SAMPLE)

    def in_copies():
        return [pltpu.make_async_copy(hn_p.at[p_rows], hn_t.at[tile_p], sem_in.at[0]),
                pltpu.make_async_copy(hn_s.at[s_rows], hn_t.at[tile_s], sem_in.at[1]),
                pltpu.make_async_copy(x1_p.at[p_rows], acc.at[tile_p], sem_in.at[2]),
                pltpu.make_async_copy(x1_s.at[s_rows], acc.at[tile_s], sem_in.at[3])]

    def out_copies():
        return [pltpu.make_async_copy(acc.at[tile_p], y_p.at[p_rows], sem_out.at[0]),
                pltpu.make_async_copy(acc.at[tile_s], y_s.at[s_rows], sem_out.at[1])]

    xbufs = (xg0, xg1)
    ybufs = (yb0, yb1)
    end_slot = seg_ref[tau, N_EXPERTS - 1] + nblk_ref[tau, N_EXPERTS - 1] * MOE_ROWS

    def gather_block(b, xdst):
        base = b * MOE_ROWS
        for j in range(MOE_ROWS):
            t = lax.shift_right_logical(tok[base + j], 1)
            xdst[j // 8, pl.ds(j % 8, 1), :] = hn_t[pl.ds(t, 1), :]

    def scatter_block(b, ysrc):
        base = b * MOE_ROWS
        for j0 in range(0, MOE_ROWS, SCATTER_BATCH):
            ents = [tok[base + j0 + i] for i in range(SCATTER_BATCH)]
            rows = [lax.shift_right_logical(en, 1) for en in ents]
            vals = [acc[pl.ds(rows[i], 1), :]
                    + wts_ref[0, 0, ents[i]] * ysrc[(j0 + i) // 8, pl.ds((j0 + i) % 8, 1), :]
                    for i in range(SCATTER_BATCH)]
            for i in range(SCATTER_BATCH):
                acc[pl.ds(rows[i], 1), :] = vals[i]

    @pl.when(e == 0)
    def _load_tile():
        for cp in in_copies():
            cp.start()

        hn_t[t_rows:, :] = jnp.zeros((DUMMY_ROWS, D_MODEL), F32)
        yb0[...] = jnp.zeros_like(yb0)

        def pad_range(lo, hi):
            def fill(p, c):
                tok[p] = dummy
                return c
            lax.fori_loop(lo, hi, fill, 0)
        pad_range(0, MOE_ROWS)
        pad_range(end_slot, end_slot + MOE_ROWS)

        def pad_fill(ei, c):
            s0 = seg_ref[tau, ei]
            pad_range(s0 + cnt_ref[tau, ei], s0 + nblk_ref[tau, ei] * MOE_ROWS)
            return c
        lax.fori_loop(0, N_EXPERTS, pad_fill, 0)

        def invert(g, c):
            t0 = g * 8
            for i in range(8):
                for k in range(2):
                    tok[slots_ref[0, k, t0 + i]] = 2 * (t0 + i) + k
            return c
        lax.fori_loop(0, t_rows // 8, invert, 0)

        for cp in in_copies():
            cp.wait()
        acc[t_rows:, :] = jnp.zeros((DUMMY_ROWS, D_MODEL), F32)
        gather_block(1, xbufs[1])

    wgb[...] = wg_ref[0].astype(BF16)
    wub[...] = wu_ref[0].astype(BF16)
    wdb[...] = wd_ref[0].astype(BF16)

    first_block = lax.shift_right_logical(seg_ref[tau, e], MOE_ROWS.bit_length() - 1)

    def block(jb, c):
        b = first_block + jb
        for par in range(2):
            @pl.when((b & 1) == par)
            def _():
                gather_block(b + 1, xbufs[1 - par])
                xb = xbufs[par][...].reshape(MOE_ROWS, D_MODEL).astype(BF16)
                act = (jax.nn.silu(_bdot(xb, wgb[...])) * _bdot(xb, wub[...])).astype(BF16)
                ybufs[par][...] = _bdot(act, wdb[...]).reshape(MOE_ROWS // 8, 8, D_MODEL)
                scatter_block(b - 1, ybufs[1 - par])
        return c
    lax.fori_loop(0, nblk_ref[tau, e], block, 0)

    @pl.when(e == N_EXPERTS - 1)
    def _store_tile():
        last = lax.shift_right_logical(end_slot, MOE_ROWS.bit_length() - 1) - 1
        for par in range(2):
            @pl.when((last & 1) == par)
            def _():
                scatter_block(last, ybufs[par])
        cps = out_copies()
        for cp in cps:
            cp.start()
        for cp in cps:
            cp.wait()


def _experts(nblk, seg, cnt, slots, wts, hn_p, hn_s, x1_p, x1_s, w_gate, w_up, w_down):
    any_spec = pl.BlockSpec(memory_space=pl.ANY)
    wspec = lambda a: pl.BlockSpec((1,) + a.shape[1:], lambda t, e, *_: (e, 0, 0))
    smem = lambda a: pl.BlockSpec((1,) + a.shape[1:], lambda t, e, *_: (t, 0, 0),
                                  memory_space=pltpu.SMEM)
    grid_spec = pltpu.PrefetchScalarGridSpec(
        num_scalar_prefetch=3,
        grid=(N_TILES, N_EXPERTS),
        in_specs=[smem(slots), smem(wts), any_spec, any_spec, any_spec, any_spec,
                  wspec(w_gate), wspec(w_up), wspec(w_down)],
        out_specs=(any_spec, any_spec),
        scratch_shapes=[
            pltpu.VMEM((TILE_TOKENS + DUMMY_ROWS, D_MODEL), F32),
            pltpu.VMEM((TILE_TOKENS + DUMMY_ROWS, D_MODEL), F32),
            pltpu.VMEM((MOE_ROWS // 8, 8, D_MODEL), F32),
            pltpu.VMEM((MOE_ROWS // 8, 8, D_MODEL), F32),
            pltpu.VMEM((MOE_ROWS // 8, 8, D_MODEL), F32),
            pltpu.VMEM((MOE_ROWS // 8, 8, D_MODEL), F32),
            pltpu.VMEM((D_MODEL, D_EXPERT), BF16),
            pltpu.VMEM((D_MODEL, D_EXPERT), BF16),
            pltpu.VMEM((D_EXPERT, D_MODEL), BF16),
            pltpu.SMEM((LIST_CAP,), jnp.int32),
            pltpu.SemaphoreType.DMA((4,)),
            pltpu.SemaphoreType.DMA((2,)),
        ],
    )
    return pl.pallas_call(
        _experts_body,
        grid_spec=grid_spec,
        out_shape=(jax.ShapeDtypeStruct(x1_p.shape, F32), jax.ShapeDtypeStruct(x1_s.shape, F32)),
        compiler_params=pltpu.CompilerParams(dimension_semantics=("arbitrary", "arbitrary"),
                                             vmem_limit_bytes=EXPERTS_VMEM_LIMIT),
        name="experts",
    )(nblk, seg, cnt, slots, wts, hn_p, hn_s, x1_p, x1_s, w_gate, w_up, w_down)


def _rope_tables(pos):
    half = HEAD_DIM // 2
    inv = ROPE_THETA ** (-jnp.arange(half, dtype=F32) * (2.0 / HEAD_DIM))
    ang = pos.astype(F32)[:, None] * inv[None, :]
    cos = jnp.tile(jnp.cos(ang), (1, LANES // half))
    sin = jnp.sin(ang)
    sin = jnp.tile(jnp.concatenate([-sin, sin], axis=1), (1, LANES // HEAD_DIM))
    return cos, sin


def _split_bf16(w):
    hi = w.astype(BF16)
    return hi, (w - hi.astype(F32)).astype(BF16)


def kernel(x_prompt, x_sample, cache_k, cache_v, state_pool, norm_mix_g, w_in, q_norm_g, k_norm_g,
           attn_sinks, w_attn_branch, pool_mix_w, pool_scale, w_pool_branch, w_out, norm_ffn_g,
           w_route_group, b_route_group, w_route_expert, b_route_expert, w_expert_gate,
           w_expert_up, w_expert_down):
    batch, seq, d = x_prompt.shape
    dec_batch, dec_seq, _ = x_sample.shape
    past_len = 16384
    assert x_prompt.shape == (4, 4096, D_MODEL) and x_sample.shape == (128, 4, D_MODEL)
    assert w_in.shape[0] == 1, "single layer"
    n_p, n_s = batch * seq, dec_batch * dec_seq

    g_mix = norm_mix_g[0][None, :]
    w_in_b = w_in[0].astype(BF16)
    gq = jnp.tile(q_norm_g[0], LANES // HEAD_DIM)[None, :]
    gk = jnp.tile(k_norm_g[0], LANES // HEAD_DIM)[None, :]
    sinks = attn_sinks[0]
    mix_b = pool_mix_w[0].astype(BF16)
    ps = pool_scale[0][None, :]
    wa_b = w_attn_branch[0].astype(BF16)
    wp_b = w_pool_branch[0].astype(BF16)
    wo_b = w_out[0].astype(BF16)
    gf = norm_ffn_g[0][None, :]
    wr = jnp.zeros((D_MODEL, LANES), F32)
    wr = wr.at[:, :N_EXPERT_GROUPS].set(w_route_group[0])
    wr = wr.at[:, EXPERT_LANE0:EXPERT_LANE0 + N_EXPERTS].set(w_route_expert[0])
    wr_hi, wr_lo = _split_bf16(wr)
    br = jnp.zeros((1, LANES), F32)
    br = br.at[0, :N_EXPERT_GROUPS].set(b_route_group[0])
    br = br.at[0, EXPERT_LANE0:EXPERT_LANE0 + N_EXPERTS].set(b_route_expert[0])

    cos_p, sin_p = _rope_tables(jnp.arange(seq, dtype=jnp.int32))
    pos_s = past_len + (jnp.arange(n_s, dtype=jnp.int32) % dec_seq)
    cos_s, sin_s = _rope_tables(pos_s)

    dense_consts = (mix_b, ps, wa_b, wp_b, wo_b, gf, wr_hi, wr_lo, br)
    assert n_s == ROW_BLOCK == N_TILES * TILE_SAMPLE and n_p == N_TILES * TILE_PROMPT
    ridx = jnp.arange(ROW_BLOCK, dtype=jnp.int32)
    lower = ridx[:, None] > ridx[None, :]
    seg_of = ridx // TILE_SAMPLE
    ltri_p = lower.astype(BF16)
    ltri_s = (lower & (seg_of[:, None] == seg_of[None, :])).astype(BF16)
    ssel_p = (jnp.arange(8, dtype=jnp.int32)[:, None] == 0) & (ridx[None, :] >= 0)
    ssel_s = jnp.arange(8, dtype=jnp.int32)[:, None] == seg_of[None, :]

    xp = x_prompt.reshape(n_p, d)
    q, k, v, kk, vv, u, sga, sgp = _inproj(xp, g_mix, w_in_b, cos_p, sin_p, gq, gk)
    attn, pd = _mixer_prompt(sinks, q, kk, vv, u, batch, seq)
    x1_p, hn_p, rt_p, cnt_p, _ = _dense(xp, attn, pd, sga, sgp, *dense_consts, ltri_p,
                                        ssel_p.astype(BF16), jnp.zeros((ROW_BLOCK, LANES), F32))
    new_k_p = k.reshape(batch, seq, N_KV_HEADS, HEAD_DIM)[:, -WINDOW:][None]
    new_v_p = v.reshape(batch, seq, N_KV_HEADS, HEAD_DIM)[:, -WINDOW:][None]
    new_u_p = u.reshape(batch, seq, d)[:, -POOL_STATE_LEN:][None]

    xs = x_sample.reshape(n_s, d)
    q, k, v, kk, vv, u, sga, sgp = _inproj(xs, g_mix, w_in_b, cos_s, sin_s, gq, gk)
    ck = cache_k[0].reshape(dec_batch, WINDOW, KV_W)
    cv = cache_v[0].reshape(dec_batch, WINDOW, KV_W)
    state16 = jnp.pad(state_pool[0], ((0, 0), (1, 0), (0, 0)))
    attn, pd = _mixer_sample(sinks, q.astype(F32), k, v, u, ck, cv, state16)
    cnt_tiles_p = cnt_p[TILE_BLOCKS - 1::TILE_BLOCKS, 0, :]
    crow = jnp.repeat(cnt_tiles_p, TILE_SAMPLE, axis=0)
    x1_s, hn_s, rt_s, _, seg_s = _dense(xs, attn, pd, sga, sgp, *dense_consts, ltri_s,
                                        ssel_s.astype(BF16), crow)
    k_new = k.reshape(dec_batch, dec_seq, KV_W)
    v_new = v.reshape(dec_batch, dec_seq, KV_W)
    u_new = u.reshape(dec_batch, dec_seq, d)
    new_k_s = jnp.concatenate([ck, k_new], axis=1)[:, -WINDOW:].reshape(
        1, dec_batch, WINDOW, N_KV_HEADS, HEAD_DIM)
    new_v_s = jnp.concatenate([cv, v_new], axis=1)[:, -WINDOW:].reshape(
        1, dec_batch, WINDOW, N_KV_HEADS, HEAD_DIM)
    new_u_s = jnp.concatenate([state_pool[0], u_new], axis=1)[:, -POOL_STATE_LEN:][None]

    ex = slice(EXPERT_LANE0, EXPERT_LANE0 + N_EXPERTS)
    cnt = (cnt_tiles_p[:, ex] + seg_s[0, :N_TILES, ex]).astype(jnp.int32)
    nblk = (cnt + (MOE_ROWS - 1)) // MOE_ROWS
    seg = MOE_ROWS * (1 + jnp.cumsum(nblk, axis=1) - nblk)
    rt = jnp.concatenate([rt_p[:, :6].reshape(N_TILES, TILE_PROMPT, 6),
                          rt_s[:, :6].reshape(N_TILES, TILE_SAMPLE, 6)], axis=1)
    expert_hit = rt[:, :, 0:2].astype(jnp.int32)[..., None] == jnp.arange(N_EXPERTS, dtype=jnp.int32)
    slot = (rt[:, :, 2:4].astype(jnp.int32)
            + jnp.sum(jnp.where(expert_hit, seg[:, None, None, :], 0), axis=-1))
    slots = slot.transpose(0, 2, 1)
    wts = rt[:, :, 4:6].reshape(N_TILES, 1, 2 * TILE_TOKENS)
    wts = jnp.pad(wts, ((0, 0), (0, 0), (0, 8)))
    y_p, y_s = _experts(nblk, seg, cnt, slots, wts, hn_p, hn_s, x1_p, x1_s,
                        w_expert_gate[0], w_expert_up[0], w_expert_down[0])
    y_prompt = y_p.reshape(batch, seq, d)
    y_sample = y_s.reshape(dec_batch, dec_seq, d)

    return (y_prompt, y_sample, new_k_p, new_v_p, new_u_p, new_k_s, new_v_s, new_u_s)
```

```python
import functools

import jax
import jax.numpy as jnp
from jax import lax
from jax.experimental import pallas as pl
from jax.experimental.pallas import tpu as pltpu

F32 = jnp.float32
BF16 = jnp.bfloat16

D_MODEL = 1024
HEAD_DIM = 64
N_Q_HEADS = 16
N_KV_HEADS = 2
WINDOW = 128
ROPE_THETA = 10000.0
POOL_WINDOWS = (2, 4, 8, 16)
POOL_GROUP_WIDTH = D_MODEL // len(POOL_WINDOWS)
POOL_STATE_LEN = max(POOL_WINDOWS) - 1
N_EXPERT_GROUPS = 4
EXPERTS_PER_GROUP = 8
N_EXPERTS = N_EXPERT_GROUPS * EXPERTS_PER_GROUP
D_EXPERT = 512
RMS_EPS = 1e-6
Q_W = N_Q_HEADS * HEAD_DIM
KV_W = N_KV_HEADS * HEAD_DIM
OFF_K = Q_W
OFF_V = OFF_K + KV_W
OFF_U = OFF_V + KV_W
OFF_GA = OFF_U + D_MODEL
OFF_GP = OFF_GA + D_MODEL
IN_W = OFF_GP + D_MODEL

LANES = 128
ROW_BLOCK = 512
EXPERT_LANE0 = 32
VMEM_LIMIT = 56 * 1024 * 1024
EXPERTS_VMEM_LIMIT = 60 * 1024 * 1024
NEG_INF = float("-inf")

N_TILES = 4
TILE_BLOCKS = 8
TILE_PROMPT = TILE_BLOCKS * ROW_BLOCK
TILE_SAMPLE = 128
TILE_TOKENS = TILE_PROMPT + TILE_SAMPLE
MOE_ROWS = 128
DUMMY_ROWS = 8
SCATTER_BATCH = 4
LIST_CAP = 2 * TILE_TOKENS + N_EXPERTS * (MOE_ROWS - 1) + 2 * MOE_ROWS + 1


def _rms(x, g):
    return x * lax.rsqrt(jnp.mean(x * x, axis=-1, keepdims=True) + RMS_EPS) * g


def _bdot(a, b):
    return jnp.dot(a, b, preferred_element_type=F32)


def _inproj_body(x_ref, g_ref, w_ref, cos_ref, sin_ref, gq_ref, gk_ref,
                 q_ref, k_ref, v_ref, kk_ref, vv_ref, u_ref, sga_ref, sgp_ref):
    rows = x_ref.shape[0]
    hb = _rms(x_ref[...], g_ref[...]).astype(BF16)
    cos = cos_ref[...]
    sin = sin_ref[...]
    lane = lax.broadcasted_iota(jnp.int32, (rows, LANES), 1)
    lo = lane < HEAD_DIM
    first_half = (lane % HEAD_DIM) < (HEAD_DIM // 2)

    def head_norm_rope(zc, gain):
        sq = zc * zc
        ss_lo = jnp.sum(jnp.where(lo, sq, 0.0), axis=-1, keepdims=True)
        ss_hi = jnp.sum(jnp.where(lo, 0.0, sq), axis=-1, keepdims=True)
        r = lax.rsqrt(jnp.where(lo, ss_lo, ss_hi) * (1.0 / HEAD_DIM) + RMS_EPS)
        y = zc * r * gain
        partner = jnp.where(first_half, pltpu.roll(y, LANES - HEAD_DIM // 2, 1),
                            pltpu.roll(y, HEAD_DIM // 2, 1))
        return y * cos + partner * sin

    gq = gq_ref[...]
    for j in range(Q_W // 256):
        z = _bdot(hb, w_ref[:, 256 * j:256 * (j + 1)])
        for c in range(2):
            qn = head_norm_rope(z[:, LANES * c:LANES * (c + 1)], gq) * (HEAD_DIM ** -0.5)
            q_ref[:, 256 * j + LANES * c:256 * j + LANES * (c + 1)] = qn.astype(BF16)

    z = _bdot(hb, w_ref[:, OFF_K:OFF_U])
    kn = head_norm_rope(z[:, :KV_W], gk_ref[...])
    vr = z[:, KV_W:]
    k_ref[...] = kn
    v_ref[...] = vr
    kr = pltpu.roll(kn, HEAD_DIM, 1)
    vrr = pltpu.roll(vr, HEAD_DIM, 1)
    kk_ref[:, :LANES] = jnp.where(lo, kn, kr).astype(BF16)
    kk_ref[:, LANES:] = jnp.where(lo, kr, kn).astype(BF16)
    vv_ref[:, :LANES] = jnp.where(lo, vr, vrr).astype(BF16)
    vv_ref[:, LANES:] = jnp.where(lo, vrr, vr).astype(BF16)

    for j in range(D_MODEL // 256):
        u_ref[:, 256 * j:256 * (j + 1)] = _bdot(hb, w_ref[:, OFF_U + 256 * j:OFF_U + 256 * (j + 1)])
        sga_ref[:, 256 * j:256 * (j + 1)] = jax.nn.sigmoid(
            _bdot(hb, w_ref[:, OFF_GA + 256 * j:OFF_GA + 256 * (j + 1)]))
        sgp_ref[:, 256 * j:256 * (j + 1)] = jax.nn.sigmoid(
            _bdot(hb, w_ref[:, OFF_GP + 256 * j:OFF_GP + 256 * (j + 1)]))


def _inproj(x, g, w_in_b, cos, sin, gq, gk):
    n = x.shape[0]
    nb = n // ROW_BLOCK
    ncos = cos.shape[0] // ROW_BLOCK
    row = lambda w: pl.BlockSpec((ROW_BLOCK, w), lambda i: (i, 0))
    full = lambda a: pl.BlockSpec(a.shape, lambda i: (0,) * a.ndim)
    tab = pl.BlockSpec((ROW_BLOCK, LANES), lambda i: (i % ncos, 0))
    out_shapes = (
        jax.ShapeDtypeStruct((n, Q_W), BF16),
        jax.ShapeDtypeStruct((n, KV_W), F32),
        jax.ShapeDtypeStruct((n, KV_W), F32),
        jax.ShapeDtypeStruct((n, 2 * LANES), BF16),
        jax.ShapeDtypeStruct((n, 2 * LANES), BF16),
        jax.ShapeDtypeStruct((n, D_MODEL), F32),
        jax.ShapeDtypeStruct((n, D_MODEL), F32),
        jax.ShapeDtypeStruct((n, D_MODEL), F32),
    )
    return pl.pallas_call(
        _inproj_body,
        grid=(nb,),
        in_specs=[row(D_MODEL), full(g), full(w_in_b), tab, tab, full(gq), full(gk)],
        out_specs=tuple(row(s.shape[1]) for s in out_shapes),
        out_shape=out_shapes,
        compiler_params=pltpu.CompilerParams(dimension_semantics=("arbitrary",),
                                             vmem_limit_bytes=VMEM_LIMIT),
        name="inproj",
    )(x, g, w_in_b, cos, sin, gq, gk)


def _attend(q2, kg, vg, cur_valid, sink_col, prev_dead):
    s = lax.dot_general(q2, kg, (((1,), (1,)), ((), ())), preferred_element_type=F32)
    s_prev = s[:, :WINDOW]
    if prev_dead is not None:
        s_prev = jnp.where(prev_dead, NEG_INF, s_prev)
    sc = jnp.where(cur_valid, s[:, WINDOW:], s_prev)
    m = jnp.maximum(jnp.max(sc, axis=-1, keepdims=True), sink_col)
    p = jnp.exp(sc - m)
    den = jnp.sum(p, axis=-1, keepdims=True) + jnp.exp(sink_col - m)
    pn = p * (1.0 / den)
    p2 = jnp.concatenate([jnp.where(cur_valid, 0.0, pn), jnp.where(cur_valid, pn, 0.0)], axis=1)
    return _bdot(p2.astype(BF16), vg)


POOL_PAD_ROWS = 8


def _mixer_prompt_body(sink_ref, q_ref, kkc_ref, kkp_ref, vvc_ref, vvp_ref, uc_ref, up_ref,
                       attn_ref, pd_ref, ext_ref, s_a, s_b):
    i = pl.program_id(1)
    first = i == 0
    m2 = 2 * WINDOW
    row = lax.broadcasted_iota(jnp.int32, (m2, LANES), 0)
    col = lax.broadcasted_iota(jnp.int32, (m2, LANES), 1)
    cur_valid = (row % WINDOW) >= col
    upper = lax.broadcasted_iota(jnp.int32, (m2, 1), 0) < WINDOW
    lo = lax.broadcasted_iota(jnp.int32, (WINDOW, LANES), 1) < HEAD_DIM

    for j in range(ROW_BLOCK // WINDOW):
        r0 = WINDOW * j
        if j == 0:
            kprev, vprev = kkp_ref[...], vvp_ref[...]
            prev_dead = first
        else:
            kprev, vprev = kkc_ref[r0 - WINDOW:r0, :], vvc_ref[r0 - WINDOW:r0, :]
            prev_dead = None
        kband = jnp.concatenate([kprev, kkc_ref[r0:r0 + WINDOW, :]], axis=0)
        vband = jnp.concatenate([vprev, vvc_ref[r0:r0 + WINDOW, :]], axis=0)
        for g in range(N_KV_HEADS):
            kg = kband[:, LANES * g:LANES * (g + 1)]
            vg = vband[:, LANES * g:LANES * (g + 1)]
            for c in range(4):
                cc = 4 * g + c
                qc = q_ref[r0:r0 + WINDOW, LANES * cc:LANES * (cc + 1)]
                zero = jnp.zeros_like(qc)
                q2 = jnp.concatenate([jnp.where(lo, qc, zero), jnp.where(lo, zero, qc)], axis=0)
                sink_col = jnp.where(upper, sink_ref[2 * cc], sink_ref[2 * cc + 1])
                o = _attend(q2, kg, vg, cur_valid, sink_col, prev_dead)
                attn_ref[r0:r0 + WINDOW, LANES * cc:LANES * (cc + 1)] = jnp.where(
                    lo, o[:WINDOW], o[WINDOW:]).astype(BF16)

    pad, top = POOL_PAD_ROWS, POOL_PAD_ROWS + 16
    end = top + ROW_BLOCK
    for ref in (ext_ref, s_a, s_b):
        ref[0:pad, :] = jnp.zeros((pad, D_MODEL), F32)
    ext_ref[pad:top, :] = jnp.where(first, 0.0, up_ref[...])
    ext_ref[top:, :] = uc_ref[...]
    pos = i * ROW_BLOCK + lax.broadcasted_iota(jnp.int32, (ROW_BLOCK, 1), 0)
    src = ext_ref
    for g, w in enumerate(POOL_WINDOWS):
        c0 = POOL_GROUP_WIDTH * g
        dst = (s_a, s_b)[g % 2]
        shift = w // 2
        dst[pad:end, c0:] = src[pad:end, c0:] + src[pad - shift:end - shift, c0:]
        c1 = c0 + POOL_GROUP_WIDTH
        cnt = jnp.minimum(w, pos + 1).astype(F32)
        pd_ref[:, c0:c1] = (dst[top:end, c0:c1] / cnt - uc_ref[:, c0:c1]).astype(BF16)
        src = dst


def _mixer_prompt(sinks, q, kk, vv, u, batch, seq):
    nb = seq // ROW_BLOCK
    sub = ROW_BLOCK // WINDOW
    cur = lambda w: pl.BlockSpec((ROW_BLOCK, w), lambda b, i: (b * nb + i, 0))
    prev_kv = pl.BlockSpec((WINDOW, 2 * LANES),
                           lambda b, i: (jnp.maximum((b * nb + i) * sub - 1, 0), 0))
    prev_u = pl.BlockSpec((16, D_MODEL),
                          lambda b, i: (jnp.maximum((b * nb + i) * (ROW_BLOCK // 16) - 1, 0), 0))
    n = batch * seq
    return pl.pallas_call(
        _mixer_prompt_body,
        grid=(batch, nb),
        in_specs=[pl.BlockSpec(memory_space=pltpu.SMEM), cur(Q_W), cur(2 * LANES), prev_kv,
                  cur(2 * LANES), prev_kv, cur(D_MODEL), prev_u],
        out_specs=(cur(Q_W), cur(D_MODEL)),
        out_shape=(jax.ShapeDtypeStruct((n, Q_W), BF16), jax.ShapeDtypeStruct((n, D_MODEL), BF16)),
        scratch_shapes=[pltpu.VMEM((POOL_PAD_ROWS + 16 + ROW_BLOCK, D_MODEL), F32)] * 3,
        compiler_params=pltpu.CompilerParams(dimension_semantics=("arbitrary", "arbitrary"),
                                             vmem_limit_bytes=VMEM_LIMIT),
        name="mixer_prompt",
    )(sinks, q, kk, kk, vv, vv, u, u)


SEQ_PAIR_ROWS = 8
SAMPLE_PAIRS_PER_STEP = 8


def _mixer_sample_body(sink_ref, q_ref, k_ref, v_ref, u_ref, ck_ref, cv_ref, st_ref,
                       attn_ref, pd_ref, nk_ref, nv_ref, nu_ref, ext_ref):
    m = SEQ_PAIR_ROWS
    half = m // 2
    hist = POOL_STATE_LEN
    ext_ref[16:, :] = jnp.zeros((ext_ref.shape[0] - 16, D_MODEL), F32)
    row8 = lax.broadcasted_iota(jnp.int32, (m, LANES), 0)
    n_chunks = 4
    mq = n_chunks * m
    row1 = lax.broadcasted_iota(jnp.int32, (m, 1), 0)
    row = lax.broadcasted_iota(jnp.int32, (mq, LANES), 0)
    col = lax.broadcasted_iota(jnp.int32, (mq, LANES), 1)
    cur_valid = (row % half) >= col
    lane8 = lax.broadcasted_iota(jnp.int32, (m, LANES), 1)
    lo8 = lane8 < HEAD_DIM
    top8 = lax.broadcasted_iota(jnp.int32, (m, LANES), 0) < half
    lob = lax.broadcasted_iota(jnp.int32, (2 * WINDOW, LANES), 1) < HEAD_DIM
    tail = jnp.zeros((WINDOW - m, LANES), F32)
    top = row1 < half
    sink_cols = [jnp.concatenate([jnp.where(top, sink_ref[2 * (4 * g + c)], sink_ref[2 * (4 * g + c) + 1])
                                  for c in range(n_chunks)], axis=0) for g in range(N_KV_HEADS)]

    def pair(p, carry):
        r0 = pl.multiple_of(p * m, m)
        q8 = q_ref[pl.ds(r0, m), :]
        k8 = k_ref[pl.ds(r0, m), :]
        v8 = v_ref[pl.ds(r0, m), :]
        u8 = u_ref[pl.ds(r0, m), :]
        attn_parts = []
        pd_parts = []
        for s in range(2):
            shift = lambda a: a if s == 0 else pltpu.roll(a, half, 0)
            seq = 2 * p + s
            ck, cv = ck_ref[seq], cv_ref[seq]
            ks, vs = shift(k8), shift(v8)
            kb = jnp.concatenate([ck, ks, tail], axis=0)
            vb = jnp.concatenate([cv, vs, tail], axis=0)
            for cache, new, out in ((ck, ks, nk_ref), (cv, vs, nv_ref)):
                up = pltpu.roll(cache, WINDOW - half, 0)
                out[seq, 0:WINDOW - m, :] = up[0:WINDOW - m]
                out[seq, WINDOW - m:WINDOW, :] = jnp.where(row8 < half, up[WINDOW - m:],
                                                           pltpu.roll(new, half, 0))
            kbr = pltpu.roll(kb, HEAD_DIM, 1)
            vbr = pltpu.roll(vb, HEAD_DIM, 1)
            qs = shift(q8)
            qr = pltpu.roll(qs, half, 0)
            chunks = []
            for g in range(N_KV_HEADS):
                kg = (jnp.where(lob, kb, kbr) if g == 0 else jnp.where(lob, kbr, kb)).astype(BF16)
                vg = (jnp.where(lob, vb, vbr) if g == 0 else jnp.where(lob, vbr, vb)).astype(BF16)
                q2 = []
                for c in range(n_chunks):
                    cc = n_chunks * g + c
                    qa = qs[:, LANES * cc:LANES * (cc + 1)]
                    qb = qr[:, LANES * cc:LANES * (cc + 1)]
                    q2.append(jnp.where(top8, jnp.where(lo8, qa, 0.0), jnp.where(lo8, 0.0, qb)))
                q2 = jnp.concatenate(q2, axis=0).astype(BF16)
                o = _attend(q2, kg, vg, cur_valid, sink_cols[g], None)
                for c in range(n_chunks):
                    oc = o[m * c:m * (c + 1)]
                    chunks.append(jnp.where(lo8, oc, pltpu.roll(oc, half, 0)))
            attn_parts.append(jnp.concatenate(chunks, axis=1))

            ext_ref[0:hist, :] = st_ref[seq]
            ext_ref[hist:hist + half, :] = shift(u8)[0:half]
            nu_ref[seq] = ext_ref[half:hist + half, :]
            cols = []
            for g, w in enumerate(POOL_WINDOWS):
                c0, c1 = POOL_GROUP_WIDTH * g, POOL_GROUP_WIDTH * (g + 1)
                acc = ext_ref[hist:hist + m, c0:c1]
                for k in range(1, w):
                    acc = acc + ext_ref[hist - k:hist - k + m, c0:c1]
                cols.append(acc / float(w) - ext_ref[hist:hist + m, c0:c1])
            pd_parts.append(jnp.concatenate(cols, axis=1))

        attn_ref[pl.ds(r0, m), :] = jnp.where(top, attn_parts[0], pltpu.roll(attn_parts[1], half, 0))
        pd_ref[pl.ds(r0, m), :] = jnp.where(top, pd_parts[0], pltpu.roll(pd_parts[1], half, 0))
        return carry

    lax.fori_loop(0, SAMPLE_PAIRS_PER_STEP, pair, 0)


def _mixer_sample(sinks, q, k, v, u, cache_k, cache_v, state):
    n = q.shape[0]
    m = SEQ_PAIR_ROWS * SAMPLE_PAIRS_PER_STEP
    rows = lambda w: pl.BlockSpec((m, w), lambda i: (i, 0))
    seqs = lambda a: pl.BlockSpec((2 * SAMPLE_PAIRS_PER_STEP,) + a.shape[1:], lambda i: (i, 0, 0))
    like = lambda a: jax.ShapeDtypeStruct(a.shape, F32)
    return pl.pallas_call(
        _mixer_sample_body,
        grid=(n // m,),
        in_specs=[pl.BlockSpec(memory_space=pltpu.SMEM), rows(Q_W), rows(KV_W), rows(KV_W),
                  rows(D_MODEL), seqs(cache_k), seqs(cache_v), seqs(state)],
        out_specs=(rows(Q_W), rows(D_MODEL), seqs(cache_k), seqs(cache_v), seqs(state)),
        out_shape=(jax.ShapeDtypeStruct((n, Q_W), F32), jax.ShapeDtypeStruct((n, D_MODEL), F32),
                   like(cache_k), like(cache_v), like(state)),
        scratch_shapes=[pltpu.VMEM((16 + SEQ_PAIR_ROWS, D_MODEL), F32)],
        compiler_params=pltpu.CompilerParams(dimension_semantics=("arbitrary",),
                                             vmem_limit_bytes=VMEM_LIMIT),
        name="mixer_sample",
    )(sinks, q, k, v, u, cache_k, cache_v, state)


def _dense_body(x_ref, attn_ref, pd_ref, sga_ref, sgp_ref, mix_ref, ps_ref, wa_ref, wp_ref,
                wo_ref, gf_ref, wrh_ref, wrl_ref, br_ref, ltri_ref, ssel_ref, crow_ref,
                x1_ref, hn_ref, rt_ref, cnt_ref, seg_ref, carry_ref):
    rows = x_ref.shape[0]
    step = pl.program_id(0)

    @pl.when(step % TILE_BLOCKS == 0)
    def _():
        carry_ref[...] = jnp.zeros_like(carry_ref)

    pd = pd_ref[...].astype(BF16)
    pooled = []
    for g in range(len(POOL_WINDOWS)):
        c0, c1 = POOL_GROUP_WIDTH * g, POOL_GROUP_WIDTH * (g + 1)
        pooled.append((_bdot(pd[:, c0:c1], mix_ref[g]) * ps_ref[:, c0:c1]).astype(BF16))
    pooled = jnp.concatenate(pooled, axis=1)
    merged = (sga_ref[...] * _bdot(attn_ref[...].astype(BF16), wa_ref[...])
              + sgp_ref[...] * _bdot(pooled, wp_ref[...]))
    x1 = x_ref[...] + _bdot(merged.astype(BF16), wo_ref[...])
    x1_ref[...] = x1
    hn = _rms(x1, gf_ref[...])
    hn_ref[...] = hn
    hi = hn.astype(BF16)
    lo = (hn - hi.astype(F32)).astype(BF16)
    logits = (_bdot(hi, wrh_ref[...]) + _bdot(hi, wrl_ref[...]) + _bdot(lo, wrh_ref[...])
              + br_ref[...])

    lane = lax.broadcasted_iota(jnp.int32, (rows, LANES), 1)
    big = jnp.int32(LANES)
    gl = jnp.where(lane < N_EXPERT_GROUPS, logits, NEG_INF)
    gmax = jnp.max(gl, axis=-1, keepdims=True)
    gidx = jnp.min(jnp.where(gl == gmax, lane, big), axis=-1, keepdims=True)
    g_w = 1.0 / jnp.sum(jnp.exp(gl - gmax), axis=-1, keepdims=True)
    e0 = EXPERT_LANE0 + gidx * EXPERTS_PER_GROUP
    el = jnp.where((lane >= e0) & (lane < e0 + EXPERTS_PER_GROUP), logits, NEG_INF)
    l1 = jnp.max(el, axis=-1, keepdims=True)
    i1 = jnp.min(jnp.where(el == l1, lane, big), axis=-1, keepdims=True)
    el2 = jnp.where(lane == i1, NEG_INF, el)
    l2 = jnp.max(el2, axis=-1, keepdims=True)
    i2 = jnp.min(jnp.where(el2 == l2, lane, big), axis=-1, keepdims=True)
    e = jnp.exp(l2 - l1)
    w1 = 1.0 / (1.0 + e)
    w2 = e * w1

    sel = (lane == i1) | (lane == i2)
    onehot = jnp.where(sel, 1.0, 0.0).astype(BF16)
    rank = _bdot(ltri_ref[...], onehot) + carry_ref[0:1, :] + crow_ref[...]
    segsum = _bdot(ssel_ref[...], onehot)
    seg_ref[0] = segsum
    carry_ref[...] = carry_ref[...] + segsum[0:1, :]
    cnt_ref[0] = carry_ref[...]
    r1 = jnp.sum(jnp.where(lane == i1, rank, 0.0), axis=-1, keepdims=True)
    r2 = jnp.sum(jnp.where(lane == i2, rank, 0.0), axis=-1, keepdims=True)
    cols = ((i1 - EXPERT_LANE0).astype(F32), (i2 - EXPERT_LANE0).astype(F32), r1, r2,
            g_w * w1, g_w * w2)
    tile = jnp.zeros((rows, LANES), F32)
    for c, val in enumerate(cols):
        tile = jnp.where(lane == c, val, tile)
    rt_ref[...] = tile


def _dense(x, attn, pd, sga, sgp, mix_b, pool_scale, wa_b, wp_b, wo_b, gf, wr_hi, wr_lo, br,
           ltri, ssel, crow):
    n = x.shape[0]
    nb = n // ROW_BLOCK
    row = lambda w: pl.BlockSpec((ROW_BLOCK, w), lambda i: (i, 0))
    full = lambda a: pl.BlockSpec(a.shape, lambda i: (0,) * a.ndim)
    stat = pl.BlockSpec((1, 8, LANES), lambda i: (i, 0, 0))
    consts = (mix_b, pool_scale, wa_b, wp_b, wo_b, gf, wr_hi, wr_lo, br, ltri, ssel, crow)
    return pl.pallas_call(
        _dense_body,
        grid=(nb,),
        in_specs=[row(D_MODEL)] * 5 + [full(a) for a in consts],
        out_specs=(row(D_MODEL), row(D_MODEL), row(LANES), stat, stat),
        out_shape=(jax.ShapeDtypeStruct((n, D_MODEL), F32),
                   jax.ShapeDtypeStruct((n, D_MODEL), F32),
                   jax.ShapeDtypeStruct((n, LANES), F32),
                   jax.ShapeDtypeStruct((nb, 8, LANES), F32),
                   jax.ShapeDtypeStruct((nb, 8, LANES), F32)),
        scratch_shapes=[pltpu.VMEM((8, LANES), F32)],
        compiler_params=pltpu.CompilerParams(dimension_semantics=("arbitrary",),
                                             vmem_limit_bytes=VMEM_LIMIT),
        name="dense",
    )(x, attn, pd, sga, sgp, *consts)


def _experts_body(nblk_ref, seg_ref, cnt_ref, slots_ref, wts_ref, hn_p, hn_s, x1_p, x1_s,
                  wg_ref, wu_ref, wd_ref, y_p, y_s,
                  hn_t, acc, xg0, xg1, yb0, yb1, wgb, wub, wdb, tok, sem_in, sem_out):
    tau = pl.program_id(0)
    e = pl.program_id(1)
    t_rows = TILE_TOKENS
    dummy = 2 * t_rows
    p_rows = pl.ds(pl.multiple_of(tau * TILE_PROMPT, TILE_PROMPT), TILE_PROMPT)
    s_rows = pl.ds(pl.multiple_of(tau * TILE_SAMPLE, TILE_SAMPLE), TILE_SAMPLE)
    tile_p = pl.ds(0, TILE_PROMPT)
    tile_s = pl.ds(TILE_PROMPT, TILE_SAMPLE)

    def in_copies():
        return [pltpu.make_async_copy(hn_p.at[p_rows], hn_t.at[tile_p], sem_in.at[0]),
                pltpu.make_async_copy(hn_s.at[s_rows], hn_t.at[tile_s], sem_in.at[1]),
                pltpu.make_async_copy(x1_p.at[p_rows], acc.at[tile_p], sem_in.at[2]),
                pltpu.make_async_copy(x1_s.at[s_rows], acc.at[tile_s], sem_in.at[3])]

    def out_copies():
        return [pltpu.make_async_copy(acc.at[tile_p], y_p.at[p_rows], sem_out.at[0]),
                pltpu.make_async_copy(acc.at[tile_s], y_s.at[s_rows], sem_out.at[1])]

    xbufs = (xg0, xg1)
    ybufs = (yb0, yb1)
    end_slot = seg_ref[tau, N_EXPERTS - 1] + nblk_ref[tau, N_EXPERTS - 1] * MOE_ROWS

    def gather_block(b, xdst):
        base = b * MOE_ROWS
        for j in range(MOE_ROWS):
            t = lax.shift_right_logical(tok[base + j], 1)
            xdst[j // 8, pl.ds(j % 8, 1), :] = hn_t[pl.ds(t, 1), :]

    def scatter_block(b, ysrc):
        base = b * MOE_ROWS
        for j0 in range(0, MOE_ROWS, SCATTER_BATCH):
            ents = [tok[base + j0 + i] for i in range(SCATTER_BATCH)]
            rows = [lax.shift_right_logical(en, 1) for en in ents]
            vals = [acc[pl.ds(rows[i], 1), :]
                    + wts_ref[0, 0, ents[i]] * ysrc[(j0 + i) // 8, pl.ds((j0 + i) % 8, 1), :]
                    for i in range(SCATTER_BATCH)]
            for i in range(SCATTER_BATCH):
                acc[pl.ds(rows[i], 1), :] = vals[i]

    @pl.when(e == 0)
    def _load_tile():
        for cp in in_copies():
            cp.start()

        hn_t[t_rows:, :] = jnp.zeros((DUMMY_ROWS, D_MODEL), F32)
        yb0[...] = jnp.zeros_like(yb0)

        def pad_range(lo, hi):
            def fill(p, c):
                tok[p] = dummy
                return c
            lax.fori_loop(lo, hi, fill, 0)
        pad_range(0, MOE_ROWS)
        pad_range(end_slot, end_slot + MOE_ROWS)

        def pad_fill(ei, c):
            s0 = seg_ref[tau, ei]
            pad_range(s0 + cnt_ref[tau, ei], s0 + nblk_ref[tau, ei] * MOE_ROWS)
            return c
        lax.fori_loop(0, N_EXPERTS, pad_fill, 0)

        def invert(g, c):
            t0 = g * 8
            for i in range(8):
                for k in range(2):
                    tok[slots_ref[0, k, t0 + i]] = 2 * (t0 + i) + k
            return c
        lax.fori_loop(0, t_rows // 8, invert, 0)

        for cp in in_copies():
            cp.wait()
        acc[t_rows:, :] = jnp.zeros((DUMMY_ROWS, D_MODEL), F32)
        gather_block(1, xbufs[1])

    wgb[...] = wg_ref[0].astype(BF16)
    wub[...] = wu_ref[0].astype(BF16)
    wdb[...] = wd_ref[0].astype(BF16)

    first_block = lax.shift_right_logical(seg_ref[tau, e], MOE_ROWS.bit_length() - 1)

    def block(jb, c):
        b = first_block + jb
        for par in range(2):
            @pl.when((b & 1) == par)
            def _():
                gather_block(b + 1, xbufs[1 - par])
                xb = xbufs[par][...].reshape(MOE_ROWS, D_MODEL).astype(BF16)
                act = (jax.nn.silu(_bdot(xb, wgb[...])) * _bdot(xb, wub[...])).astype(BF16)
                ybufs[par][...] = _bdot(act, wdb[...]).reshape(MOE_ROWS // 8, 8, D_MODEL)
                scatter_block(b - 1, ybufs[1 - par])
        return c
    lax.fori_loop(0, nblk_ref[tau, e], block, 0)

    @pl.when(e == N_EXPERTS - 1)
    def _store_tile():
        last = lax.shift_right_logical(end_slot, MOE_ROWS.bit_length() - 1) - 1
        for par in range(2):
            @pl.when((last & 1) == par)
            def _():
                scatter_block(last, ybufs[par])
        cps = out_copies()
        for cp in cps:
            cp.start()
        for cp in cps:
            cp.wait()


def _experts(nblk, seg, cnt, slots, wts, hn_p, hn_s, x1_p, x1_s, w_gate, w_up, w_down):
    any_spec = pl.BlockSpec(memory_space=pl.ANY)
    wspec = lambda a: pl.BlockSpec((1,) + a.shape[1:], lambda t, e, *_: (e, 0, 0))
    smem = lambda a: pl.BlockSpec((1,) + a.shape[1:], lambda t, e, *_: (t, 0, 0),
                                  memory_space=pltpu.SMEM)
    grid_spec = pltpu.PrefetchScalarGridSpec(
        num_scalar_prefetch=3,
        grid=(N_TILES, N_EXPERTS),
        in_specs=[smem(slots), smem(wts), any_spec, any_spec, any_spec, any_spec,
                  wspec(w_gate), wspec(w_up), wspec(w_down)],
        out_specs=(any_spec, any_spec),
        scratch_shapes=[
            pltpu.VMEM((TILE_TOKENS + DUMMY_ROWS, D_MODEL), F32),
            pltpu.VMEM((TILE_TOKENS + DUMMY_ROWS, D_MODEL), F32),
            pltpu.VMEM((MOE_ROWS // 8, 8, D_MODEL), F32),
            pltpu.VMEM((MOE_ROWS // 8, 8, D_MODEL), F32),
            pltpu.VMEM((MOE_ROWS // 8, 8, D_MODEL), F32),
            pltpu.VMEM((MOE_ROWS // 8, 8, D_MODEL), F32),
            pltpu.VMEM((D_MODEL, D_EXPERT), BF16),
            pltpu.VMEM((D_MODEL, D_EXPERT), BF16),
            pltpu.VMEM((D_EXPERT, D_MODEL), BF16),
            pltpu.SMEM((LIST_CAP,), jnp.int32),
            pltpu.SemaphoreType.DMA((4,)),
            pltpu.SemaphoreType.DMA((2,)),
        ],
    )
    return pl.pallas_call(
        _experts_body,
        grid_spec=grid_spec,
        out_shape=(jax.ShapeDtypeStruct(x1_p.shape, F32), jax.ShapeDtypeStruct(x1_s.shape, F32)),
        compiler_params=pltpu.CompilerParams(dimension_semantics=("arbitrary", "arbitrary"),
                                             vmem_limit_bytes=EXPERTS_VMEM_LIMIT),
        name="experts",
    )(nblk, seg, cnt, slots, wts, hn_p, hn_s, x1_p, x1_s, w_gate, w_up, w_down)


def _rope_tables(pos):
    half = HEAD_DIM // 2
    inv = ROPE_THETA ** (-jnp.arange(half, dtype=F32) * (2.0 / HEAD_DIM))
    ang = pos.astype(F32)[:, None] * inv[None, :]
    cos = jnp.tile(jnp.cos(ang), (1, LANES // half))
    sin = jnp.sin(ang)
    sin = jnp.tile(jnp.concatenate([-sin, sin], axis=1), (1, LANES // HEAD_DIM))
    return cos, sin


def _split_bf16(w):
    hi = w.astype(BF16)
    return hi, (w - hi.astype(F32)).astype(BF16)


def kernel(x_prompt, x_sample, cache_k, cache_v, state_pool, norm_mix_g, w_in, q_norm_g, k_norm_g,
           attn_sinks, w_attn_branch, pool_mix_w, pool_scale, w_pool_branch, w_out, norm_ffn_g,
           w_route_group, b_route_group, w_route_expert, b_route_expert, w_expert_gate,
           w_expert_up, w_expert_down):
    batch, seq, d = x_prompt.shape
    dec_batch, dec_seq, _ = x_sample.shape
    past_len = 16384
    assert x_prompt.shape == (4, 4096, D_MODEL) and x_sample.shape == (128, 4, D_MODEL)
    assert w_in.shape[0] == 1, "single layer"
    n_p, n_s = batch * seq, dec_batch * dec_seq

    g_mix = norm_mix_g[0][None, :]
    w_in_b = w_in[0].astype(BF16)
    gq = jnp.tile(q_norm_g[0], LANES // HEAD_DIM)[None, :]
    gk = jnp.tile(k_norm_g[0], LANES // HEAD_DIM)[None, :]
    sinks = attn_sinks[0]
    mix_b = pool_mix_w[0].astype(BF16)
    ps = pool_scale[0][None, :]
    wa_b = w_attn_branch[0].astype(BF16)
    wp_b = w_pool_branch[0].astype(BF16)
    wo_b = w_out[0].astype(BF16)
    gf = norm_ffn_g[0][None, :]
    wr = jnp.zeros((D_MODEL, LANES), F32)
    wr = wr.at[:, :N_EXPERT_GROUPS].set(w_route_group[0])
    wr = wr.at[:, EXPERT_LANE0:EXPERT_LANE0 + N_EXPERTS].set(w_route_expert[0])
    wr_hi, wr_lo = _split_bf16(wr)
    br = jnp.zeros((1, LANES), F32)
    br = br.at[0, :N_EXPERT_GROUPS].set(b_route_group[0])
    br = br.at[0, EXPERT_LANE0:EXPERT_LANE0 + N_EXPERTS].set(b_route_expert[0])

    cos_p, sin_p = _rope_tables(jnp.arange(seq, dtype=jnp.int32))
    pos_s = past_len + (jnp.arange(n_s, dtype=jnp.int32) % dec_seq)
    cos_s, sin_s = _rope_tables(pos_s)

    dense_consts = (mix_b, ps, wa_b, wp_b, wo_b, gf, wr_hi, wr_lo, br)
    assert n_s == ROW_BLOCK == N_TILES * TILE_SAMPLE and n_p == N_TILES * TILE_PROMPT
    ridx = jnp.arange(ROW_BLOCK, dtype=jnp.int32)
    lower = ridx[:, None] > ridx[None, :]
    seg_of = ridx // TILE_SAMPLE
    ltri_p = lower.astype(BF16)
    ltri_s = (lower & (seg_of[:, None] == seg_of[None, :])).astype(BF16)
    ssel_p = (jnp.arange(8, dtype=jnp.int32)[:, None] == 0) & (ridx[None, :] >= 0)
    ssel_s = jnp.arange(8, dtype=jnp.int32)[:, None] == seg_of[None, :]

    xp = x_prompt.reshape(n_p, d)
    q, k, v, kk, vv, u, sga, sgp = _inproj(xp, g_mix, w_in_b, cos_p, sin_p, gq, gk)
    attn, pd = _mixer_prompt(sinks, q, kk, vv, u, batch, seq)
    x1_p, hn_p, rt_p, cnt_p, _ = _dense(xp, attn, pd, sga, sgp, *dense_consts, ltri_p,
                                        ssel_p.astype(BF16), jnp.zeros((ROW_BLOCK, LANES), F32))
    new_k_p = k.reshape(batch, seq, N_KV_HEADS, HEAD_DIM)[:, -WINDOW:][None]
    new_v_p = v.reshape(batch, seq, N_KV_HEADS, HEAD_DIM)[:, -WINDOW:][None]
    new_u_p = u.reshape(batch, seq, d)[:, -POOL_STATE_LEN:][None]

    xs = x_sample.reshape(n_s, d)
    q, k, v, kk, vv, u, sga, sgp = _inproj(xs, g_mix, w_in_b, cos_s, sin_s, gq, gk)
    ck = cache_k[0].reshape(dec_batch, WINDOW, KV_W)
    cv = cache_v[0].reshape(dec_batch, WINDOW, KV_W)
    attn, pd, nk_s, nv_s, nu_s = _mixer_sample(sinks, q.astype(F32), k, v, u, ck, cv, state_pool[0])
    cnt_tiles_p = cnt_p[TILE_BLOCKS - 1::TILE_BLOCKS, 0, :]
    crow = jnp.repeat(cnt_tiles_p, TILE_SAMPLE, axis=0)
    x1_s, hn_s, rt_s, _, seg_s = _dense(xs, attn, pd, sga, sgp, *dense_consts, ltri_s,
                                        ssel_s.astype(BF16), crow)
    new_k_s = nk_s.reshape(1, dec_batch, WINDOW, N_KV_HEADS, HEAD_DIM)
    new_v_s = nv_s.reshape(1, dec_batch, WINDOW, N_KV_HEADS, HEAD_DIM)
    new_u_s = nu_s[None]

    ex = slice(EXPERT_LANE0, EXPERT_LANE0 + N_EXPERTS)
    cnt = (cnt_tiles_p[:, ex] + seg_s[0, :N_TILES, ex]).astype(jnp.int32)
    nblk = (cnt + (MOE_ROWS - 1)) // MOE_ROWS
    seg = MOE_ROWS * (1 + jnp.cumsum(nblk, axis=1) - nblk)
    rt = jnp.concatenate([rt_p[:, :6].reshape(N_TILES, TILE_PROMPT, 6),
                          rt_s[:, :6].reshape(N_TILES, TILE_SAMPLE, 6)], axis=1)
    expert_hit = rt[:, :, 0:2].astype(jnp.int32)[..., None] == jnp.arange(N_EXPERTS, dtype=jnp.int32)
    slot = (rt[:, :, 2:4].astype(jnp.int32)
            + jnp.sum(jnp.where(expert_hit, seg[:, None, None, :], 0), axis=-1))
    slots = slot.transpose(0, 2, 1)
    wts = rt[:, :, 4:6].reshape(N_TILES, 1, 2 * TILE_TOKENS)
    wts = jnp.pad(wts, ((0, 0), (0, 0), (0, 8)))
    y_p, y_s = _experts(nblk, seg, cnt, slots, wts, hn_p, hn_s, x1_p, x1_s,
                        w_expert_gate[0], w_expert_up[0], w_expert_down[0])
    y_prompt = y_p.reshape(batch, seq, d)
    y_sample = y_s.reshape(dec_batch, dec_seq, d)

    return (y_prompt, y_sample, new_k_p, new_v_p, new_u_p, new_k_s, new_v_s, new_u_s)
```

```python
import functools

import jax
import jax.numpy as jnp
from jax import lax
from jax.experimental import pallas as pl
from jax.experimental.pallas import tpu as pltpu

F32 = jnp.float32
BF16 = jnp.bfloat16

D_MODEL = 1024
HEAD_DIM = 64
N_Q_HEADS = 16
N_KV_HEADS = 2
WINDOW = 128
ROPE_THETA = 10000.0
POOL_WINDOWS = (2, 4, 8, 16)
POOL_GROUP_WIDTH = D_MODEL // len(POOL_WINDOWS)
POOL_STATE_LEN = max(POOL_WINDOWS) - 1
N_EXPERT_GROUPS = 4
EXPERTS_PER_GROUP = 8
N_EXPERTS = N_EXPERT_GROUPS * EXPERTS_PER_GROUP
D_EXPERT = 512
RMS_EPS = 1e-6
Q_W = N_Q_HEADS * HEAD_DIM
KV_W = N_KV_HEADS * HEAD_DIM
OFF_K = Q_W
OFF_V = OFF_K + KV_W
OFF_U = OFF_V + KV_W
OFF_GA = OFF_U + D_MODEL
OFF_GP = OFF_GA + D_MODEL
IN_W = OFF_GP + D_MODEL

LANES = 128
ROW_BLOCK = 512
EXPERT_LANE0 = 32
VMEM_LIMIT = 56 * 1024 * 1024
EXPERTS_VMEM_LIMIT = 60 * 1024 * 1024
NEG_INF = float("-inf")

N_TILES = 4
TILE_BLOCKS = 8
TILE_PROMPT = TILE_BLOCKS * ROW_BLOCK
TILE_SAMPLE = 128
TILE_TOKENS = TILE_PROMPT + TILE_SAMPLE
MOE_ROWS = 128
DUMMY_ROWS = 8
SCATTER_BATCH = 4
LIST_CAP = 2 * TILE_TOKENS + N_EXPERTS * (MOE_ROWS - 1) + 2 * MOE_ROWS + 1


def _rms(x, g):
    return x * lax.rsqrt(jnp.mean(x * x, axis=-1, keepdims=True) + RMS_EPS) * g


def _bdot(a, b):
    return jnp.dot(a, b, preferred_element_type=F32)


def _inproj_body(x_ref, g_ref, w_ref, cos_ref, sin_ref, gq_ref, gk_ref,
                 q_ref, k_ref, v_ref, kk_ref, vv_ref, u_ref, sga_ref, sgp_ref):
    rows = x_ref.shape[0]
    hb = _rms(x_ref[...], g_ref[...]).astype(BF16)
    cos = cos_ref[...]
    sin = sin_ref[...]
    lane = lax.broadcasted_iota(jnp.int32, (rows, LANES), 1)
    lo = lane < HEAD_DIM
    first_half = (lane % HEAD_DIM) < (HEAD_DIM // 2)

    def head_norm_rope(zc, gain):
        sq = zc * zc
        ss_lo = jnp.sum(jnp.where(lo, sq, 0.0), axis=-1, keepdims=True)
        ss_hi = jnp.sum(jnp.where(lo, 0.0, sq), axis=-1, keepdims=True)
        r = lax.rsqrt(jnp.where(lo, ss_lo, ss_hi) * (1.0 / HEAD_DIM) + RMS_EPS)
        y = zc * r * gain
        partner = jnp.where(first_half, pltpu.roll(y, LANES - HEAD_DIM // 2, 1),
                            pltpu.roll(y, HEAD_DIM // 2, 1))
        return y * cos + partner * sin

    gq = gq_ref[...]
    for j in range(Q_W // 256):
        z = _bdot(hb, w_ref[:, 256 * j:256 * (j + 1)])
        for c in range(2):
            qn = head_norm_rope(z[:, LANES * c:LANES * (c + 1)], gq) * (HEAD_DIM ** -0.5)
            q_ref[:, 256 * j + LANES * c:256 * j + LANES * (c + 1)] = qn.astype(BF16)

    z = _bdot(hb, w_ref[:, OFF_K:OFF_U])
    kn = head_norm_rope(z[:, :KV_W], gk_ref[...])
    vr = z[:, KV_W:]
    k_ref[...] = kn
    v_ref[...] = vr
    kr = pltpu.roll(kn, HEAD_DIM, 1)
    vrr = pltpu.roll(vr, HEAD_DIM, 1)
    kk_ref[:, :LANES] = jnp.where(lo, kn, kr).astype(BF16)
    kk_ref[:, LANES:] = jnp.where(lo, kr, kn).astype(BF16)
    vv_ref[:, :LANES] = jnp.where(lo, vr, vrr).astype(BF16)
    vv_ref[:, LANES:] = jnp.where(lo, vrr, vr).astype(BF16)

    for j in range(D_MODEL // 256):
        u_ref[:, 256 * j:256 * (j + 1)] = _bdot(hb, w_ref[:, OFF_U + 256 * j:OFF_U + 256 * (j + 1)])
        sga_ref[:, 256 * j:256 * (j + 1)] = jax.nn.sigmoid(
            _bdot(hb, w_ref[:, OFF_GA + 256 * j:OFF_GA + 256 * (j + 1)]))
        sgp_ref[:, 256 * j:256 * (j + 1)] = jax.nn.sigmoid(
            _bdot(hb, w_ref[:, OFF_GP + 256 * j:OFF_GP + 256 * (j + 1)]))


def _inproj(x, g, w_in_b, cos, sin, gq, gk):
    n = x.shape[0]
    nb = n // ROW_BLOCK
    ncos = cos.shape[0] // ROW_BLOCK
    row = lambda w: pl.BlockSpec((ROW_BLOCK, w), lambda i: (i, 0))
    full = lambda a: pl.BlockSpec(a.shape, lambda i: (0,) * a.ndim)
    tab = pl.BlockSpec((ROW_BLOCK, LANES), lambda i: (i % ncos, 0))
    out_shapes = (
        jax.ShapeDtypeStruct((n, Q_W), BF16),
        jax.ShapeDtypeStruct((n, KV_W), F32),
        jax.ShapeDtypeStruct((n, KV_W), F32),
        jax.ShapeDtypeStruct((n, 2 * LANES), BF16),
        jax.ShapeDtypeStruct((n, 2 * LANES), BF16),
        jax.ShapeDtypeStruct((n, D_MODEL), F32),
        jax.ShapeDtypeStruct((n, D_MODEL), F32),
        jax.ShapeDtypeStruct((n, D_MODEL), F32),
    )
    return pl.pallas_call(
        _inproj_body,
        grid=(nb,),
        in_specs=[row(D_MODEL), full(g), full(w_in_b), tab, tab, full(gq), full(gk)],
        out_specs=tuple(row(s.shape[1]) for s in out_shapes),
        out_shape=out_shapes,
        compiler_params=pltpu.CompilerParams(dimension_semantics=("arbitrary",),
                                             vmem_limit_bytes=VMEM_LIMIT),
        name="inproj",
    )(x, g, w_in_b, cos, sin, gq, gk)


def _attend(q2, kg, vg, cur_valid, sink_col, prev_dead):
    s = lax.dot_general(q2, kg, (((1,), (1,)), ((), ())), preferred_element_type=F32)
    s_prev = s[:, :WINDOW]
    if prev_dead is not None:
        s_prev = jnp.where(prev_dead, NEG_INF, s_prev)
    sc = jnp.where(cur_valid, s[:, WINDOW:], s_prev)
    m = jnp.maximum(jnp.max(sc, axis=-1, keepdims=True), sink_col)
    p = jnp.exp(sc - m)
    den = jnp.sum(p, axis=-1, keepdims=True) + jnp.exp(sink_col - m)
    pn = p * (1.0 / den)
    p2 = jnp.concatenate([jnp.where(cur_valid, 0.0, pn), jnp.where(cur_valid, pn, 0.0)], axis=1)
    return _bdot(p2.astype(BF16), vg)


POOL_PAD_ROWS = 8


def _mixer_prompt_body(sink_ref, q_ref, kkc_ref, kkp_ref, vvc_ref, vvp_ref, uc_ref, up_ref,
                       attn_ref, pd_ref, ext_ref, s_a, s_b):
    i = pl.program_id(1)
    first = i == 0
    m2 = 2 * WINDOW
    row = lax.broadcasted_iota(jnp.int32, (m2, LANES), 0)
    col = lax.broadcasted_iota(jnp.int32, (m2, LANES), 1)
    cur_valid = (row % WINDOW) >= col
    upper = lax.broadcasted_iota(jnp.int32, (m2, 1), 0) < WINDOW
    lo = lax.broadcasted_iota(jnp.int32, (WINDOW, LANES), 1) < HEAD_DIM

    for j in range(ROW_BLOCK // WINDOW):
        r0 = WINDOW * j
        if j == 0:
            kprev, vprev = kkp_ref[...], vvp_ref[...]
            prev_dead = first
        else:
            kprev, vprev = kkc_ref[r0 - WINDOW:r0, :], vvc_ref[r0 - WINDOW:r0, :]
            prev_dead = None
        kband = jnp.concatenate([kprev, kkc_ref[r0:r0 + WINDOW, :]], axis=0)
        vband = jnp.concatenate([vprev, vvc_ref[r0:r0 + WINDOW, :]], axis=0)
        for g in range(N_KV_HEADS):
            kg = kband[:, LANES * g:LANES * (g + 1)]
            vg = vband[:, LANES * g:LANES * (g + 1)]
            for c in range(4):
                cc = 4 * g + c
                qc = q_ref[r0:r0 + WINDOW, LANES * cc:LANES * (cc + 1)]
                zero = jnp.zeros_like(qc)
                q2 = jnp.concatenate([jnp.where(lo, qc, zero), jnp.where(lo, zero, qc)], axis=0)
                sink_col = jnp.where(upper, sink_ref[2 * cc], sink_ref[2 * cc + 1])
                o = _attend(q2, kg, vg, cur_valid, sink_col, prev_dead)
                attn_ref[r0:r0 + WINDOW, LANES * cc:LANES * (cc + 1)] = jnp.where(
                    lo, o[:WINDOW], o[WINDOW:]).astype(BF16)

    pad, top = POOL_PAD_ROWS, POOL_PAD_ROWS + 16
    end = top + ROW_BLOCK
    for ref in (ext_ref, s_a, s_b):
        ref[0:pad, :] = jnp.zeros((pad, D_MODEL), F32)
    ext_ref[pad:top, :] = jnp.where(first, 0.0, up_ref[...])
    ext_ref[top:, :] = uc_ref[...]
    pos = i * ROW_BLOCK + lax.broadcasted_iota(jnp.int32, (ROW_BLOCK, 1), 0)
    src = ext_ref
    for g, w in enumerate(POOL_WINDOWS):
        c0 = POOL_GROUP_WIDTH * g
        dst = (s_a, s_b)[g % 2]
        shift = w // 2
        dst[pad:end, c0:] = src[pad:end, c0:] + src[pad - shift:end - shift, c0:]
        c1 = c0 + POOL_GROUP_WIDTH
        cnt = jnp.minimum(w, pos + 1).astype(F32)
        pd_ref[:, c0:c1] = (dst[top:end, c0:c1] / cnt - uc_ref[:, c0:c1]).astype(BF16)
        src = dst


def _mixer_prompt(sinks, q, kk, vv, u, batch, seq):
    nb = seq // ROW_BLOCK
    sub = ROW_BLOCK // WINDOW
    cur = lambda w: pl.BlockSpec((ROW_BLOCK, w), lambda b, i: (b * nb + i, 0))
    prev_kv = pl.BlockSpec((WINDOW, 2 * LANES),
                           lambda b, i: (jnp.maximum((b * nb + i) * sub - 1, 0), 0))
    prev_u = pl.BlockSpec((16, D_MODEL),
                          lambda b, i: (jnp.maximum((b * nb + i) * (ROW_BLOCK // 16) - 1, 0), 0))
    n = batch * seq
    return pl.pallas_call(
        _mixer_prompt_body,
        grid=(batch, nb),
        in_specs=[pl.BlockSpec(memory_space=pltpu.SMEM), cur(Q_W), cur(2 * LANES), prev_kv,
                  cur(2 * LANES), prev_kv, cur(D_MODEL), prev_u],
        out_specs=(cur(Q_W), cur(D_MODEL)),
        out_shape=(jax.ShapeDtypeStruct((n, Q_W), BF16), jax.ShapeDtypeStruct((n, D_MODEL), BF16)),
        scratch_shapes=[pltpu.VMEM((POOL_PAD_ROWS + 16 + ROW_BLOCK, D_MODEL), F32)] * 3,
        compiler_params=pltpu.CompilerParams(dimension_semantics=("arbitrary", "arbitrary"),
                                             vmem_limit_bytes=VMEM_LIMIT),
        name="mixer_prompt",
    )(sinks, q, kk, kk, vv, vv, u, u)


SEQ_PAIR_ROWS = 8
SAMPLE_PAIRS_PER_STEP = 8


def _mixer_sample_body(sink_ref, q_ref, k_ref, v_ref, u_ref, ck_ref, cv_ref, st_ref,
                       attn_ref, pd_ref, nk_ref, nv_ref, nu_ref, ext_ref):
    m = SEQ_PAIR_ROWS
    half = m // 2
    hist = POOL_STATE_LEN
    ext_ref[16:, :] = jnp.zeros((ext_ref.shape[0] - 16, D_MODEL), F32)
    row8 = lax.broadcasted_iota(jnp.int32, (m, LANES), 0)
    n_chunks = Q_W // LANES
    mq = n_chunks * m
    row1 = lax.broadcasted_iota(jnp.int32, (m, 1), 0)
    row = lax.broadcasted_iota(jnp.int32, (mq, LANES), 0)
    col = lax.broadcasted_iota(jnp.int32, (mq, LANES), 1)
    cur_valid = (row % half) >= col
    lane8 = lax.broadcasted_iota(jnp.int32, (m, LANES), 1)
    lo8 = lane8 < HEAD_DIM
    top8 = lax.broadcasted_iota(jnp.int32, (m, LANES), 0) < half
    tail = jnp.zeros((WINDOW - m, LANES), F32)
    top = row1 < half
    sink_col = jnp.concatenate([jnp.where(top, sink_ref[2 * cc], sink_ref[2 * cc + 1])
                                for cc in range(n_chunks)], axis=0)

    def pair(p, carry):
        r0 = pl.multiple_of(p * m, m)
        q8 = q_ref[pl.ds(r0, m), :]
        k8 = k_ref[pl.ds(r0, m), :]
        v8 = v_ref[pl.ds(r0, m), :]
        u8 = u_ref[pl.ds(r0, m), :]
        attn_parts = []
        pd_parts = []
        for s in range(2):
            shift = lambda a: a if s == 0 else pltpu.roll(a, half, 0)
            seq = 2 * p + s
            ck, cv = ck_ref[seq], cv_ref[seq]
            ks, vs = shift(k8), shift(v8)
            kb = jnp.concatenate([ck, ks, tail], axis=0)
            vb = jnp.concatenate([cv, vs, tail], axis=0)
            for cache, new, out in ((ck, ks, nk_ref), (cv, vs, nv_ref)):
                up = pltpu.roll(cache, WINDOW - half, 0)
                out[seq, 0:WINDOW - m, :] = up[0:WINDOW - m]
                out[seq, WINDOW - m:WINDOW, :] = jnp.where(row8 < half, up[WINDOW - m:],
                                                           pltpu.roll(new, half, 0))
            qs = shift(q8)
            qr = pltpu.roll(qs, half, 0)
            q2 = []
            for cc in range(n_chunks):
                qa = qs[:, LANES * cc:LANES * (cc + 1)]
                qb = qr[:, LANES * cc:LANES * (cc + 1)]
                if cc < n_chunks // N_KV_HEADS:
                    first = jnp.where(lo8, qa, 0.0)
                    second = jnp.where(lo8, pltpu.roll(qb, HEAD_DIM, 1), 0.0)
                else:
                    first = jnp.where(lo8, 0.0, pltpu.roll(qa, HEAD_DIM, 1))
                    second = jnp.where(lo8, 0.0, qb)
                q2.append(jnp.where(top8, first, second))
            q2 = jnp.concatenate(q2, axis=0).astype(BF16)
            o = _attend(q2, kb.astype(BF16), vb.astype(BF16), cur_valid, sink_col, None)
            chunks = []
            for cc in range(n_chunks):
                oc = o[m * cc:m * (cc + 1)]
                if cc < n_chunks // N_KV_HEADS:
                    chunks.append(jnp.where(lo8, oc, pltpu.roll(pltpu.roll(oc, half, 0), HEAD_DIM, 1)))
                else:
                    chunks.append(jnp.where(lo8, pltpu.roll(oc, HEAD_DIM, 1), pltpu.roll(oc, half, 0)))
            attn_parts.append(jnp.concatenate(chunks, axis=1))

            ext_ref[0:hist, :] = st_ref[seq]
            ext_ref[hist:hist + half, :] = shift(u8)[0:half]
            nu_ref[seq] = ext_ref[half:hist + half, :]
            cols = []
            for g, w in enumerate(POOL_WINDOWS):
                c0, c1 = POOL_GROUP_WIDTH * g, POOL_GROUP_WIDTH * (g + 1)
                acc = ext_ref[hist:hist + m, c0:c1]
                for k in range(1, w):
                    acc = acc + ext_ref[hist - k:hist - k + m, c0:c1]
                cols.append(acc / float(w) - ext_ref[hist:hist + m, c0:c1])
            pd_parts.append(jnp.concatenate(cols, axis=1))

        attn_ref[pl.ds(r0, m), :] = jnp.where(top, attn_parts[0], pltpu.roll(attn_parts[1], half, 0))
        pd_ref[pl.ds(r0, m), :] = jnp.where(top, pd_parts[0], pltpu.roll(pd_parts[1], half, 0))
        return carry

    lax.fori_loop(0, SAMPLE_PAIRS_PER_STEP, pair, 0)


def _mixer_sample(sinks, q, k, v, u, cache_k, cache_v, state):
    n = q.shape[0]
    m = SEQ_PAIR_ROWS * SAMPLE_PAIRS_PER_STEP
    rows = lambda w: pl.BlockSpec((m, w), lambda i: (i, 0))
    seqs = lambda a: pl.BlockSpec((2 * SAMPLE_PAIRS_PER_STEP,) + a.shape[1:], lambda i: (i, 0, 0))
    like = lambda a: jax.ShapeDtypeStruct(a.shape, F32)
    return pl.pallas_call(
        _mixer_sample_body,
        grid=(n // m,),
        in_specs=[pl.BlockSpec(memory_space=pltpu.SMEM), rows(Q_W), rows(KV_W), rows(KV_W),
                  rows(D_MODEL), seqs(cache_k), seqs(cache_v), seqs(state)],
        out_specs=(rows(Q_W), rows(D_MODEL), seqs(cache_k), seqs(cache_v), seqs(state)),
        out_shape=(jax.ShapeDtypeStruct((n, Q_W), F32), jax.ShapeDtypeStruct((n, D_MODEL), F32),
                   like(cache_k), like(cache_v), like(state)),
        scratch_shapes=[pltpu.VMEM((16 + SEQ_PAIR_ROWS, D_MODEL), F32)],
        compiler_params=pltpu.CompilerParams(dimension_semantics=("arbitrary",),
                                             vmem_limit_bytes=VMEM_LIMIT),
        name="mixer_sample",
    )(sinks, q, k, v, u, cache_k, cache_v, state)


def _dense_body(x_ref, attn_ref, pd_ref, sga_ref, sgp_ref, mix_ref, ps_ref, wa_ref, wp_ref,
                wo_ref, gf_ref, wrh_ref, wrl_ref, br_ref, ltri_ref, ssel_ref, crow_ref,
                x1_ref, hn_ref, rt_ref, cnt_ref, seg_ref, carry_ref):
    rows = x_ref.shape[0]
    step = pl.program_id(0)

    @pl.when(step % TILE_BLOCKS == 0)
    def _():
        carry_ref[...] = jnp.zeros_like(carry_ref)

    pd = pd_ref[...].astype(BF16)
    pooled = []
    for g in range(len(POOL_WINDOWS)):
        c0, c1 = POOL_GROUP_WIDTH * g, POOL_GROUP_WIDTH * (g + 1)
        pooled.append((_bdot(pd[:, c0:c1], mix_ref[g]) * ps_ref[:, c0:c1]).astype(BF16))
    pooled = jnp.concatenate(pooled, axis=1)
    merged = (sga_ref[...] * _bdot(attn_ref[...].astype(BF16), wa_ref[...])
              + sgp_ref[...] * _bdot(pooled, wp_ref[...]))
    x1 = x_ref[...] + _bdot(merged.astype(BF16), wo_ref[...])
    x1_ref[...] = x1
    hn = _rms(x1, gf_ref[...])
    hn_ref[...] = hn
    hi = hn.astype(BF16)
    lo = (hn - hi.astype(F32)).astype(BF16)
    logits = (_bdot(hi, wrh_ref[...]) + _bdot(hi, wrl_ref[...]) + _bdot(lo, wrh_ref[...])
              + br_ref[...])

    lane = lax.broadcasted_iota(jnp.int32, (rows, LANES), 1)
    big = jnp.int32(LANES)
    gl = jnp.where(lane < N_EXPERT_GROUPS, logits, NEG_INF)
    gmax = jnp.max(gl, axis=-1, keepdims=True)
    gidx = jnp.min(jnp.where(gl == gmax, lane, big), axis=-1, keepdims=True)
    g_w = 1.0 / jnp.sum(jnp.exp(gl - gmax), axis=-1, keepdims=True)
    e0 = EXPERT_LANE0 + gidx * EXPERTS_PER_GROUP
    el = jnp.where((lane >= e0) & (lane < e0 + EXPERTS_PER_GROUP), logits, NEG_INF)
    l1 = jnp.max(el, axis=-1, keepdims=True)
    i1 = jnp.min(jnp.where(el == l1, lane, big), axis=-1, keepdims=True)
    el2 = jnp.where(lane == i1, NEG_INF, el)
    l2 = jnp.max(el2, axis=-1, keepdims=True)
    i2 = jnp.min(jnp.where(el2 == l2, lane, big), axis=-1, keepdims=True)
    e = jnp.exp(l2 - l1)
    w1 = 1.0 / (1.0 + e)
    w2 = e * w1

    sel = (lane == i1) | (lane == i2)
    onehot = jnp.where(sel, 1.0, 0.0).astype(BF16)
    rank = _bdot(ltri_ref[...], onehot) + carry_ref[0:1, :] + crow_ref[...]
    segsum = _bdot(ssel_ref[...], onehot)
    seg_ref[0] = segsum
    carry_ref[...] = carry_ref[...] + segsum[0:1, :]
    cnt_ref[0] = carry_ref[...]
    r1 = jnp.sum(jnp.where(lane == i1, rank, 0.0), axis=-1, keepdims=True)
    r2 = jnp.sum(jnp.where(lane == i2, rank, 0.0), axis=-1, keepdims=True)
    cols = ((i1 - EXPERT_LANE0).astype(F32), (i2 - EXPERT_LANE0).astype(F32), r1, r2,
            g_w * w1, g_w * w2)
    tile = jnp.zeros((rows, LANES), F32)
    for c, val in enumerate(cols):
        tile = jnp.where(lane == c, val, tile)
    rt_ref[...] = tile


def _dense(x, attn, pd, sga, sgp, mix_b, pool_scale, wa_b, wp_b, wo_b, gf, wr_hi, wr_lo, br,
           ltri, ssel, crow):
    n = x.shape[0]
    nb = n // ROW_BLOCK
    row = lambda w: pl.BlockSpec((ROW_BLOCK, w), lambda i: (i, 0))
    full = lambda a: pl.BlockSpec(a.shape, lambda i: (0,) * a.ndim)
    stat = pl.BlockSpec((1, 8, LANES), lambda i: (i, 0, 0))
    consts = (mix_b, pool_scale, wa_b, wp_b, wo_b, gf, wr_hi, wr_lo, br, ltri, ssel, crow)
    return pl.pallas_call(
        _dense_body,
        grid=(nb,),
        in_specs=[row(D_MODEL)] * 5 + [full(a) for a in consts],
        out_specs=(row(D_MODEL), row(D_MODEL), row(LANES), stat, stat),
        out_shape=(jax.ShapeDtypeStruct((n, D_MODEL), F32),
                   jax.ShapeDtypeStruct((n, D_MODEL), F32),
                   jax.ShapeDtypeStruct((n, LANES), F32),
                   jax.ShapeDtypeStruct((nb, 8, LANES), F32),
                   jax.ShapeDtypeStruct((nb, 8, LANES), F32)),
        scratch_shapes=[pltpu.VMEM((8, LANES), F32)],
        compiler_params=pltpu.CompilerParams(dimension_semantics=("arbitrary",),
                                             vmem_limit_bytes=VMEM_LIMIT),
        name="dense",
    )(x, attn, pd, sga, sgp, *consts)


def _experts_body(nblk_ref, seg_ref, cnt_ref, slots_ref, wts_ref, hn_p, hn_s, x1_p, x1_s,
                  wg_ref, wu_ref, wd_ref, y_p, y_s,
                  hn_t, acc, xg0, xg1, yb0, yb1, wgb, wub, wdb, tok, sem_in, sem_out):
    tau = pl.program_id(0)
    e = pl.program_id(1)
    t_rows = TILE_TOKENS
    dummy = 2 * t_rows
    p_rows = pl.ds(pl.multiple_of(tau * TILE_PROMPT, TILE_PROMPT), TILE_PROMPT)
    s_rows = pl.ds(pl.multiple_of(tau * TILE_SAMPLE, TILE_SAMPLE), TILE_SAMPLE)
    tile_p = pl.ds(0, TILE_PROMPT)
    tile_s = pl.ds(TILE_PROMPT, TILE_SAMPLE)

    def in_copies():
        return [pltpu.make_async_copy(hn_p.at[p_rows], hn_t.at[tile_p], sem_in.at[0]),
                pltpu.make_async_copy(hn_s.at[s_rows], hn_t.at[tile_s], sem_in.at[1]),
                pltpu.make_async_copy(x1_p.at[p_rows], acc.at[tile_p], sem_in.at[2]),
                pltpu.make_async_copy(x1_s.at[s_rows], acc.at[tile_s], sem_in.at[3])]

    def out_copies():
        return [pltpu.make_async_copy(acc.at[tile_p], y_p.at[p_rows], sem_out.at[0]),
                pltpu.make_async_copy(acc.at[tile_s], y_s.at[s_rows], sem_out.at[1])]

    xbufs = (xg0, xg1)
    ybufs = (yb0, yb1)
    end_slot = seg_ref[tau, N_EXPERTS - 1] + nblk_ref[tau, N_EXPERTS - 1] * MOE_ROWS

    def gather_block(b, xdst):
        base = b * MOE_ROWS
        for j in range(MOE_ROWS):
            t = lax.shift_right_logical(tok[base + j], 1)
            xdst[j // 8, pl.ds(j % 8, 1), :] = hn_t[pl.ds(t, 1), :]

    def scatter_block(b, ysrc):
        base = b * MOE_ROWS
        for j0 in range(0, MOE_ROWS, SCATTER_BATCH):
            ents = [tok[base + j0 + i] for i in range(SCATTER_BATCH)]
            rows = [lax.shift_right_logical(en, 1) for en in ents]
            vals = [acc[pl.ds(rows[i], 1), :]
                    + wts_ref[0, 0, ents[i]] * ysrc[(j0 + i) // 8, pl.ds((j0 + i) % 8, 1), :]
                    for i in range(SCATTER_BATCH)]
            for i in range(SCATTER_BATCH):
                acc[pl.ds(rows[i], 1), :] = vals[i]

    @pl.when(e == 0)
    def _load_tile():
        for cp in in_copies():
            cp.start()

        hn_t[t_rows:, :] = jnp.zeros((DUMMY_ROWS, D_MODEL), F32)
        yb0[...] = jnp.zeros_like(yb0)

        def pad_range(lo, hi):
            def fill(p, c):
                tok[p] = dummy
                return c
            lax.fori_loop(lo, hi, fill, 0)
        pad_range(0, MOE_ROWS)
        pad_range(end_slot, end_slot + MOE_ROWS)

        def pad_fill(ei, c):
            s0 = seg_ref[tau, ei]
            pad_range(s0 + cnt_ref[tau, ei], s0 + nblk_ref[tau, ei] * MOE_ROWS)
            return c
        lax.fori_loop(0, N_EXPERTS, pad_fill, 0)

        def invert(g, c):
            t0 = g * 8
            for i in range(8):
                for k in range(2):
                    tok[slots_ref[0, k, t0 + i]] = 2 * (t0 + i) + k
            return c
        lax.fori_loop(0, t_rows // 8, invert, 0)

        for cp in in_copies():
            cp.wait()
        acc[t_rows:, :] = jnp.zeros((DUMMY_ROWS, D_MODEL), F32)
        gather_block(1, xbufs[1])

    wgb[...] = wg_ref[0].astype(BF16)
    wub[...] = wu_ref[0].astype(BF16)
    wdb[...] = wd_ref[0].astype(BF16)

    first_block = lax.shift_right_logical(seg_ref[tau, e], MOE_ROWS.bit_length() - 1)

    def block(jb, c):
        b = first_block + jb
        for par in range(2):
            @pl.when((b & 1) == par)
            def _():
                gather_block(b + 1, xbufs[1 - par])
                xb = xbufs[par][...].reshape(MOE_ROWS, D_MODEL).astype(BF16)
                act = (jax.nn.silu(_bdot(xb, wgb[...])) * _bdot(xb, wub[...])).astype(BF16)
                ybufs[par][...] = _bdot(act, wdb[...]).reshape(MOE_ROWS // 8, 8, D_MODEL)
                scatter_block(b - 1, ybufs[1 - par])
        return c
    lax.fori_loop(0, nblk_ref[tau, e], block, 0)

    @pl.when(e == N_EXPERTS - 1)
    def _store_tile():
        last = lax.shift_right_logical(end_slot, MOE_ROWS.bit_length() - 1) - 1
        for par in range(2):
            @pl.when((last & 1) == par)
            def _():
                scatter_block(last, ybufs[par])
        cps = out_copies()
        for cp in cps:
            cp.start()
        for cp in cps:
            cp.wait()


def _experts(nblk, seg, cnt, slots, wts, hn_p, hn_s, x1_p, x1_s, w_gate, w_up, w_down):
    any_spec = pl.BlockSpec(memory_space=pl.ANY)
    wspec = lambda a: pl.BlockSpec((1,) + a.shape[1:], lambda t, e, *_: (e, 0, 0))
    smem = lambda a: pl.BlockSpec((1,) + a.shape[1:], lambda t, e, *_: (t, 0, 0),
                                  memory_space=pltpu.SMEM)
    grid_spec = pltpu.PrefetchScalarGridSpec(
        num_scalar_prefetch=3,
        grid=(N_TILES, N_EXPERTS),
        in_specs=[smem(slots), smem(wts), any_spec, any_spec, any_spec, any_spec,
                  wspec(w_gate), wspec(w_up), wspec(w_down)],
        out_specs=(any_spec, any_spec),
        scratch_shapes=[
            pltpu.VMEM((TILE_TOKENS + DUMMY_ROWS, D_MODEL), F32),
            pltpu.VMEM((TILE_TOKENS + DUMMY_ROWS, D_MODEL), F32),
            pltpu.VMEM((MOE_ROWS // 8, 8, D_MODEL), F32),
            pltpu.VMEM((MOE_ROWS // 8, 8, D_MODEL), F32),
            pltpu.VMEM((MOE_ROWS // 8, 8, D_MODEL), F32),
            pltpu.VMEM((MOE_ROWS // 8, 8, D_MODEL), F32),
            pltpu.VMEM((D_MODEL, D_EXPERT), BF16),
            pltpu.VMEM((D_MODEL, D_EXPERT), BF16),
            pltpu.VMEM((D_EXPERT, D_MODEL), BF16),
            pltpu.SMEM((LIST_CAP,), jnp.int32),
            pltpu.SemaphoreType.DMA((4,)),
            pltpu.SemaphoreType.DMA((2,)),
        ],
    )
    return pl.pallas_call(
        _experts_body,
        grid_spec=grid_spec,
        out_shape=(jax.ShapeDtypeStruct(x1_p.shape, F32), jax.ShapeDtypeStruct(x1_s.shape, F32)),
        compiler_params=pltpu.CompilerParams(dimension_semantics=("arbitrary", "arbitrary"),
                                             vmem_limit_bytes=EXPERTS_VMEM_LIMIT),
        name="experts",
    )(nblk, seg, cnt, slots, wts, hn_p, hn_s, x1_p, x1_s, w_gate, w_up, w_down)


def _rope_tables(pos):
    half = HEAD_DIM // 2
    inv = ROPE_THETA ** (-jnp.arange(half, dtype=F32) * (2.0 / HEAD_DIM))
    ang = pos.astype(F32)[:, None] * inv[None, :]
    cos = jnp.tile(jnp.cos(ang), (1, LANES // half))
    sin = jnp.sin(ang)
    sin = jnp.tile(jnp.concatenate([-sin, sin], axis=1), (1, LANES // HEAD_DIM))
    return cos, sin


def _split_bf16(w):
    hi = w.astype(BF16)
    return hi, (w - hi.astype(F32)).astype(BF16)


def kernel(x_prompt, x_sample, cache_k, cache_v, state_pool, norm_mix_g, w_in, q_norm_g, k_norm_g,
           attn_sinks, w_attn_branch, pool_mix_w, pool_scale, w_pool_branch, w_out, norm_ffn_g,
           w_route_group, b_route_group, w_route_expert, b_route_expert, w_expert_gate,
           w_expert_up, w_expert_down):
    batch, seq, d = x_prompt.shape
    dec_batch, dec_seq, _ = x_sample.shape
    past_len = 16384
    assert x_prompt.shape == (4, 4096, D_MODEL) and x_sample.shape == (128, 4, D_MODEL)
    assert w_in.shape[0] == 1, "single layer"
    n_p, n_s = batch * seq, dec_batch * dec_seq

    g_mix = norm_mix_g[0][None, :]
    w_in_b = w_in[0].astype(BF16)
    gq = jnp.tile(q_norm_g[0], LANES // HEAD_DIM)[None, :]
    gk = jnp.tile(k_norm_g[0], LANES // HEAD_DIM)[None, :]
    sinks = attn_sinks[0]
    mix_b = pool_mix_w[0].astype(BF16)
    ps = pool_scale[0][None, :]
    wa_b = w_attn_branch[0].astype(BF16)
    wp_b = w_pool_branch[0].astype(BF16)
    wo_b = w_out[0].astype(BF16)
    gf = norm_ffn_g[0][None, :]
    wr = jnp.zeros((D_MODEL, LANES), F32)
    wr = wr.at[:, :N_EXPERT_GROUPS].set(w_route_group[0])
    wr = wr.at[:, EXPERT_LANE0:EXPERT_LANE0 + N_EXPERTS].set(w_route_expert[0])
    wr_hi, wr_lo = _split_bf16(wr)
    br = jnp.zeros((1, LANES), F32)
    br = br.at[0, :N_EXPERT_GROUPS].set(b_route_group[0])
    br = br.at[0, EXPERT_LANE0:EXPERT_LANE0 + N_EXPERTS].set(b_route_expert[0])

    cos_p, sin_p = _rope_tables(jnp.arange(seq, dtype=jnp.int32))
    pos_s = past_len + (jnp.arange(n_s, dtype=jnp.int32) % dec_seq)
    cos_s, sin_s = _rope_tables(pos_s)

    dense_consts = (mix_b, ps, wa_b, wp_b, wo_b, gf, wr_hi, wr_lo, br)
    assert n_s == ROW_BLOCK == N_TILES * TILE_SAMPLE and n_p == N_TILES * TILE_PROMPT
    ridx = jnp.arange(ROW_BLOCK, dtype=jnp.int32)
    lower = ridx[:, None] > ridx[None, :]
    seg_of = ridx // TILE_SAMPLE
    ltri_p = lower.astype(BF16)
    ltri_s = (lower & (seg_of[:, None] == seg_of[None, :])).astype(BF16)
    ssel_p = (jnp.arange(8, dtype=jnp.int32)[:, None] == 0) & (ridx[None, :] >= 0)
    ssel_s = jnp.arange(8, dtype=jnp.int32)[:, None] == seg_of[None, :]

    xp = x_prompt.reshape(n_p, d)
    q, k, v, kk, vv, u, sga, sgp = _inproj(xp, g_mix, w_in_b, cos_p, sin_p, gq, gk)
    attn, pd = _mixer_prompt(sinks, q, kk, vv, u, batch, seq)
    x1_p, hn_p, rt_p, cnt_p, _ = _dense(xp, attn, pd, sga, sgp, *dense_consts, ltri_p,
                                        ssel_p.astype(BF16), jnp.zeros((ROW_BLOCK, LANES), F32))
    new_k_p = k.reshape(batch, seq, N_KV_HEADS, HEAD_DIM)[:, -WINDOW:][None]
    new_v_p = v.reshape(batch, seq, N_KV_HEADS, HEAD_DIM)[:, -WINDOW:][None]
    new_u_p = u.reshape(batch, seq, d)[:, -POOL_STATE_LEN:][None]

    xs = x_sample.reshape(n_s, d)
    q, k, v, kk, vv, u, sga, sgp = _inproj(xs, g_mix, w_in_b, cos_s, sin_s, gq, gk)
    ck = cache_k[0].reshape(dec_batch, WINDOW, KV_W)
    cv = cache_v[0].reshape(dec_batch, WINDOW, KV_W)
    attn, pd, nk_s, nv_s, nu_s = _mixer_sample(sinks, q.astype(F32), k, v, u, ck, cv, state_pool[0])
    cnt_tiles_p = cnt_p[TILE_BLOCKS - 1::TILE_BLOCKS, 0, :]
    crow = jnp.repeat(cnt_tiles_p, TILE_SAMPLE, axis=0)
    x1_s, hn_s, rt_s, _, seg_s = _dense(xs, attn, pd, sga, sgp, *dense_consts, ltri_s,
                                        ssel_s.astype(BF16), crow)
    new_k_s = nk_s.reshape(1, dec_batch, WINDOW, N_KV_HEADS, HEAD_DIM)
    new_v_s = nv_s.reshape(1, dec_batch, WINDOW, N_KV_HEADS, HEAD_DIM)
    new_u_s = nu_s[None]

    ex = slice(EXPERT_LANE0, EXPERT_LANE0 + N_EXPERTS)
    cnt = (cnt_tiles_p[:, ex] + seg_s[0, :N_TILES, ex]).astype(jnp.int32)
    nblk = (cnt + (MOE_ROWS - 1)) // MOE_ROWS
    seg = MOE_ROWS * (1 + jnp.cumsum(nblk, axis=1) - nblk)
    rt = jnp.concatenate([rt_p[:, :6].reshape(N_TILES, TILE_PROMPT, 6),
                          rt_s[:, :6].reshape(N_TILES, TILE_SAMPLE, 6)], axis=1)
    expert_hit = rt[:, :, 0:2].astype(jnp.int32)[..., None] == jnp.arange(N_EXPERTS, dtype=jnp.int32)
    slot = (rt[:, :, 2:4].astype(jnp.int32)
            + jnp.sum(jnp.where(expert_hit, seg[:, None, None, :], 0), axis=-1))
    slots = slot.transpose(0, 2, 1)
    wts = rt[:, :, 4:6].reshape(N_TILES, 1, 2 * TILE_TOKENS)
    wts = jnp.pad(wts, ((0, 0), (0, 0), (0, 8)))
    y_p, y_s = _experts(nblk, seg, cnt, slots, wts, hn_p, hn_s, x1_p, x1_s,
                        w_expert_gate[0], w_expert_up[0], w_expert_down[0])
    y_prompt = y_p.reshape(batch, seq, d)
    y_sample = y_s.reshape(dec_batch, dec_seq, d)

    return (y_prompt, y_sample, new_k_p, new_v_p, new_u_p, new_k_s, new_v_s, new_u_s)
```

```python
import functools

import jax
import jax.numpy as jnp
from jax import lax
from jax.experimental import pallas as pl
from jax.experimental.pallas import tpu as pltpu

F32 = jnp.float32
BF16 = jnp.bfloat16

D_MODEL = 1024
HEAD_DIM = 64
N_Q_HEADS = 16
N_KV_HEADS = 2
WINDOW = 128
ROPE_THETA = 10000.0
POOL_WINDOWS = (2, 4, 8, 16)
POOL_GROUP_WIDTH = D_MODEL // len(POOL_WINDOWS)
POOL_STATE_LEN = max(POOL_WINDOWS) - 1
N_EXPERT_GROUPS = 4
EXPERTS_PER_GROUP = 8
N_EXPERTS = N_EXPERT_GROUPS * EXPERTS_PER_GROUP
D_EXPERT = 512
RMS_EPS = 1e-6
Q_W = N_Q_HEADS * HEAD_DIM
KV_W = N_KV_HEADS * HEAD_DIM
OFF_K = Q_W
OFF_V = OFF_K + KV_W
OFF_U = OFF_V + KV_W
OFF_GA = OFF_U + D_MODEL
OFF_GP = OFF_GA + D_MODEL
IN_W = OFF_GP + D_MODEL

LANES = 128
ROW_BLOCK = 512
EXPERT_LANE0 = 32
VMEM_LIMIT = 56 * 1024 * 1024
EXPERTS_VMEM_LIMIT = 60 * 1024 * 1024
NEG_INF = float("-inf")

N_TILES = 4
TILE_BLOCKS = 8
TILE_PROMPT = TILE_BLOCKS * ROW_BLOCK
TILE_SAMPLE = 128
TILE_TOKENS = TILE_PROMPT + TILE_SAMPLE
MOE_ROWS = 128
DUMMY_ROWS = 8
SCATTER_BATCH = 4
LIST_CAP = 2 * TILE_TOKENS + N_EXPERTS * (MOE_ROWS - 1) + 2 * MOE_ROWS + 1


def _rms(x, g):
    return x * lax.rsqrt(jnp.mean(x * x, axis=-1, keepdims=True) + RMS_EPS) * g


def _bdot(a, b):
    return jnp.dot(a, b, preferred_element_type=F32)


def _inproj_body(x_ref, g_ref, w_ref, cos_ref, sin_ref, gq_ref, gk_ref,
                 q_ref, k_ref, v_ref, kk_ref, vv_ref, u_ref, sga_ref, sgp_ref):
    rows = x_ref.shape[0]
    hb = _rms(x_ref[...], g_ref[...]).astype(BF16)
    cos = cos_ref[...]
    sin = sin_ref[...]
    lane = lax.broadcasted_iota(jnp.int32, (rows, LANES), 1)
    lo = lane < HEAD_DIM
    first_half = (lane % HEAD_DIM) < (HEAD_DIM // 2)

    def head_norm_rope(zc, gain):
        sq = zc * zc
        ss_lo = jnp.sum(jnp.where(lo, sq, 0.0), axis=-1, keepdims=True)
        ss_hi = jnp.sum(jnp.where(lo, 0.0, sq), axis=-1, keepdims=True)
        r = lax.rsqrt(jnp.where(lo, ss_lo, ss_hi) * (1.0 / HEAD_DIM) + RMS_EPS)
        y = zc * r * gain
        partner = jnp.where(first_half, pltpu.roll(y, LANES - HEAD_DIM // 2, 1),
                            pltpu.roll(y, HEAD_DIM // 2, 1))
        return y * cos + partner * sin

    gq = gq_ref[...]
    for j in range(Q_W // 256):
        z = _bdot(hb, w_ref[:, 256 * j:256 * (j + 1)])
        for c in range(2):
            qn = head_norm_rope(z[:, LANES * c:LANES * (c + 1)], gq) * (HEAD_DIM ** -0.5)
            q_ref[:, 256 * j + LANES * c:256 * j + LANES * (c + 1)] = qn.astype(BF16)

    z = _bdot(hb, w_ref[:, OFF_K:OFF_U])
    kn = head_norm_rope(z[:, :KV_W], gk_ref[...])
    vr = z[:, KV_W:]
    k_ref[...] = kn
    v_ref[...] = vr
    kr = pltpu.roll(kn, HEAD_DIM, 1)
    vrr = pltpu.roll(vr, HEAD_DIM, 1)
    kk_ref[:, :LANES] = jnp.where(lo, kn, kr).astype(BF16)
    kk_ref[:, LANES:] = jnp.where(lo, kr, kn).astype(BF16)
    vv_ref[:, :LANES] = jnp.where(lo, vr, vrr).astype(BF16)
    vv_ref[:, LANES:] = jnp.where(lo, vrr, vr).astype(BF16)

    for j in range(D_MODEL // 256):
        u_ref[:, 256 * j:256 * (j + 1)] = _bdot(hb, w_ref[:, OFF_U + 256 * j:OFF_U + 256 * (j + 1)])
        sga_ref[:, 256 * j:256 * (j + 1)] = jax.nn.sigmoid(
            _bdot(hb, w_ref[:, OFF_GA + 256 * j:OFF_GA + 256 * (j + 1)]))
        sgp_ref[:, 256 * j:256 * (j + 1)] = jax.nn.sigmoid(
            _bdot(hb, w_ref[:, OFF_GP + 256 * j:OFF_GP + 256 * (j + 1)]))


def _inproj(x, g, w_in_b, cos, sin, gq, gk):
    n = x.shape[0]
    nb = n // ROW_BLOCK
    ncos = cos.shape[0] // ROW_BLOCK
    row = lambda w: pl.BlockSpec((ROW_BLOCK, w), lambda i: (i, 0))
    full = lambda a: pl.BlockSpec(a.shape, lambda i: (0,) * a.ndim)
    tab = pl.BlockSpec((ROW_BLOCK, LANES), lambda i: (i % ncos, 0))
    out_shapes = (
        jax.ShapeDtypeStruct((n, Q_W), BF16),
        jax.ShapeDtypeStruct((n, KV_W), F32),
        jax.ShapeDtypeStruct((n, KV_W), F32),
        jax.ShapeDtypeStruct((n, 2 * LANES), BF16),
        jax.ShapeDtypeStruct((n, 2 * LANES), BF16),
        jax.ShapeDtypeStruct((n, D_MODEL), F32),
        jax.ShapeDtypeStruct((n, D_MODEL), F32),
        jax.ShapeDtypeStruct((n, D_MODEL), F32),
    )
    return pl.pallas_call(
        _inproj_body,
        grid=(nb,),
        in_specs=[row(D_MODEL), full(g), full(w_in_b), tab, tab, full(gq), full(gk)],
        out_specs=tuple(row(s.shape[1]) for s in out_shapes),
        out_shape=out_shapes,
        compiler_params=pltpu.CompilerParams(dimension_semantics=("arbitrary",),
                                             vmem_limit_bytes=VMEM_LIMIT),
        name="inproj",
    )(x, g, w_in_b, cos, sin, gq, gk)


def _attend(q2, kg, vg, cur_valid, sink_col, prev_dead):
    s = lax.dot_general(q2, kg, (((1,), (1,)), ((), ())), preferred_element_type=F32)
    s_prev = s[:, :WINDOW]
    if prev_dead is not None:
        s_prev = jnp.where(prev_dead, NEG_INF, s_prev)
    sc = jnp.where(cur_valid, s[:, WINDOW:], s_prev)
    m = jnp.maximum(jnp.max(sc, axis=-1, keepdims=True), sink_col)
    p = jnp.exp(sc - m)
    den = jnp.sum(p, axis=-1, keepdims=True) + jnp.exp(sink_col - m)
    pn = p * (1.0 / den)
    p2 = jnp.concatenate([jnp.where(cur_valid, 0.0, pn), jnp.where(cur_valid, pn, 0.0)], axis=1)
    return _bdot(p2.astype(BF16), vg)


POOL_PAD_ROWS = 8


def _mixer_prompt_body(sink_ref, q_ref, kkc_ref, kkp_ref, vvc_ref, vvp_ref, uc_ref, up_ref,
                       wg_ref, wu_ref, wd_ref,
                       attn_ref, pd_ref, wgb_ref, wub_ref, wdb_ref, ext_ref, s_a, s_b):
    wgb_ref[...] = wg_ref[...].astype(BF16)
    wub_ref[...] = wu_ref[...].astype(BF16)
    wdb_ref[...] = wd_ref[...].astype(BF16)
    i = pl.program_id(1)
    first = i == 0
    m2 = 2 * WINDOW
    row = lax.broadcasted_iota(jnp.int32, (m2, LANES), 0)
    col = lax.broadcasted_iota(jnp.int32, (m2, LANES), 1)
    cur_valid = (row % WINDOW) >= col
    upper = lax.broadcasted_iota(jnp.int32, (m2, 1), 0) < WINDOW
    lo = lax.broadcasted_iota(jnp.int32, (WINDOW, LANES), 1) < HEAD_DIM

    for j in range(ROW_BLOCK // WINDOW):
        r0 = WINDOW * j
        if j == 0:
            kprev, vprev = kkp_ref[...], vvp_ref[...]
            prev_dead = first
        else:
            kprev, vprev = kkc_ref[r0 - WINDOW:r0, :], vvc_ref[r0 - WINDOW:r0, :]
            prev_dead = None
        kband = jnp.concatenate([kprev, kkc_ref[r0:r0 + WINDOW, :]], axis=0)
        vband = jnp.concatenate([vprev, vvc_ref[r0:r0 + WINDOW, :]], axis=0)
        for g in range(N_KV_HEADS):
            kg = kband[:, LANES * g:LANES * (g + 1)]
            vg = vband[:, LANES * g:LANES * (g + 1)]
            for c in range(4):
                cc = 4 * g + c
                qc = q_ref[r0:r0 + WINDOW, LANES * cc:LANES * (cc + 1)]
                zero = jnp.zeros_like(qc)
                q2 = jnp.concatenate([jnp.where(lo, qc, zero), jnp.where(lo, zero, qc)], axis=0)
                sink_col = jnp.where(upper, sink_ref[2 * cc], sink_ref[2 * cc + 1])
                o = _attend(q2, kg, vg, cur_valid, sink_col, prev_dead)
                attn_ref[r0:r0 + WINDOW, LANES * cc:LANES * (cc + 1)] = jnp.where(
                    lo, o[:WINDOW], o[WINDOW:]).astype(BF16)

    pad, top = POOL_PAD_ROWS, POOL_PAD_ROWS + 16
    end = top + ROW_BLOCK
    for ref in (ext_ref, s_a, s_b):
        ref[0:pad, :] = jnp.zeros((pad, D_MODEL), F32)
    ext_ref[pad:top, :] = jnp.where(first, 0.0, up_ref[...])
    ext_ref[top:, :] = uc_ref[...]
    pos = i * ROW_BLOCK + lax.broadcasted_iota(jnp.int32, (ROW_BLOCK, 1), 0)
    src = ext_ref
    for g, w in enumerate(POOL_WINDOWS):
        c0 = POOL_GROUP_WIDTH * g
        dst = (s_a, s_b)[g % 2]
        shift = w // 2
        dst[pad:end, c0:] = src[pad:end, c0:] + src[pad - shift:end - shift, c0:]
        c1 = c0 + POOL_GROUP_WIDTH
        cnt = jnp.minimum(w, pos + 1).astype(F32)
        pd_ref[:, c0:c1] = (dst[top:end, c0:c1] / cnt - uc_ref[:, c0:c1]).astype(BF16)
        src = dst


def _mixer_prompt(sinks, q, kk, vv, u, w_gate, w_up, w_down, batch, seq):
    nb = seq // ROW_BLOCK
    assert batch * nb == N_EXPERTS, "one expert's weights are cast per grid step"
    sub = ROW_BLOCK // WINDOW
    cur = lambda w: pl.BlockSpec((ROW_BLOCK, w), lambda b, i: (b * nb + i, 0))
    wspec = lambda a: pl.BlockSpec((1,) + a.shape[1:], lambda b, i: (b * nb + i, 0, 0))
    weights = (w_gate, w_up, w_down)
    prev_kv = pl.BlockSpec((WINDOW, 2 * LANES),
                           lambda b, i: (jnp.maximum((b * nb + i) * sub - 1, 0), 0))
    prev_u = pl.BlockSpec((16, D_MODEL),
                          lambda b, i: (jnp.maximum((b * nb + i) * (ROW_BLOCK // 16) - 1, 0), 0))
    n = batch * seq
    return pl.pallas_call(
        _mixer_prompt_body,
        grid=(batch, nb),
        in_specs=[pl.BlockSpec(memory_space=pltpu.SMEM), cur(Q_W), cur(2 * LANES), prev_kv,
                  cur(2 * LANES), prev_kv, cur(D_MODEL), prev_u] + [wspec(a) for a in weights],
        out_specs=(cur(Q_W), cur(D_MODEL)) + tuple(wspec(a) for a in weights),
        out_shape=(jax.ShapeDtypeStruct((n, Q_W), BF16), jax.ShapeDtypeStruct((n, D_MODEL), BF16))
        + tuple(jax.ShapeDtypeStruct(a.shape, BF16) for a in weights),
        scratch_shapes=[pltpu.VMEM((POOL_PAD_ROWS + 16 + ROW_BLOCK, D_MODEL), F32)] * 3,
        compiler_params=pltpu.CompilerParams(dimension_semantics=("arbitrary", "arbitrary"),
                                             vmem_limit_bytes=VMEM_LIMIT),
        name="mixer_prompt",
    )(sinks, q, kk, kk, vv, vv, u, u, *weights)


SEQ_PAIR_ROWS = 8
SAMPLE_PAIRS_PER_STEP = 8


def _mixer_sample_body(sink_ref, q_ref, k_ref, v_ref, u_ref, ck_ref, cv_ref, st_ref,
                       attn_ref, pd_ref, nk_ref, nv_ref, nu_ref, ext_ref):
    m = SEQ_PAIR_ROWS
    half = m // 2
    hist = POOL_STATE_LEN
    ext_ref[16:, :] = jnp.zeros((ext_ref.shape[0] - 16, D_MODEL), F32)
    row8 = lax.broadcasted_iota(jnp.int32, (m, LANES), 0)
    n_chunks = Q_W // LANES
    mq = n_chunks * m
    row1 = lax.broadcasted_iota(jnp.int32, (m, 1), 0)
    row = lax.broadcasted_iota(jnp.int32, (mq, LANES), 0)
    col = lax.broadcasted_iota(jnp.int32, (mq, LANES), 1)
    cur_valid = (row % half) >= col
    lane8 = lax.broadcasted_iota(jnp.int32, (m, LANES), 1)
    lo8 = lane8 < HEAD_DIM
    top8 = lax.broadcasted_iota(jnp.int32, (m, LANES), 0) < half
    tail = jnp.zeros((WINDOW - m, LANES), F32)
    top = row1 < half
    sink_col = jnp.concatenate([jnp.where(top, sink_ref[2 * cc], sink_ref[2 * cc + 1])
                                for cc in range(n_chunks)], axis=0)

    def pair(p, carry):
        r0 = pl.multiple_of(p * m, m)
        q8 = q_ref[pl.ds(r0, m), :]
        k8 = k_ref[pl.ds(r0, m), :]
        v8 = v_ref[pl.ds(r0, m), :]
        u8 = u_ref[pl.ds(r0, m), :]
        attn_parts = []
        pd_parts = []
        for s in range(2):
            shift = lambda a: a if s == 0 else pltpu.roll(a, half, 0)
            seq = 2 * p + s
            ck, cv = ck_ref[seq], cv_ref[seq]
            ks, vs = shift(k8), shift(v8)
            kb = jnp.concatenate([ck, ks, tail], axis=0)
            vb = jnp.concatenate([cv, vs, tail], axis=0)
            for cache, new, out in ((ck, ks, nk_ref), (cv, vs, nv_ref)):
                up = pltpu.roll(cache, WINDOW - half, 0)
                out[seq, 0:WINDOW - m, :] = up[0:WINDOW - m]
                out[seq, WINDOW - m:WINDOW, :] = jnp.where(row8 < half, up[WINDOW - m:],
                                                           pltpu.roll(new, half, 0))
            qs = shift(q8)
            qr = pltpu.roll(qs, half, 0)
            q2 = []
            for cc in range(n_chunks):
                qa = qs[:, LANES * cc:LANES * (cc + 1)]
                qb = qr[:, LANES * cc:LANES * (cc + 1)]
                if cc < n_chunks // N_KV_HEADS:
                    first = jnp.where(lo8, qa, 0.0)
                    second = jnp.where(lo8, pltpu.roll(qb, HEAD_DIM, 1), 0.0)
                else:
                    first = jnp.where(lo8, 0.0, pltpu.roll(qa, HEAD_DIM, 1))
                    second = jnp.where(lo8, 0.0, qb)
                q2.append(jnp.where(top8, first, second))
            q2 = jnp.concatenate(q2, axis=0).astype(BF16)
            o = _attend(q2, kb.astype(BF16), vb.astype(BF16), cur_valid, sink_col, None)
            chunks = []
            for cc in range(n_chunks):
                oc = o[m * cc:m * (cc + 1)]
                if cc < n_chunks // N_KV_HEADS:
                    chunks.append(jnp.where(lo8, oc, pltpu.roll(pltpu.roll(oc, half, 0), HEAD_DIM, 1)))
                else:
                    chunks.append(jnp.where(lo8, pltpu.roll(oc, HEAD_DIM, 1), pltpu.roll(oc, half, 0)))
            attn_parts.append(jnp.concatenate(chunks, axis=1))

            ext_ref[0:hist, :] = st_ref[seq]
            ext_ref[hist:hist + half, :] = shift(u8)[0:half]
            nu_ref[seq] = ext_ref[half:hist + half, :]
            cols = []
            for g, w in enumerate(POOL_WINDOWS):
                c0, c1 = POOL_GROUP_WIDTH * g, POOL_GROUP_WIDTH * (g + 1)
                acc = ext_ref[hist:hist + m, c0:c1]
                for k in range(1, w):
                    acc = acc + ext_ref[hist - k:hist - k + m, c0:c1]
                cols.append(acc / float(w) - ext_ref[hist:hist + m, c0:c1])
            pd_parts.append(jnp.concatenate(cols, axis=1))

        attn_ref[pl.ds(r0, m), :] = jnp.where(top, attn_parts[0], pltpu.roll(attn_parts[1], half, 0))
        pd_ref[pl.ds(r0, m), :] = jnp.where(top, pd_parts[0], pltpu.roll(pd_parts[1], half, 0))
        return carry

    lax.fori_loop(0, SAMPLE_PAIRS_PER_STEP, pair, 0)


def _mixer_sample(sinks, q, k, v, u, cache_k, cache_v, state):
    n = q.shape[0]
    m = SEQ_PAIR_ROWS * SAMPLE_PAIRS_PER_STEP
    rows = lambda w: pl.BlockSpec((m, w), lambda i: (i, 0))
    seqs = lambda a: pl.BlockSpec((2 * SAMPLE_PAIRS_PER_STEP,) + a.shape[1:], lambda i: (i, 0, 0))
    like = lambda a: jax.ShapeDtypeStruct(a.shape, F32)
    return pl.pallas_call(
        _mixer_sample_body,
        grid=(n // m,),
        in_specs=[pl.BlockSpec(memory_space=pltpu.SMEM), rows(Q_W), rows(KV_W), rows(KV_W),
                  rows(D_MODEL), seqs(cache_k), seqs(cache_v), seqs(state)],
        out_specs=(rows(Q_W), rows(D_MODEL), seqs(cache_k), seqs(cache_v), seqs(state)),
        out_shape=(jax.ShapeDtypeStruct((n, Q_W), F32), jax.ShapeDtypeStruct((n, D_MODEL), F32),
                   like(cache_k), like(cache_v), like(state)),
        scratch_shapes=[pltpu.VMEM((16 + SEQ_PAIR_ROWS, D_MODEL), F32)],
        compiler_params=pltpu.CompilerParams(dimension_semantics=("arbitrary",),
                                             vmem_limit_bytes=VMEM_LIMIT),
        name="mixer_sample",
    )(sinks, q, k, v, u, cache_k, cache_v, state)


def _dense_body(x_ref, attn_ref, pd_ref, sga_ref, sgp_ref, mix_ref, ps_ref, wa_ref, wp_ref,
                wo_ref, gf_ref, wrh_ref, wrl_ref, br_ref, ltri_ref, ssel_ref, crow_ref,
                x1_ref, hn_ref, rt_ref, cnt_ref, seg_ref, carry_ref):
    rows = x_ref.shape[0]
    step = pl.program_id(0)

    @pl.when(step % TILE_BLOCKS == 0)
    def _():
        carry_ref[...] = jnp.zeros_like(carry_ref)

    pd = pd_ref[...].astype(BF16)
    pooled = []
    for g in range(len(POOL_WINDOWS)):
        c0, c1 = POOL_GROUP_WIDTH * g, POOL_GROUP_WIDTH * (g + 1)
        pooled.append((_bdot(pd[:, c0:c1], mix_ref[g]) * ps_ref[:, c0:c1]).astype(BF16))
    pooled = jnp.concatenate(pooled, axis=1)
    merged = (sga_ref[...] * _bdot(attn_ref[...].astype(BF16), wa_ref[...])
              + sgp_ref[...] * _bdot(pooled, wp_ref[...]))
    x1 = x_ref[...] + _bdot(merged.astype(BF16), wo_ref[...])
    x1_ref[...] = x1
    hn = _rms(x1, gf_ref[...])
    hn_ref[...] = hn
    hi = hn.astype(BF16)
    lo = (hn - hi.astype(F32)).astype(BF16)
    logits = (_bdot(hi, wrh_ref[...]) + _bdot(hi, wrl_ref[...]) + _bdot(lo, wrh_ref[...])
              + br_ref[...])

    lane = lax.broadcasted_iota(jnp.int32, (rows, LANES), 1)
    big = jnp.int32(LANES)
    gl = jnp.where(lane < N_EXPERT_GROUPS, logits, NEG_INF)
    gmax = jnp.max(gl, axis=-1, keepdims=True)
    gidx = jnp.min(jnp.where(gl == gmax, lane, big), axis=-1, keepdims=True)
    g_w = 1.0 / jnp.sum(jnp.exp(gl - gmax), axis=-1, keepdims=True)
    e0 = EXPERT_LANE0 + gidx * EXPERTS_PER_GROUP
    el = jnp.where((lane >= e0) & (lane < e0 + EXPERTS_PER_GROUP), logits, NEG_INF)
    l1 = jnp.max(el, axis=-1, keepdims=True)
    i1 = jnp.min(jnp.where(el == l1, lane, big), axis=-1, keepdims=True)
    el2 = jnp.where(lane == i1, NEG_INF, el)
    l2 = jnp.max(el2, axis=-1, keepdims=True)
    i2 = jnp.min(jnp.where(el2 == l2, lane, big), axis=-1, keepdims=True)
    e = jnp.exp(l2 - l1)
    w1 = 1.0 / (1.0 + e)
    w2 = e * w1

    sel = (lane == i1) | (lane == i2)
    onehot = jnp.where(sel, 1.0, 0.0).astype(BF16)
    rank = _bdot(ltri_ref[...], onehot) + carry_ref[0:1, :] + crow_ref[...]
    segsum = _bdot(ssel_ref[...], onehot)
    seg_ref[0] = segsum
    carry_ref[...] = carry_ref[...] + segsum[0:1, :]
    cnt_ref[0] = carry_ref[...]
    r1 = jnp.sum(jnp.where(lane == i1, rank, 0.0), axis=-1, keepdims=True)
    r2 = jnp.sum(jnp.where(lane == i2, rank, 0.0), axis=-1, keepdims=True)
    cols = ((i1 - EXPERT_LANE0).astype(F32), (i2 - EXPERT_LANE0).astype(F32), r1, r2,
            g_w * w1, g_w * w2)
    tile = jnp.zeros((rows, LANES), F32)
    for c, val in enumerate(cols):
        tile = jnp.where(lane == c, val, tile)
    rt_ref[...] = tile


def _dense(x, attn, pd, sga, sgp, mix_b, pool_scale, wa_b, wp_b, wo_b, gf, wr_hi, wr_lo, br,
           ltri, ssel, crow):
    n = x.shape[0]
    nb = n // ROW_BLOCK
    row = lambda w: pl.BlockSpec((ROW_BLOCK, w), lambda i: (i, 0))
    full = lambda a: pl.BlockSpec(a.shape, lambda i: (0,) * a.ndim)
    stat = pl.BlockSpec((1, 8, LANES), lambda i: (i, 0, 0))
    consts = (mix_b, pool_scale, wa_b, wp_b, wo_b, gf, wr_hi, wr_lo, br, ltri, ssel, crow)
    return pl.pallas_call(
        _dense_body,
        grid=(nb,),
        in_specs=[row(D_MODEL)] * 5 + [full(a) for a in consts],
        out_specs=(row(D_MODEL), row(D_MODEL), row(LANES), stat, stat),
        out_shape=(jax.ShapeDtypeStruct((n, D_MODEL), F32),
                   jax.ShapeDtypeStruct((n, D_MODEL), F32),
                   jax.ShapeDtypeStruct((n, LANES), F32),
                   jax.ShapeDtypeStruct((nb, 8, LANES), F32),
                   jax.ShapeDtypeStruct((nb, 8, LANES), F32)),
        scratch_shapes=[pltpu.VMEM((8, LANES), F32)],
        compiler_params=pltpu.CompilerParams(dimension_semantics=("arbitrary",),
                                             vmem_limit_bytes=VMEM_LIMIT),
        name="dense",
    )(x, attn, pd, sga, sgp, *consts)


def _experts_body(nblk_ref, seg_ref, cnt_ref, slots_ref, wts_ref, hn_p, hn_s, x1_p, x1_s,
                  wg_ref, wu_ref, wd_ref, y_p, y_s,
                  hn_t, acc, xg0, xg1, yb0, yb1, tok, sem_in, sem_out):
    tau = pl.program_id(0)
    e = pl.program_id(1)
    t_rows = TILE_TOKENS
    dummy = 2 * t_rows
    p_rows = pl.ds(pl.multiple_of(tau * TILE_PROMPT, TILE_PROMPT), TILE_PROMPT)
    s_rows = pl.ds(pl.multiple_of(tau * TILE_SAMPLE, TILE_SAMPLE), TILE_SAMPLE)
    tile_p = pl.ds(0, TILE_PROMPT)
    tile_s = pl.ds(TILE_PROMPT, TILE_SAMPLE)

    def in_copies():
        return [pltpu.make_async_copy(hn_p.at[p_rows], hn_t.at[tile_p], sem_in.at[0]),
                pltpu.make_async_copy(hn_s.at[s_rows], hn_t.at[tile_s], sem_in.at[1]),
                pltpu.make_async_copy(x1_p.at[p_rows], acc.at[tile_p], sem_in.at[2]),
                pltpu.make_async_copy(x1_s.at[s_rows], acc.at[tile_s], sem_in.at[3])]

    def out_copies():
        return [pltpu.make_async_copy(acc.at[tile_p], y_p.at[p_rows], sem_out.at[0]),
                pltpu.make_async_copy(acc.at[tile_s], y_s.at[s_rows], sem_out.at[1])]

    xbufs = (xg0, xg1)
    ybufs = (yb0, yb1)
    end_slot = seg_ref[tau, N_EXPERTS - 1] + nblk_ref[tau, N_EXPERTS - 1] * MOE_ROWS

    def gather_block(b, xdst):
        base = b * MOE_ROWS
        for j in range(MOE_ROWS):
            t = lax.shift_right_logical(tok[base + j], 1)
            xdst[j // 8, pl.ds(j % 8, 1), :] = hn_t[pl.ds(t, 1), :]

    def scatter_block(b, ysrc):
        base = b * MOE_ROWS
        for j0 in range(0, MOE_ROWS, SCATTER_BATCH):
            ents = [tok[base + j0 + i] for i in range(SCATTER_BATCH)]
            rows = [lax.shift_right_logical(en, 1) for en in ents]
            vals = [acc[pl.ds(rows[i], 1), :]
                    + wts_ref[0, 0, ents[i]] * ysrc[(j0 + i) // 8, pl.ds((j0 + i) % 8, 1), :]
                    for i in range(SCATTER_BATCH)]
            for i in range(SCATTER_BATCH):
                acc[pl.ds(rows[i], 1), :] = vals[i]

    @pl.when(e == 0)
    def _load_tile():
        for cp in in_copies():
            cp.start()

        hn_t[t_rows:, :] = jnp.zeros((DUMMY_ROWS, D_MODEL), F32)
        yb0[...] = jnp.zeros_like(yb0)

        def pad_range(lo, hi):
            def fill(p, c):
                tok[p] = dummy
                return c
            lax.fori_loop(lo, hi, fill, 0)
        pad_range(0, MOE_ROWS)
        pad_range(end_slot, end_slot + MOE_ROWS)

        def pad_fill(ei, c):
            s0 = seg_ref[tau, ei]
            pad_range(s0 + cnt_ref[tau, ei], s0 + nblk_ref[tau, ei] * MOE_ROWS)
            return c
        lax.fori_loop(0, N_EXPERTS, pad_fill, 0)

        def invert(g, c):
            t0 = g * 8
            for i in range(8):
                for k in range(2):
                    tok[slots_ref[0, k, t0 + i]] = 2 * (t0 + i) + k
            return c
        lax.fori_loop(0, t_rows // 8, invert, 0)

        for cp in in_copies():
            cp.wait()
        acc[t_rows:, :] = jnp.zeros((DUMMY_ROWS, D_MODEL), F32)
        gather_block(1, xbufs[1])

    first_block = lax.shift_right_logical(seg_ref[tau, e], MOE_ROWS.bit_length() - 1)

    def block(jb, c):
        b = first_block + jb
        for par in range(2):
            @pl.when((b & 1) == par)
            def _():
                gather_block(b + 1, xbufs[1 - par])
                xb = xbufs[par][...].reshape(MOE_ROWS, D_MODEL).astype(BF16)
                act = (jax.nn.silu(_bdot(xb, wg_ref[0])) * _bdot(xb, wu_ref[0])).astype(BF16)
                ybufs[par][...] = _bdot(act, wd_ref[0]).reshape(MOE_ROWS // 8, 8, D_MODEL)
                scatter_block(b - 1, ybufs[1 - par])
        return c
    lax.fori_loop(0, nblk_ref[tau, e], block, 0)

    @pl.when(e == N_EXPERTS - 1)
    def _store_tile():
        last = lax.shift_right_logical(end_slot, MOE_ROWS.bit_length() - 1) - 1
        for par in range(2):
            @pl.when((last & 1) == par)
            def _():
                scatter_block(last, ybufs[par])
        cps = out_copies()
        for cp in cps:
            cp.start()
        for cp in cps:
            cp.wait()


def _experts(nblk, seg, cnt, slots, wts, hn_p, hn_s, x1_p, x1_s, w_gate, w_up, w_down):
    any_spec = pl.BlockSpec(memory_space=pl.ANY)
    wspec = lambda a: pl.BlockSpec((1,) + a.shape[1:], lambda t, e, *_: (e, 0, 0))
    smem = lambda a: pl.BlockSpec((1,) + a.shape[1:], lambda t, e, *_: (t, 0, 0),
                                  memory_space=pltpu.SMEM)
    grid_spec = pltpu.PrefetchScalarGridSpec(
        num_scalar_prefetch=3,
        grid=(N_TILES, N_EXPERTS),
        in_specs=[smem(slots), smem(wts), any_spec, any_spec, any_spec, any_spec,
                  wspec(w_gate), wspec(w_up), wspec(w_down)],
        out_specs=(any_spec, any_spec),
        scratch_shapes=[
            pltpu.VMEM((TILE_TOKENS + DUMMY_ROWS, D_MODEL), F32),
            pltpu.VMEM((TILE_TOKENS + DUMMY_ROWS, D_MODEL), F32),
            pltpu.VMEM((MOE_ROWS // 8, 8, D_MODEL), F32),
            pltpu.VMEM((MOE_ROWS // 8, 8, D_MODEL), F32),
            pltpu.VMEM((MOE_ROWS // 8, 8, D_MODEL), F32),
            pltpu.VMEM((MOE_ROWS // 8, 8, D_MODEL), F32),
            pltpu.SMEM((LIST_CAP,), jnp.int32),
            pltpu.SemaphoreType.DMA((4,)),
            pltpu.SemaphoreType.DMA((2,)),
        ],
    )
    return pl.pallas_call(
        _experts_body,
        grid_spec=grid_spec,
        out_shape=(jax.ShapeDtypeStruct(x1_p.shape, F32), jax.ShapeDtypeStruct(x1_s.shape, F32)),
        compiler_params=pltpu.CompilerParams(dimension_semantics=("arbitrary", "arbitrary"),
                                             vmem_limit_bytes=EXPERTS_VMEM_LIMIT),
        name="experts",
    )(nblk, seg, cnt, slots, wts, hn_p, hn_s, x1_p, x1_s, w_gate, w_up, w_down)


def _rope_tables(pos):
    half = HEAD_DIM // 2
    inv = ROPE_THETA ** (-jnp.arange(half, dtype=F32) * (2.0 / HEAD_DIM))
    ang = pos.astype(F32)[:, None] * inv[None, :]
    cos = jnp.tile(jnp.cos(ang), (1, LANES // half))
    sin = jnp.sin(ang)
    sin = jnp.tile(jnp.concatenate([-sin, sin], axis=1), (1, LANES // HEAD_DIM))
    return cos, sin


def _split_bf16(w):
    hi = w.astype(BF16)
    return hi, (w - hi.astype(F32)).astype(BF16)


def kernel(x_prompt, x_sample, cache_k, cache_v, state_pool, norm_mix_g, w_in, q_norm_g, k_norm_g,
           attn_sinks, w_attn_branch, pool_mix_w, pool_scale, w_pool_branch, w_out, norm_ffn_g,
           w_route_group, b_route_group, w_route_expert, b_route_expert, w_expert_gate,
           w_expert_up, w_expert_down):
    batch, seq, d = x_prompt.shape
    dec_batch, dec_seq, _ = x_sample.shape
    past_len = 16384
    assert x_prompt.shape == (4, 4096, D_MODEL) and x_sample.shape == (128, 4, D_MODEL)
    assert w_in.shape[0] == 1, "single layer"
    n_p, n_s = batch * seq, dec_batch * dec_seq

    g_mix = norm_mix_g[0][None, :]
    w_in_b = w_in[0].astype(BF16)
    gq = jnp.tile(q_norm_g[0], LANES // HEAD_DIM)[None, :]
    gk = jnp.tile(k_norm_g[0], LANES // HEAD_DIM)[None, :]
    sinks = attn_sinks[0]
    mix_b = pool_mix_w[0].astype(BF16)
    ps = pool_scale[0][None, :]
    wa_b = w_attn_branch[0].astype(BF16)
    wp_b = w_pool_branch[0].astype(BF16)
    wo_b = w_out[0].astype(BF16)
    gf = norm_ffn_g[0][None, :]
    wr = jnp.zeros((D_MODEL, LANES), F32)
    wr = wr.at[:, :N_EXPERT_GROUPS].set(w_route_group[0])
    wr = wr.at[:, EXPERT_LANE0:EXPERT_LANE0 + N_EXPERTS].set(w_route_expert[0])
    wr_hi, wr_lo = _split_bf16(wr)
    br = jnp.zeros((1, LANES), F32)
    br = br.at[0, :N_EXPERT_GROUPS].set(b_route_group[0])
    br = br.at[0, EXPERT_LANE0:EXPERT_LANE0 + N_EXPERTS].set(b_route_expert[0])

    cos_p, sin_p = _rope_tables(jnp.arange(seq, dtype=jnp.int32))
    pos_s = past_len + (jnp.arange(n_s, dtype=jnp.int32) % dec_seq)
    cos_s, sin_s = _rope_tables(pos_s)

    dense_consts = (mix_b, ps, wa_b, wp_b, wo_b, gf, wr_hi, wr_lo, br)
    assert n_s == ROW_BLOCK == N_TILES * TILE_SAMPLE and n_p == N_TILES * TILE_PROMPT
    ridx = jnp.arange(ROW_BLOCK, dtype=jnp.int32)
    lower = ridx[:, None] > ridx[None, :]
    seg_of = ridx // TILE_SAMPLE
    ltri_p = lower.astype(BF16)
    ltri_s = (lower & (seg_of[:, None] == seg_of[None, :])).astype(BF16)
    ssel_p = (jnp.arange(8, dtype=jnp.int32)[:, None] == 0) & (ridx[None, :] >= 0)
    ssel_s = jnp.arange(8, dtype=jnp.int32)[:, None] == seg_of[None, :]

    xp = x_prompt.reshape(n_p, d)
    q, k, v, kk, vv, u, sga, sgp = _inproj(xp, g_mix, w_in_b, cos_p, sin_p, gq, gk)
    attn, pd, wg_b, wu_b, wd_b = _mixer_prompt(sinks, q, kk, vv, u, w_expert_gate[0], w_expert_up[0],
                                               w_expert_down[0], batch, seq)
    x1_p, hn_p, rt_p, cnt_p, _ = _dense(xp, attn, pd, sga, sgp, *dense_consts, ltri_p,
                                        ssel_p.astype(BF16), jnp.zeros((ROW_BLOCK, LANES), F32))
    new_k_p = k.reshape(batch, seq, N_KV_HEADS, HEAD_DIM)[:, -WINDOW:][None]
    new_v_p = v.reshape(batch, seq, N_KV_HEADS, HEAD_DIM)[:, -WINDOW:][None]
    new_u_p = u.reshape(batch, seq, d)[:, -POOL_STATE_LEN:][None]

    xs = x_sample.reshape(n_s, d)
    q, k, v, kk, vv, u, sga, sgp = _inproj(xs, g_mix, w_in_b, cos_s, sin_s, gq, gk)
    ck = cache_k[0].reshape(dec_batch, WINDOW, KV_W)
    cv = cache_v[0].reshape(dec_batch, WINDOW, KV_W)
    attn, pd, nk_s, nv_s, nu_s = _mixer_sample(sinks, q.astype(F32), k, v, u, ck, cv, state_pool[0])
    cnt_tiles_p = cnt_p[TILE_BLOCKS - 1::TILE_BLOCKS, 0, :]
    crow = jnp.repeat(cnt_tiles_p, TILE_SAMPLE, axis=0)
    x1_s, hn_s, rt_s, _, seg_s = _dense(xs, attn, pd, sga, sgp, *dense_consts, ltri_s,
                                        ssel_s.astype(BF16), crow)
    new_k_s = nk_s.reshape(1, dec_batch, WINDOW, N_KV_HEADS, HEAD_DIM)
    new_v_s = nv_s.reshape(1, dec_batch, WINDOW, N_KV_HEADS, HEAD_DIM)
    new_u_s = nu_s[None]

    ex = slice(EXPERT_LANE0, EXPERT_LANE0 + N_EXPERTS)
    cnt = (cnt_tiles_p[:, ex] + seg_s[0, :N_TILES, ex]).astype(jnp.int32)
    nblk = (cnt + (MOE_ROWS - 1)) // MOE_ROWS
    seg = MOE_ROWS * (1 + jnp.cumsum(nblk, axis=1) - nblk)
    rt = jnp.concatenate([rt_p[:, :6].reshape(N_TILES, TILE_PROMPT, 6),
                          rt_s[:, :6].reshape(N_TILES, TILE_SAMPLE, 6)], axis=1)
    expert_hit = rt[:, :, 0:2].astype(jnp.int32)[..., None] == jnp.arange(N_EXPERTS, dtype=jnp.int32)
    slot = (rt[:, :, 2:4].astype(jnp.int32)
            + jnp.sum(jnp.where(expert_hit, seg[:, None, None, :], 0), axis=-1))
    slots = slot.transpose(0, 2, 1)
    wts = rt[:, :, 4:6].reshape(N_TILES, 1, 2 * TILE_TOKENS)
    wts = jnp.pad(wts, ((0, 0), (0, 0), (0, 8)))
    y_p, y_s = _experts(nblk, seg, cnt, slots, wts, hn_p, hn_s, x1_p, x1_s, wg_b, wu_b, wd_b)
    y_prompt = y_p.reshape(batch, seq, d)
    y_sample = y_s.reshape(dec_batch, dec_seq, d)

    return (y_prompt, y_sample, new_k_p, new_v_p, new_u_p, new_k_s, new_v_s, new_u_s)
```

```python
import functools

import jax
import jax.numpy as jnp
from jax import lax
from jax.experimental import pallas as pl
from jax.experimental.pallas import tpu as pltpu

F32 = jnp.float32
BF16 = jnp.bfloat16

D_MODEL = 1024
HEAD_DIM = 64
N_Q_HEADS = 16
N_KV_HEADS = 2
WINDOW = 128
ROPE_THETA = 10000.0
POOL_WINDOWS = (2, 4, 8, 16)
POOL_GROUP_WIDTH = D_MODEL // len(POOL_WINDOWS)
POOL_STATE_LEN = max(POOL_WINDOWS) - 1
N_EXPERT_GROUPS = 4
EXPERTS_PER_GROUP = 8
N_EXPERTS = N_EXPERT_GROUPS * EXPERTS_PER_GROUP
D_EXPERT = 512
RMS_EPS = 1e-6
Q_W = N_Q_HEADS * HEAD_DIM
KV_W = N_KV_HEADS * HEAD_DIM
OFF_K = Q_W
OFF_V = OFF_K + KV_W
OFF_U = OFF_V + KV_W
OFF_GA = OFF_U + D_MODEL
OFF_GP = OFF_GA + D_MODEL
IN_W = OFF_GP + D_MODEL

LANES = 128
ROW_BLOCK = 512
EXPERT_LANE0 = 32
VMEM_LIMIT = 56 * 1024 * 1024
EXPERTS_VMEM_LIMIT = 60 * 1024 * 1024
NEG_INF = float("-inf")

N_TILES = 4
TILE_BLOCKS = 8
TILE_PROMPT = TILE_BLOCKS * ROW_BLOCK
TILE_SAMPLE = 128
TILE_TOKENS = TILE_PROMPT + TILE_SAMPLE
MOE_ROWS = 320
DUMMY_ROWS = 8
SCATTER_BATCH = 4
LIST_CAP = 2 * TILE_TOKENS + N_EXPERTS * (MOE_ROWS - 1) + 2 * MOE_ROWS + 1


def _rms(x, g):
    return x * lax.rsqrt(jnp.mean(x * x, axis=-1, keepdims=True) + RMS_EPS) * g


def _bdot(a, b):
    return jnp.dot(a, b, preferred_element_type=F32)


def _inproj_body(x_ref, g_ref, w_ref, cos_ref, sin_ref, gq_ref, gk_ref,
                 q_ref, k_ref, v_ref, kk_ref, vv_ref, u_ref, sga_ref, sgp_ref):
    rows = x_ref.shape[0]
    hb = _rms(x_ref[...], g_ref[...]).astype(BF16)
    cos = cos_ref[...]
    sin = sin_ref[...]
    lane = lax.broadcasted_iota(jnp.int32, (rows, LANES), 1)
    lo = lane < HEAD_DIM
    first_half = (lane % HEAD_DIM) < (HEAD_DIM // 2)

    def head_norm_rope(zc, gain):
        sq = zc * zc
        ss_lo = jnp.sum(jnp.where(lo, sq, 0.0), axis=-1, keepdims=True)
        ss_hi = jnp.sum(jnp.where(lo, 0.0, sq), axis=-1, keepdims=True)
        r = lax.rsqrt(jnp.where(lo, ss_lo, ss_hi) * (1.0 / HEAD_DIM) + RMS_EPS)
        y = zc * r * gain
        partner = jnp.where(first_half, pltpu.roll(y, LANES - HEAD_DIM // 2, 1),
                            pltpu.roll(y, HEAD_DIM // 2, 1))
        return y * cos + partner * sin

    gq = gq_ref[...]
    for j in range(Q_W // 256):
        z = _bdot(hb, w_ref[:, 256 * j:256 * (j + 1)])
        for c in range(2):
            qn = head_norm_rope(z[:, LANES * c:LANES * (c + 1)], gq) * (HEAD_DIM ** -0.5)
            q_ref[:, 256 * j + LANES * c:256 * j + LANES * (c + 1)] = qn.astype(BF16)

    z = _bdot(hb, w_ref[:, OFF_K:OFF_U])
    kn = head_norm_rope(z[:, :KV_W], gk_ref[...])
    vr = z[:, KV_W:]
    k_ref[...] = kn
    v_ref[...] = vr
    kr = pltpu.roll(kn, HEAD_DIM, 1)
    vrr = pltpu.roll(vr, HEAD_DIM, 1)
    kk_ref[:, :LANES] = jnp.where(lo, kn, kr).astype(BF16)
    kk_ref[:, LANES:] = jnp.where(lo, kr, kn).astype(BF16)
    vv_ref[:, :LANES] = jnp.where(lo, vr, vrr).astype(BF16)
    vv_ref[:, LANES:] = jnp.where(lo, vrr, vr).astype(BF16)

    for j in range(D_MODEL // 256):
        u_ref[:, 256 * j:256 * (j + 1)] = _bdot(hb, w_ref[:, OFF_U + 256 * j:OFF_U + 256 * (j + 1)])
        sga_ref[:, 256 * j:256 * (j + 1)] = jax.nn.sigmoid(
            _bdot(hb, w_ref[:, OFF_GA + 256 * j:OFF_GA + 256 * (j + 1)]))
        sgp_ref[:, 256 * j:256 * (j + 1)] = jax.nn.sigmoid(
            _bdot(hb, w_ref[:, OFF_GP + 256 * j:OFF_GP + 256 * (j + 1)]))


def _inproj(x, g, w_in_b, cos, sin, gq, gk):
    n = x.shape[0]
    nb = n // ROW_BLOCK
    ncos = cos.shape[0] // ROW_BLOCK
    row = lambda w: pl.BlockSpec((ROW_BLOCK, w), lambda i: (i, 0))
    full = lambda a: pl.BlockSpec(a.shape, lambda i: (0,) * a.ndim)
    tab = pl.BlockSpec((ROW_BLOCK, LANES), lambda i: (i % ncos, 0))
    out_shapes = (
        jax.ShapeDtypeStruct((n, Q_W), BF16),
        jax.ShapeDtypeStruct((n, KV_W), F32),
        jax.ShapeDtypeStruct((n, KV_W), F32),
        jax.ShapeDtypeStruct((n, 2 * LANES), BF16),
        jax.ShapeDtypeStruct((n, 2 * LANES), BF16),
        jax.ShapeDtypeStruct((n, D_MODEL), F32),
        jax.ShapeDtypeStruct((n, D_MODEL), F32),
        jax.ShapeDtypeStruct((n, D_MODEL), F32),
    )
    return pl.pallas_call(
        _inproj_body,
        grid=(nb,),
        in_specs=[row(D_MODEL), full(g), full(w_in_b), tab, tab, full(gq), full(gk)],
        out_specs=tuple(row(s.shape[1]) for s in out_shapes),
        out_shape=out_shapes,
        compiler_params=pltpu.CompilerParams(dimension_semantics=("arbitrary",),
                                             vmem_limit_bytes=VMEM_LIMIT),
        name="inproj",
    )(x, g, w_in_b, cos, sin, gq, gk)


def _attend(q2, kg, vg, cur_valid, sink_col, prev_dead):
    s = lax.dot_general(q2, kg, (((1,), (1,)), ((), ())), preferred_element_type=F32)
    s_prev = s[:, :WINDOW]
    if prev_dead is not None:
        s_prev = jnp.where(prev_dead, NEG_INF, s_prev)
    sc = jnp.where(cur_valid, s[:, WINDOW:], s_prev)
    m = jnp.maximum(jnp.max(sc, axis=-1, keepdims=True), sink_col)
    p = jnp.exp(sc - m)
    den = jnp.sum(p, axis=-1, keepdims=True) + jnp.exp(sink_col - m)
    pn = p * (1.0 / den)
    p2 = jnp.concatenate([jnp.where(cur_valid, 0.0, pn), jnp.where(cur_valid, pn, 0.0)], axis=1)
    return _bdot(p2.astype(BF16), vg)


POOL_PAD_ROWS = 8


def _mixer_prompt_body(sink_ref, q_ref, kkc_ref, kkp_ref, vvc_ref, vvp_ref, uc_ref, up_ref,
                       wg_ref, wu_ref, wd_ref,
                       attn_ref, pd_ref, wgb_ref, wub_ref, wdb_ref, ext_ref, s_a, s_b):
    wgb_ref[...] = wg_ref[...].astype(BF16)
    wub_ref[...] = wu_ref[...].astype(BF16)
    wdb_ref[...] = wd_ref[...].astype(BF16)
    i = pl.program_id(1)
    first = i == 0
    m2 = 2 * WINDOW
    row = lax.broadcasted_iota(jnp.int32, (m2, LANES), 0)
    col = lax.broadcasted_iota(jnp.int32, (m2, LANES), 1)
    cur_valid = (row % WINDOW) >= col
    upper = lax.broadcasted_iota(jnp.int32, (m2, 1), 0) < WINDOW
    lo = lax.broadcasted_iota(jnp.int32, (WINDOW, LANES), 1) < HEAD_DIM

    for j in range(ROW_BLOCK // WINDOW):
        r0 = WINDOW * j
        if j == 0:
            kprev, vprev = kkp_ref[...], vvp_ref[...]
            prev_dead = first
        else:
            kprev, vprev = kkc_ref[r0 - WINDOW:r0, :], vvc_ref[r0 - WINDOW:r0, :]
            prev_dead = None
        kband = jnp.concatenate([kprev, kkc_ref[r0:r0 + WINDOW, :]], axis=0)
        vband = jnp.concatenate([vprev, vvc_ref[r0:r0 + WINDOW, :]], axis=0)
        for g in range(N_KV_HEADS):
            kg = kband[:, LANES * g:LANES * (g + 1)]
            vg = vband[:, LANES * g:LANES * (g + 1)]
            for c in range(4):
                cc = 4 * g + c
                qc = q_ref[r0:r0 + WINDOW, LANES * cc:LANES * (cc + 1)]
                zero = jnp.zeros_like(qc)
                q2 = jnp.concatenate([jnp.where(lo, qc, zero), jnp.where(lo, zero, qc)], axis=0)
                sink_col = jnp.where(upper, sink_ref[2 * cc], sink_ref[2 * cc + 1])
                o = _attend(q2, kg, vg, cur_valid, sink_col, prev_dead)
                attn_ref[r0:r0 + WINDOW, LANES * cc:LANES * (cc + 1)] = jnp.where(
                    lo, o[:WINDOW], o[WINDOW:]).astype(BF16)

    pad, top = POOL_PAD_ROWS, POOL_PAD_ROWS + 16
    end = top + ROW_BLOCK
    for ref in (ext_ref, s_a, s_b):
        ref[0:pad, :] = jnp.zeros((pad, D_MODEL), F32)
    ext_ref[pad:top, :] = jnp.where(first, 0.0, up_ref[...])
    ext_ref[top:, :] = uc_ref[...]
    pos = i * ROW_BLOCK + lax.broadcasted_iota(jnp.int32, (ROW_BLOCK, 1), 0)
    src = ext_ref
    for g, w in enumerate(POOL_WINDOWS):
        c0 = POOL_GROUP_WIDTH * g
        dst = (s_a, s_b)[g % 2]
        shift = w // 2
        dst[pad:end, c0:] = src[pad:end, c0:] + src[pad - shift:end - shift, c0:]
        c1 = c0 + POOL_GROUP_WIDTH
        cnt = jnp.minimum(w, pos + 1).astype(F32)
        pd_ref[:, c0:c1] = (dst[top:end, c0:c1] / cnt - uc_ref[:, c0:c1]).astype(BF16)
        src = dst


def _mixer_prompt(sinks, q, kk, vv, u, w_gate, w_up, w_down, batch, seq):
    nb = seq // ROW_BLOCK
    assert batch * nb == N_EXPERTS, "one expert's weights are cast per grid step"
    sub = ROW_BLOCK // WINDOW
    cur = lambda w: pl.BlockSpec((ROW_BLOCK, w), lambda b, i: (b * nb + i, 0))
    wspec = lambda a: pl.BlockSpec((1,) + a.shape[1:], lambda b, i: (b * nb + i, 0, 0))
    weights = (w_gate, w_up, w_down)
    prev_kv = pl.BlockSpec((WINDOW, 2 * LANES),
                           lambda b, i: (jnp.maximum((b * nb + i) * sub - 1, 0), 0))
    prev_u = pl.BlockSpec((16, D_MODEL),
                          lambda b, i: (jnp.maximum((b * nb + i) * (ROW_BLOCK // 16) - 1, 0), 0))
    n = batch * seq
    return pl.pallas_call(
        _mixer_prompt_body,
        grid=(batch, nb),
        in_specs=[pl.BlockSpec(memory_space=pltpu.SMEM), cur(Q_W), cur(2 * LANES), prev_kv,
                  cur(2 * LANES), prev_kv, cur(D_MODEL), prev_u] + [wspec(a) for a in weights],
        out_specs=(cur(Q_W), cur(D_MODEL)) + tuple(wspec(a) for a in weights),
        out_shape=(jax.ShapeDtypeStruct((n, Q_W), BF16), jax.ShapeDtypeStruct((n, D_MODEL), BF16))
        + tuple(jax.ShapeDtypeStruct(a.shape, BF16) for a in weights),
        scratch_shapes=[pltpu.VMEM((POOL_PAD_ROWS + 16 + ROW_BLOCK, D_MODEL), F32)] * 3,
        compiler_params=pltpu.CompilerParams(dimension_semantics=("arbitrary", "arbitrary"),
                                             vmem_limit_bytes=VMEM_LIMIT),
        name="mixer_prompt",
    )(sinks, q, kk, kk, vv, vv, u, u, *weights)


SEQ_PAIR_ROWS = 8
SAMPLE_PAIRS_PER_STEP = 8


def _mixer_sample_body(sink_ref, q_ref, k_ref, v_ref, u_ref, ck_ref, cv_ref, st_ref,
                       attn_ref, pd_ref, nk_ref, nv_ref, nu_ref, ext_ref):
    m = SEQ_PAIR_ROWS
    half = m // 2
    hist = POOL_STATE_LEN
    ext_ref[16:, :] = jnp.zeros((ext_ref.shape[0] - 16, D_MODEL), F32)
    row8 = lax.broadcasted_iota(jnp.int32, (m, LANES), 0)
    n_chunks = Q_W // LANES
    mq = n_chunks * m
    row1 = lax.broadcasted_iota(jnp.int32, (m, 1), 0)
    row = lax.broadcasted_iota(jnp.int32, (mq, LANES), 0)
    col = lax.broadcasted_iota(jnp.int32, (mq, LANES), 1)
    cur_valid = (row % half) >= col
    lane8 = lax.broadcasted_iota(jnp.int32, (m, LANES), 1)
    lo8 = lane8 < HEAD_DIM
    top8 = lax.broadcasted_iota(jnp.int32, (m, LANES), 0) < half
    tail = jnp.zeros((WINDOW - m, LANES), F32)
    top = row1 < half
    sink_col = jnp.concatenate([jnp.where(top, sink_ref[2 * cc], sink_ref[2 * cc + 1])
                                for cc in range(n_chunks)], axis=0)

    def pair(p, carry):
        r0 = pl.multiple_of(p * m, m)
        q8 = q_ref[pl.ds(r0, m), :]
        k8 = k_ref[pl.ds(r0, m), :]
        v8 = v_ref[pl.ds(r0, m), :]
        u8 = u_ref[pl.ds(r0, m), :]
        attn_parts = []
        pd_parts = []
        for s in range(2):
            shift = lambda a: a if s == 0 else pltpu.roll(a, half, 0)
            seq = 2 * p + s
            ck, cv = ck_ref[seq], cv_ref[seq]
            ks, vs = shift(k8), shift(v8)
            kb = jnp.concatenate([ck, ks, tail], axis=0)
            vb = jnp.concatenate([cv, vs, tail], axis=0)
            for cache, new, out in ((ck, ks, nk_ref), (cv, vs, nv_ref)):
                up = pltpu.roll(cache, WINDOW - half, 0)
                out[seq, 0:WINDOW - m, :] = up[0:WINDOW - m]
                out[seq, WINDOW - m:WINDOW, :] = jnp.where(row8 < half, up[WINDOW - m:],
                                                           pltpu.roll(new, half, 0))
            qs = shift(q8)
            qr = pltpu.roll(qs, half, 0)
            q2 = []
            for cc in range(n_chunks):
                qa = qs[:, LANES * cc:LANES * (cc + 1)]
                qb = qr[:, LANES * cc:LANES * (cc + 1)]
                if cc < n_chunks // N_KV_HEADS:
                    first = jnp.where(lo8, qa, 0.0)
                    second = jnp.where(lo8, pltpu.roll(qb, HEAD_DIM, 1), 0.0)
                else:
                    first = jnp.where(lo8, 0.0, pltpu.roll(qa, HEAD_DIM, 1))
                    second = jnp.where(lo8, 0.0, qb)
                q2.append(jnp.where(top8, first, second))
            q2 = jnp.concatenate(q2, axis=0).astype(BF16)
            o = _attend(q2, kb.astype(BF16), vb.astype(BF16), cur_valid, sink_col, None)
            chunks = []
            for cc in range(n_chunks):
                oc = o[m * cc:m * (cc + 1)]
                if cc < n_chunks // N_KV_HEADS:
                    chunks.append(jnp.where(lo8, oc, pltpu.roll(pltpu.roll(oc, half, 0), HEAD_DIM, 1)))
                else:
                    chunks.append(jnp.where(lo8, pltpu.roll(oc, HEAD_DIM, 1), pltpu.roll(oc, half, 0)))
            attn_parts.append(jnp.concatenate(chunks, axis=1))

            ext_ref[0:hist, :] = st_ref[seq]
            ext_ref[hist:hist + half, :] = shift(u8)[0:half]
            nu_ref[seq] = ext_ref[half:hist + half, :]
            cols = []
            for g, w in enumerate(POOL_WINDOWS):
                c0, c1 = POOL_GROUP_WIDTH * g, POOL_GROUP_WIDTH * (g + 1)
                acc = ext_ref[hist:hist + m, c0:c1]
                for k in range(1, w):
                    acc = acc + ext_ref[hist - k:hist - k + m, c0:c1]
                cols.append(acc / float(w) - ext_ref[hist:hist + m, c0:c1])
            pd_parts.append(jnp.concatenate(cols, axis=1))

        attn_ref[pl.ds(r0, m), :] = jnp.where(top, attn_parts[0], pltpu.roll(attn_parts[1], half, 0))
        pd_ref[pl.ds(r0, m), :] = jnp.where(top, pd_parts[0], pltpu.roll(pd_parts[1], half, 0))
        return carry

    lax.fori_loop(0, SAMPLE_PAIRS_PER_STEP, pair, 0)


def _mixer_sample(sinks, q, k, v, u, cache_k, cache_v, state):
    n = q.shape[0]
    m = SEQ_PAIR_ROWS * SAMPLE_PAIRS_PER_STEP
    rows = lambda w: pl.BlockSpec((m, w), lambda i: (i, 0))
    seqs = lambda a: pl.BlockSpec((2 * SAMPLE_PAIRS_PER_STEP,) + a.shape[1:], lambda i: (i, 0, 0))
    like = lambda a: jax.ShapeDtypeStruct(a.shape, F32)
    return pl.pallas_call(
        _mixer_sample_body,
        grid=(n // m,),
        in_specs=[pl.BlockSpec(memory_space=pltpu.SMEM), rows(Q_W), rows(KV_W), rows(KV_W),
                  rows(D_MODEL), seqs(cache_k), seqs(cache_v), seqs(state)],
        out_specs=(rows(Q_W), rows(D_MODEL), seqs(cache_k), seqs(cache_v), seqs(state)),
        out_shape=(jax.ShapeDtypeStruct((n, Q_W), F32), jax.ShapeDtypeStruct((n, D_MODEL), F32),
                   like(cache_k), like(cache_v), like(state)),
        scratch_shapes=[pltpu.VMEM((16 + SEQ_PAIR_ROWS, D_MODEL), F32)],
        compiler_params=pltpu.CompilerParams(dimension_semantics=("arbitrary",),
                                             vmem_limit_bytes=VMEM_LIMIT),
        name="mixer_sample",
    )(sinks, q, k, v, u, cache_k, cache_v, state)


def _dense_body(x_ref, attn_ref, pd_ref, sga_ref, sgp_ref, mix_ref, ps_ref, wa_ref, wp_ref,
                wo_ref, gf_ref, wrh_ref, wrl_ref, br_ref, ltri_ref, ssel_ref, crow_ref,
                x1_ref, hn_ref, rt_ref, cnt_ref, seg_ref, carry_ref):
    rows = x_ref.shape[0]
    step = pl.program_id(0)

    @pl.when(step % TILE_BLOCKS == 0)
    def _():
        carry_ref[...] = jnp.zeros_like(carry_ref)

    pd = pd_ref[...].astype(BF16)
    pooled = []
    for g in range(len(POOL_WINDOWS)):
        c0, c1 = POOL_GROUP_WIDTH * g, POOL_GROUP_WIDTH * (g + 1)
        pooled.append((_bdot(pd[:, c0:c1], mix_ref[g]) * ps_ref[:, c0:c1]).astype(BF16))
    pooled = jnp.concatenate(pooled, axis=1)
    merged = (sga_ref[...] * _bdot(attn_ref[...].astype(BF16), wa_ref[...])
              + sgp_ref[...] * _bdot(pooled, wp_ref[...]))
    x1 = x_ref[...] + _bdot(merged.astype(BF16), wo_ref[...])
    x1_ref[...] = x1
    hn = _rms(x1, gf_ref[...])
    hn_ref[...] = hn
    hi = hn.astype(BF16)
    lo = (hn - hi.astype(F32)).astype(BF16)
    logits = (_bdot(hi, wrh_ref[...]) + _bdot(hi, wrl_ref[...]) + _bdot(lo, wrh_ref[...])
              + br_ref[...])

    lane = lax.broadcasted_iota(jnp.int32, (rows, LANES), 1)
    big = jnp.int32(LANES)
    gl = jnp.where(lane < N_EXPERT_GROUPS, logits, NEG_INF)
    gmax = jnp.max(gl, axis=-1, keepdims=True)
    gidx = jnp.min(jnp.where(gl == gmax, lane, big), axis=-1, keepdims=True)
    g_w = 1.0 / jnp.sum(jnp.exp(gl - gmax), axis=-1, keepdims=True)
    e0 = EXPERT_LANE0 + gidx * EXPERTS_PER_GROUP
    el = jnp.where((lane >= e0) & (lane < e0 + EXPERTS_PER_GROUP), logits, NEG_INF)
    l1 = jnp.max(el, axis=-1, keepdims=True)
    i1 = jnp.min(jnp.where(el == l1, lane, big), axis=-1, keepdims=True)
    el2 = jnp.where(lane == i1, NEG_INF, el)
    l2 = jnp.max(el2, axis=-1, keepdims=True)
    i2 = jnp.min(jnp.where(el2 == l2, lane, big), axis=-1, keepdims=True)
    e = jnp.exp(l2 - l1)
    w1 = 1.0 / (1.0 + e)
    w2 = e * w1

    sel = (lane == i1) | (lane == i2)
    onehot = jnp.where(sel, 1.0, 0.0).astype(BF16)
    rank = _bdot(ltri_ref[...], onehot) + carry_ref[0:1, :] + crow_ref[...]
    segsum = _bdot(ssel_ref[...], onehot)
    seg_ref[0] = segsum
    carry_ref[...] = carry_ref[...] + segsum[0:1, :]
    cnt_ref[0] = carry_ref[...]
    r1 = jnp.sum(jnp.where(lane == i1, rank, 0.0), axis=-1, keepdims=True)
    r2 = jnp.sum(jnp.where(lane == i2, rank, 0.0), axis=-1, keepdims=True)
    cols = ((i1 - EXPERT_LANE0).astype(F32), (i2 - EXPERT_LANE0).astype(F32), r1, r2,
            g_w * w1, g_w * w2)
    tile = jnp.zeros((rows, LANES), F32)
    for c, val in enumerate(cols):
        tile = jnp.where(lane == c, val, tile)
    rt_ref[...] = tile


def _dense(x, attn, pd, sga, sgp, mix_b, pool_scale, wa_b, wp_b, wo_b, gf, wr_hi, wr_lo, br,
           ltri, ssel, crow):
    n = x.shape[0]
    nb = n // ROW_BLOCK
    row = lambda w: pl.BlockSpec((ROW_BLOCK, w), lambda i: (i, 0))
    full = lambda a: pl.BlockSpec(a.shape, lambda i: (0,) * a.ndim)
    stat = pl.BlockSpec((1, 8, LANES), lambda i: (i, 0, 0))
    consts = (mix_b, pool_scale, wa_b, wp_b, wo_b, gf, wr_hi, wr_lo, br, ltri, ssel, crow)
    return pl.pallas_call(
        _dense_body,
        grid=(nb,),
        in_specs=[row(D_MODEL)] * 5 + [full(a) for a in consts],
        out_specs=(row(D_MODEL), row(D_MODEL), row(LANES), stat, stat),
        out_shape=(jax.ShapeDtypeStruct((n, D_MODEL), F32),
                   jax.ShapeDtypeStruct((n, D_MODEL), F32),
                   jax.ShapeDtypeStruct((n, LANES), F32),
                   jax.ShapeDtypeStruct((nb, 8, LANES), F32),
                   jax.ShapeDtypeStruct((nb, 8, LANES), F32)),
        scratch_shapes=[pltpu.VMEM((8, LANES), F32)],
        compiler_params=pltpu.CompilerParams(dimension_semantics=("arbitrary",),
                                             vmem_limit_bytes=VMEM_LIMIT),
        name="dense",
    )(x, attn, pd, sga, sgp, *consts)


def _experts_body(nblk_ref, blk0_ref, cnt_ref, slots_ref, wts_ref, hn_p, hn_s, x1_p, x1_s,
                  wg_ref, wu_ref, wd_ref, y_p, y_s,
                  hn_t, acc, xg0, xg1, yb0, yb1, tok, sem_in, sem_out):
    tau = pl.program_id(0)
    e = pl.program_id(1)
    t_rows = TILE_TOKENS
    dummy = 2 * t_rows
    p_rows = pl.ds(pl.multiple_of(tau * TILE_PROMPT, TILE_PROMPT), TILE_PROMPT)
    s_rows = pl.ds(pl.multiple_of(tau * TILE_SAMPLE, TILE_SAMPLE), TILE_SAMPLE)
    tile_p = pl.ds(0, TILE_PROMPT)
    tile_s = pl.ds(TILE_PROMPT, TILE_SAMPLE)

    def in_copies():
        return [pltpu.make_async_copy(hn_p.at[p_rows], hn_t.at[tile_p], sem_in.at[0]),
                pltpu.make_async_copy(hn_s.at[s_rows], hn_t.at[tile_s], sem_in.at[1]),
                pltpu.make_async_copy(x1_p.at[p_rows], acc.at[tile_p], sem_in.at[2]),
                pltpu.make_async_copy(x1_s.at[s_rows], acc.at[tile_s], sem_in.at[3])]

    def out_copies():
        return [pltpu.make_async_copy(acc.at[tile_p], y_p.at[p_rows], sem_out.at[0]),
                pltpu.make_async_copy(acc.at[tile_s], y_s.at[s_rows], sem_out.at[1])]

    xbufs = (xg0, xg1)
    ybufs = (yb0, yb1)
    end_block = blk0_ref[tau, N_EXPERTS - 1] + nblk_ref[tau, N_EXPERTS - 1]
    end_slot = end_block * MOE_ROWS

    def gather_block(b, xdst):
        base = b * MOE_ROWS
        for j in range(MOE_ROWS):
            t = lax.shift_right_logical(tok[base + j], 1)
            xdst[j // 8, pl.ds(j % 8, 1), :] = hn_t[pl.ds(t, 1), :]

    def scatter_block(b, ysrc):
        base = b * MOE_ROWS
        for j0 in range(0, MOE_ROWS, SCATTER_BATCH):
            ents = [tok[base + j0 + i] for i in range(SCATTER_BATCH)]
            rows = [lax.shift_right_logical(en, 1) for en in ents]
            vals = [acc[pl.ds(rows[i], 1), :]
                    + wts_ref[0, 0, ents[i]] * ysrc[(j0 + i) // 8, pl.ds((j0 + i) % 8, 1), :]
                    for i in range(SCATTER_BATCH)]
            for i in range(SCATTER_BATCH):
                acc[pl.ds(rows[i], 1), :] = vals[i]

    @pl.when(e == 0)
    def _load_tile():
        for cp in in_copies():
            cp.start()

        hn_t[t_rows:, :] = jnp.zeros((DUMMY_ROWS, D_MODEL), F32)
        yb0[...] = jnp.zeros_like(yb0)

        def pad_range(lo, hi):
            def fill(p, c):
                tok[p] = dummy
                return c
            lax.fori_loop(lo, hi, fill, 0)
        pad_range(0, MOE_ROWS)
        pad_range(end_slot, end_slot + MOE_ROWS)

        def pad_fill(ei, c):
            s0 = blk0_ref[tau, ei] * MOE_ROWS
            pad_range(s0 + cnt_ref[tau, ei], s0 + nblk_ref[tau, ei] * MOE_ROWS)
            return c
        lax.fori_loop(0, N_EXPERTS, pad_fill, 0)

        def invert(g, c):
            t0 = g * 8
            for i in range(8):
                for k in range(2):
                    tok[slots_ref[0, k, t0 + i]] = 2 * (t0 + i) + k
            return c
        lax.fori_loop(0, t_rows // 8, invert, 0)

        for cp in in_copies():
            cp.wait()
        acc[t_rows:, :] = jnp.zeros((DUMMY_ROWS, D_MODEL), F32)
        gather_block(1, xbufs[1])

    first_block = blk0_ref[tau, e]

    def block(jb, c):
        b = first_block + jb
        for par in range(2):
            @pl.when((b & 1) == par)
            def _():
                gather_block(b + 1, xbufs[1 - par])
                xb = xbufs[par][...].reshape(MOE_ROWS, D_MODEL).astype(BF16)
                act = (jax.nn.silu(_bdot(xb, wg_ref[0])) * _bdot(xb, wu_ref[0])).astype(BF16)
                ybufs[par][...] = _bdot(act, wd_ref[0]).reshape(MOE_ROWS // 8, 8, D_MODEL)
                scatter_block(b - 1, ybufs[1 - par])
        return c
    lax.fori_loop(0, nblk_ref[tau, e], block, 0)

    @pl.when(e == N_EXPERTS - 1)
    def _store_tile():
        last = end_block - 1
        for par in range(2):
            @pl.when((last & 1) == par)
            def _():
                scatter_block(last, ybufs[par])
        cps = out_copies()
        for cp in cps:
            cp.start()
        for cp in cps:
            cp.wait()


def _experts(nblk, blk0, cnt, slots, wts, hn_p, hn_s, x1_p, x1_s, w_gate, w_up, w_down):
    any_spec = pl.BlockSpec(memory_space=pl.ANY)
    wspec = lambda a: pl.BlockSpec((1,) + a.shape[1:], lambda t, e, *_: (e, 0, 0))
    smem = lambda a: pl.BlockSpec((1,) + a.shape[1:], lambda t, e, *_: (t, 0, 0),
                                  memory_space=pltpu.SMEM)
    grid_spec = pltpu.PrefetchScalarGridSpec(
        num_scalar_prefetch=3,
        grid=(N_TILES, N_EXPERTS),
        in_specs=[smem(slots), smem(wts), any_spec, any_spec, any_spec, any_spec,
                  wspec(w_gate), wspec(w_up), wspec(w_down)],
        out_specs=(any_spec, any_spec),
        scratch_shapes=[
            pltpu.VMEM((TILE_TOKENS + DUMMY_ROWS, D_MODEL), F32),
            pltpu.VMEM((TILE_TOKENS + DUMMY_ROWS, D_MODEL), F32),
            pltpu.VMEM((MOE_ROWS // 8, 8, D_MODEL), F32),
            pltpu.VMEM((MOE_ROWS // 8, 8, D_MODEL), F32),
            pltpu.VMEM((MOE_ROWS // 8, 8, D_MODEL), F32),
            pltpu.VMEM((MOE_ROWS // 8, 8, D_MODEL), F32),
            pltpu.SMEM((LIST_CAP,), jnp.int32),
            pltpu.SemaphoreType.DMA((4,)),
            pltpu.SemaphoreType.DMA((2,)),
        ],
    )
    return pl.pallas_call(
        _experts_body,
        grid_spec=grid_spec,
        out_shape=(jax.ShapeDtypeStruct(x1_p.shape, F32), jax.ShapeDtypeStruct(x1_s.shape, F32)),
        compiler_params=pltpu.CompilerParams(dimension_semantics=("arbitrary", "arbitrary"),
                                             vmem_limit_bytes=EXPERTS_VMEM_LIMIT),
        name="experts",
    )(nblk, blk0, cnt, slots, wts, hn_p, hn_s, x1_p, x1_s, w_gate, w_up, w_down)


def _rope_tables(pos):
    half = HEAD_DIM // 2
    inv = ROPE_THETA ** (-jnp.arange(half, dtype=F32) * (2.0 / HEAD_DIM))
    ang = pos.astype(F32)[:, None] * inv[None, :]
    cos = jnp.tile(jnp.cos(ang), (1, LANES // half))
    sin = jnp.sin(ang)
    sin = jnp.tile(jnp.concatenate([-sin, sin], axis=1), (1, LANES // HEAD_DIM))
    return cos, sin


def _split_bf16(w):
    hi = w.astype(BF16)
    return hi, (w - hi.astype(F32)).astype(BF16)


def kernel(x_prompt, x_sample, cache_k, cache_v, state_pool, norm_mix_g, w_in, q_norm_g, k_norm_g,
           attn_sinks, w_attn_branch, pool_mix_w, pool_scale, w_pool_branch, w_out, norm_ffn_g,
           w_route_group, b_route_group, w_route_expert, b_route_expert, w_expert_gate,
           w_expert_up, w_expert_down):
    batch, seq, d = x_prompt.shape
    dec_batch, dec_seq, _ = x_sample.shape
    past_len = 16384
    assert x_prompt.shape == (4, 4096, D_MODEL) and x_sample.shape == (128, 4, D_MODEL)
    assert w_in.shape[0] == 1, "single layer"
    n_p, n_s = batch * seq, dec_batch * dec_seq

    g_mix = norm_mix_g[0][None, :]
    w_in_b = w_in[0].astype(BF16)
    gq = jnp.tile(q_norm_g[0], LANES // HEAD_DIM)[None, :]
    gk = jnp.tile(k_norm_g[0], LANES // HEAD_DIM)[None, :]
    sinks = attn_sinks[0]
    mix_b = pool_mix_w[0].astype(BF16)
    ps = pool_scale[0][None, :]
    wa_b = w_attn_branch[0].astype(BF16)
    wp_b = w_pool_branch[0].astype(BF16)
    wo_b = w_out[0].astype(BF16)
    gf = norm_ffn_g[0][None, :]
    wr = jnp.zeros((D_MODEL, LANES), F32)
    wr = wr.at[:, :N_EXPERT_GROUPS].set(w_route_group[0])
    wr = wr.at[:, EXPERT_LANE0:EXPERT_LANE0 + N_EXPERTS].set(w_route_expert[0])
    wr_hi, wr_lo = _split_bf16(wr)
    br = jnp.zeros((1, LANES), F32)
    br = br.at[0, :N_EXPERT_GROUPS].set(b_route_group[0])
    br = br.at[0, EXPERT_LANE0:EXPERT_LANE0 + N_EXPERTS].set(b_route_expert[0])

    cos_p, sin_p = _rope_tables(jnp.arange(seq, dtype=jnp.int32))
    pos_s = past_len + (jnp.arange(n_s, dtype=jnp.int32) % dec_seq)
    cos_s, sin_s = _rope_tables(pos_s)

    dense_consts = (mix_b, ps, wa_b, wp_b, wo_b, gf, wr_hi, wr_lo, br)
    assert n_s == ROW_BLOCK == N_TILES * TILE_SAMPLE and n_p == N_TILES * TILE_PROMPT
    ridx = jnp.arange(ROW_BLOCK, dtype=jnp.int32)
    lower = ridx[:, None] > ridx[None, :]
    seg_of = ridx // TILE_SAMPLE
    ltri_p = lower.astype(BF16)
    ltri_s = (lower & (seg_of[:, None] == seg_of[None, :])).astype(BF16)
    ssel_p = (jnp.arange(8, dtype=jnp.int32)[:, None] == 0) & (ridx[None, :] >= 0)
    ssel_s = jnp.arange(8, dtype=jnp.int32)[:, None] == seg_of[None, :]

    xp = x_prompt.reshape(n_p, d)
    q, k, v, kk, vv, u, sga, sgp = _inproj(xp, g_mix, w_in_b, cos_p, sin_p, gq, gk)
    attn, pd, wg_b, wu_b, wd_b = _mixer_prompt(sinks, q, kk, vv, u, w_expert_gate[0], w_expert_up[0],
                                               w_expert_down[0], batch, seq)
    x1_p, hn_p, rt_p, cnt_p, _ = _dense(xp, attn, pd, sga, sgp, *dense_consts, ltri_p,
                                        ssel_p.astype(BF16), jnp.zeros((ROW_BLOCK, LANES), F32))
    new_k_p = k.reshape(batch, seq, N_KV_HEADS, HEAD_DIM)[:, -WINDOW:][None]
    new_v_p = v.reshape(batch, seq, N_KV_HEADS, HEAD_DIM)[:, -WINDOW:][None]
    new_u_p = u.reshape(batch, seq, d)[:, -POOL_STATE_LEN:][None]

    xs = x_sample.reshape(n_s, d)
    q, k, v, kk, vv, u, sga, sgp = _inproj(xs, g_mix, w_in_b, cos_s, sin_s, gq, gk)
    ck = cache_k[0].reshape(dec_batch, WINDOW, KV_W)
    cv = cache_v[0].reshape(dec_batch, WINDOW, KV_W)
    attn, pd, nk_s, nv_s, nu_s = _mixer_sample(sinks, q.astype(F32), k, v, u, ck, cv, state_pool[0])
    cnt_tiles_p = cnt_p[TILE_BLOCKS - 1::TILE_BLOCKS, 0, :]
    crow = jnp.repeat(cnt_tiles_p, TILE_SAMPLE, axis=0)
    x1_s, hn_s, rt_s, _, seg_s = _dense(xs, attn, pd, sga, sgp, *dense_consts, ltri_s,
                                        ssel_s.astype(BF16), crow)
    new_k_s = nk_s.reshape(1, dec_batch, WINDOW, N_KV_HEADS, HEAD_DIM)
    new_v_s = nv_s.reshape(1, dec_batch, WINDOW, N_KV_HEADS, HEAD_DIM)
    new_u_s = nu_s[None]

    ex = slice(EXPERT_LANE0, EXPERT_LANE0 + N_EXPERTS)
    cnt = (cnt_tiles_p[:, ex] + seg_s[0, :N_TILES, ex]).astype(jnp.int32)
    nblk = (cnt + (MOE_ROWS - 1)) // MOE_ROWS
    blk0 = 1 + jnp.cumsum(nblk, axis=1) - nblk
    seg = MOE_ROWS * blk0
    rt = jnp.concatenate([rt_p[:, :6].reshape(N_TILES, TILE_PROMPT, 6),
                          rt_s[:, :6].reshape(N_TILES, TILE_SAMPLE, 6)], axis=1)
    expert_hit = rt[:, :, 0:2].astype(jnp.int32)[..., None] == jnp.arange(N_EXPERTS, dtype=jnp.int32)
    slot = (rt[:, :, 2:4].astype(jnp.int32)
            + jnp.sum(jnp.where(expert_hit, seg[:, None, None, :], 0), axis=-1))
    slots = slot.transpose(0, 2, 1)
    wts = rt[:, :, 4:6].reshape(N_TILES, 1, 2 * TILE_TOKENS)
    wts = jnp.pad(wts, ((0, 0), (0, 0), (0, 8)))
    y_p, y_s = _experts(nblk, blk0, cnt, slots, wts, hn_p, hn_s, x1_p, x1_s, wg_b, wu_b, wd_b)
    y_prompt = y_p.reshape(batch, seq, d)
    y_sample = y_s.reshape(dec_batch, dec_seq, d)

    return (y_prompt, y_sample, new_k_p, new_v_p, new_u_p, new_k_s, new_v_s, new_u_s)
```

```python
import functools

import jax
import jax.numpy as jnp
from jax import lax
from jax.experimental import pallas as pl
from jax.experimental.pallas import tpu as pltpu

F32 = jnp.float32
BF16 = jnp.bfloat16

D_MODEL = 1024
HEAD_DIM = 64
N_Q_HEADS = 16
N_KV_HEADS = 2
WINDOW = 128
ROPE_THETA = 10000.0
POOL_WINDOWS = (2, 4, 8, 16)
POOL_GROUP_WIDTH = D_MODEL // len(POOL_WINDOWS)
POOL_STATE_LEN = max(POOL_WINDOWS) - 1
N_EXPERT_GROUPS = 4
EXPERTS_PER_GROUP = 8
N_EXPERTS = N_EXPERT_GROUPS * EXPERTS_PER_GROUP
D_EXPERT = 512
RMS_EPS = 1e-6
Q_W = N_Q_HEADS * HEAD_DIM
KV_W = N_KV_HEADS * HEAD_DIM
OFF_K = Q_W
OFF_V = OFF_K + KV_W
OFF_U = OFF_V + KV_W
OFF_GA = OFF_U + D_MODEL
OFF_GP = OFF_GA + D_MODEL
IN_W = OFF_GP + D_MODEL

LANES = 128
ROW_BLOCK = 512
EXPERT_LANE0 = 32
VMEM_LIMIT = 56 * 1024 * 1024
EXPERTS_VMEM_LIMIT = 60 * 1024 * 1024
NEG_INF = float("-inf")

N_TILES = 4
TILE_BLOCKS = 8
TILE_PROMPT = TILE_BLOCKS * ROW_BLOCK
TILE_SAMPLE = 128
TILE_TOKENS = TILE_PROMPT + TILE_SAMPLE
MOE_ROWS = 320
DUMMY_ROWS = 8
SCATTER_BATCH = 4
TILE_LIST = -(-(2 * TILE_TOKENS + 1) // 1024) * 1024
LIST_CAP = 2 * TILE_TOKENS + N_EXPERTS * (MOE_ROWS - 1) + 2 * MOE_ROWS + 8


def _rms(x, g):
    return x * lax.rsqrt(jnp.mean(x * x, axis=-1, keepdims=True) + RMS_EPS) * g


def _bdot(a, b):
    return jnp.dot(a, b, preferred_element_type=F32)


def _inproj_body(x_ref, g_ref, w_ref, cos_ref, sin_ref, gq_ref, gk_ref,
                 q_ref, k_ref, v_ref, kk_ref, vv_ref, u_ref, sga_ref, sgp_ref):
    rows = x_ref.shape[0]
    hb = _rms(x_ref[...], g_ref[...]).astype(BF16)
    cos = cos_ref[...]
    sin = sin_ref[...]
    lane = lax.broadcasted_iota(jnp.int32, (rows, LANES), 1)
    lo = lane < HEAD_DIM
    first_half = (lane % HEAD_DIM) < (HEAD_DIM // 2)

    def head_norm_rope(zc, gain):
        sq = zc * zc
        ss_lo = jnp.sum(jnp.where(lo, sq, 0.0), axis=-1, keepdims=True)
        ss_hi = jnp.sum(jnp.where(lo, 0.0, sq), axis=-1, keepdims=True)
        r = lax.rsqrt(jnp.where(lo, ss_lo, ss_hi) * (1.0 / HEAD_DIM) + RMS_EPS)
        y = zc * r * gain
        partner = jnp.where(first_half, pltpu.roll(y, LANES - HEAD_DIM // 2, 1),
                            pltpu.roll(y, HEAD_DIM // 2, 1))
        return y * cos + partner * sin

    gq = gq_ref[...]
    for j in range(Q_W // 256):
        z = _bdot(hb, w_ref[:, 256 * j:256 * (j + 1)])
        for c in range(2):
            qn = head_norm_rope(z[:, LANES * c:LANES * (c + 1)], gq) * (HEAD_DIM ** -0.5)
            q_ref[:, 256 * j + LANES * c:256 * j + LANES * (c + 1)] = qn.astype(BF16)

    z = _bdot(hb, w_ref[:, OFF_K:OFF_U])
    kn = head_norm_rope(z[:, :KV_W], gk_ref[...])
    vr = z[:, KV_W:]
    k_ref[...] = kn
    v_ref[...] = vr
    kr = pltpu.roll(kn, HEAD_DIM, 1)
    vrr = pltpu.roll(vr, HEAD_DIM, 1)
    kk_ref[:, :LANES] = jnp.where(lo, kn, kr).astype(BF16)
    kk_ref[:, LANES:] = jnp.where(lo, kr, kn).astype(BF16)
    vv_ref[:, :LANES] = jnp.where(lo, vr, vrr).astype(BF16)
    vv_ref[:, LANES:] = jnp.where(lo, vrr, vr).astype(BF16)

    for j in range(D_MODEL // 256):
        u_ref[:, 256 * j:256 * (j + 1)] = _bdot(hb, w_ref[:, OFF_U + 256 * j:OFF_U + 256 * (j + 1)])
        sga_ref[:, 256 * j:256 * (j + 1)] = jax.nn.sigmoid(
            _bdot(hb, w_ref[:, OFF_GA + 256 * j:OFF_GA + 256 * (j + 1)]))
        sgp_ref[:, 256 * j:256 * (j + 1)] = jax.nn.sigmoid(
            _bdot(hb, w_ref[:, OFF_GP + 256 * j:OFF_GP + 256 * (j + 1)]))


def _inproj(x, g, w_in_b, cos, sin, gq, gk):
    n = x.shape[0]
    nb = n // ROW_BLOCK
    ncos = cos.shape[0] // ROW_BLOCK
    row = lambda w: pl.BlockSpec((ROW_BLOCK, w), lambda i: (i, 0))
    full = lambda a: pl.BlockSpec(a.shape, lambda i: (0,) * a.ndim)
    tab = pl.BlockSpec((ROW_BLOCK, LANES), lambda i: (i % ncos, 0))
    out_shapes = (
        jax.ShapeDtypeStruct((n, Q_W), BF16),
        jax.ShapeDtypeStruct((n, KV_W), F32),
        jax.ShapeDtypeStruct((n, KV_W), F32),
        jax.ShapeDtypeStruct((n, 2 * LANES), BF16),
        jax.ShapeDtypeStruct((n, 2 * LANES), BF16),
        jax.ShapeDtypeStruct((n, D_MODEL), F32),
        jax.ShapeDtypeStruct((n, D_MODEL), F32),
        jax.ShapeDtypeStruct((n, D_MODEL), F32),
    )
    return pl.pallas_call(
        _inproj_body,
        grid=(nb,),
        in_specs=[row(D_MODEL), full(g), full(w_in_b), tab, tab, full(gq), full(gk)],
        out_specs=tuple(row(s.shape[1]) for s in out_shapes),
        out_shape=out_shapes,
        compiler_params=pltpu.CompilerParams(dimension_semantics=("arbitrary",),
                                             vmem_limit_bytes=VMEM_LIMIT),
        name="inproj",
    )(x, g, w_in_b, cos, sin, gq, gk)


def _attend(q2, kg, vg, cur_valid, sink_col, prev_dead):
    s = lax.dot_general(q2, kg, (((1,), (1,)), ((), ())), preferred_element_type=F32)
    s_prev = s[:, :WINDOW]
    if prev_dead is not None:
        s_prev = jnp.where(prev_dead, NEG_INF, s_prev)
    sc = jnp.where(cur_valid, s[:, WINDOW:], s_prev)
    m = jnp.maximum(jnp.max(sc, axis=-1, keepdims=True), sink_col)
    p = jnp.exp(sc - m)
    den = jnp.sum(p, axis=-1, keepdims=True) + jnp.exp(sink_col - m)
    pn = p * (1.0 / den)
    p2 = jnp.concatenate([jnp.where(cur_valid, 0.0, pn), jnp.where(cur_valid, pn, 0.0)], axis=1)
    return _bdot(p2.astype(BF16), vg)


POOL_PAD_ROWS = 8


def _mixer_prompt_body(sink_ref, q_ref, kkc_ref, kkp_ref, vvc_ref, vvp_ref, uc_ref, up_ref,
                       wg_ref, wu_ref, wd_ref,
                       attn_ref, pd_ref, wgb_ref, wub_ref, wdb_ref, ext_ref, s_a, s_b):
    wgb_ref[...] = wg_ref[...].astype(BF16)
    wub_ref[...] = wu_ref[...].astype(BF16)
    wdb_ref[...] = wd_ref[...].astype(BF16)
    i = pl.program_id(1)
    first = i == 0
    m2 = 2 * WINDOW
    row = lax.broadcasted_iota(jnp.int32, (m2, LANES), 0)
    col = lax.broadcasted_iota(jnp.int32, (m2, LANES), 1)
    cur_valid = (row % WINDOW) >= col
    upper = lax.broadcasted_iota(jnp.int32, (m2, 1), 0) < WINDOW
    lo = lax.broadcasted_iota(jnp.int32, (WINDOW, LANES), 1) < HEAD_DIM

    for j in range(ROW_BLOCK // WINDOW):
        r0 = WINDOW * j
        if j == 0:
            kprev, vprev = kkp_ref[...], vvp_ref[...]
            prev_dead = first
        else:
            kprev, vprev = kkc_ref[r0 - WINDOW:r0, :], vvc_ref[r0 - WINDOW:r0, :]
            prev_dead = None
        kband = jnp.concatenate([kprev, kkc_ref[r0:r0 + WINDOW, :]], axis=0)
        vband = jnp.concatenate([vprev, vvc_ref[r0:r0 + WINDOW, :]], axis=0)
        for g in range(N_KV_HEADS):
            kg = kband[:, LANES * g:LANES * (g + 1)]
            vg = vband[:, LANES * g:LANES * (g + 1)]
            for c in range(4):
                cc = 4 * g + c
                qc = q_ref[r0:r0 + WINDOW, LANES * cc:LANES * (cc + 1)]
                zero = jnp.zeros_like(qc)
                q2 = jnp.concatenate([jnp.where(lo, qc, zero), jnp.where(lo, zero, qc)], axis=0)
                sink_col = jnp.where(upper, sink_ref[2 * cc], sink_ref[2 * cc + 1])
                o = _attend(q2, kg, vg, cur_valid, sink_col, prev_dead)
                attn_ref[r0:r0 + WINDOW, LANES * cc:LANES * (cc + 1)] = jnp.where(
                    lo, o[:WINDOW], o[WINDOW:]).astype(BF16)

    pad, top = POOL_PAD_ROWS, POOL_PAD_ROWS + 16
    end = top + ROW_BLOCK
    for ref in (ext_ref, s_a, s_b):
        ref[0:pad, :] = jnp.zeros((pad, D_MODEL), F32)
    ext_ref[pad:top, :] = jnp.where(first, 0.0, up_ref[...])
    ext_ref[top:, :] = uc_ref[...]
    pos = i * ROW_BLOCK + lax.broadcasted_iota(jnp.int32, (ROW_BLOCK, 1), 0)
    src = ext_ref
    for g, w in enumerate(POOL_WINDOWS):
        c0 = POOL_GROUP_WIDTH * g
        dst = (s_a, s_b)[g % 2]
        shift = w // 2
        dst[pad:end, c0:] = src[pad:end, c0:] + src[pad - shift:end - shift, c0:]
        c1 = c0 + POOL_GROUP_WIDTH
        cnt = jnp.minimum(w, pos + 1).astype(F32)
        pd_ref[:, c0:c1] = (dst[top:end, c0:c1] / cnt - uc_ref[:, c0:c1]).astype(BF16)
        src = dst


def _mixer_prompt(sinks, q, kk, vv, u, w_gate, w_up, w_down, batch, seq):
    nb = seq // ROW_BLOCK
    assert batch * nb == N_EXPERTS, "one expert's weights are cast per grid step"
    sub = ROW_BLOCK // WINDOW
    cur = lambda w: pl.BlockSpec((ROW_BLOCK, w), lambda b, i: (b * nb + i, 0))
    wspec = lambda a: pl.BlockSpec((1,) + a.shape[1:], lambda b, i: (b * nb + i, 0, 0))
    weights = (w_gate, w_up, w_down)
    prev_kv = pl.BlockSpec((WINDOW, 2 * LANES),
                           lambda b, i: (jnp.maximum((b * nb + i) * sub - 1, 0), 0))
    prev_u = pl.BlockSpec((16, D_MODEL),
                          lambda b, i: (jnp.maximum((b * nb + i) * (ROW_BLOCK // 16) - 1, 0), 0))
    n = batch * seq
    return pl.pallas_call(
        _mixer_prompt_body,
        grid=(batch, nb),
        in_specs=[pl.BlockSpec(memory_space=pltpu.SMEM), cur(Q_W), cur(2 * LANES), prev_kv,
                  cur(2 * LANES), prev_kv, cur(D_MODEL), prev_u] + [wspec(a) for a in weights],
        out_specs=(cur(Q_W), cur(D_MODEL)) + tuple(wspec(a) for a in weights),
        out_shape=(jax.ShapeDtypeStruct((n, Q_W), BF16), jax.ShapeDtypeStruct((n, D_MODEL), BF16))
        + tuple(jax.ShapeDtypeStruct(a.shape, BF16) for a in weights),
        scratch_shapes=[pltpu.VMEM((POOL_PAD_ROWS + 16 + ROW_BLOCK, D_MODEL), F32)] * 3,
        compiler_params=pltpu.CompilerParams(dimension_semantics=("arbitrary", "arbitrary"),
                                             vmem_limit_bytes=VMEM_LIMIT),
        name="mixer_prompt",
    )(sinks, q, kk, kk, vv, vv, u, u, *weights)


SEQ_PAIR_ROWS = 8
SAMPLE_PAIRS_PER_STEP = 8


def _mixer_sample_body(sink_ref, q_ref, k_ref, v_ref, u_ref, ck_ref, cv_ref, st_ref,
                       attn_ref, pd_ref, nk_ref, nv_ref, nu_ref, ext_ref):
    m = SEQ_PAIR_ROWS
    half = m // 2
    hist = POOL_STATE_LEN
    ext_ref[16:, :] = jnp.zeros((ext_ref.shape[0] - 16, D_MODEL), F32)
    row8 = lax.broadcasted_iota(jnp.int32, (m, LANES), 0)
    n_chunks = Q_W // LANES
    mq = n_chunks * m
    row1 = lax.broadcasted_iota(jnp.int32, (m, 1), 0)
    row = lax.broadcasted_iota(jnp.int32, (mq, LANES), 0)
    col = lax.broadcasted_iota(jnp.int32, (mq, LANES), 1)
    cur_valid = (row % half) >= col
    lane8 = lax.broadcasted_iota(jnp.int32, (m, LANES), 1)
    lo8 = lane8 < HEAD_DIM
    top8 = lax.broadcasted_iota(jnp.int32, (m, LANES), 0) < half
    tail = jnp.zeros((WINDOW - m, LANES), F32)
    top = row1 < half
    sink_col = jnp.concatenate([jnp.where(top, sink_ref[2 * cc], sink_ref[2 * cc + 1])
                                for cc in range(n_chunks)], axis=0)

    def pair(p, carry):
        r0 = pl.multiple_of(p * m, m)
        q8 = q_ref[pl.ds(r0, m), :]
        k8 = k_ref[pl.ds(r0, m), :]
        v8 = v_ref[pl.ds(r0, m), :]
        u8 = u_ref[pl.ds(r0, m), :]
        attn_parts = []
        pd_parts = []
        for s in range(2):
            shift = lambda a: a if s == 0 else pltpu.roll(a, half, 0)
            seq = 2 * p + s
            ck, cv = ck_ref[seq], cv_ref[seq]
            ks, vs = shift(k8), shift(v8)
            kb = jnp.concatenate([ck, ks, tail], axis=0)
            vb = jnp.concatenate([cv, vs, tail], axis=0)
            for cache, new, out in ((ck, ks, nk_ref), (cv, vs, nv_ref)):
                up = pltpu.roll(cache, WINDOW - half, 0)
                out[seq, 0:WINDOW - m, :] = up[0:WINDOW - m]
                out[seq, WINDOW - m:WINDOW, :] = jnp.where(row8 < half, up[WINDOW - m:],
                                                           pltpu.roll(new, half, 0))
            qs = shift(q8)
            qr = pltpu.roll(qs, half, 0)
            q2 = []
            for cc in range(n_chunks):
                qa = qs[:, LANES * cc:LANES * (cc + 1)]
                qb = qr[:, LANES * cc:LANES * (cc + 1)]
                if cc < n_chunks // N_KV_HEADS:
                    first = jnp.where(lo8, qa, 0.0)
                    second = jnp.where(lo8, pltpu.roll(qb, HEAD_DIM, 1), 0.0)
                else:
                    first = jnp.where(lo8, 0.0, pltpu.roll(qa, HEAD_DIM, 1))
                    second = jnp.where(lo8, 0.0, qb)
                q2.append(jnp.where(top8, first, second))
            q2 = jnp.concatenate(q2, axis=0).astype(BF16)
            o = _attend(q2, kb.astype(BF16), vb.astype(BF16), cur_valid, sink_col, None)
            chunks = []
            for cc in range(n_chunks):
                oc = o[m * cc:m * (cc + 1)]
                if cc < n_chunks // N_KV_HEADS:
                    chunks.append(jnp.where(lo8, oc, pltpu.roll(pltpu.roll(oc, half, 0), HEAD_DIM, 1)))
                else:
                    chunks.append(jnp.where(lo8, pltpu.roll(oc, HEAD_DIM, 1), pltpu.roll(oc, half, 0)))
            attn_parts.append(jnp.concatenate(chunks, axis=1))

            ext_ref[0:hist, :] = st_ref[seq]
            ext_ref[hist:hist + half, :] = shift(u8)[0:half]
            nu_ref[seq] = ext_ref[half:hist + half, :]
            cols = []
            for g, w in enumerate(POOL_WINDOWS):
                c0, c1 = POOL_GROUP_WIDTH * g, POOL_GROUP_WIDTH * (g + 1)
                acc = ext_ref[hist:hist + m, c0:c1]
                for k in range(1, w):
                    acc = acc + ext_ref[hist - k:hist - k + m, c0:c1]
                cols.append(acc / float(w) - ext_ref[hist:hist + m, c0:c1])
            pd_parts.append(jnp.concatenate(cols, axis=1))

        attn_ref[pl.ds(r0, m), :] = jnp.where(top, attn_parts[0], pltpu.roll(attn_parts[1], half, 0))
        pd_ref[pl.ds(r0, m), :] = jnp.where(top, pd_parts[0], pltpu.roll(pd_parts[1], half, 0))
        return carry

    lax.fori_loop(0, SAMPLE_PAIRS_PER_STEP, pair, 0)


def _mixer_sample(sinks, q, k, v, u, cache_k, cache_v, state):
    n = q.shape[0]
    m = SEQ_PAIR_ROWS * SAMPLE_PAIRS_PER_STEP
    rows = lambda w: pl.BlockSpec((m, w), lambda i: (i, 0))
    seqs = lambda a: pl.BlockSpec((2 * SAMPLE_PAIRS_PER_STEP,) + a.shape[1:], lambda i: (i, 0, 0))
    like = lambda a: jax.ShapeDtypeStruct(a.shape, F32)
    return pl.pallas_call(
        _mixer_sample_body,
        grid=(n // m,),
        in_specs=[pl.BlockSpec(memory_space=pltpu.SMEM), rows(Q_W), rows(KV_W), rows(KV_W),
                  rows(D_MODEL), seqs(cache_k), seqs(cache_v), seqs(state)],
        out_specs=(rows(Q_W), rows(D_MODEL), seqs(cache_k), seqs(cache_v), seqs(state)),
        out_shape=(jax.ShapeDtypeStruct((n, Q_W), F32), jax.ShapeDtypeStruct((n, D_MODEL), F32),
                   like(cache_k), like(cache_v), like(state)),
        scratch_shapes=[pltpu.VMEM((16 + SEQ_PAIR_ROWS, D_MODEL), F32)],
        compiler_params=pltpu.CompilerParams(dimension_semantics=("arbitrary",),
                                             vmem_limit_bytes=VMEM_LIMIT),
        name="mixer_sample",
    )(sinks, q, k, v, u, cache_k, cache_v, state)


def _dense_body(x_ref, attn_ref, pd_ref, sga_ref, sgp_ref, mix_ref, ps_ref, wa_ref, wp_ref,
                wo_ref, gf_ref, wrh_ref, wrl_ref, br_ref, ltri_ref, ssel_ref, crow_ref,
                x1_ref, hn_ref, rt_ref, cnt_ref, seg_ref, carry_ref):
    rows = x_ref.shape[0]
    step = pl.program_id(0)

    @pl.when(step % TILE_BLOCKS == 0)
    def _():
        carry_ref[...] = jnp.zeros_like(carry_ref)

    pd = pd_ref[...].astype(BF16)
    pooled = []
    for g in range(len(POOL_WINDOWS)):
        c0, c1 = POOL_GROUP_WIDTH * g, POOL_GROUP_WIDTH * (g + 1)
        pooled.append((_bdot(pd[:, c0:c1], mix_ref[g]) * ps_ref[:, c0:c1]).astype(BF16))
    pooled = jnp.concatenate(pooled, axis=1)
    merged = (sga_ref[...] * _bdot(attn_ref[...].astype(BF16), wa_ref[...])
              + sgp_ref[...] * _bdot(pooled, wp_ref[...]))
    x1 = x_ref[...] + _bdot(merged.astype(BF16), wo_ref[...])
    x1_ref[...] = x1
    hn = _rms(x1, gf_ref[...])
    hn_ref[...] = hn
    hi = hn.astype(BF16)
    lo = (hn - hi.astype(F32)).astype(BF16)
    logits = (_bdot(hi, wrh_ref[...]) + _bdot(hi, wrl_ref[...]) + _bdot(lo, wrh_ref[...])
              + br_ref[...])

    lane = lax.broadcasted_iota(jnp.int32, (rows, LANES), 1)
    big = jnp.int32(LANES)
    gl = jnp.where(lane < N_EXPERT_GROUPS, logits, NEG_INF)
    gmax = jnp.max(gl, axis=-1, keepdims=True)
    gidx = jnp.min(jnp.where(gl == gmax, lane, big), axis=-1, keepdims=True)
    g_w = 1.0 / jnp.sum(jnp.exp(gl - gmax), axis=-1, keepdims=True)
    e0 = EXPERT_LANE0 + gidx * EXPERTS_PER_GROUP
    el = jnp.where((lane >= e0) & (lane < e0 + EXPERTS_PER_GROUP), logits, NEG_INF)
    l1 = jnp.max(el, axis=-1, keepdims=True)
    i1 = jnp.min(jnp.where(el == l1, lane, big), axis=-1, keepdims=True)
    el2 = jnp.where(lane == i1, NEG_INF, el)
    l2 = jnp.max(el2, axis=-1, keepdims=True)
    i2 = jnp.min(jnp.where(el2 == l2, lane, big), axis=-1, keepdims=True)
    e = jnp.exp(l2 - l1)
    w1 = 1.0 / (1.0 + e)
    w2 = e * w1

    sel = (lane == i1) | (lane == i2)
    onehot = jnp.where(sel, 1.0, 0.0).astype(BF16)
    rank = _bdot(ltri_ref[...], onehot) + carry_ref[0:1, :] + crow_ref[...]
    segsum = _bdot(ssel_ref[...], onehot)
    seg_ref[0] = segsum
    carry_ref[...] = carry_ref[...] + segsum[0:1, :]
    cnt_ref[0] = carry_ref[...]
    r1 = jnp.sum(jnp.where(lane == i1, rank, 0.0), axis=-1, keepdims=True)
    r2 = jnp.sum(jnp.where(lane == i2, rank, 0.0), axis=-1, keepdims=True)
    cols = ((i1 - EXPERT_LANE0).astype(F32), (i2 - EXPERT_LANE0).astype(F32), r1, r2,
            g_w * w1, g_w * w2)
    tile = jnp.zeros((rows, LANES), F32)
    for c, val in enumerate(cols):
        tile = jnp.where(lane == c, val, tile)
    rt_ref[...] = tile


def _dense(x, attn, pd, sga, sgp, mix_b, pool_scale, wa_b, wp_b, wo_b, gf, wr_hi, wr_lo, br,
           ltri, ssel, crow):
    n = x.shape[0]
    nb = n // ROW_BLOCK
    row = lambda w: pl.BlockSpec((ROW_BLOCK, w), lambda i: (i, 0))
    full = lambda a: pl.BlockSpec(a.shape, lambda i: (0,) * a.ndim)
    stat = pl.BlockSpec((1, 8, LANES), lambda i: (i, 0, 0))
    consts = (mix_b, pool_scale, wa_b, wp_b, wo_b, gf, wr_hi, wr_lo, br, ltri, ssel, crow)
    return pl.pallas_call(
        _dense_body,
        grid=(nb,),
        in_specs=[row(D_MODEL)] * 5 + [full(a) for a in consts],
        out_specs=(row(D_MODEL), row(D_MODEL), row(LANES), stat, stat),
        out_shape=(jax.ShapeDtypeStruct((n, D_MODEL), F32),
                   jax.ShapeDtypeStruct((n, D_MODEL), F32),
                   jax.ShapeDtypeStruct((n, LANES), F32),
                   jax.ShapeDtypeStruct((nb, 8, LANES), F32),
                   jax.ShapeDtypeStruct((nb, 8, LANES), F32)),
        scratch_shapes=[pltpu.VMEM((8, LANES), F32)],
        compiler_params=pltpu.CompilerParams(dimension_semantics=("arbitrary",),
                                             vmem_limit_bytes=VMEM_LIMIT),
        name="dense",
    )(x, attn, pd, sga, sgp, *consts)


def _experts_body(nblk_ref, blk0_ref, cnt_ref, slots_ref, wts_ref, hn_p, hn_s, x1_p, x1_s,
                  wg_ref, wu_ref, wd_ref, y_p, y_s,
                  hn_t, acc, xg0, xg1, yb0, yb1, tok, sem_in, sem_out):
    tau = pl.program_id(0)
    e = pl.program_id(1)
    t_rows = TILE_TOKENS
    dummy = 2 * t_rows
    p_rows = pl.ds(pl.multiple_of(tau * TILE_PROMPT, TILE_PROMPT), TILE_PROMPT)
    s_rows = pl.ds(pl.multiple_of(tau * TILE_SAMPLE, TILE_SAMPLE), TILE_SAMPLE)
    tile_p = pl.ds(0, TILE_PROMPT)
    tile_s = pl.ds(TILE_PROMPT, TILE_SAMPLE)

    def in_copies():
        return [pltpu.make_async_copy(hn_p.at[p_rows], hn_t.at[tile_p], sem_in.at[0]),
                pltpu.make_async_copy(hn_s.at[s_rows], hn_t.at[tile_s], sem_in.at[1]),
                pltpu.make_async_copy(x1_p.at[p_rows], acc.at[tile_p], sem_in.at[2]),
                pltpu.make_async_copy(x1_s.at[s_rows], acc.at[tile_s], sem_in.at[3])]

    def out_copies():
        return [pltpu.make_async_copy(acc.at[tile_p], y_p.at[p_rows], sem_out.at[0]),
                pltpu.make_async_copy(acc.at[tile_s], y_s.at[s_rows], sem_out.at[1])]

    xbufs = (xg0, xg1)
    ybufs = (yb0, yb1)
    end_block = blk0_ref[tau, N_EXPERTS - 1] + nblk_ref[tau, N_EXPERTS - 1]
    end_slot = end_block * MOE_ROWS

    def gather_block(b, xdst):
        base = b * MOE_ROWS
        for j in range(MOE_ROWS):
            t = lax.shift_right_logical(tok[base + j], 1)
            xdst[j // 8, pl.ds(j % 8, 1), :] = hn_t[pl.ds(t, 1), :]

    def scatter_block(b, ysrc):
        base = b * MOE_ROWS
        for j0 in range(0, MOE_ROWS, SCATTER_BATCH):
            ents = [tok[base + j0 + i] for i in range(SCATTER_BATCH)]
            rows = [lax.shift_right_logical(en, 1) for en in ents]
            vals = [acc[pl.ds(rows[i], 1), :]
                    + wts_ref[ents[i]] * ysrc[(j0 + i) // 8, pl.ds((j0 + i) % 8, 1), :]
                    for i in range(SCATTER_BATCH)]
            for i in range(SCATTER_BATCH):
                acc[pl.ds(rows[i], 1), :] = vals[i]

    @pl.when(e == 0)
    def _load_tile():
        for cp in in_copies():
            cp.start()

        hn_t[t_rows:, :] = jnp.zeros((DUMMY_ROWS, D_MODEL), F32)
        yb0[...] = jnp.zeros_like(yb0)

        def pad_range(lo, hi):
            def fill(g, c):
                for i in range(8):
                    tok[lo + g * 8 + i] = dummy
                return c
            lax.fori_loop(0, lax.shift_right_logical(hi - lo + 7, 3), fill, 0)
        pad_range(0, MOE_ROWS)
        pad_range(end_slot, end_slot + MOE_ROWS)

        def pad_fill(ei, c):
            s0 = blk0_ref[tau, ei] * MOE_ROWS
            pad_range(s0 + cnt_ref[tau, ei], s0 + nblk_ref[tau, ei] * MOE_ROWS)
            return c
        lax.fori_loop(0, N_EXPERTS, pad_fill, 0)

        def invert(g, c):
            a0 = g * 16
            for i in range(16):
                tok[slots_ref[a0 + i]] = a0 + i
            return c
        lax.fori_loop(0, t_rows // 8, invert, 0)

        for cp in in_copies():
            cp.wait()
        acc[t_rows:, :] = jnp.zeros((DUMMY_ROWS, D_MODEL), F32)
        gather_block(1, xbufs[1])

    first_block = blk0_ref[tau, e]

    def block(jb, c):
        b = first_block + jb
        for par in range(2):
            @pl.when((b & 1) == par)
            def _():
                gather_block(b + 1, xbufs[1 - par])
                xb = xbufs[par][...].reshape(MOE_ROWS, D_MODEL).astype(BF16)
                act = (jax.nn.silu(_bdot(xb, wg_ref[0])) * _bdot(xb, wu_ref[0])).astype(BF16)
                ybufs[par][...] = _bdot(act, wd_ref[0]).reshape(MOE_ROWS // 8, 8, D_MODEL)
                scatter_block(b - 1, ybufs[1 - par])
        return c
    lax.fori_loop(0, nblk_ref[tau, e], block, 0)

    @pl.when(e == N_EXPERTS - 1)
    def _store_tile():
        last = end_block - 1
        for par in range(2):
            @pl.when((last & 1) == par)
            def _():
                scatter_block(last, ybufs[par])
        cps = out_copies()
        for cp in cps:
            cp.start()
        for cp in cps:
            cp.wait()


def _experts(nblk, blk0, cnt, slots, wts, hn_p, hn_s, x1_p, x1_s, w_gate, w_up, w_down):
    any_spec = pl.BlockSpec(memory_space=pl.ANY)
    wspec = lambda a: pl.BlockSpec((1,) + a.shape[1:], lambda t, e, *_: (e, 0, 0))
    smem = lambda a: pl.BlockSpec((a.shape[0] // N_TILES,), lambda t, e, *_: (t,),
                                  memory_space=pltpu.SMEM)
    grid_spec = pltpu.PrefetchScalarGridSpec(
        num_scalar_prefetch=3,
        grid=(N_TILES, N_EXPERTS),
        in_specs=[smem(slots), smem(wts), any_spec, any_spec, any_spec, any_spec,
                  wspec(w_gate), wspec(w_up), wspec(w_down)],
        out_specs=(any_spec, any_spec),
        scratch_shapes=[
            pltpu.VMEM((TILE_TOKENS + DUMMY_ROWS, D_MODEL), F32),
            pltpu.VMEM((TILE_TOKENS + DUMMY_ROWS, D_MODEL), F32),
            pltpu.VMEM((MOE_ROWS // 8, 8, D_MODEL), F32),
            pltpu.VMEM((MOE_ROWS // 8, 8, D_MODEL), F32),
            pltpu.VMEM((MOE_ROWS // 8, 8, D_MODEL), F32),
            pltpu.VMEM((MOE_ROWS // 8, 8, D_MODEL), F32),
            pltpu.SMEM((LIST_CAP,), jnp.int32),
            pltpu.SemaphoreType.DMA((4,)),
            pltpu.SemaphoreType.DMA((2,)),
        ],
    )
    return pl.pallas_call(
        _experts_body,
        grid_spec=grid_spec,
        out_shape=(jax.ShapeDtypeStruct(x1_p.shape, F32), jax.ShapeDtypeStruct(x1_s.shape, F32)),
        compiler_params=pltpu.CompilerParams(dimension_semantics=("arbitrary", "arbitrary"),
                                             vmem_limit_bytes=EXPERTS_VMEM_LIMIT),
        name="experts",
    )(nblk, blk0, cnt, slots, wts, hn_p, hn_s, x1_p, x1_s, w_gate, w_up, w_down)


def _rope_tables(pos):
    half = HEAD_DIM // 2
    inv = ROPE_THETA ** (-jnp.arange(half, dtype=F32) * (2.0 / HEAD_DIM))
    ang = pos.astype(F32)[:, None] * inv[None, :]
    cos = jnp.tile(jnp.cos(ang), (1, LANES // half))
    sin = jnp.sin(ang)
    sin = jnp.tile(jnp.concatenate([-sin, sin], axis=1), (1, LANES // HEAD_DIM))
    return cos, sin


def _split_bf16(w):
    hi = w.astype(BF16)
    return hi, (w - hi.astype(F32)).astype(BF16)


def kernel(x_prompt, x_sample, cache_k, cache_v, state_pool, norm_mix_g, w_in, q_norm_g, k_norm_g,
           attn_sinks, w_attn_branch, pool_mix_w, pool_scale, w_pool_branch, w_out, norm_ffn_g,
           w_route_group, b_route_group, w_route_expert, b_route_expert, w_expert_gate,
           w_expert_up, w_expert_down):
    batch, seq, d = x_prompt.shape
    dec_batch, dec_seq, _ = x_sample.shape
    past_len = 16384
    assert x_prompt.shape == (4, 4096, D_MODEL) and x_sample.shape == (128, 4, D_MODEL)
    assert w_in.shape[0] == 1, "single layer"
    n_p, n_s = batch * seq, dec_batch * dec_seq

    g_mix = norm_mix_g[0][None, :]
    w_in_b = w_in[0].astype(BF16)
    gq = jnp.tile(q_norm_g[0], LANES // HEAD_DIM)[None, :]
    gk = jnp.tile(k_norm_g[0], LANES // HEAD_DIM)[None, :]
    sinks = attn_sinks[0]
    mix_b = pool_mix_w[0].astype(BF16)
    ps = pool_scale[0][None, :]
    wa_b = w_attn_branch[0].astype(BF16)
    wp_b = w_pool_branch[0].astype(BF16)
    wo_b = w_out[0].astype(BF16)
    gf = norm_ffn_g[0][None, :]
    wr = jnp.zeros((D_MODEL, LANES), F32)
    wr = wr.at[:, :N_EXPERT_GROUPS].set(w_route_group[0])
    wr = wr.at[:, EXPERT_LANE0:EXPERT_LANE0 + N_EXPERTS].set(w_route_expert[0])
    wr_hi, wr_lo = _split_bf16(wr)
    br = jnp.zeros((1, LANES), F32)
    br = br.at[0, :N_EXPERT_GROUPS].set(b_route_group[0])
    br = br.at[0, EXPERT_LANE0:EXPERT_LANE0 + N_EXPERTS].set(b_route_expert[0])

    cos_p, sin_p = _rope_tables(jnp.arange(seq, dtype=jnp.int32))
    pos_s = past_len + (jnp.arange(n_s, dtype=jnp.int32) % dec_seq)
    cos_s, sin_s = _rope_tables(pos_s)

    dense_consts = (mix_b, ps, wa_b, wp_b, wo_b, gf, wr_hi, wr_lo, br)
    assert n_s == ROW_BLOCK == N_TILES * TILE_SAMPLE and n_p == N_TILES * TILE_PROMPT
    ridx = jnp.arange(ROW_BLOCK, dtype=jnp.int32)
    lower = ridx[:, None] > ridx[None, :]
    seg_of = ridx // TILE_SAMPLE
    ltri_p = lower.astype(BF16)
    ltri_s = (lower & (seg_of[:, None] == seg_of[None, :])).astype(BF16)
    ssel_p = (jnp.arange(8, dtype=jnp.int32)[:, None] == 0) & (ridx[None, :] >= 0)
    ssel_s = jnp.arange(8, dtype=jnp.int32)[:, None] == seg_of[None, :]

    xp = x_prompt.reshape(n_p, d)
    q, k, v, kk, vv, u, sga, sgp = _inproj(xp, g_mix, w_in_b, cos_p, sin_p, gq, gk)
    attn, pd, wg_b, wu_b, wd_b = _mixer_prompt(sinks, q, kk, vv, u, w_expert_gate[0], w_expert_up[0],
                                               w_expert_down[0], batch, seq)
    x1_p, hn_p, rt_p, cnt_p, _ = _dense(xp, attn, pd, sga, sgp, *dense_consts, ltri_p,
                                        ssel_p.astype(BF16), jnp.zeros((ROW_BLOCK, LANES), F32))
    new_k_p = k.reshape(batch, seq, N_KV_HEADS, HEAD_DIM)[:, -WINDOW:][None]
    new_v_p = v.reshape(batch, seq, N_KV_HEADS, HEAD_DIM)[:, -WINDOW:][None]
    new_u_p = u.reshape(batch, seq, d)[:, -POOL_STATE_LEN:][None]

    xs = x_sample.reshape(n_s, d)
    q, k, v, kk, vv, u, sga, sgp = _inproj(xs, g_mix, w_in_b, cos_s, sin_s, gq, gk)
    ck = cache_k[0].reshape(dec_batch, WINDOW, KV_W)
    cv = cache_v[0].reshape(dec_batch, WINDOW, KV_W)
    attn, pd, nk_s, nv_s, nu_s = _mixer_sample(sinks, q.astype(F32), k, v, u, ck, cv, state_pool[0])
    cnt_tiles_p = cnt_p[TILE_BLOCKS - 1::TILE_BLOCKS, 0, :]
    crow = jnp.repeat(cnt_tiles_p, TILE_SAMPLE, axis=0)
    x1_s, hn_s, rt_s, _, seg_s = _dense(xs, attn, pd, sga, sgp, *dense_consts, ltri_s,
                                        ssel_s.astype(BF16), crow)
    new_k_s = nk_s.reshape(1, dec_batch, WINDOW, N_KV_HEADS, HEAD_DIM)
    new_v_s = nv_s.reshape(1, dec_batch, WINDOW, N_KV_HEADS, HEAD_DIM)
    new_u_s = nu_s[None]

    ex = slice(EXPERT_LANE0, EXPERT_LANE0 + N_EXPERTS)
    cnt = (cnt_tiles_p[:, ex] + seg_s[0, :N_TILES, ex]).astype(jnp.int32)
    nblk = (cnt + (MOE_ROWS - 1)) // MOE_ROWS
    blk0 = 1 + jnp.cumsum(nblk, axis=1) - nblk
    seg = MOE_ROWS * blk0
    rt = jnp.concatenate([rt_p[:, :6].reshape(N_TILES, TILE_PROMPT, 6),
                          rt_s[:, :6].reshape(N_TILES, TILE_SAMPLE, 6)], axis=1)
    expert_hit = rt[:, :, 0:2].astype(jnp.int32)[..., None] == jnp.arange(N_EXPERTS, dtype=jnp.int32)
    slot = (rt[:, :, 2:4].astype(jnp.int32)
            + jnp.sum(jnp.where(expert_hit, seg[:, None, None, :], 0), axis=-1))
    per_tile = lambda a: jnp.pad(a.reshape(N_TILES, 2 * TILE_TOKENS),
                                 ((0, 0), (0, TILE_LIST - 2 * TILE_TOKENS))).reshape(-1)
    slots = per_tile(slot)
    wts = per_tile(rt[:, :, 4:6])
    y_p, y_s = _experts(nblk, blk0, cnt, slots, wts, hn_p, hn_s, x1_p, x1_s, wg_b, wu_b, wd_b)
    y_prompt = y_p.reshape(batch, seq, d)
    y_sample = y_s.reshape(dec_batch, dec_seq, d)

    return (y_prompt, y_sample, new_k_p, new_v_p, new_u_p, new_k_s, new_v_s, new_u_s)
```

```python
import functools

import jax
import jax.numpy as jnp
from jax import lax
from jax.experimental import pallas as pl
from jax.experimental.pallas import tpu as pltpu

F32 = jnp.float32
BF16 = jnp.bfloat16

D_MODEL = 1024
HEAD_DIM = 64
N_Q_HEADS = 16
N_KV_HEADS = 2
WINDOW = 128
ROPE_THETA = 10000.0
POOL_WINDOWS = (2, 4, 8, 16)
POOL_GROUP_WIDTH = D_MODEL // len(POOL_WINDOWS)
POOL_STATE_LEN = max(POOL_WINDOWS) - 1
N_EXPERT_GROUPS = 4
EXPERTS_PER_GROUP = 8
N_EXPERTS = N_EXPERT_GROUPS * EXPERTS_PER_GROUP
D_EXPERT = 512
RMS_EPS = 1e-6
Q_W = N_Q_HEADS * HEAD_DIM
KV_W = N_KV_HEADS * HEAD_DIM
OFF_K = Q_W
OFF_V = OFF_K + KV_W
OFF_U = OFF_V + KV_W
OFF_GA = OFF_U + D_MODEL
OFF_GP = OFF_GA + D_MODEL
IN_W = OFF_GP + D_MODEL

LANES = 128
ROW_BLOCK = 512
EXPERT_LANE0 = 32
VMEM_LIMIT = 56 * 1024 * 1024
EXPERTS_VMEM_LIMIT = 60 * 1024 * 1024
NEG_INF = float("-inf")

N_TILES = 4
TILE_BLOCKS = 8
TILE_PROMPT = TILE_BLOCKS * ROW_BLOCK
TILE_SAMPLE = 128
TILE_TOKENS = TILE_PROMPT + TILE_SAMPLE
MOE_ROWS = 320
DUMMY_ROWS = 8
SCATTER_BATCH = 4
TILE_LIST = -(-(2 * TILE_TOKENS + 1) // 1024) * 1024
LIST_CAP = 2 * TILE_TOKENS + N_EXPERTS * (MOE_ROWS - 1) + 2 * MOE_ROWS + 8


def _rms(x, g):
    return x * lax.rsqrt(jnp.mean(x * x, axis=-1, keepdims=True) + RMS_EPS) * g


def _bdot(a, b):
    return jnp.dot(a, b, preferred_element_type=F32)


def _inproj_body(x_ref, g_ref, w_ref, cos_ref, sin_ref, gq_ref, gk_ref,
                 q_ref, k_ref, v_ref, kk_ref, vv_ref, u_ref, sga_ref, sgp_ref):
    rows = x_ref.shape[0]
    hb = _rms(x_ref[...], g_ref[...]).astype(BF16)
    cos = cos_ref[...]
    sin = sin_ref[...]
    lane = lax.broadcasted_iota(jnp.int32, (rows, LANES), 1)
    lo = lane < HEAD_DIM
    first_half = (lane % HEAD_DIM) < (HEAD_DIM // 2)

    def head_norm_rope(zc, gain):
        sq = zc * zc
        ss_lo = jnp.sum(jnp.where(lo, sq, 0.0), axis=-1, keepdims=True)
        ss_hi = jnp.sum(jnp.where(lo, 0.0, sq), axis=-1, keepdims=True)
        r = lax.rsqrt(jnp.where(lo, ss_lo, ss_hi) * (1.0 / HEAD_DIM) + RMS_EPS)
        y = zc * r * gain
        partner = jnp.where(first_half, pltpu.roll(y, LANES - HEAD_DIM // 2, 1),
                            pltpu.roll(y, HEAD_DIM // 2, 1))
        return y * cos + partner * sin

    gq = gq_ref[...]
    for j in range(Q_W // 256):
        z = _bdot(hb, w_ref[:, 256 * j:256 * (j + 1)])
        for c in range(2):
            qn = head_norm_rope(z[:, LANES * c:LANES * (c + 1)], gq) * (HEAD_DIM ** -0.5)
            q_ref[:, 256 * j + LANES * c:256 * j + LANES * (c + 1)] = qn.astype(BF16)

    z = _bdot(hb, w_ref[:, OFF_K:OFF_U])
    kn = head_norm_rope(z[:, :KV_W], gk_ref[...])
    vr = z[:, KV_W:]
    k_ref[...] = kn
    v_ref[...] = vr
    kr = pltpu.roll(kn, HEAD_DIM, 1)
    vrr = pltpu.roll(vr, HEAD_DIM, 1)
    kk_ref[:, :LANES] = jnp.where(lo, kn, kr).astype(BF16)
    kk_ref[:, LANES:] = jnp.where(lo, kr, kn).astype(BF16)
    vv_ref[:, :LANES] = jnp.where(lo, vr, vrr).astype(BF16)
    vv_ref[:, LANES:] = jnp.where(lo, vrr, vr).astype(BF16)

    for j in range(D_MODEL // 256):
        u_ref[:, 256 * j:256 * (j + 1)] = _bdot(hb, w_ref[:, OFF_U + 256 * j:OFF_U + 256 * (j + 1)])
        sga_ref[:, 256 * j:256 * (j + 1)] = jax.nn.sigmoid(
            _bdot(hb, w_ref[:, OFF_GA + 256 * j:OFF_GA + 256 * (j + 1)]))
        sgp_ref[:, 256 * j:256 * (j + 1)] = jax.nn.sigmoid(
            _bdot(hb, w_ref[:, OFF_GP + 256 * j:OFF_GP + 256 * (j + 1)]))


def _inproj(x, g, w_in_b, cos, sin, gq, gk):
    n = x.shape[0]
    nb = n // ROW_BLOCK
    ncos = cos.shape[0] // ROW_BLOCK
    row = lambda w: pl.BlockSpec((ROW_BLOCK, w), lambda i: (i, 0))
    full = lambda a: pl.BlockSpec(a.shape, lambda i: (0,) * a.ndim)
    tab = pl.BlockSpec((ROW_BLOCK, LANES), lambda i: (i % ncos, 0))
    out_shapes = (
        jax.ShapeDtypeStruct((n, Q_W), BF16),
        jax.ShapeDtypeStruct((n, KV_W), F32),
        jax.ShapeDtypeStruct((n, KV_W), F32),
        jax.ShapeDtypeStruct((n, 2 * LANES), BF16),
        jax.ShapeDtypeStruct((n, 2 * LANES), BF16),
        jax.ShapeDtypeStruct((n, D_MODEL), F32),
        jax.ShapeDtypeStruct((n, D_MODEL), F32),
        jax.ShapeDtypeStruct((n, D_MODEL), F32),
    )
    return pl.pallas_call(
        _inproj_body,
        grid=(nb,),
        in_specs=[row(D_MODEL), full(g), full(w_in_b), tab, tab, full(gq), full(gk)],
        out_specs=tuple(row(s.shape[1]) for s in out_shapes),
        out_shape=out_shapes,
        compiler_params=pltpu.CompilerParams(dimension_semantics=("arbitrary",),
                                             vmem_limit_bytes=VMEM_LIMIT),
        name="inproj",
    )(x, g, w_in_b, cos, sin, gq, gk)


def _attend(q2, kg, vg, cur_valid, sink_col, prev_dead):
    s = lax.dot_general(q2, kg, (((1,), (1,)), ((), ())), preferred_element_type=F32)
    s_prev = s[:, :WINDOW]
    if prev_dead is not None:
        s_prev = jnp.where(prev_dead, NEG_INF, s_prev)
    sc = jnp.where(cur_valid, s[:, WINDOW:], s_prev)
    m = jnp.maximum(jnp.max(sc, axis=-1, keepdims=True), sink_col)
    p = jnp.exp(sc - m)
    den = jnp.sum(p, axis=-1, keepdims=True) + jnp.exp(sink_col - m)
    pn = p * (1.0 / den)
    p2 = jnp.concatenate([jnp.where(cur_valid, 0.0, pn), jnp.where(cur_valid, pn, 0.0)], axis=1)
    return _bdot(p2.astype(BF16), vg)


POOL_PAD_ROWS = 8


def _mixer_prompt_body(sink_ref, q_ref, kkc_ref, kkp_ref, vvc_ref, vvp_ref, uc_ref, up_ref,
                       wg_ref, wu_ref, wd_ref,
                       attn_ref, pd_ref, wgb_ref, wub_ref, wdb_ref, ext_ref, s_a, s_b):
    wgb_ref[...] = wg_ref[...].astype(BF16)
    wub_ref[...] = wu_ref[...].astype(BF16)
    wdb_ref[...] = wd_ref[...].astype(BF16)
    i = pl.program_id(1)
    first = i == 0
    m2 = 2 * WINDOW
    row = lax.broadcasted_iota(jnp.int32, (m2, LANES), 0)
    col = lax.broadcasted_iota(jnp.int32, (m2, LANES), 1)
    cur_valid = (row % WINDOW) >= col
    upper = lax.broadcasted_iota(jnp.int32, (m2, 1), 0) < WINDOW
    lo = lax.broadcasted_iota(jnp.int32, (WINDOW, LANES), 1) < HEAD_DIM

    for j in range(ROW_BLOCK // WINDOW):
        r0 = WINDOW * j
        if j == 0:
            kprev, vprev = kkp_ref[...], vvp_ref[...]
            prev_dead = first
        else:
            kprev, vprev = kkc_ref[r0 - WINDOW:r0, :], vvc_ref[r0 - WINDOW:r0, :]
            prev_dead = None
        kband = jnp.concatenate([kprev, kkc_ref[r0:r0 + WINDOW, :]], axis=0)
        vband = jnp.concatenate([vprev, vvc_ref[r0:r0 + WINDOW, :]], axis=0)
        for g in range(N_KV_HEADS):
            kg = kband[:, LANES * g:LANES * (g + 1)]
            vg = vband[:, LANES * g:LANES * (g + 1)]
            for c in range(4):
                cc = 4 * g + c
                qc = q_ref[r0:r0 + WINDOW, LANES * cc:LANES * (cc + 1)]
                zero = jnp.zeros_like(qc)
                q2 = jnp.concatenate([jnp.where(lo, qc, zero), jnp.where(lo, zero, qc)], axis=0)
                sink_col = jnp.where(upper, sink_ref[2 * cc], sink_ref[2 * cc + 1])
                o = _attend(q2, kg, vg, cur_valid, sink_col, prev_dead)
                attn_ref[r0:r0 + WINDOW, LANES * cc:LANES * (cc + 1)] = jnp.where(
                    lo, o[:WINDOW], o[WINDOW:]).astype(BF16)

    pad, top = POOL_PAD_ROWS, POOL_PAD_ROWS + 16
    end = top + ROW_BLOCK
    for ref in (ext_ref, s_a, s_b):
        ref[0:pad, :] = jnp.zeros((pad, D_MODEL), F32)
    ext_ref[pad:top, :] = jnp.where(first, 0.0, up_ref[...])
    ext_ref[top:, :] = uc_ref[...]
    pos = i * ROW_BLOCK + lax.broadcasted_iota(jnp.int32, (ROW_BLOCK, 1), 0)
    src = ext_ref
    for g, w in enumerate(POOL_WINDOWS):
        c0 = POOL_GROUP_WIDTH * g
        dst = (s_a, s_b)[g % 2]
        shift = w // 2
        dst[pad:end, c0:] = src[pad:end, c0:] + src[pad - shift:end - shift, c0:]
        c1 = c0 + POOL_GROUP_WIDTH
        cnt = jnp.minimum(w, pos + 1).astype(F32)
        pd_ref[:, c0:c1] = (dst[top:end, c0:c1] / cnt - uc_ref[:, c0:c1]).astype(BF16)
        src = dst


def _mixer_prompt(sinks, q, kk, vv, u, w_gate, w_up, w_down, batch, seq):
    nb = seq // ROW_BLOCK
    assert batch * nb == N_EXPERTS, "one expert's weights are cast per grid step"
    sub = ROW_BLOCK // WINDOW
    cur = lambda w: pl.BlockSpec((ROW_BLOCK, w), lambda b, i: (b * nb + i, 0))
    wspec = lambda a: pl.BlockSpec((1,) + a.shape[1:], lambda b, i: (b * nb + i, 0, 0))
    weights = (w_gate, w_up, w_down)
    prev_kv = pl.BlockSpec((WINDOW, 2 * LANES),
                           lambda b, i: (jnp.maximum((b * nb + i) * sub - 1, 0), 0))
    prev_u = pl.BlockSpec((16, D_MODEL),
                          lambda b, i: (jnp.maximum((b * nb + i) * (ROW_BLOCK // 16) - 1, 0), 0))
    n = batch * seq
    return pl.pallas_call(
        _mixer_prompt_body,
        grid=(batch, nb),
        in_specs=[pl.BlockSpec(memory_space=pltpu.SMEM), cur(Q_W), cur(2 * LANES), prev_kv,
                  cur(2 * LANES), prev_kv, cur(D_MODEL), prev_u] + [wspec(a) for a in weights],
        out_specs=(cur(Q_W), cur(D_MODEL)) + tuple(wspec(a) for a in weights),
        out_shape=(jax.ShapeDtypeStruct((n, Q_W), BF16), jax.ShapeDtypeStruct((n, D_MODEL), BF16))
        + tuple(jax.ShapeDtypeStruct(a.shape, BF16) for a in weights),
        scratch_shapes=[pltpu.VMEM((POOL_PAD_ROWS + 16 + ROW_BLOCK, D_MODEL), F32)] * 3,
        compiler_params=pltpu.CompilerParams(dimension_semantics=("arbitrary", "arbitrary"),
                                             vmem_limit_bytes=VMEM_LIMIT),
        name="mixer_prompt",
    )(sinks, q, kk, kk, vv, vv, u, u, *weights)


SEQ_PAIR_ROWS = 8
SAMPLE_PAIRS_PER_STEP = 8


def _mixer_sample_body(sink_ref, q_ref, k_ref, v_ref, u_ref, ck_ref, cv_ref, st_ref,
                       attn_ref, pd_ref, nk_ref, nv_ref, nu_ref, ext_ref):
    m = SEQ_PAIR_ROWS
    half = m // 2
    hist = POOL_STATE_LEN
    ext_ref[16:, :] = jnp.zeros((ext_ref.shape[0] - 16, D_MODEL), F32)
    row8 = lax.broadcasted_iota(jnp.int32, (m, LANES), 0)
    n_chunks = Q_W // LANES
    mq = n_chunks * m
    row1 = lax.broadcasted_iota(jnp.int32, (m, 1), 0)
    row = lax.broadcasted_iota(jnp.int32, (mq, LANES), 0)
    col = lax.broadcasted_iota(jnp.int32, (mq, LANES), 1)
    cur_valid = (row % half) >= col
    lane8 = lax.broadcasted_iota(jnp.int32, (m, LANES), 1)
    lo8 = lane8 < HEAD_DIM
    top8 = lax.broadcasted_iota(jnp.int32, (m, LANES), 0) < half
    tail = jnp.zeros((WINDOW - m, LANES), F32)
    top = row1 < half
    sink_col = jnp.concatenate([jnp.where(top, sink_ref[2 * cc], sink_ref[2 * cc + 1])
                                for cc in range(n_chunks)], axis=0)

    def pair(p, carry):
        r0 = pl.multiple_of(p * m, m)
        q8 = q_ref[pl.ds(r0, m), :]
        k8 = k_ref[pl.ds(r0, m), :]
        v8 = v_ref[pl.ds(r0, m), :]
        u8 = u_ref[pl.ds(r0, m), :]
        attn_parts = []
        pd_parts = []
        for s in range(2):
            shift = lambda a: a if s == 0 else pltpu.roll(a, half, 0)
            seq = 2 * p + s
            ck, cv = ck_ref[seq], cv_ref[seq]
            ks, vs = shift(k8), shift(v8)
            kb = jnp.concatenate([ck, ks, tail], axis=0)
            vb = jnp.concatenate([cv, vs, tail], axis=0)
            for cache, new, out in ((ck, ks, nk_ref), (cv, vs, nv_ref)):
                up = pltpu.roll(cache, WINDOW - half, 0)
                out[seq, 0:WINDOW - m, :] = up[0:WINDOW - m]
                out[seq, WINDOW - m:WINDOW, :] = jnp.where(row8 < half, up[WINDOW - m:],
                                                           pltpu.roll(new, half, 0))
            qs = shift(q8)
            qr = pltpu.roll(qs, half, 0)
            q2 = []
            for cc in range(n_chunks):
                qa = qs[:, LANES * cc:LANES * (cc + 1)]
                qb = qr[:, LANES * cc:LANES * (cc + 1)]
                if cc < n_chunks // N_KV_HEADS:
                    first = jnp.where(lo8, qa, 0.0)
                    second = jnp.where(lo8, pltpu.roll(qb, HEAD_DIM, 1), 0.0)
                else:
                    first = jnp.where(lo8, 0.0, pltpu.roll(qa, HEAD_DIM, 1))
                    second = jnp.where(lo8, 0.0, qb)
                q2.append(jnp.where(top8, first, second))
            q2 = jnp.concatenate(q2, axis=0).astype(BF16)
            o = _attend(q2, kb.astype(BF16), vb.astype(BF16), cur_valid, sink_col, None)
            chunks = []
            for cc in range(n_chunks):
                oc = o[m * cc:m * (cc + 1)]
                if cc < n_chunks // N_KV_HEADS:
                    chunks.append(jnp.where(lo8, oc, pltpu.roll(pltpu.roll(oc, half, 0), HEAD_DIM, 1)))
                else:
                    chunks.append(jnp.where(lo8, pltpu.roll(oc, HEAD_DIM, 1), pltpu.roll(oc, half, 0)))
            attn_parts.append(jnp.concatenate(chunks, axis=1))

            ext_ref[0:hist, :] = st_ref[0, seq]
            ext_ref[hist:hist + half, :] = shift(u8)[0:half]
            nu_ref[0, seq] = ext_ref[half:hist + half, :]
            cols = []
            for g, w in enumerate(POOL_WINDOWS):
                c0, c1 = POOL_GROUP_WIDTH * g, POOL_GROUP_WIDTH * (g + 1)
                acc = ext_ref[hist:hist + m, c0:c1]
                for k in range(1, w):
                    acc = acc + ext_ref[hist - k:hist - k + m, c0:c1]
                cols.append(acc / float(w) - ext_ref[hist:hist + m, c0:c1])
            pd_parts.append(jnp.concatenate(cols, axis=1))

        attn_ref[pl.ds(r0, m), :] = jnp.where(top, attn_parts[0], pltpu.roll(attn_parts[1], half, 0))
        pd_ref[pl.ds(r0, m), :] = jnp.where(top, pd_parts[0], pltpu.roll(pd_parts[1], half, 0))
        return carry

    lax.fori_loop(0, SAMPLE_PAIRS_PER_STEP, pair, 0)


def _mixer_sample(sinks, q, k, v, u, cache_k, cache_v, state):
    n = q.shape[0]
    m = SEQ_PAIR_ROWS * SAMPLE_PAIRS_PER_STEP
    rows = lambda w: pl.BlockSpec((m, w), lambda i: (i, 0))
    seqs = lambda a: pl.BlockSpec((2 * SAMPLE_PAIRS_PER_STEP,) + a.shape[1:], lambda i: (i, 0, 0))
    st_spec = pl.BlockSpec((1, 2 * SAMPLE_PAIRS_PER_STEP) + state.shape[2:], lambda i: (0, i, 0, 0))
    like = lambda a: jax.ShapeDtypeStruct(a.shape, F32)
    return pl.pallas_call(
        _mixer_sample_body,
        grid=(n // m,),
        in_specs=[pl.BlockSpec(memory_space=pltpu.SMEM), rows(Q_W), rows(KV_W), rows(KV_W),
                  rows(D_MODEL), seqs(cache_k), seqs(cache_v), st_spec],
        out_specs=(rows(Q_W), rows(D_MODEL), seqs(cache_k), seqs(cache_v), st_spec),
        out_shape=(jax.ShapeDtypeStruct((n, Q_W), F32), jax.ShapeDtypeStruct((n, D_MODEL), F32),
                   like(cache_k), like(cache_v), like(state)),
        scratch_shapes=[pltpu.VMEM((16 + SEQ_PAIR_ROWS, D_MODEL), F32)],
        compiler_params=pltpu.CompilerParams(dimension_semantics=("arbitrary",),
                                             vmem_limit_bytes=VMEM_LIMIT),
        name="mixer_sample",
    )(sinks, q, k, v, u, cache_k, cache_v, state)


def _dense_body(x_ref, attn_ref, pd_ref, sga_ref, sgp_ref, mix_ref, ps_ref, wa_ref, wp_ref,
                wo_ref, gf_ref, wrh_ref, wrl_ref, br_ref, ltri_ref, ssel_ref, crow_ref,
                x1_ref, hn_ref, rt_ref, cnt_ref, seg_ref, carry_ref):
    rows = x_ref.shape[0]
    step = pl.program_id(0)

    @pl.when(step % TILE_BLOCKS == 0)
    def _():
        carry_ref[...] = jnp.zeros_like(carry_ref)

    pd = pd_ref[...].astype(BF16)
    pooled = []
    for g in range(len(POOL_WINDOWS)):
        c0, c1 = POOL_GROUP_WIDTH * g, POOL_GROUP_WIDTH * (g + 1)
        pooled.append((_bdot(pd[:, c0:c1], mix_ref[g]) * ps_ref[:, c0:c1]).astype(BF16))
    pooled = jnp.concatenate(pooled, axis=1)
    merged = (sga_ref[...] * _bdot(attn_ref[...].astype(BF16), wa_ref[...])
              + sgp_ref[...] * _bdot(pooled, wp_ref[...]))
    x1 = x_ref[...] + _bdot(merged.astype(BF16), wo_ref[...])
    x1_ref[...] = x1
    hn = _rms(x1, gf_ref[...])
    hn_ref[...] = hn
    hi = hn.astype(BF16)
    lo = (hn - hi.astype(F32)).astype(BF16)
    both = _bdot(hi, wrl_ref[...])
    logits = both[:, :LANES] + both[:, LANES:] + _bdot(lo, wrh_ref[...]) + br_ref[...]

    lane = lax.broadcasted_iota(jnp.int32, (rows, LANES), 1)
    big = jnp.int32(LANES)
    gl = jnp.where(lane < N_EXPERT_GROUPS, logits, NEG_INF)
    gmax = jnp.max(gl, axis=-1, keepdims=True)
    gidx = jnp.min(jnp.where(gl == gmax, lane, big), axis=-1, keepdims=True)
    g_w = 1.0 / jnp.sum(jnp.exp(gl - gmax), axis=-1, keepdims=True)
    e0 = EXPERT_LANE0 + gidx * EXPERTS_PER_GROUP
    el = jnp.where((lane >= e0) & (lane < e0 + EXPERTS_PER_GROUP), logits, NEG_INF)
    l1 = jnp.max(el, axis=-1, keepdims=True)
    i1 = jnp.min(jnp.where(el == l1, lane, big), axis=-1, keepdims=True)
    el2 = jnp.where(lane == i1, NEG_INF, el)
    l2 = jnp.max(el2, axis=-1, keepdims=True)
    i2 = jnp.min(jnp.where(el2 == l2, lane, big), axis=-1, keepdims=True)
    e = jnp.exp(l2 - l1)
    w1 = 1.0 / (1.0 + e)
    w2 = e * w1

    sel = (lane == i1) | (lane == i2)
    onehot = jnp.where(sel, 1.0, 0.0).astype(BF16)
    rank = _bdot(ltri_ref[...], onehot) + carry_ref[0:1, :] + crow_ref[...]
    segsum = _bdot(ssel_ref[...], onehot)
    seg_ref[0] = segsum
    carry_ref[...] = carry_ref[...] + segsum[0:1, :]
    cnt_ref[0] = carry_ref[...]
    r1 = jnp.sum(jnp.where(lane == i1, rank, 0.0), axis=-1, keepdims=True)
    r2 = jnp.sum(jnp.where(lane == i2, rank, 0.0), axis=-1, keepdims=True)
    cols = ((i1 - EXPERT_LANE0).astype(F32), (i2 - EXPERT_LANE0).astype(F32), r1, r2,
            g_w * w1, g_w * w2)
    tile = jnp.zeros((rows, LANES), F32)
    for c, val in enumerate(cols):
        tile = jnp.where(lane == c, val, tile)
    rt_ref[...] = tile


def _dense(x, attn, pd, sga, sgp, mix_b, pool_scale, wa_b, wp_b, wo_b, gf, wr_hi, wr_lo, br,
           ltri, ssel, crow):
    n = x.shape[0]
    nb = n // ROW_BLOCK
    row = lambda w: pl.BlockSpec((ROW_BLOCK, w), lambda i: (i, 0))
    full = lambda a: pl.BlockSpec(a.shape, lambda i: (0,) * a.ndim)
    stat = pl.BlockSpec((1, 8, LANES), lambda i: (i, 0, 0))
    consts = (mix_b, pool_scale, wa_b, wp_b, wo_b, gf, wr_hi, wr_lo, br, ltri, ssel, crow)
    return pl.pallas_call(
        _dense_body,
        grid=(nb,),
        in_specs=[row(D_MODEL)] * 5 + [full(a) for a in consts],
        out_specs=(row(D_MODEL), row(D_MODEL), row(LANES), stat, stat),
        out_shape=(jax.ShapeDtypeStruct((n, D_MODEL), F32),
                   jax.ShapeDtypeStruct((n, D_MODEL), F32),
                   jax.ShapeDtypeStruct((n, LANES), F32),
                   jax.ShapeDtypeStruct((nb, 8, LANES), F32),
                   jax.ShapeDtypeStruct((nb, 8, LANES), F32)),
        scratch_shapes=[pltpu.VMEM((8, LANES), F32)],
        compiler_params=pltpu.CompilerParams(dimension_semantics=("arbitrary",),
                                             vmem_limit_bytes=VMEM_LIMIT),
        name="dense",
    )(x, attn, pd, sga, sgp, *consts)


def _experts_body(nblk_ref, blk0_ref, cnt_ref, slots_ref, wts_ref, hn_p, hn_s, x1_p, x1_s,
                  wg_ref, wu_ref, wd_ref, y_p, y_s,
                  hn_t, acc, xg0, xg1, yb0, yb1, tok, sem_in, sem_out):
    tau = pl.program_id(0)
    e = pl.program_id(1)
    t_rows = TILE_TOKENS
    dummy = 2 * t_rows
    p_rows = pl.ds(pl.multiple_of(tau * TILE_PROMPT, TILE_PROMPT), TILE_PROMPT)
    s_rows = pl.ds(pl.multiple_of(tau * TILE_SAMPLE, TILE_SAMPLE), TILE_SAMPLE)
    tile_p = pl.ds(0, TILE_PROMPT)
    tile_s = pl.ds(TILE_PROMPT, TILE_SAMPLE)

    def in_copies():
        return [pltpu.make_async_copy(hn_p.at[p_rows], hn_t.at[tile_p], sem_in.at[0]),
                pltpu.make_async_copy(hn_s.at[s_rows], hn_t.at[tile_s], sem_in.at[1]),
                pltpu.make_async_copy(x1_p.at[p_rows], acc.at[tile_p], sem_in.at[2]),
                pltpu.make_async_copy(x1_s.at[s_rows], acc.at[tile_s], sem_in.at[3])]

    def out_copies():
        return [pltpu.make_async_copy(acc.at[tile_p], y_p.at[p_rows], sem_out.at[0]),
                pltpu.make_async_copy(acc.at[tile_s], y_s.at[s_rows], sem_out.at[1])]

    xbufs = (xg0, xg1)
    ybufs = (yb0, yb1)
    end_block = blk0_ref[tau, N_EXPERTS - 1] + nblk_ref[tau, N_EXPERTS - 1]
    end_slot = end_block * MOE_ROWS

    def gather_block(b, xdst):
        base = b * MOE_ROWS
        for j in range(MOE_ROWS):
            t = lax.shift_right_logical(tok[base + j], 1)
            xdst[j // 8, pl.ds(j % 8, 1), :] = hn_t[pl.ds(t, 1), :]

    def scatter_block(b, ysrc):
        base = b * MOE_ROWS
        for j0 in range(0, MOE_ROWS, SCATTER_BATCH):
            ents = [tok[base + j0 + i] for i in range(SCATTER_BATCH)]
            rows = [lax.shift_right_logical(en, 1) for en in ents]
            vals = [acc[pl.ds(rows[i], 1), :]
                    + wts_ref[ents[i]] * ysrc[(j0 + i) // 8, pl.ds((j0 + i) % 8, 1), :]
                    for i in range(SCATTER_BATCH)]
            for i in range(SCATTER_BATCH):
                acc[pl.ds(rows[i], 1), :] = vals[i]

    @pl.when(e == 0)
    def _load_tile():
        for cp in in_copies():
            cp.start()

        hn_t[t_rows:, :] = jnp.zeros((DUMMY_ROWS, D_MODEL), F32)
        yb0[...] = jnp.zeros_like(yb0)

        def pad_range(lo, hi):
            def fill(g, c):
                for i in range(8):
                    tok[lo + g * 8 + i] = dummy
                return c
            lax.fori_loop(0, lax.shift_right_logical(hi - lo + 7, 3), fill, 0)
        pad_range(0, MOE_ROWS)
        pad_range(end_slot, end_slot + MOE_ROWS)

        def pad_fill(ei, c):
            s0 = blk0_ref[tau, ei] * MOE_ROWS
            pad_range(s0 + cnt_ref[tau, ei], s0 + nblk_ref[tau, ei] * MOE_ROWS)
            return c
        lax.fori_loop(0, N_EXPERTS, pad_fill, 0)

        def invert(g, c):
            a0 = g * 16
            for i in range(16):
                tok[slots_ref[a0 + i]] = a0 + i
            return c
        lax.fori_loop(0, t_rows // 8, invert, 0)

        for cp in in_copies():
            cp.wait()
        acc[t_rows:, :] = jnp.zeros((DUMMY_ROWS, D_MODEL), F32)
        gather_block(1, xbufs[1])

    first_block = blk0_ref[tau, e]

    def block(jb, c):
        b = first_block + jb
        for par in range(2):
            @pl.when((b & 1) == par)
            def _():
                gather_block(b + 1, xbufs[1 - par])
                xb = xbufs[par][...].reshape(MOE_ROWS, D_MODEL).astype(BF16)
                act = (jax.nn.silu(_bdot(xb, wg_ref[0])) * _bdot(xb, wu_ref[0])).astype(BF16)
                ybufs[par][...] = _bdot(act, wd_ref[0]).reshape(MOE_ROWS // 8, 8, D_MODEL)
                scatter_block(b - 1, ybufs[1 - par])
        return c
    lax.fori_loop(0, nblk_ref[tau, e], block, 0)

    @pl.when(e == N_EXPERTS - 1)
    def _store_tile():
        last = end_block - 1
        for par in range(2):
            @pl.when((last & 1) == par)
            def _():
                scatter_block(last, ybufs[par])
        cps = out_copies()
        for cp in cps:
            cp.start()
        for cp in cps:
            cp.wait()


def _experts(nblk, blk0, cnt, slots, wts, hn_p, hn_s, x1_p, x1_s, w_gate, w_up, w_down):
    any_spec = pl.BlockSpec(memory_space=pl.ANY)
    wspec = lambda a: pl.BlockSpec((1,) + a.shape[1:], lambda t, e, *_: (e, 0, 0))
    smem = lambda a: pl.BlockSpec((a.shape[0] // N_TILES,), lambda t, e, *_: (t,),
                                  memory_space=pltpu.SMEM)
    grid_spec = pltpu.PrefetchScalarGridSpec(
        num_scalar_prefetch=3,
        grid=(N_TILES, N_EXPERTS),
        in_specs=[smem(slots), smem(wts), any_spec, any_spec, any_spec, any_spec,
                  wspec(w_gate), wspec(w_up), wspec(w_down)],
        out_specs=(any_spec, any_spec),
        scratch_shapes=[
            pltpu.VMEM((TILE_TOKENS + DUMMY_ROWS, D_MODEL), F32),
            pltpu.VMEM((TILE_TOKENS + DUMMY_ROWS, D_MODEL), F32),
            pltpu.VMEM((MOE_ROWS // 8, 8, D_MODEL), F32),
            pltpu.VMEM((MOE_ROWS // 8, 8, D_MODEL), F32),
            pltpu.VMEM((MOE_ROWS // 8, 8, D_MODEL), F32),
            pltpu.VMEM((MOE_ROWS // 8, 8, D_MODEL), F32),
            pltpu.SMEM((LIST_CAP,), jnp.int32),
            pltpu.SemaphoreType.DMA((4,)),
            pltpu.SemaphoreType.DMA((2,)),
        ],
    )
    return pl.pallas_call(
        _experts_body,
        grid_spec=grid_spec,
        out_shape=(jax.ShapeDtypeStruct(x1_p.shape, F32), jax.ShapeDtypeStruct(x1_s.shape, F32)),
        compiler_params=pltpu.CompilerParams(dimension_semantics=("arbitrary", "arbitrary"),
                                             vmem_limit_bytes=EXPERTS_VMEM_LIMIT),
        name="experts",
    )(nblk, blk0, cnt, slots, wts, hn_p, hn_s, x1_p, x1_s, w_gate, w_up, w_down)


def _rope_tables(pos):
    half = HEAD_DIM // 2
    inv = ROPE_THETA ** (-jnp.arange(half, dtype=F32) * (2.0 / HEAD_DIM))
    ang = pos.astype(F32)[:, None] * inv[None, :]
    cos = jnp.tile(jnp.cos(ang), (1, LANES // half))
    sin = jnp.sin(ang)
    sin = jnp.tile(jnp.concatenate([-sin, sin], axis=1), (1, LANES // HEAD_DIM))
    return cos, sin


def _split_bf16(w):
    hi = w.astype(BF16)
    return hi, (w - hi.astype(F32)).astype(BF16)


def kernel(x_prompt, x_sample, cache_k, cache_v, state_pool, norm_mix_g, w_in, q_norm_g, k_norm_g,
           attn_sinks, w_attn_branch, pool_mix_w, pool_scale, w_pool_branch, w_out, norm_ffn_g,
           w_route_group, b_route_group, w_route_expert, b_route_expert, w_expert_gate,
           w_expert_up, w_expert_down):
    batch, seq, d = x_prompt.shape
    dec_batch, dec_seq, _ = x_sample.shape
    past_len = 16384
    assert x_prompt.shape == (4, 4096, D_MODEL) and x_sample.shape == (128, 4, D_MODEL)
    assert w_in.shape[0] == 1, "single layer"
    n_p, n_s = batch * seq, dec_batch * dec_seq

    g_mix = norm_mix_g[0][None, :]
    w_in_b = w_in[0].astype(BF16)
    gq = jnp.tile(q_norm_g[0], LANES // HEAD_DIM)[None, :]
    gk = jnp.tile(k_norm_g[0], LANES // HEAD_DIM)[None, :]
    sinks = attn_sinks[0]
    mix_b = pool_mix_w[0].astype(BF16)
    ps = pool_scale[0][None, :]
    wa_b = w_attn_branch[0].astype(BF16)
    wp_b = w_pool_branch[0].astype(BF16)
    wo_b = w_out[0].astype(BF16)
    gf = norm_ffn_g[0][None, :]
    wr = jnp.zeros((D_MODEL, LANES), F32)
    wr = wr.at[:, :N_EXPERT_GROUPS].set(w_route_group[0])
    wr = wr.at[:, EXPERT_LANE0:EXPERT_LANE0 + N_EXPERTS].set(w_route_expert[0])
    wr_hi, wr_lo = _split_bf16(wr)
    br = jnp.zeros((1, LANES), F32)
    br = br.at[0, :N_EXPERT_GROUPS].set(b_route_group[0])
    br = br.at[0, EXPERT_LANE0:EXPERT_LANE0 + N_EXPERTS].set(b_route_expert[0])

    cos_p, sin_p = _rope_tables(jnp.arange(seq, dtype=jnp.int32))
    pos_s = past_len + (jnp.arange(n_s, dtype=jnp.int32) % dec_seq)
    cos_s, sin_s = _rope_tables(pos_s)

    dense_consts = (mix_b, ps, wa_b, wp_b, wo_b, gf, wr_hi, jnp.concatenate([wr_hi, wr_lo], axis=1), br)
    assert n_s == ROW_BLOCK == N_TILES * TILE_SAMPLE and n_p == N_TILES * TILE_PROMPT
    ridx = jnp.arange(ROW_BLOCK, dtype=jnp.int32)
    lower = ridx[:, None] > ridx[None, :]
    seg_of = ridx // TILE_SAMPLE
    ltri_p = lower.astype(BF16)
    ltri_s = (lower & (seg_of[:, None] == seg_of[None, :])).astype(BF16)
    ssel_p = (jnp.arange(8, dtype=jnp.int32)[:, None] == 0) & (ridx[None, :] >= 0)
    ssel_s = jnp.arange(8, dtype=jnp.int32)[:, None] == seg_of[None, :]

    xp = x_prompt.reshape(n_p, d)
    q, k, v, kk, vv, u, sga, sgp = _inproj(xp, g_mix, w_in_b, cos_p, sin_p, gq, gk)
    attn, pd, wg_b, wu_b, wd_b = _mixer_prompt(sinks, q, kk, vv, u, w_expert_gate[0], w_expert_up[0],
                                               w_expert_down[0], batch, seq)
    x1_p, hn_p, rt_p, cnt_p, _ = _dense(xp, attn, pd, sga, sgp, *dense_consts, ltri_p,
                                        ssel_p.astype(BF16), jnp.zeros((ROW_BLOCK, LANES), F32))
    last_window = lambda a: a.reshape(batch, seq, KV_W)[:, -WINDOW:].reshape(
        1, batch, WINDOW, N_KV_HEADS, HEAD_DIM)
    new_k_p = last_window(k)
    new_v_p = last_window(v)
    new_u_p = u.reshape(batch, seq, d)[:, -POOL_STATE_LEN:][None]

    xs = x_sample.reshape(n_s, d)
    q, k, v, kk, vv, u, sga, sgp = _inproj(xs, g_mix, w_in_b, cos_s, sin_s, gq, gk)
    ck = cache_k[0].reshape(dec_batch, WINDOW, KV_W)
    cv = cache_v[0].reshape(dec_batch, WINDOW, KV_W)
    attn, pd, nk_s, nv_s, nu_s = _mixer_sample(sinks, q.astype(F32), k, v, u, ck, cv, state_pool)
    cnt_tiles_p = cnt_p[TILE_BLOCKS - 1::TILE_BLOCKS, 0, :]
    crow = jnp.repeat(cnt_tiles_p, TILE_SAMPLE, axis=0)
    x1_s, hn_s, rt_s, _, seg_s = _dense(xs, attn, pd, sga, sgp, *dense_consts, ltri_s,
                                        ssel_s.astype(BF16), crow)
    new_k_s = nk_s.reshape(1, dec_batch, WINDOW, N_KV_HEADS, HEAD_DIM)
    new_v_s = nv_s.reshape(1, dec_batch, WINDOW, N_KV_HEADS, HEAD_DIM)
    new_u_s = nu_s

    ex = slice(EXPERT_LANE0, EXPERT_LANE0 + N_EXPERTS)
    cnt = (cnt_tiles_p[:, ex] + seg_s[0, :N_TILES, ex]).astype(jnp.int32)
    nblk = (cnt + (MOE_ROWS - 1)) // MOE_ROWS
    blk0 = 1 + jnp.cumsum(nblk, axis=1) - nblk
    seg = MOE_ROWS * blk0
    rt = jnp.concatenate([rt_p[:, :6].reshape(N_TILES, TILE_PROMPT, 6),
                          rt_s[:, :6].reshape(N_TILES, TILE_SAMPLE, 6)], axis=1)
    expert_hit = rt[:, :, 0:2].astype(jnp.int32)[..., None] == jnp.arange(N_EXPERTS, dtype=jnp.int32)
    slot = (rt[:, :, 2:4].astype(jnp.int32)
            + jnp.sum(jnp.where(expert_hit, seg[:, None, None, :], 0), axis=-1))
    per_tile = lambda a: jnp.pad(a.reshape(N_TILES, 2 * TILE_TOKENS),
                                 ((0, 0), (0, TILE_LIST - 2 * TILE_TOKENS))).reshape(-1)
    slots = per_tile(slot)
    wts = per_tile(rt[:, :, 4:6])
    y_p, y_s = _experts(nblk, blk0, cnt, slots, wts, hn_p, hn_s, x1_p, x1_s, wg_b, wu_b, wd_b)
    y_prompt = y_p.reshape(batch, seq, d)
    y_sample = y_s.reshape(dec_batch, dec_seq, d)

    return (y_prompt, y_sample, new_k_p, new_v_p, new_u_p, new_k_s, new_v_s, new_u_s)
```

```python
import jax
import jax.numpy as jnp
import numpy as np
from jax import lax
from jax.experimental import pallas as pl
from jax.experimental.pallas import tpu as pltpu

F32 = jnp.float32
BF16 = jnp.bfloat16

D_MODEL = 1024
HEAD_DIM = 64
N_Q_HEADS = 16
N_KV_HEADS = 2
WINDOW = 128
ROPE_THETA = 10000.0
POOL_WINDOWS = (2, 4, 8, 16)
POOL_GROUP_WIDTH = D_MODEL // len(POOL_WINDOWS)
POOL_STATE_LEN = max(POOL_WINDOWS) - 1
N_EXPERT_GROUPS = 4
EXPERTS_PER_GROUP = 8
N_EXPERTS = N_EXPERT_GROUPS * EXPERTS_PER_GROUP
D_EXPERT = 512
RMS_EPS = 1e-6
Q_W = N_Q_HEADS * HEAD_DIM
KV_W = N_KV_HEADS * HEAD_DIM
OFF_K = Q_W
OFF_V = OFF_K + KV_W
OFF_U = OFF_V + KV_W
OFF_GA = OFF_U + D_MODEL
OFF_GP = OFF_GA + D_MODEL
IN_W = OFF_GP + D_MODEL

LANES = 128
ROW_BLOCK = 512
EXPERT_LANE0 = 32
VMEM_LIMIT = 56 * 1024 * 1024
EXPERTS_VMEM_LIMIT = 60 * 1024 * 1024
NEG_INF = float("-inf")

N_TILES = 4
TILE_BLOCKS = 8
TILE_PROMPT = TILE_BLOCKS * ROW_BLOCK
TILE_SAMPLE = 128
TILE_TOKENS = TILE_PROMPT + TILE_SAMPLE
MOE_ROWS = 320
DUMMY_ROWS = 8
SCATTER_BATCH = 4
TILE_LIST = -(-(2 * TILE_TOKENS + 1) // 1024) * 1024
LIST_CAP = 2 * TILE_TOKENS + N_EXPERTS * (MOE_ROWS - 1) + 2 * MOE_ROWS + 8


def _rms(x, g):
    return x * lax.rsqrt(jnp.mean(x * x, axis=-1, keepdims=True) + RMS_EPS) * g


def _bdot(a, b):
    return jnp.dot(a, b, preferred_element_type=F32)


def _inproj_body(x_ref, g_ref, w_ref, cos_ref, sin_ref, gq_ref, gk_ref, *rest):
    if len(rest) > 8:
        for src, dst in zip(rest[:3], rest[-3:]):
            dst[...] = src[...].astype(BF16)
        rest = rest[3:-3]
    q_ref, k_ref, v_ref, kk_ref, vv_ref, u_ref, sga_ref, sgp_ref = rest
    rows = x_ref.shape[0]
    hb = _rms(x_ref[...], g_ref[...]).astype(BF16)
    cos = cos_ref[...]
    sin = sin_ref[...]
    lane = lax.broadcasted_iota(jnp.int32, (rows, LANES), 1)
    lo = lane < HEAD_DIM
    first_half = (lane % HEAD_DIM) < (HEAD_DIM // 2)

    def head_norm_rope(zc, gain):
        sq = zc * zc
        ss_lo = jnp.sum(jnp.where(lo, sq, 0.0), axis=-1, keepdims=True)
        ss_hi = jnp.sum(jnp.where(lo, 0.0, sq), axis=-1, keepdims=True)
        r = lax.rsqrt(jnp.where(lo, ss_lo, ss_hi) * (1.0 / HEAD_DIM) + RMS_EPS)
        y = zc * r * gain
        partner = jnp.where(first_half, pltpu.roll(y, LANES - HEAD_DIM // 2, 1),
                            pltpu.roll(y, HEAD_DIM // 2, 1))
        return y * cos + partner * sin

    gq = gq_ref[...]
    for j in range(Q_W // 256):
        z = _bdot(hb, w_ref[:, 256 * j:256 * (j + 1)])
        for c in range(2):
            qn = head_norm_rope(z[:, LANES * c:LANES * (c + 1)], gq) * (HEAD_DIM ** -0.5)
            q_ref[:, 256 * j + LANES * c:256 * j + LANES * (c + 1)] = qn.astype(BF16)

    z = _bdot(hb, w_ref[:, OFF_K:OFF_U])
    kn = head_norm_rope(z[:, :KV_W], gk_ref[...])
    vr = z[:, KV_W:]
    k_ref[...] = kn
    v_ref[...] = vr
    kr = pltpu.roll(kn, HEAD_DIM, 1)
    vrr = pltpu.roll(vr, HEAD_DIM, 1)
    kk_ref[:, :LANES] = jnp.where(lo, kn, kr).astype(BF16)
    kk_ref[:, LANES:] = jnp.where(lo, kr, kn).astype(BF16)
    vv_ref[:, :LANES] = jnp.where(lo, vr, vrr).astype(BF16)
    vv_ref[:, LANES:] = jnp.where(lo, vrr, vr).astype(BF16)

    for j in range(D_MODEL // 256):
        u_ref[:, 256 * j:256 * (j + 1)] = _bdot(hb, w_ref[:, OFF_U + 256 * j:OFF_U + 256 * (j + 1)])
        sga_ref[:, 256 * j:256 * (j + 1)] = jax.nn.sigmoid(
            _bdot(hb, w_ref[:, OFF_GA + 256 * j:OFF_GA + 256 * (j + 1)]))
        sgp_ref[:, 256 * j:256 * (j + 1)] = jax.nn.sigmoid(
            _bdot(hb, w_ref[:, OFF_GP + 256 * j:OFF_GP + 256 * (j + 1)]))


def _inproj(x, g, w_in_b, cos, sin, gq, gk, experts=()):
    n = x.shape[0]
    nb = n // ROW_BLOCK
    assert not experts or nb == N_EXPERTS, "one expert's weights are cast per grid step"
    wspec = lambda a: pl.BlockSpec((1,) + a.shape[1:], lambda i: (i, 0, 0))
    ncos = cos.shape[0] // ROW_BLOCK
    row = lambda w: pl.BlockSpec((ROW_BLOCK, w), lambda i: (i, 0))
    full = lambda a: pl.BlockSpec(a.shape, lambda i: (0,) * a.ndim)
    resident = pl.BlockSpec(w_in_b.shape, lambda i: (0, 0), pipeline_mode=pl.Buffered(1))
    tab = pl.BlockSpec((ROW_BLOCK, LANES), lambda i: (i % ncos, 0))
    out_shapes = (
        jax.ShapeDtypeStruct((n, Q_W), BF16),
        jax.ShapeDtypeStruct((n, KV_W), F32),
        jax.ShapeDtypeStruct((n, KV_W), F32),
        jax.ShapeDtypeStruct((n, 2 * LANES), BF16),
        jax.ShapeDtypeStruct((n, 2 * LANES), BF16),
        jax.ShapeDtypeStruct((n, D_MODEL), F32),
        jax.ShapeDtypeStruct((n, D_MODEL), F32),
        jax.ShapeDtypeStruct((n, D_MODEL), F32),
    )
    return pl.pallas_call(
        _inproj_body,
        grid=(nb,),
        in_specs=[row(D_MODEL), full(g), resident, tab, tab, full(gq), full(gk)] + [wspec(a) for a in experts],
        out_specs=tuple(row(s.shape[1]) for s in out_shapes) + tuple(wspec(a) for a in experts),
        out_shape=out_shapes + tuple(jax.ShapeDtypeStruct(a.shape, BF16) for a in experts),
        compiler_params=pltpu.CompilerParams(dimension_semantics=("arbitrary",),
                                             vmem_limit_bytes=VMEM_LIMIT),
        name="inproj",
    )(x, g, w_in_b, cos, sin, gq, gk, *experts)


def _attend(q2, kg, vg, cur_valid, sink_col, prev_dead):
    s = lax.dot_general(q2, kg, (((1,), (1,)), ((), ())), preferred_element_type=F32)
    s_prev = s[:, :WINDOW]
    if prev_dead is not None:
        s_prev = jnp.where(prev_dead, NEG_INF, s_prev)
    sc = jnp.where(cur_valid, s[:, WINDOW:], s_prev)
    m = jnp.maximum(jnp.max(sc, axis=-1, keepdims=True), sink_col)
    p = jnp.exp(sc - m)
    den = jnp.sum(p, axis=-1, keepdims=True) + jnp.exp(sink_col - m)
    pn = p * (1.0 / den)
    p2 = jnp.concatenate([jnp.where(cur_valid, 0.0, pn), jnp.where(cur_valid, pn, 0.0)], axis=1)
    return _bdot(p2.astype(BF16), vg)


POOL_PAD_ROWS = 8


def _mixer_prompt_body(sink_ref, q_ref, kkc_ref, kkp_ref, vvc_ref, vvp_ref, uc_ref, up_ref,
                       attn_ref, pd_ref, ext_ref, s_a, s_b):
    i = pl.program_id(1)
    first = i == 0
    m2 = 2 * WINDOW
    row = lax.broadcasted_iota(jnp.int32, (m2, LANES), 0)
    col = lax.broadcasted_iota(jnp.int32, (m2, LANES), 1)
    cur_valid = (row % WINDOW) >= col
    upper = lax.broadcasted_iota(jnp.int32, (m2, 1), 0) < WINDOW
    lo = lax.broadcasted_iota(jnp.int32, (WINDOW, LANES), 1) < HEAD_DIM

    for j in range(ROW_BLOCK // WINDOW):
        r0 = WINDOW * j
        if j == 0:
            kprev, vprev = kkp_ref[...], vvp_ref[...]
            prev_dead = first
        else:
            kprev, vprev = kkc_ref[r0 - WINDOW:r0, :], vvc_ref[r0 - WINDOW:r0, :]
            prev_dead = None
        kband = jnp.concatenate([kprev, kkc_ref[r0:r0 + WINDOW, :]], axis=0)
        vband = jnp.concatenate([vprev, vvc_ref[r0:r0 + WINDOW, :]], axis=0)
        for g in range(N_KV_HEADS):
            kg = kband[:, LANES * g:LANES * (g + 1)]
            vg = vband[:, LANES * g:LANES * (g + 1)]
            for c in range(4):
                cc = 4 * g + c
                qc = q_ref[r0:r0 + WINDOW, LANES * cc:LANES * (cc + 1)]
                zero = jnp.zeros_like(qc)
                q2 = jnp.concatenate([jnp.where(lo, qc, zero), jnp.where(lo, zero, qc)], axis=0)
                sink_col = jnp.where(upper, sink_ref[2 * cc], sink_ref[2 * cc + 1])
                o = _attend(q2, kg, vg, cur_valid, sink_col, prev_dead)
                attn_ref[r0:r0 + WINDOW, LANES * cc:LANES * (cc + 1)] = jnp.where(
                    lo, o[:WINDOW], o[WINDOW:]).astype(BF16)

    pad, top = POOL_PAD_ROWS, POOL_PAD_ROWS + 16
    end = top + ROW_BLOCK
    for ref in (ext_ref, s_a, s_b):
        ref[0:pad, :] = jnp.zeros((pad, D_MODEL), F32)
    ext_ref[pad:top, :] = jnp.where(first, 0.0, up_ref[...])
    ext_ref[top:, :] = uc_ref[...]
    pos = i * ROW_BLOCK + lax.broadcasted_iota(jnp.int32, (ROW_BLOCK, 1), 0)
    src = ext_ref
    for g, w in enumerate(POOL_WINDOWS):
        c0 = POOL_GROUP_WIDTH * g
        dst = (s_a, s_b)[g % 2]
        shift = w // 2
        dst[pad:end, c0:] = src[pad:end, c0:] + src[pad - shift:end - shift, c0:]
        c1 = c0 + POOL_GROUP_WIDTH
        cnt = jnp.minimum(w, pos + 1).astype(F32)
        pd_ref[:, c0:c1] = (dst[top:end, c0:c1] / cnt - uc_ref[:, c0:c1]).astype(BF16)
        src = dst


def _mixer_prompt(sinks, q, kk, vv, u, batch, seq):
    nb = seq // ROW_BLOCK
    sub = ROW_BLOCK // WINDOW
    cur = lambda w: pl.BlockSpec((ROW_BLOCK, w), lambda b, i: (b * nb + i, 0))
    prev_kv = pl.BlockSpec((WINDOW, 2 * LANES),
                           lambda b, i: (jnp.maximum((b * nb + i) * sub - 1, 0), 0))
    prev_u = pl.BlockSpec((16, D_MODEL),
                          lambda b, i: (jnp.maximum((b * nb + i) * (ROW_BLOCK // 16) - 1, 0), 0))
    n = batch * seq
    return pl.pallas_call(
        _mixer_prompt_body,
        grid=(batch, nb),
        in_specs=[pl.BlockSpec(memory_space=pltpu.SMEM), cur(Q_W), cur(2 * LANES), prev_kv,
                  cur(2 * LANES), prev_kv, cur(D_MODEL), prev_u],
        out_specs=(cur(Q_W), cur(D_MODEL)),
        out_shape=(jax.ShapeDtypeStruct((n, Q_W), BF16), jax.ShapeDtypeStruct((n, D_MODEL), BF16)),
        scratch_shapes=[pltpu.VMEM((POOL_PAD_ROWS + 16 + ROW_BLOCK, D_MODEL), F32)] * 3,
        compiler_params=pltpu.CompilerParams(dimension_semantics=("arbitrary", "arbitrary"),
                                             vmem_limit_bytes=VMEM_LIMIT),
        name="mixer_prompt",
    )(sinks, q, kk, kk, vv, vv, u, u)


SEQ_PAIR_ROWS = 8
SAMPLE_PAIRS_PER_STEP = 8


def _mixer_sample_body(sink_ref, q_ref, k_ref, v_ref, u_ref, ck_ref, cv_ref, st_ref,
                       attn_ref, pd_ref, nk_ref, nv_ref, nu_ref, ext_ref):
    m = SEQ_PAIR_ROWS
    half = m // 2
    hist = POOL_STATE_LEN
    ext_ref[16:, :] = jnp.zeros((ext_ref.shape[0] - 16, D_MODEL), F32)
    row8 = lax.broadcasted_iota(jnp.int32, (m, LANES), 0)
    n_chunks = Q_W // LANES
    mq = n_chunks * m
    row1 = lax.broadcasted_iota(jnp.int32, (m, 1), 0)
    row = lax.broadcasted_iota(jnp.int32, (mq, LANES), 0)
    col = lax.broadcasted_iota(jnp.int32, (mq, LANES), 1)
    cur_valid = (row % half) >= col
    lane8 = lax.broadcasted_iota(jnp.int32, (m, LANES), 1)
    lo8 = lane8 < HEAD_DIM
    top8 = lax.broadcasted_iota(jnp.int32, (m, LANES), 0) < half
    tail = jnp.zeros((WINDOW - m, LANES), F32)
    top = row1 < half
    sink_col = jnp.concatenate([jnp.where(top, sink_ref[2 * cc], sink_ref[2 * cc + 1])
                                for cc in range(n_chunks)], axis=0)

    def pair(p, carry):
        r0 = pl.multiple_of(p * m, m)
        q8 = q_ref[pl.ds(r0, m), :]
        k8 = k_ref[pl.ds(r0, m), :]
        v8 = v_ref[pl.ds(r0, m), :]
        u8 = u_ref[pl.ds(r0, m), :]
        attn_parts = []
        pd_parts = []
        for s in range(2):
            shift = lambda a: a if s == 0 else pltpu.roll(a, half, 0)
            seq = 2 * p + s
            ck, cv = ck_ref[seq], cv_ref[seq]
            ks, vs = shift(k8), shift(v8)
            kb = jnp.concatenate([ck, ks, tail], axis=0)
            vb = jnp.concatenate([cv, vs, tail], axis=0)
            for cache, new, out in ((ck, ks, nk_ref), (cv, vs, nv_ref)):
                up = pltpu.roll(cache, WINDOW - half, 0)
                out[seq, 0:WINDOW - m, :] = up[0:WINDOW - m]
                out[seq, WINDOW - m:WINDOW, :] = jnp.where(row8 < half, up[WINDOW - m:],
                                                           pltpu.roll(new, half, 0))
            qs = shift(q8)
            qr = pltpu.roll(qs, half, 0)
            q2 = []
            for cc in range(n_chunks):
                qa = qs[:, LANES * cc:LANES * (cc + 1)]
                qb = qr[:, LANES * cc:LANES * (cc + 1)]
                if cc < n_chunks // N_KV_HEADS:
                    first = jnp.where(lo8, qa, 0.0)
                    second = jnp.where(lo8, pltpu.roll(qb, HEAD_DIM, 1), 0.0)
                else:
                    first = jnp.where(lo8, 0.0, pltpu.roll(qa, HEAD_DIM, 1))
                    second = jnp.where(lo8, 0.0, qb)
                q2.append(jnp.where(top8, first, second))
            q2 = jnp.concatenate(q2, axis=0).astype(BF16)
            o = _attend(q2, kb.astype(BF16), vb.astype(BF16), cur_valid, sink_col, None)
            chunks = []
            for cc in range(n_chunks):
                oc = o[m * cc:m * (cc + 1)]
                if cc < n_chunks // N_KV_HEADS:
                    chunks.append(jnp.where(lo8, oc, pltpu.roll(pltpu.roll(oc, half, 0), HEAD_DIM, 1)))
                else:
                    chunks.append(jnp.where(lo8, pltpu.roll(oc, HEAD_DIM, 1), pltpu.roll(oc, half, 0)))
            attn_parts.append(jnp.concatenate(chunks, axis=1))

            ext_ref[0:hist, :] = st_ref[0, seq]
            ext_ref[hist:hist + half, :] = shift(u8)[0:half]
            nu_ref[0, seq] = ext_ref[half:hist + half, :]
            cols = []
            for g, w in enumerate(POOL_WINDOWS):
                c0, c1 = POOL_GROUP_WIDTH * g, POOL_GROUP_WIDTH * (g + 1)
                acc = ext_ref[hist:hist + m, c0:c1]
                for k in range(1, w):
                    acc = acc + ext_ref[hist - k:hist - k + m, c0:c1]
                cols.append(acc / float(w) - ext_ref[hist:hist + m, c0:c1])
            pd_parts.append(jnp.concatenate(cols, axis=1))

        attn_ref[pl.ds(r0, m), :] = jnp.where(top, attn_parts[0], pltpu.roll(attn_parts[1], half, 0))
        pd_ref[pl.ds(r0, m), :] = jnp.where(top, pd_parts[0], pltpu.roll(pd_parts[1], half, 0))
        return carry

    lax.fori_loop(0, SAMPLE_PAIRS_PER_STEP, pair, 0)


def _mixer_sample(sinks, q, k, v, u, cache_k, cache_v, state):
    n = q.shape[0]
    m = SEQ_PAIR_ROWS * SAMPLE_PAIRS_PER_STEP
    rows = lambda w: pl.BlockSpec((m, w), lambda i: (i, 0))
    seqs = lambda a: pl.BlockSpec((2 * SAMPLE_PAIRS_PER_STEP,) + a.shape[1:], lambda i: (i, 0, 0))
    st_spec = pl.BlockSpec((1, 2 * SAMPLE_PAIRS_PER_STEP) + state.shape[2:], lambda i: (0, i, 0, 0))
    like = lambda a: jax.ShapeDtypeStruct(a.shape, F32)
    return pl.pallas_call(
        _mixer_sample_body,
        grid=(n // m,),
        in_specs=[pl.BlockSpec(memory_space=pltpu.SMEM), rows(Q_W), rows(KV_W), rows(KV_W),
                  rows(D_MODEL), seqs(cache_k), seqs(cache_v), st_spec],
        out_specs=(rows(Q_W), rows(D_MODEL), seqs(cache_k), seqs(cache_v), st_spec),
        out_shape=(jax.ShapeDtypeStruct((n, Q_W), F32), jax.ShapeDtypeStruct((n, D_MODEL), F32),
                   like(cache_k), like(cache_v), like(state)),
        scratch_shapes=[pltpu.VMEM((16 + SEQ_PAIR_ROWS, D_MODEL), F32)],
        compiler_params=pltpu.CompilerParams(dimension_semantics=("arbitrary",),
                                             vmem_limit_bytes=VMEM_LIMIT),
        name="mixer_sample",
    )(sinks, q, k, v, u, cache_k, cache_v, state)


def _dense_body(x_ref, attn_ref, pd_ref, sga_ref, sgp_ref, mix_ref, ps_ref, wa_ref, wp_ref,
                wo_ref, gf_ref, wrh_ref, wrl_ref, br_ref, ltri_ref, ssel_ref, crow_ref,
                x1_ref, hn_ref, rt_ref, cnt_ref, seg_ref, carry_ref):
    rows = x_ref.shape[0]
    step = pl.program_id(0)

    @pl.when(step % TILE_BLOCKS == 0)
    def _():
        carry_ref[...] = jnp.zeros_like(carry_ref)

    pd = pd_ref[...].astype(BF16)
    pooled = []
    for g in range(len(POOL_WINDOWS)):
        c0, c1 = POOL_GROUP_WIDTH * g, POOL_GROUP_WIDTH * (g + 1)
        pooled.append((_bdot(pd[:, c0:c1], mix_ref[g]) * ps_ref[:, c0:c1]).astype(BF16))
    pooled = jnp.concatenate(pooled, axis=1)
    merged = (sga_ref[...] * _bdot(attn_ref[...].astype(BF16), wa_ref[...])
              + sgp_ref[...] * _bdot(pooled, wp_ref[...]))
    x1 = x_ref[...] + _bdot(merged.astype(BF16), wo_ref[...])
    x1_ref[...] = x1
    hn = _rms(x1, gf_ref[...])
    hn_ref[...] = hn
    hi = hn.astype(BF16)
    lo = (hn - hi.astype(F32)).astype(BF16)
    both = _bdot(hi, wrl_ref[...])
    logits = both[:, :LANES] + both[:, LANES:] + _bdot(lo, wrh_ref[...]) + br_ref[...]

    lane = lax.broadcasted_iota(jnp.int32, (rows, LANES), 1)
    big = jnp.int32(LANES)
    gl = jnp.where(lane < N_EXPERT_GROUPS, logits, NEG_INF)
    gmax = jnp.max(gl, axis=-1, keepdims=True)
    gidx = jnp.min(jnp.where(gl == gmax, lane, big), axis=-1, keepdims=True)
    g_w = 1.0 / jnp.sum(jnp.exp(gl - gmax), axis=-1, keepdims=True)
    e0 = EXPERT_LANE0 + gidx * EXPERTS_PER_GROUP
    el = jnp.where((lane >= e0) & (lane < e0 + EXPERTS_PER_GROUP), logits, NEG_INF)
    l1 = jnp.max(el, axis=-1, keepdims=True)
    i1 = jnp.min(jnp.where(el == l1, lane, big), axis=-1, keepdims=True)
    el2 = jnp.where(lane == i1, NEG_INF, el)
    l2 = jnp.max(el2, axis=-1, keepdims=True)
    i2 = jnp.min(jnp.where(el2 == l2, lane, big), axis=-1, keepdims=True)
    e = jnp.exp(l2 - l1)
    w1 = 1.0 / (1.0 + e)
    w2 = e * w1

    sel = (lane == i1) | (lane == i2)
    onehot = jnp.where(sel, 1.0, 0.0).astype(BF16)
    rank = _bdot(ltri_ref[...], onehot) + carry_ref[0:1, :] + crow_ref[...]
    segsum = _bdot(ssel_ref[...], onehot)
    seg_ref[0] = segsum
    carry_ref[...] = carry_ref[...] + segsum[0:1, :]
    cnt_ref[0] = carry_ref[...]
    r1 = jnp.sum(jnp.where(lane == i1, rank, 0.0), axis=-1, keepdims=True)
    r2 = jnp.sum(jnp.where(lane == i2, rank, 0.0), axis=-1, keepdims=True)
    cols = ((i1 - EXPERT_LANE0).astype(F32), (i2 - EXPERT_LANE0).astype(F32), r1, r2,
            g_w * w1, g_w * w2)
    tile = jnp.zeros((rows, LANES), F32)
    for c, val in enumerate(cols):
        tile = jnp.where(lane == c, val, tile)
    rt_ref[...] = tile


def _dense(x, attn, pd, sga, sgp, mix_b, pool_scale, wa_b, wp_b, wo_b, gf, wr_hi, wr_lo, br,
           ltri, ssel, crow):
    n = x.shape[0]
    nb = n // ROW_BLOCK
    row = lambda w: pl.BlockSpec((ROW_BLOCK, w), lambda i: (i, 0))
    full = lambda a: pl.BlockSpec(a.shape, lambda i: (0,) * a.ndim)
    stat = pl.BlockSpec((1, 8, LANES), lambda i: (i, 0, 0))
    consts = (mix_b, pool_scale, wa_b, wp_b, wo_b, gf, wr_hi, wr_lo, br, ltri, ssel, crow)
    return pl.pallas_call(
        _dense_body,
        grid=(nb,),
        in_specs=[row(D_MODEL)] * 5 + [full(a) for a in consts],
        out_specs=(row(D_MODEL), row(D_MODEL), row(LANES), stat, stat),
        out_shape=(jax.ShapeDtypeStruct((n, D_MODEL), F32),
                   jax.ShapeDtypeStruct((n, D_MODEL), F32),
                   jax.ShapeDtypeStruct((n, LANES), F32),
                   jax.ShapeDtypeStruct((nb, 8, LANES), F32),
                   jax.ShapeDtypeStruct((nb, 8, LANES), F32)),
        scratch_shapes=[pltpu.VMEM((8, LANES), F32)],
        compiler_params=pltpu.CompilerParams(dimension_semantics=("arbitrary",),
                                             vmem_limit_bytes=VMEM_LIMIT),
        name="dense",
    )(x, attn, pd, sga, sgp, *consts)


def _experts_body(nblk_ref, blk0_ref, cnt_ref, slots_ref, wts_ref, hn_p, hn_s, x1_p, x1_s,
                  wg_ref, wu_ref, wd_ref, y_p, y_s,
                  hn_t, acc, xg0, xg1, yb0, yb1, tok, sem_in, sem_out):
    tau = pl.program_id(0)
    e = pl.program_id(1)
    t_rows = TILE_TOKENS
    dummy = 2 * t_rows
    p_rows = pl.ds(pl.multiple_of(tau * TILE_PROMPT, TILE_PROMPT), TILE_PROMPT)
    s_rows = pl.ds(pl.multiple_of(tau * TILE_SAMPLE, TILE_SAMPLE), TILE_SAMPLE)
    tile_p = pl.ds(0, TILE_PROMPT)
    tile_s = pl.ds(TILE_PROMPT, TILE_SAMPLE)

    def in_copies():
        return [pltpu.make_async_copy(hn_p.at[p_rows], hn_t.at[tile_p], sem_in.at[0]),
                pltpu.make_async_copy(hn_s.at[s_rows], hn_t.at[tile_s], sem_in.at[1]),
                pltpu.make_async_copy(x1_p.at[p_rows], acc.at[tile_p], sem_in.at[2]),
                pltpu.make_async_copy(x1_s.at[s_rows], acc.at[tile_s], sem_in.at[3])]

    def out_copies():
        return [pltpu.make_async_copy(acc.at[tile_p], y_p.at[p_rows], sem_out.at[0]),
                pltpu.make_async_copy(acc.at[tile_s], y_s.at[s_rows], sem_out.at[1])]

    xbufs = (xg0, xg1)
    ybufs = (yb0, yb1)
    end_block = blk0_ref[tau, N_EXPERTS - 1] + nblk_ref[tau, N_EXPERTS - 1]
    end_slot = end_block * MOE_ROWS

    def gather_block(b, xdst):
        base = b * MOE_ROWS
        for j in range(MOE_ROWS):
            t = lax.shift_right_logical(tok[base + j], 1)
            xdst[j // 8, pl.ds(j % 8, 1), :] = hn_t[pl.ds(t, 1), :]

    def scatter_block(b, ysrc):
        base = b * MOE_ROWS
        for j0 in range(0, MOE_ROWS, SCATTER_BATCH):
            ents = [tok[base + j0 + i] for i in range(SCATTER_BATCH)]
            rows = [lax.shift_right_logical(en, 1) for en in ents]
            vals = [acc[pl.ds(rows[i], 1), :]
                    + wts_ref[ents[i]] * ysrc[(j0 + i) // 8, pl.ds((j0 + i) % 8, 1), :]
                    for i in range(SCATTER_BATCH)]
            for i in range(SCATTER_BATCH):
                acc[pl.ds(rows[i], 1), :] = vals[i]

    @pl.when(e == 0)
    def _load_tile():
        for cp in in_copies():
            cp.start()

        hn_t[t_rows:, :] = jnp.zeros((DUMMY_ROWS, D_MODEL), F32)
        yb0[...] = jnp.zeros_like(yb0)

        def pad_range(lo, hi):
            def fill(g, c):
                for i in range(8):
                    tok[lo + g * 8 + i] = dummy
                return c
            lax.fori_loop(0, lax.shift_right_logical(hi - lo + 7, 3), fill, 0)
        pad_range(0, MOE_ROWS)
        pad_range(end_slot, end_slot + MOE_ROWS)

        def pad_fill(ei, c):
            s0 = blk0_ref[tau, ei] * MOE_ROWS
            pad_range(s0 + cnt_ref[tau, ei], s0 + nblk_ref[tau, ei] * MOE_ROWS)
            return c
        lax.fori_loop(0, N_EXPERTS, pad_fill, 0)

        def invert(g, c):
            a0 = g * 16
            for i in range(16):
                tok[slots_ref[a0 + i]] = a0 + i
            return c
        lax.fori_loop(0, t_rows // 8, invert, 0)

        for cp in in_copies():
            cp.wait()
        acc[t_rows:, :] = jnp.zeros((DUMMY_ROWS, D_MODEL), F32)
        gather_block(1, xbufs[1])

    first_block = blk0_ref[tau, e]

    def block(jb, c):
        b = first_block + jb
        for par in range(2):
            @pl.when((b & 1) == par)
            def _():
                gather_block(b + 1, xbufs[1 - par])
                xb = xbufs[par][...].reshape(MOE_ROWS, D_MODEL).astype(BF16)
                act = (jax.nn.silu(_bdot(xb, wg_ref[0])) * _bdot(xb, wu_ref[0])).astype(BF16)
                ybufs[par][...] = _bdot(act, wd_ref[0]).reshape(MOE_ROWS // 8, 8, D_MODEL)
                scatter_block(b - 1, ybufs[1 - par])
        return c
    lax.fori_loop(0, nblk_ref[tau, e], block, 0)

    @pl.when(e == N_EXPERTS - 1)
    def _store_tile():
        last = end_block - 1
        for par in range(2):
            @pl.when((last & 1) == par)
            def _():
                scatter_block(last, ybufs[par])
        cps = out_copies()
        for cp in cps:
            cp.start()
        for cp in cps:
            cp.wait()


def _experts(nblk, blk0, cnt, slots, wts, hn_p, hn_s, x1_p, x1_s, w_gate, w_up, w_down):
    any_spec = pl.BlockSpec(memory_space=pl.ANY)
    wspec = lambda a: pl.BlockSpec((1,) + a.shape[1:], lambda t, e, *_: (e, 0, 0))
    smem = lambda a: pl.BlockSpec((a.shape[0] // N_TILES,), lambda t, e, *_: (t,),
                                  memory_space=pltpu.SMEM)
    grid_spec = pltpu.PrefetchScalarGridSpec(
        num_scalar_prefetch=3,
        grid=(N_TILES, N_EXPERTS),
        in_specs=[smem(slots), smem(wts), any_spec, any_spec, any_spec, any_spec,
                  wspec(w_gate), wspec(w_up), wspec(w_down)],
        out_specs=(any_spec, any_spec),
        scratch_shapes=[
            pltpu.VMEM((TILE_TOKENS + DUMMY_ROWS, D_MODEL), F32),
            pltpu.VMEM((TILE_TOKENS + DUMMY_ROWS, D_MODEL), F32),
            pltpu.VMEM((MOE_ROWS // 8, 8, D_MODEL), F32),
            pltpu.VMEM((MOE_ROWS // 8, 8, D_MODEL), F32),
            pltpu.VMEM((MOE_ROWS // 8, 8, D_MODEL), F32),
            pltpu.VMEM((MOE_ROWS // 8, 8, D_MODEL), F32),
            pltpu.SMEM((LIST_CAP,), jnp.int32),
            pltpu.SemaphoreType.DMA((4,)),
            pltpu.SemaphoreType.DMA((2,)),
        ],
    )
    return pl.pallas_call(
        _experts_body,
        grid_spec=grid_spec,
        out_shape=(jax.ShapeDtypeStruct(x1_p.shape, F32), jax.ShapeDtypeStruct(x1_s.shape, F32)),
        compiler_params=pltpu.CompilerParams(dimension_semantics=("arbitrary", "arbitrary"),
                                             vmem_limit_bytes=EXPERTS_VMEM_LIMIT),
        name="experts",
    )(nblk, blk0, cnt, slots, wts, hn_p, hn_s, x1_p, x1_s, w_gate, w_up, w_down)


def _rope_tables(pos):
    half = HEAD_DIM // 2
    inv = ROPE_THETA ** (-jnp.arange(half, dtype=F32) * (2.0 / HEAD_DIM))
    ang = pos.astype(F32)[:, None] * inv[None, :]
    cos = jnp.tile(jnp.cos(ang), (1, LANES // half))
    sin = jnp.sin(ang)
    sin = jnp.tile(jnp.concatenate([-sin, sin], axis=1), (1, LANES // HEAD_DIM))
    return cos, sin


def _split_bf16(w):
    hi = w.astype(BF16)
    return hi, (w - hi.astype(F32)).astype(BF16)


def kernel(x_prompt, x_sample, cache_k, cache_v, state_pool, norm_mix_g, w_in, q_norm_g, k_norm_g,
           attn_sinks, w_attn_branch, pool_mix_w, pool_scale, w_pool_branch, w_out, norm_ffn_g,
           w_route_group, b_route_group, w_route_expert, b_route_expert, w_expert_gate,
           w_expert_up, w_expert_down):
    batch, seq, d = x_prompt.shape
    dec_batch, dec_seq, _ = x_sample.shape
    past_len = 16384
    assert x_prompt.shape == (4, 4096, D_MODEL) and x_sample.shape == (128, 4, D_MODEL)
    assert w_in.shape[0] == 1, "single layer"
    n_p, n_s = batch * seq, dec_batch * dec_seq

    g_mix = norm_mix_g[0][None, :]
    w_in_b = w_in[0].astype(BF16)
    gq = jnp.tile(q_norm_g[0], LANES // HEAD_DIM)[None, :]
    gk = jnp.tile(k_norm_g[0], LANES // HEAD_DIM)[None, :]
    sinks = attn_sinks[0]
    mix_b = pool_mix_w[0].astype(BF16)
    ps = pool_scale[0][None, :]
    wa_b = w_attn_branch[0].astype(BF16)
    wp_b = w_pool_branch[0].astype(BF16)
    wo_b = w_out[0].astype(BF16)
    gf = norm_ffn_g[0][None, :]
    wr = jnp.zeros((D_MODEL, LANES), F32)
    wr = wr.at[:, :N_EXPERT_GROUPS].set(w_route_group[0])
    wr = wr.at[:, EXPERT_LANE0:EXPERT_LANE0 + N_EXPERTS].set(w_route_expert[0])
    wr_hi, wr_lo = _split_bf16(wr)
    br = jnp.zeros((1, LANES), F32)
    br = br.at[0, :N_EXPERT_GROUPS].set(b_route_group[0])
    br = br.at[0, EXPERT_LANE0:EXPERT_LANE0 + N_EXPERTS].set(b_route_expert[0])

    cos_p, sin_p = _rope_tables(jnp.arange(seq, dtype=jnp.int32))
    pos_s = past_len + (jnp.arange(n_s, dtype=jnp.int32) % dec_seq)
    cos_s, sin_s = _rope_tables(pos_s)

    dense_consts = (mix_b, ps, wa_b, wp_b, wo_b, gf, wr_hi, jnp.concatenate([wr_hi, wr_lo], axis=1), br)
    assert n_s == ROW_BLOCK == N_TILES * TILE_SAMPLE and n_p == N_TILES * TILE_PROMPT
    ridx = np.arange(ROW_BLOCK)
    lower = ridx[:, None] > ridx[None, :]
    seg_of = ridx // TILE_SAMPLE
    as_bf16 = lambda a: jnp.asarray(a.astype(np.float32), BF16)
    ltri_p = as_bf16(lower)
    ltri_s = as_bf16(lower & (seg_of[:, None] == seg_of[None, :]))
    ssel_p = as_bf16(np.arange(8)[:, None] == np.zeros_like(ridx)[None, :])
    ssel_s = as_bf16(np.arange(8)[:, None] == seg_of[None, :])

    xp = x_prompt.reshape(n_p, d)
    q, k, v, kk, vv, u, sga, sgp, wg_b, wu_b, wd_b = _inproj(
        xp, g_mix, w_in_b, cos_p, sin_p, gq, gk,
        experts=(w_expert_gate[0], w_expert_up[0], w_expert_down[0]))
    attn, pd = _mixer_prompt(sinks, q, kk, vv, u, batch, seq)
    x1_p, hn_p, rt_p, cnt_p, _ = _dense(xp, attn, pd, sga, sgp, *dense_consts, ltri_p,
                                        ssel_p, jnp.zeros((ROW_BLOCK, LANES), F32))
    last_window = lambda a: a.reshape(batch, seq, KV_W)[:, -WINDOW:].reshape(
        1, batch, WINDOW, N_KV_HEADS, HEAD_DIM)
    new_k_p = last_window(k)
    new_v_p = last_window(v)
    new_u_p = u.reshape(batch, seq, d)[:, -POOL_STATE_LEN:][None]

    xs = x_sample.reshape(n_s, d)
    q, k, v, kk, vv, u, sga, sgp = _inproj(xs, g_mix, w_in_b, cos_s, sin_s, gq, gk)
    ck = cache_k[0].reshape(dec_batch, WINDOW, KV_W)
    cv = cache_v[0].reshape(dec_batch, WINDOW, KV_W)
    attn, pd, nk_s, nv_s, nu_s = _mixer_sample(sinks, q.astype(F32), k, v, u, ck, cv, state_pool)
    cnt_tiles_p = cnt_p[TILE_BLOCKS - 1::TILE_BLOCKS, 0, :]
    crow = jnp.repeat(cnt_tiles_p, TILE_SAMPLE, axis=0)
    x1_s, hn_s, rt_s, _, seg_s = _dense(xs, attn, pd, sga, sgp, *dense_consts, ltri_s,
                                        ssel_s, crow)
    new_k_s = nk_s.reshape(1, dec_batch, WINDOW, N_KV_HEADS, HEAD_DIM)
    new_v_s = nv_s.reshape(1, dec_batch, WINDOW, N_KV_HEADS, HEAD_DIM)
    new_u_s = nu_s

    ex = slice(EXPERT_LANE0, EXPERT_LANE0 + N_EXPERTS)
    cnt = (cnt_tiles_p[:, ex] + seg_s[0, :N_TILES, ex]).astype(jnp.int32)
    nblk = (cnt + (MOE_ROWS - 1)) // MOE_ROWS
    blk0 = 1 + jnp.cumsum(nblk, axis=1) - nblk
    seg = MOE_ROWS * blk0
    rt = jnp.concatenate([rt_p[:, :6].reshape(N_TILES, TILE_PROMPT, 6),
                          rt_s[:, :6].reshape(N_TILES, TILE_SAMPLE, 6)], axis=1)
    expert_hit = rt[:, :, 0:2].astype(jnp.int32)[..., None] == jnp.arange(N_EXPERTS, dtype=jnp.int32)
    slot = (rt[:, :, 2:4].astype(jnp.int32)
            + jnp.sum(jnp.where(expert_hit, seg[:, None, None, :], 0), axis=-1))
    per_tile = lambda a: jnp.pad(a.reshape(N_TILES, 2 * TILE_TOKENS),
                                 ((0, 0), (0, TILE_LIST - 2 * TILE_TOKENS))).reshape(-1)
    slots = per_tile(slot)
    wts = per_tile(rt[:, :, 4:6])
    y_p, y_s = _experts(nblk, blk0, cnt, slots, wts, hn_p, hn_s, x1_p, x1_s, wg_b, wu_b, wd_b)
    y_prompt = y_p.reshape(batch, seq, d)
    y_sample = y_s.reshape(dec_batch, dec_seq, d)

    return (y_prompt, y_sample, new_k_p, new_v_p, new_u_p, new_k_s, new_v_s, new_u_s)
```

```python
import jax
import jax.numpy as jnp
import numpy as np
from jax import lax
from jax.experimental import pallas as pl
from jax.experimental.pallas import tpu as pltpu

F32 = jnp.float32
BF16 = jnp.bfloat16

D_MODEL = 1024
HEAD_DIM = 64
N_Q_HEADS = 16
N_KV_HEADS = 2
WINDOW = 128
ROPE_THETA = 10000.0
POOL_WINDOWS = (2, 4, 8, 16)
POOL_GROUP_WIDTH = D_MODEL // len(POOL_WINDOWS)
POOL_STATE_LEN = max(POOL_WINDOWS) - 1
N_EXPERT_GROUPS = 4
EXPERTS_PER_GROUP = 8
N_EXPERTS = N_EXPERT_GROUPS * EXPERTS_PER_GROUP
D_EXPERT = 512
RMS_EPS = 1e-6
Q_W = N_Q_HEADS * HEAD_DIM
KV_W = N_KV_HEADS * HEAD_DIM
OFF_K = Q_W
OFF_V = OFF_K + KV_W
OFF_U = OFF_V + KV_W
OFF_GA = OFF_U + D_MODEL
OFF_GP = OFF_GA + D_MODEL
IN_W = OFF_GP + D_MODEL

LANES = 128
ROW_BLOCK = 512
EXPERT_LANE0 = 32
VMEM_LIMIT = 56 * 1024 * 1024
EXPERTS_VMEM_LIMIT = 60 * 1024 * 1024
NEG_INF = float("-inf")

N_TILES = 4
TILE_BLOCKS = 8
TILE_PROMPT = TILE_BLOCKS * ROW_BLOCK
TILE_SAMPLE = 128
TILE_TOKENS = TILE_PROMPT + TILE_SAMPLE
MOE_ROWS = 320
DUMMY_ROWS = 8
SCATTER_BATCH = 4
EXPERTS_PER_STEP = 2
TILE_LIST = -(-(2 * TILE_TOKENS + 1) // 1024) * 1024
LIST_CAP = 2 * TILE_TOKENS + N_EXPERTS * (MOE_ROWS - 1) + 2 * MOE_ROWS + 8


def _rms(x, g):
    return x * lax.rsqrt(jnp.mean(x * x, axis=-1, keepdims=True) + RMS_EPS) * g


def _bdot(a, b):
    return jnp.dot(a, b, preferred_element_type=F32)


def _inproj_body(x_ref, g_ref, w_ref, cos_ref, sin_ref, gq_ref, gk_ref,
                 q_ref, k_ref, v_ref, kk_ref, vv_ref, u_ref, sga_ref, sgp_ref):
    rows = x_ref.shape[0]
    hb = _rms(x_ref[...], g_ref[...]).astype(BF16)
    cos = cos_ref[...]
    sin = sin_ref[...]
    lane = lax.broadcasted_iota(jnp.int32, (rows, LANES), 1)
    lo = lane < HEAD_DIM
    first_half = (lane % HEAD_DIM) < (HEAD_DIM // 2)

    def head_norm_rope(zc, gain):
        sq = zc * zc
        ss_lo = jnp.sum(jnp.where(lo, sq, 0.0), axis=-1, keepdims=True)
        ss_hi = jnp.sum(jnp.where(lo, 0.0, sq), axis=-1, keepdims=True)
        r = lax.rsqrt(jnp.where(lo, ss_lo, ss_hi) * (1.0 / HEAD_DIM) + RMS_EPS)
        y = zc * r * gain
        partner = jnp.where(first_half, pltpu.roll(y, LANES - HEAD_DIM // 2, 1),
                            pltpu.roll(y, HEAD_DIM // 2, 1))
        return y * cos + partner * sin

    gq = gq_ref[...]
    for j in range(Q_W // 256):
        z = _bdot(hb, w_ref[:, 256 * j:256 * (j + 1)])
        for c in range(2):
            qn = head_norm_rope(z[:, LANES * c:LANES * (c + 1)], gq) * (HEAD_DIM ** -0.5)
            q_ref[:, 256 * j + LANES * c:256 * j + LANES * (c + 1)] = qn.astype(BF16)

    z = _bdot(hb, w_ref[:, OFF_K:OFF_U])
    kn = head_norm_rope(z[:, :KV_W], gk_ref[...])
    vr = z[:, KV_W:]
    k_ref[...] = kn
    v_ref[...] = vr
    kr = pltpu.roll(kn, HEAD_DIM, 1)
    vrr = pltpu.roll(vr, HEAD_DIM, 1)
    kk_ref[:, :LANES] = jnp.where(lo, kn, kr).astype(BF16)
    kk_ref[:, LANES:] = jnp.where(lo, kr, kn).astype(BF16)
    vv_ref[:, :LANES] = jnp.where(lo, vr, vrr).astype(BF16)
    vv_ref[:, LANES:] = jnp.where(lo, vrr, vr).astype(BF16)

    for j in range(D_MODEL // 256):
        u_ref[:, 256 * j:256 * (j + 1)] = _bdot(hb, w_ref[:, OFF_U + 256 * j:OFF_U + 256 * (j + 1)])
        sga_ref[:, 256 * j:256 * (j + 1)] = jax.nn.sigmoid(
            _bdot(hb, w_ref[:, OFF_GA + 256 * j:OFF_GA + 256 * (j + 1)]))
        sgp_ref[:, 256 * j:256 * (j + 1)] = jax.nn.sigmoid(
            _bdot(hb, w_ref[:, OFF_GP + 256 * j:OFF_GP + 256 * (j + 1)]))


def _inproj(x, g, w_in_b, cos, sin, gq, gk):
    n = x.shape[0]
    nb = n // ROW_BLOCK
    ncos = cos.shape[0] // ROW_BLOCK
    row = lambda w: pl.BlockSpec((ROW_BLOCK, w), lambda i: (i, 0))
    full = lambda a: pl.BlockSpec(a.shape, lambda i: (0,) * a.ndim)
    tab = pl.BlockSpec((ROW_BLOCK, LANES), lambda i: (i % ncos, 0))
    out_shapes = (
        jax.ShapeDtypeStruct((n, Q_W), BF16),
        jax.ShapeDtypeStruct((n, KV_W), F32),
        jax.ShapeDtypeStruct((n, KV_W), F32),
        jax.ShapeDtypeStruct((n, 2 * LANES), BF16),
        jax.ShapeDtypeStruct((n, 2 * LANES), BF16),
        jax.ShapeDtypeStruct((n, D_MODEL), F32),
        jax.ShapeDtypeStruct((n, D_MODEL), F32),
        jax.ShapeDtypeStruct((n, D_MODEL), F32),
    )
    return pl.pallas_call(
        _inproj_body,
        grid=(nb,),
        in_specs=[row(D_MODEL), full(g), full(w_in_b), tab, tab, full(gq), full(gk)],
        out_specs=tuple(row(s.shape[1]) for s in out_shapes),
        out_shape=out_shapes,
        compiler_params=pltpu.CompilerParams(dimension_semantics=("arbitrary",),
                                             vmem_limit_bytes=VMEM_LIMIT),
        name="inproj",
    )(x, g, w_in_b, cos, sin, gq, gk)


def _attend(q2, kg, vg, cur_valid, sink_col, prev_dead):
    s = lax.dot_general(q2, kg, (((1,), (1,)), ((), ())), preferred_element_type=F32)
    s_prev = s[:, :WINDOW]
    if prev_dead is not None:
        s_prev = jnp.where(prev_dead, NEG_INF, s_prev)
    sc = jnp.where(cur_valid, s[:, WINDOW:], s_prev)
    m = jnp.maximum(jnp.max(sc, axis=-1, keepdims=True), sink_col)
    p = jnp.exp(sc - m)
    den = jnp.sum(p, axis=-1, keepdims=True) + jnp.exp(sink_col - m)
    pn = p * (1.0 / den)
    p2 = jnp.concatenate([jnp.where(cur_valid, 0.0, pn), jnp.where(cur_valid, pn, 0.0)], axis=1)
    return _bdot(p2.astype(BF16), vg)


POOL_PAD_ROWS = 8


def _mixer_prompt_body(sink_ref, q_ref, kkc_ref, kkp_ref, vvc_ref, vvp_ref, uc_ref, up_ref,
                       wg_ref, wu_ref, wd_ref,
                       attn_ref, pd_ref, wgb_ref, wub_ref, wdb_ref, ext_ref, s_a, s_b):
    wgb_ref[...] = wg_ref[...].astype(BF16)
    wub_ref[...] = wu_ref[...].astype(BF16)
    wdb_ref[...] = wd_ref[...].astype(BF16)
    i = pl.program_id(1)
    first = i == 0
    m2 = 2 * WINDOW
    row = lax.broadcasted_iota(jnp.int32, (m2, LANES), 0)
    col = lax.broadcasted_iota(jnp.int32, (m2, LANES), 1)
    cur_valid = (row % WINDOW) >= col
    upper = lax.broadcasted_iota(jnp.int32, (m2, 1), 0) < WINDOW
    lo = lax.broadcasted_iota(jnp.int32, (WINDOW, LANES), 1) < HEAD_DIM

    for j in range(ROW_BLOCK // WINDOW):
        r0 = WINDOW * j
        if j == 0:
            kprev, vprev = kkp_ref[...], vvp_ref[...]
            prev_dead = first
        else:
            kprev, vprev = kkc_ref[r0 - WINDOW:r0, :], vvc_ref[r0 - WINDOW:r0, :]
            prev_dead = None
        kband = jnp.concatenate([kprev, kkc_ref[r0:r0 + WINDOW, :]], axis=0)
        vband = jnp.concatenate([vprev, vvc_ref[r0:r0 + WINDOW, :]], axis=0)
        for g in range(N_KV_HEADS):
            kg = kband[:, LANES * g:LANES * (g + 1)]
            vg = vband[:, LANES * g:LANES * (g + 1)]
            for c in range(4):
                cc = 4 * g + c
                qc = q_ref[r0:r0 + WINDOW, LANES * cc:LANES * (cc + 1)]
                zero = jnp.zeros_like(qc)
                q2 = jnp.concatenate([jnp.where(lo, qc, zero), jnp.where(lo, zero, qc)], axis=0)
                sink_col = jnp.where(upper, sink_ref[2 * cc], sink_ref[2 * cc + 1])
                o = _attend(q2, kg, vg, cur_valid, sink_col, prev_dead)
                attn_ref[r0:r0 + WINDOW, LANES * cc:LANES * (cc + 1)] = jnp.where(
                    lo, o[:WINDOW], o[WINDOW:]).astype(BF16)

    pad, top = POOL_PAD_ROWS, POOL_PAD_ROWS + 16
    end = top + ROW_BLOCK
    for ref in (ext_ref, s_a, s_b):
        ref[0:pad, :] = jnp.zeros((pad, D_MODEL), F32)
    ext_ref[pad:top, :] = jnp.where(first, 0.0, up_ref[...])
    ext_ref[top:, :] = uc_ref[...]
    pos = i * ROW_BLOCK + lax.broadcasted_iota(jnp.int32, (ROW_BLOCK, 1), 0)
    src = ext_ref
    for g, w in enumerate(POOL_WINDOWS):
        c0 = POOL_GROUP_WIDTH * g
        dst = (s_a, s_b)[g % 2]
        shift = w // 2
        dst[pad:end, c0:] = src[pad:end, c0:] + src[pad - shift:end - shift, c0:]
        c1 = c0 + POOL_GROUP_WIDTH
        cnt = jnp.minimum(w, pos + 1).astype(F32)
        pd_ref[:, c0:c1] = (dst[top:end, c0:c1] / cnt - uc_ref[:, c0:c1]).astype(BF16)
        src = dst


def _mixer_prompt(sinks, q, kk, vv, u, w_gate, w_up, w_down, batch, seq):
    nb = seq // ROW_BLOCK
    assert batch * nb == N_EXPERTS, "one expert's weights are cast per grid step"
    wspec = lambda a: pl.BlockSpec((1,) + a.shape[1:], lambda b, i: (b * nb + i, 0, 0))
    weights = (w_gate, w_up, w_down)
    sub = ROW_BLOCK // WINDOW
    cur = lambda w: pl.BlockSpec((ROW_BLOCK, w), lambda b, i: (b * nb + i, 0))
    prev_kv = pl.BlockSpec((WINDOW, 2 * LANES),
                           lambda b, i: (jnp.maximum((b * nb + i) * sub - 1, 0), 0))
    prev_u = pl.BlockSpec((16, D_MODEL),
                          lambda b, i: (jnp.maximum((b * nb + i) * (ROW_BLOCK // 16) - 1, 0), 0))
    n = batch * seq
    return pl.pallas_call(
        _mixer_prompt_body,
        grid=(batch, nb),
        in_specs=[pl.BlockSpec(memory_space=pltpu.SMEM), cur(Q_W), cur(2 * LANES), prev_kv,
                  cur(2 * LANES), prev_kv, cur(D_MODEL), prev_u] + [wspec(a) for a in weights],
        out_specs=(cur(Q_W), cur(D_MODEL)) + tuple(wspec(a) for a in weights),
        out_shape=(jax.ShapeDtypeStruct((n, Q_W), BF16), jax.ShapeDtypeStruct((n, D_MODEL), BF16))
        + tuple(jax.ShapeDtypeStruct(a.shape, BF16) for a in weights),
        scratch_shapes=[pltpu.VMEM((POOL_PAD_ROWS + 16 + ROW_BLOCK, D_MODEL), F32)] * 3,
        compiler_params=pltpu.CompilerParams(dimension_semantics=("arbitrary", "arbitrary"),
                                             vmem_limit_bytes=VMEM_LIMIT),
        name="mixer_prompt",
    )(sinks, q, kk, kk, vv, vv, u, u, *weights)


SEQ_PAIR_ROWS = 8
SAMPLE_PAIRS_PER_STEP = 8


def _mixer_sample_body(sink_ref, q_ref, k_ref, v_ref, u_ref, ck_ref, cv_ref, st_ref,
                       attn_ref, pd_ref, nk_ref, nv_ref, nu_ref, ext_ref):
    m = SEQ_PAIR_ROWS
    half = m // 2
    hist = POOL_STATE_LEN
    ext_ref[16:, :] = jnp.zeros((ext_ref.shape[0] - 16, D_MODEL), F32)
    row8 = lax.broadcasted_iota(jnp.int32, (m, LANES), 0)
    n_chunks = Q_W // LANES
    mq = n_chunks * m
    row1 = lax.broadcasted_iota(jnp.int32, (m, 1), 0)
    row = lax.broadcasted_iota(jnp.int32, (mq, LANES), 0)
    col = lax.broadcasted_iota(jnp.int32, (mq, LANES), 1)
    cur_valid = (row % half) >= col
    lane8 = lax.broadcasted_iota(jnp.int32, (m, LANES), 1)
    lo8 = lane8 < HEAD_DIM
    top8 = lax.broadcasted_iota(jnp.int32, (m, LANES), 0) < half
    tail = jnp.zeros((WINDOW - m, LANES), F32)
    top = row1 < half
    sink_col = jnp.concatenate([jnp.where(top, sink_ref[2 * cc], sink_ref[2 * cc + 1])
                                for cc in range(n_chunks)], axis=0)

    def pair(p, carry):
        r0 = pl.multiple_of(p * m, m)
        q8 = q_ref[pl.ds(r0, m), :]
        k8 = k_ref[pl.ds(r0, m), :]
        v8 = v_ref[pl.ds(r0, m), :]
        u8 = u_ref[pl.ds(r0, m), :]
        attn_parts = []
        pd_parts = []
        for s in range(2):
            shift = lambda a: a if s == 0 else pltpu.roll(a, half, 0)
            seq = 2 * p + s
            ck, cv = ck_ref[seq], cv_ref[seq]
            ks, vs = shift(k8), shift(v8)
            kb = jnp.concatenate([ck, ks, tail], axis=0)
            vb = jnp.concatenate([cv, vs, tail], axis=0)
            for cache, new, out in ((ck, ks, nk_ref), (cv, vs, nv_ref)):
                up = pltpu.roll(cache, WINDOW - half, 0)
                out[seq, 0:WINDOW - m, :] = up[0:WINDOW - m]
                out[seq, WINDOW - m:WINDOW, :] = jnp.where(row8 < half, up[WINDOW - m:],
                                                           pltpu.roll(new, half, 0))
            qs = shift(q8)
            qr = pltpu.roll(qs, half, 0)
            q2 = []
            for cc in range(n_chunks):
                qa = qs[:, LANES * cc:LANES * (cc + 1)]
                qb = qr[:, LANES * cc:LANES * (cc + 1)]
                if cc < n_chunks // N_KV_HEADS:
                    first = jnp.where(lo8, qa, 0.0)
                    second = jnp.where(lo8, pltpu.roll(qb, HEAD_DIM, 1), 0.0)
                else:
                    first = jnp.where(lo8, 0.0, pltpu.roll(qa, HEAD_DIM, 1))
                    second = jnp.where(lo8, 0.0, qb)
                q2.append(jnp.where(top8, first, second))
            q2 = jnp.concatenate(q2, axis=0).astype(BF16)
            o = _attend(q2, kb.astype(BF16), vb.astype(BF16), cur_valid, sink_col, None)
            chunks = []
            for cc in range(n_chunks):
                oc = o[m * cc:m * (cc + 1)]
                if cc < n_chunks // N_KV_HEADS:
                    chunks.append(jnp.where(lo8, oc, pltpu.roll(pltpu.roll(oc, half, 0), HEAD_DIM, 1)))
                else:
                    chunks.append(jnp.where(lo8, pltpu.roll(oc, HEAD_DIM, 1), pltpu.roll(oc, half, 0)))
            attn_parts.append(jnp.concatenate(chunks, axis=1))

            ext_ref[0:hist, :] = st_ref[0, seq]
            ext_ref[hist:hist + half, :] = shift(u8)[0:half]
            nu_ref[0, seq] = ext_ref[half:hist + half, :]
            cols = []
            for g, w in enumerate(POOL_WINDOWS):
                c0, c1 = POOL_GROUP_WIDTH * g, POOL_GROUP_WIDTH * (g + 1)
                acc = ext_ref[hist:hist + m, c0:c1]
                for k in range(1, w):
                    acc = acc + ext_ref[hist - k:hist - k + m, c0:c1]
                cols.append(acc / float(w) - ext_ref[hist:hist + m, c0:c1])
            pd_parts.append(jnp.concatenate(cols, axis=1))

        attn_ref[pl.ds(r0, m), :] = jnp.where(top, attn_parts[0], pltpu.roll(attn_parts[1], half, 0))
        pd_ref[pl.ds(r0, m), :] = jnp.where(top, pd_parts[0], pltpu.roll(pd_parts[1], half, 0))
        return carry

    lax.fori_loop(0, SAMPLE_PAIRS_PER_STEP, pair, 0)


def _mixer_sample(sinks, q, k, v, u, cache_k, cache_v, state):
    n = q.shape[0]
    m = SEQ_PAIR_ROWS * SAMPLE_PAIRS_PER_STEP
    rows = lambda w: pl.BlockSpec((m, w), lambda i: (i, 0))
    seqs = lambda a: pl.BlockSpec((2 * SAMPLE_PAIRS_PER_STEP,) + a.shape[1:], lambda i: (i, 0, 0))
    st_spec = pl.BlockSpec((1, 2 * SAMPLE_PAIRS_PER_STEP) + state.shape[2:], lambda i: (0, i, 0, 0))
    like = lambda a: jax.ShapeDtypeStruct(a.shape, F32)
    return pl.pallas_call(
        _mixer_sample_body,
        grid=(n // m,),
        in_specs=[pl.BlockSpec(memory_space=pltpu.SMEM), rows(Q_W), rows(KV_W), rows(KV_W),
                  rows(D_MODEL), seqs(cache_k), seqs(cache_v), st_spec],
        out_specs=(rows(Q_W), rows(D_MODEL), seqs(cache_k), seqs(cache_v), st_spec),
        out_shape=(jax.ShapeDtypeStruct((n, Q_W), F32), jax.ShapeDtypeStruct((n, D_MODEL), F32),
                   like(cache_k), like(cache_v), like(state)),
        scratch_shapes=[pltpu.VMEM((16 + SEQ_PAIR_ROWS, D_MODEL), F32)],
        compiler_params=pltpu.CompilerParams(dimension_semantics=("arbitrary",),
                                             vmem_limit_bytes=VMEM_LIMIT),
        name="mixer_sample",
    )(sinks, q, k, v, u, cache_k, cache_v, state)


def _dense_body(x_ref, attn_ref, pd_ref, sga_ref, sgp_ref, mix_ref, ps_ref, wa_ref, wp_ref,
                wo_ref, gf_ref, wrh_ref, wrl_ref, br_ref, ltri_ref, ssel_ref, crow_ref,
                x1_ref, hn_ref, rt_ref, cnt_ref, seg_ref, carry_ref):
    rows = x_ref.shape[0]
    step = pl.program_id(0)

    @pl.when(step % TILE_BLOCKS == 0)
    def _():
        carry_ref[...] = jnp.zeros_like(carry_ref)

    pd = pd_ref[...].astype(BF16)
    pooled = []
    for g in range(len(POOL_WINDOWS)):
        c0, c1 = POOL_GROUP_WIDTH * g, POOL_GROUP_WIDTH * (g + 1)
        pooled.append((_bdot(pd[:, c0:c1], mix_ref[g]) * ps_ref[:, c0:c1]).astype(BF16))
    pooled = jnp.concatenate(pooled, axis=1)
    merged = (sga_ref[...] * _bdot(attn_ref[...].astype(BF16), wa_ref[...])
              + sgp_ref[...] * _bdot(pooled, wp_ref[...]))
    x1 = x_ref[...] + _bdot(merged.astype(BF16), wo_ref[...])
    x1_ref[...] = x1
    hn = _rms(x1, gf_ref[...])
    hn_ref[...] = hn
    hi = hn.astype(BF16)
    lo = (hn - hi.astype(F32)).astype(BF16)
    both = _bdot(hi, wrl_ref[...])
    logits = both[:, :LANES] + both[:, LANES:] + _bdot(lo, wrh_ref[...]) + br_ref[...]

    lane = lax.broadcasted_iota(jnp.int32, (rows, LANES), 1)
    big = jnp.int32(LANES)
    gl = jnp.where(lane < N_EXPERT_GROUPS, logits, NEG_INF)
    gmax = jnp.max(gl, axis=-1, keepdims=True)
    gidx = jnp.min(jnp.where(gl == gmax, lane, big), axis=-1, keepdims=True)
    g_w = 1.0 / jnp.sum(jnp.exp(gl - gmax), axis=-1, keepdims=True)
    e0 = EXPERT_LANE0 + gidx * EXPERTS_PER_GROUP
    el = jnp.where((lane >= e0) & (lane < e0 + EXPERTS_PER_GROUP), logits, NEG_INF)
    l1 = jnp.max(el, axis=-1, keepdims=True)
    i1 = jnp.min(jnp.where(el == l1, lane, big), axis=-1, keepdims=True)
    el2 = jnp.where(lane == i1, NEG_INF, el)
    l2 = jnp.max(el2, axis=-1, keepdims=True)
    i2 = jnp.min(jnp.where(el2 == l2, lane, big), axis=-1, keepdims=True)
    e = jnp.exp(l2 - l1)
    w1 = 1.0 / (1.0 + e)
    w2 = e * w1

    sel = (lane == i1) | (lane == i2)
    onehot = jnp.where(sel, 1.0, 0.0).astype(BF16)
    rank = _bdot(ltri_ref[...], onehot) + carry_ref[0:1, :] + crow_ref[...]
    segsum = _bdot(ssel_ref[...], onehot)
    seg_ref[0] = segsum
    carry_ref[...] = carry_ref[...] + segsum[0:1, :]
    cnt_ref[0] = carry_ref[...]
    r1 = jnp.sum(jnp.where(lane == i1, rank, 0.0), axis=-1, keepdims=True)
    r2 = jnp.sum(jnp.where(lane == i2, rank, 0.0), axis=-1, keepdims=True)
    cols = ((i1 - EXPERT_LANE0).astype(F32), (i2 - EXPERT_LANE0).astype(F32), r1, r2,
            g_w * w1, g_w * w2)
    tile = jnp.zeros((rows, LANES), F32)
    for c, val in enumerate(cols):
        tile = jnp.where(lane == c, val, tile)
    rt_ref[...] = tile


def _dense(x, attn, pd, sga, sgp, mix_b, pool_scale, wa_b, wp_b, wo_b, gf, wr_hi, wr_lo, br,
           ltri, ssel, crow):
    n = x.shape[0]
    nb = n // ROW_BLOCK
    row = lambda w: pl.BlockSpec((ROW_BLOCK, w), lambda i: (i, 0))
    full = lambda a: pl.BlockSpec(a.shape, lambda i: (0,) * a.ndim)
    stat = pl.BlockSpec((1, 8, LANES), lambda i: (i, 0, 0))
    consts = (mix_b, pool_scale, wa_b, wp_b, wo_b, gf, wr_hi, wr_lo, br, ltri, ssel, crow)
    return pl.pallas_call(
        _dense_body,
        grid=(nb,),
        in_specs=[row(D_MODEL)] * 5 + [full(a) for a in consts],
        out_specs=(row(D_MODEL), row(D_MODEL), row(LANES), stat, stat),
        out_shape=(jax.ShapeDtypeStruct((n, D_MODEL), F32),
                   jax.ShapeDtypeStruct((n, D_MODEL), F32),
                   jax.ShapeDtypeStruct((n, LANES), F32),
                   jax.ShapeDtypeStruct((nb, 8, LANES), F32),
                   jax.ShapeDtypeStruct((nb, 8, LANES), F32)),
        scratch_shapes=[pltpu.VMEM((8, LANES), F32)],
        compiler_params=pltpu.CompilerParams(dimension_semantics=("arbitrary",),
                                             vmem_limit_bytes=VMEM_LIMIT),
        name="dense",
    )(x, attn, pd, sga, sgp, *consts)


def _experts_body(nblk_ref, blk0_ref, cnt_ref, slots_ref, wts_ref, hn_p, hn_s, x1_p, x1_s,
                  wg_ref, wu_ref, wd_ref, y_p, y_s,
                  hn_t, acc, xg0, xg1, yb0, yb1, tok, sem_in, sem_out):
    tau = pl.program_id(0)
    ep = pl.program_id(1)
    last_ep = N_EXPERTS // EXPERTS_PER_STEP - 1
    t_rows = TILE_TOKENS
    dummy = 2 * t_rows
    p_rows = pl.ds(pl.multiple_of(tau * TILE_PROMPT, TILE_PROMPT), TILE_PROMPT)
    s_rows = pl.ds(pl.multiple_of(tau * TILE_SAMPLE, TILE_SAMPLE), TILE_SAMPLE)
    tile_p = pl.ds(0, TILE_PROMPT)
    tile_s = pl.ds(TILE_PROMPT, TILE_SAMPLE)

    def hn_copies():
        return [pltpu.make_async_copy(hn_p.at[p_rows], hn_t.at[tile_p], sem_in.at[0]),
                pltpu.make_async_copy(hn_s.at[s_rows], hn_t.at[tile_s], sem_in.at[1])]

    def x1_copies():
        return [pltpu.make_async_copy(x1_p.at[p_rows], acc.at[tile_p], sem_in.at[2]),
                pltpu.make_async_copy(x1_s.at[s_rows], acc.at[tile_s], sem_in.at[3])]

    def out_copies(t):
        return [pltpu.make_async_copy(acc.at[tile_p], y_p.at[pl.ds(t * TILE_PROMPT, TILE_PROMPT)],
                                      sem_out.at[0]),
                pltpu.make_async_copy(acc.at[tile_s], y_s.at[pl.ds(t * TILE_SAMPLE, TILE_SAMPLE)],
                                      sem_out.at[1])]

    xbufs = (xg0, xg1)
    ybufs = (yb0, yb1)
    end_block = blk0_ref[tau, N_EXPERTS - 1] + nblk_ref[tau, N_EXPERTS - 1]
    end_slot = end_block * MOE_ROWS

    def gather_block(b, xdst):
        base = b * MOE_ROWS
        for j in range(MOE_ROWS):
            t = lax.shift_right_logical(tok[base + j], 1)
            xdst[j // 8, pl.ds(j % 8, 1), :] = hn_t[pl.ds(t, 1), :]

    def scatter_block(b, ysrc):
        base = b * MOE_ROWS
        for j0 in range(0, MOE_ROWS, SCATTER_BATCH):
            ents = [tok[base + j0 + i] for i in range(SCATTER_BATCH)]
            rows = [lax.shift_right_logical(en, 1) for en in ents]
            vals = [acc[pl.ds(rows[i], 1), :]
                    + wts_ref[ents[i]] * ysrc[(j0 + i) // 8, pl.ds((j0 + i) % 8, 1), :]
                    for i in range(SCATTER_BATCH)]
            for i in range(SCATTER_BATCH):
                acc[pl.ds(rows[i], 1), :] = vals[i]

    @pl.when(ep == 0)
    def _load_tile():
        for cp in hn_copies():
            cp.start()

        hn_t[t_rows:, :] = jnp.zeros((DUMMY_ROWS, D_MODEL), F32)
        yb0[...] = jnp.zeros_like(yb0)

        def pad_range(lo, hi):
            def fill(g, c):
                for i in range(8):
                    tok[lo + g * 8 + i] = dummy
                return c
            lax.fori_loop(0, lax.shift_right_logical(hi - lo + 7, 3), fill, 0)
        pad_range(0, MOE_ROWS)
        pad_range(end_slot, end_slot + MOE_ROWS)

        def pad_fill(ei, c):
            s0 = blk0_ref[tau, ei] * MOE_ROWS
            pad_range(s0 + cnt_ref[tau, ei], s0 + nblk_ref[tau, ei] * MOE_ROWS)
            return c
        lax.fori_loop(0, N_EXPERTS, pad_fill, 0)

        def invert(g, c):
            a0 = g * 16
            for i in range(16):
                tok[slots_ref[a0 + i]] = a0 + i
            return c
        lax.fori_loop(0, t_rows // 8, invert, 0)

        @pl.when(tau > 0)
        def _():
            for cp in out_copies(tau - 1):
                cp.wait()
        for cp in x1_copies():
            cp.start()
        for cp in hn_copies() + x1_copies():
            cp.wait()
        acc[t_rows:, :] = jnp.zeros((DUMMY_ROWS, D_MODEL), F32)
        gather_block(1, xbufs[1])

    def expert(ei, carry):
        e = ep * EXPERTS_PER_STEP + ei
        first_block = blk0_ref[tau, e]

        def block(jb, c):
            b = first_block + jb
            for par in range(2):
                @pl.when((b & 1) == par)
                def _():
                    gather_block(b + 1, xbufs[1 - par])
                    xb = xbufs[par][...].reshape(MOE_ROWS, D_MODEL).astype(BF16)
                    act = (jax.nn.silu(_bdot(xb, wg_ref[ei])) * _bdot(xb, wu_ref[ei])).astype(BF16)
                    ybufs[par][...] = _bdot(act, wd_ref[ei]).reshape(MOE_ROWS // 8, 8, D_MODEL)
                    scatter_block(b - 1, ybufs[1 - par])
            return c
        lax.fori_loop(0, nblk_ref[tau, e], block, 0)
        return carry
    lax.fori_loop(0, EXPERTS_PER_STEP, expert, 0)

    @pl.when(ep == last_ep)
    def _store_tile():
        last = end_block - 1
        for par in range(2):
            @pl.when((last & 1) == par)
            def _():
                scatter_block(last, ybufs[par])
        for cp in out_copies(tau):
            cp.start()

        @pl.when(tau == N_TILES - 1)
        def _():
            for cp in out_copies(tau):
                cp.wait()


def _experts(nblk, blk0, cnt, slots, wts, hn_p, hn_s, x1_p, x1_s, w_gate, w_up, w_down):
    any_spec = pl.BlockSpec(memory_space=pl.ANY)
    wspec = lambda a: pl.BlockSpec((EXPERTS_PER_STEP,) + a.shape[1:], lambda t, e, *_: (e, 0, 0))
    smem = lambda a: pl.BlockSpec((a.shape[0] // N_TILES,), lambda t, e, *_: (t,),
                                  memory_space=pltpu.SMEM)
    grid_spec = pltpu.PrefetchScalarGridSpec(
        num_scalar_prefetch=3,
        grid=(N_TILES, N_EXPERTS // EXPERTS_PER_STEP),
        in_specs=[smem(slots), smem(wts), any_spec, any_spec, any_spec, any_spec,
                  wspec(w_gate), wspec(w_up), wspec(w_down)],
        out_specs=(any_spec, any_spec),
        scratch_shapes=[
            pltpu.VMEM((TILE_TOKENS + DUMMY_ROWS, D_MODEL), F32),
            pltpu.VMEM((TILE_TOKENS + DUMMY_ROWS, D_MODEL), F32),
            pltpu.VMEM((MOE_ROWS // 8, 8, D_MODEL), F32),
            pltpu.VMEM((MOE_ROWS // 8, 8, D_MODEL), F32),
            pltpu.VMEM((MOE_ROWS // 8, 8, D_MODEL), F32),
            pltpu.VMEM((MOE_ROWS // 8, 8, D_MODEL), F32),
            pltpu.SMEM((LIST_CAP,), jnp.int32),
            pltpu.SemaphoreType.DMA((4,)),
            pltpu.SemaphoreType.DMA((2,)),
        ],
    )
    return pl.pallas_call(
        _experts_body,
        grid_spec=grid_spec,
        out_shape=(jax.ShapeDtypeStruct(x1_p.shape, F32), jax.ShapeDtypeStruct(x1_s.shape, F32)),
        compiler_params=pltpu.CompilerParams(dimension_semantics=("arbitrary", "arbitrary"),
                                             vmem_limit_bytes=EXPERTS_VMEM_LIMIT),
        name="experts",
    )(nblk, blk0, cnt, slots, wts, hn_p, hn_s, x1_p, x1_s, w_gate, w_up, w_down)


def _rope_tables(pos):
    half = HEAD_DIM // 2
    inv = ROPE_THETA ** (-jnp.arange(half, dtype=F32) * (2.0 / HEAD_DIM))
    ang = pos.astype(F32)[:, None] * inv[None, :]
    cos = jnp.tile(jnp.cos(ang), (1, LANES // half))
    sin = jnp.sin(ang)
    sin = jnp.tile(jnp.concatenate([-sin, sin], axis=1), (1, LANES // HEAD_DIM))
    return cos, sin


def _split_bf16(w):
    hi = w.astype(BF16)
    return hi, (w - hi.astype(F32)).astype(BF16)


def kernel(x_prompt, x_sample, cache_k, cache_v, state_pool, norm_mix_g, w_in, q_norm_g, k_norm_g,
           attn_sinks, w_attn_branch, pool_mix_w, pool_scale, w_pool_branch, w_out, norm_ffn_g,
           w_route_group, b_route_group, w_route_expert, b_route_expert, w_expert_gate,
           w_expert_up, w_expert_down):
    batch, seq, d = x_prompt.shape
    dec_batch, dec_seq, _ = x_sample.shape
    past_len = 16384
    assert x_prompt.shape == (4, 4096, D_MODEL) and x_sample.shape == (128, 4, D_MODEL)
    assert w_in.shape[0] == 1, "single layer"
    n_p, n_s = batch * seq, dec_batch * dec_seq

    g_mix = norm_mix_g[0][None, :]
    w_in_b = w_in[0].astype(BF16)
    gq = jnp.tile(q_norm_g[0], LANES // HEAD_DIM)[None, :]
    gk = jnp.tile(k_norm_g[0], LANES // HEAD_DIM)[None, :]
    sinks = attn_sinks[0]
    mix_b = pool_mix_w[0].astype(BF16)
    ps = pool_scale[0][None, :]
    wa_b = w_attn_branch[0].astype(BF16)
    wp_b = w_pool_branch[0].astype(BF16)
    wo_b = w_out[0].astype(BF16)
    gf = norm_ffn_g[0][None, :]
    wr = jnp.zeros((D_MODEL, LANES), F32)
    wr = wr.at[:, :N_EXPERT_GROUPS].set(w_route_group[0])
    wr = wr.at[:, EXPERT_LANE0:EXPERT_LANE0 + N_EXPERTS].set(w_route_expert[0])
    wr_hi, wr_lo = _split_bf16(wr)
    br = jnp.zeros((1, LANES), F32)
    br = br.at[0, :N_EXPERT_GROUPS].set(b_route_group[0])
    br = br.at[0, EXPERT_LANE0:EXPERT_LANE0 + N_EXPERTS].set(b_route_expert[0])

    cos_p, sin_p = _rope_tables(jnp.arange(seq, dtype=jnp.int32))
    pos_s = past_len + (jnp.arange(n_s, dtype=jnp.int32) % dec_seq)
    cos_s, sin_s = _rope_tables(pos_s)

    dense_consts = (mix_b, ps, wa_b, wp_b, wo_b, gf, wr_hi, jnp.concatenate([wr_hi, wr_lo], axis=1), br)
    assert n_s == ROW_BLOCK == N_TILES * TILE_SAMPLE and n_p == N_TILES * TILE_PROMPT
    ridx = np.arange(ROW_BLOCK)
    lower = ridx[:, None] > ridx[None, :]
    seg_of = ridx // TILE_SAMPLE
    as_bf16 = lambda a: jnp.asarray(a.astype(np.float32), BF16)
    ltri_p = as_bf16(lower)
    ltri_s = as_bf16(lower & (seg_of[:, None] == seg_of[None, :]))
    ssel_p = as_bf16(np.arange(8)[:, None] == np.zeros_like(ridx)[None, :])
    ssel_s = as_bf16(np.arange(8)[:, None] == seg_of[None, :])

    xp = x_prompt.reshape(n_p, d)
    q, k, v, kk, vv, u, sga, sgp = _inproj(xp, g_mix, w_in_b, cos_p, sin_p, gq, gk)
    attn, pd, wg_b, wu_b, wd_b = _mixer_prompt(sinks, q, kk, vv, u, w_expert_gate[0], w_expert_up[0],
                                               w_expert_down[0], batch, seq)
    x1_p, hn_p, rt_p, cnt_p, _ = _dense(xp, attn, pd, sga, sgp, *dense_consts, ltri_p,
                                        ssel_p, jnp.zeros((ROW_BLOCK, LANES), F32))
    last_window = lambda a: a.reshape(batch, seq, KV_W)[:, -WINDOW:].reshape(
        1, batch, WINDOW, N_KV_HEADS, HEAD_DIM)
    new_k_p = last_window(k)
    new_v_p = last_window(v)
    new_u_p = u.reshape(batch, seq, d)[:, -POOL_STATE_LEN:][None]

    xs = x_sample.reshape(n_s, d)
    q, k, v, kk, vv, u, sga, sgp = _inproj(xs, g_mix, w_in_b, cos_s, sin_s, gq, gk)
    ck = cache_k[0].reshape(dec_batch, WINDOW, KV_W)
    cv = cache_v[0].reshape(dec_batch, WINDOW, KV_W)
    attn, pd, nk_s, nv_s, nu_s = _mixer_sample(sinks, q.astype(F32), k, v, u, ck, cv, state_pool)
    cnt_tiles_p = cnt_p[TILE_BLOCKS - 1::TILE_BLOCKS, 0, :]
    crow = jnp.repeat(cnt_tiles_p, TILE_SAMPLE, axis=0)
    x1_s, hn_s, rt_s, _, seg_s = _dense(xs, attn, pd, sga, sgp, *dense_consts, ltri_s,
                                        ssel_s, crow)
    new_k_s = nk_s.reshape(1, dec_batch, WINDOW, N_KV_HEADS, HEAD_DIM)
    new_v_s = nv_s.reshape(1, dec_batch, WINDOW, N_KV_HEADS, HEAD_DIM)
    new_u_s = nu_s

    ex = slice(EXPERT_LANE0, EXPERT_LANE0 + N_EXPERTS)
    cnt = (cnt_tiles_p[:, ex] + seg_s[0, :N_TILES, ex]).astype(jnp.int32)
    nblk = (cnt + (MOE_ROWS - 1)) // MOE_ROWS
    blk0 = 1 + jnp.cumsum(nblk, axis=1) - nblk
    seg = MOE_ROWS * blk0
    rt = jnp.concatenate([rt_p[:, :6].reshape(N_TILES, TILE_PROMPT, 6),
                          rt_s[:, :6].reshape(N_TILES, TILE_SAMPLE, 6)], axis=1)
    expert_hit = rt[:, :, 0:2].astype(jnp.int32)[..., None] == jnp.arange(N_EXPERTS, dtype=jnp.int32)
    slot = (rt[:, :, 2:4].astype(jnp.int32)
            + jnp.sum(jnp.where(expert_hit, seg[:, None, None, :], 0), axis=-1))
    per_tile = lambda a: jnp.pad(a.reshape(N_TILES, 2 * TILE_TOKENS),
                                 ((0, 0), (0, TILE_LIST - 2 * TILE_TOKENS))).reshape(-1)
    slots = per_tile(slot)
    wts = per_tile(rt[:, :, 4:6])
    y_p, y_s = _experts(nblk, blk0, cnt, slots, wts, hn_p, hn_s, x1_p, x1_s, wg_b, wu_b, wd_b)
    y_prompt = y_p.reshape(batch, seq, d)
    y_sample = y_s.reshape(dec_batch, dec_seq, d)

    return (y_prompt, y_sample, new_k_p, new_v_p, new_u_p, new_k_s, new_v_s, new_u_s)
```

```python
import jax
import jax.numpy as jnp
import numpy as np
from jax import lax
from jax.experimental import pallas as pl
from jax.experimental.pallas import tpu as pltpu

F32 = jnp.float32
BF16 = jnp.bfloat16

D_MODEL = 1024
HEAD_DIM = 64
N_Q_HEADS = 16
N_KV_HEADS = 2
WINDOW = 128
ROPE_THETA = 10000.0
POOL_WINDOWS = (2, 4, 8, 16)
POOL_GROUP_WIDTH = D_MODEL // len(POOL_WINDOWS)
POOL_STATE_LEN = max(POOL_WINDOWS) - 1
N_EXPERT_GROUPS = 4
EXPERTS_PER_GROUP = 8
N_EXPERTS = N_EXPERT_GROUPS * EXPERTS_PER_GROUP
D_EXPERT = 512
RMS_EPS = 1e-6
Q_W = N_Q_HEADS * HEAD_DIM
KV_W = N_KV_HEADS * HEAD_DIM
OFF_K = Q_W
OFF_V = OFF_K + KV_W
OFF_U = OFF_V + KV_W
OFF_GA = OFF_U + D_MODEL
OFF_GP = OFF_GA + D_MODEL
IN_W = OFF_GP + D_MODEL

LANES = 128
ROW_BLOCK = 512
EXPERT_LANE0 = 32
VMEM_LIMIT = 56 * 1024 * 1024
EXPERTS_VMEM_LIMIT = 60 * 1024 * 1024
NEG_INF = float("-inf")

N_TILES = 4
TILE_BLOCKS = 8
TILE_PROMPT = TILE_BLOCKS * ROW_BLOCK
TILE_SAMPLE = 128
TILE_TOKENS = TILE_PROMPT + TILE_SAMPLE
MOE_ROWS = 320
SCATTER_BATCH = 4
TOKEN_ROWS = D_MODEL // LANES
STAGE_ROWS = 512
TILE_LIST = -(-(2 * TILE_TOKENS + 1) // 1024) * 1024
LIST_CAP = 2 * TILE_TOKENS + N_EXPERTS * (MOE_ROWS - 1) + 2 * MOE_ROWS + 8


def _rms(x, g):
    return x * lax.rsqrt(jnp.mean(x * x, axis=-1, keepdims=True) + RMS_EPS) * g


def _bdot(a, b):
    return jnp.dot(a, b, preferred_element_type=F32)


def _inproj_body(x_ref, g_ref, w_ref, cos_ref, sin_ref, gq_ref, gk_ref,
                 q_ref, k_ref, v_ref, kk_ref, vv_ref, u_ref, sga_ref, sgp_ref):
    rows = x_ref.shape[0]
    hb = _rms(x_ref[...], g_ref[...]).astype(BF16)
    cos = cos_ref[...]
    sin = sin_ref[...]
    lane = lax.broadcasted_iota(jnp.int32, (rows, LANES), 1)
    lo = lane < HEAD_DIM
    first_half = (lane % HEAD_DIM) < (HEAD_DIM // 2)

    def head_norm_rope(zc, gain):
        sq = zc * zc
        ss_lo = jnp.sum(jnp.where(lo, sq, 0.0), axis=-1, keepdims=True)
        ss_hi = jnp.sum(jnp.where(lo, 0.0, sq), axis=-1, keepdims=True)
        r = lax.rsqrt(jnp.where(lo, ss_lo, ss_hi) * (1.0 / HEAD_DIM) + RMS_EPS)
        y = zc * r * gain
        partner = jnp.where(first_half, pltpu.roll(y, LANES - HEAD_DIM // 2, 1),
                            pltpu.roll(y, HEAD_DIM // 2, 1))
        return y * cos + partner * sin

    gq = gq_ref[...]
    for j in range(Q_W // 256):
        z = _bdot(hb, w_ref[:, 256 * j:256 * (j + 1)])
        for c in range(2):
            qn = head_norm_rope(z[:, LANES * c:LANES * (c + 1)], gq) * (HEAD_DIM ** -0.5)
            q_ref[:, 256 * j + LANES * c:256 * j + LANES * (c + 1)] = qn.astype(BF16)

    z = _bdot(hb, w_ref[:, OFF_K:OFF_U])
    kn = head_norm_rope(z[:, :KV_W], gk_ref[...])
    vr = z[:, KV_W:]
    k_ref[...] = kn
    v_ref[...] = vr
    kr = pltpu.roll(kn, HEAD_DIM, 1)
    vrr = pltpu.roll(vr, HEAD_DIM, 1)
    kk_ref[:, :LANES] = jnp.where(lo, kn, kr).astype(BF16)
    kk_ref[:, LANES:] = jnp.where(lo, kr, kn).astype(BF16)
    vv_ref[:, :LANES] = jnp.where(lo, vr, vrr).astype(BF16)
    vv_ref[:, LANES:] = jnp.where(lo, vrr, vr).astype(BF16)

    for j in range(D_MODEL // 256):
        u_ref[:, 256 * j:256 * (j + 1)] = _bdot(hb, w_ref[:, OFF_U + 256 * j:OFF_U + 256 * (j + 1)])
        sga_ref[:, 256 * j:256 * (j + 1)] = jax.nn.sigmoid(
            _bdot(hb, w_ref[:, OFF_GA + 256 * j:OFF_GA + 256 * (j + 1)]))
        sgp_ref[:, 256 * j:256 * (j + 1)] = jax.nn.sigmoid(
            _bdot(hb, w_ref[:, OFF_GP + 256 * j:OFF_GP + 256 * (j + 1)]))


def _inproj(x, g, w_in_b, cos, sin, gq, gk):
    n = x.shape[0]
    nb = n // ROW_BLOCK
    ncos = cos.shape[0] // ROW_BLOCK
    row = lambda w: pl.BlockSpec((ROW_BLOCK, w), lambda i: (i, 0))
    full = lambda a: pl.BlockSpec(a.shape, lambda i: (0,) * a.ndim)
    tab = pl.BlockSpec((ROW_BLOCK, LANES), lambda i: (i % ncos, 0))
    out_shapes = (
        jax.ShapeDtypeStruct((n, Q_W), BF16),
        jax.ShapeDtypeStruct((n, KV_W), F32),
        jax.ShapeDtypeStruct((n, KV_W), F32),
        jax.ShapeDtypeStruct((n, 2 * LANES), BF16),
        jax.ShapeDtypeStruct((n, 2 * LANES), BF16),
        jax.ShapeDtypeStruct((n, D_MODEL), F32),
        jax.ShapeDtypeStruct((n, D_MODEL), F32),
        jax.ShapeDtypeStruct((n, D_MODEL), F32),
    )
    return pl.pallas_call(
        _inproj_body,
        grid=(nb,),
        in_specs=[row(D_MODEL), full(g), full(w_in_b), tab, tab, full(gq), full(gk)],
        out_specs=tuple(row(s.shape[1]) for s in out_shapes),
        out_shape=out_shapes,
        compiler_params=pltpu.CompilerParams(dimension_semantics=("arbitrary",),
                                             vmem_limit_bytes=VMEM_LIMIT),
        name="inproj",
    )(x, g, w_in_b, cos, sin, gq, gk)


def _attend(q2, kg, vg, cur_valid, sink_col, prev_dead):
    s = lax.dot_general(q2, kg, (((1,), (1,)), ((), ())), preferred_element_type=F32)
    s_prev = s[:, :WINDOW]
    if prev_dead is not None:
        s_prev = jnp.where(prev_dead, NEG_INF, s_prev)
    sc = jnp.where(cur_valid, s[:, WINDOW:], s_prev)
    m = jnp.maximum(jnp.max(sc, axis=-1, keepdims=True), sink_col)
    p = jnp.exp(sc - m)
    den = jnp.sum(p, axis=-1, keepdims=True) + jnp.exp(sink_col - m)
    pn = p * (1.0 / den)
    p2 = jnp.concatenate([jnp.where(cur_valid, 0.0, pn), jnp.where(cur_valid, pn, 0.0)], axis=1)
    return _bdot(p2.astype(BF16), vg)


POOL_PAD_ROWS = 8


def _mixer_prompt_body(sink_ref, q_ref, kkc_ref, kkp_ref, vvc_ref, vvp_ref, uc_ref, up_ref,
                       wg_ref, wu_ref, wd_ref,
                       attn_ref, pd_ref, wgb_ref, wub_ref, wdb_ref, ext_ref, s_a, s_b):
    wgb_ref[...] = wg_ref[...].astype(BF16)
    wub_ref[...] = wu_ref[...].astype(BF16)
    wdb_ref[...] = wd_ref[...].astype(BF16)
    i = pl.program_id(1)
    first = i == 0
    m2 = 2 * WINDOW
    row = lax.broadcasted_iota(jnp.int32, (m2, LANES), 0)
    col = lax.broadcasted_iota(jnp.int32, (m2, LANES), 1)
    cur_valid = (row % WINDOW) >= col
    upper = lax.broadcasted_iota(jnp.int32, (m2, 1), 0) < WINDOW
    lo = lax.broadcasted_iota(jnp.int32, (WINDOW, LANES), 1) < HEAD_DIM

    for j in range(ROW_BLOCK // WINDOW):
        r0 = WINDOW * j
        if j == 0:
            kprev, vprev = kkp_ref[...], vvp_ref[...]
            prev_dead = first
        else:
            kprev, vprev = kkc_ref[r0 - WINDOW:r0, :], vvc_ref[r0 - WINDOW:r0, :]
            prev_dead = None
        kband = jnp.concatenate([kprev, kkc_ref[r0:r0 + WINDOW, :]], axis=0)
        vband = jnp.concatenate([vprev, vvc_ref[r0:r0 + WINDOW, :]], axis=0)
        for g in range(N_KV_HEADS):
            kg = kband[:, LANES * g:LANES * (g + 1)]
            vg = vband[:, LANES * g:LANES * (g + 1)]
            for c in range(4):
                cc = 4 * g + c
                qc = q_ref[r0:r0 + WINDOW, LANES * cc:LANES * (cc + 1)]
                zero = jnp.zeros_like(qc)
                q2 = jnp.concatenate([jnp.where(lo, qc, zero), jnp.where(lo, zero, qc)], axis=0)
                sink_col = jnp.where(upper, sink_ref[2 * cc], sink_ref[2 * cc + 1])
                o = _attend(q2, kg, vg, cur_valid, sink_col, prev_dead)
                attn_ref[r0:r0 + WINDOW, LANES * cc:LANES * (cc + 1)] = jnp.where(
                    lo, o[:WINDOW], o[WINDOW:]).astype(BF16)

    pad, top = POOL_PAD_ROWS, POOL_PAD_ROWS + 16
    end = top + ROW_BLOCK
    for ref in (ext_ref, s_a, s_b):
        ref[0:pad, :] = jnp.zeros((pad, D_MODEL), F32)
    ext_ref[pad:top, :] = jnp.where(first, 0.0, up_ref[...])
    ext_ref[top:, :] = uc_ref[...]
    pos = i * ROW_BLOCK + lax.broadcasted_iota(jnp.int32, (ROW_BLOCK, 1), 0)
    src = ext_ref
    for g, w in enumerate(POOL_WINDOWS):
        c0 = POOL_GROUP_WIDTH * g
        dst = (s_a, s_b)[g % 2]
        shift = w // 2
        dst[pad:end, c0:] = src[pad:end, c0:] + src[pad - shift:end - shift, c0:]
        c1 = c0 + POOL_GROUP_WIDTH
        cnt = jnp.minimum(w, pos + 1).astype(F32)
        pd_ref[:, c0:c1] = (dst[top:end, c0:c1] / cnt - uc_ref[:, c0:c1]).astype(BF16)
        src = dst


def _mixer_prompt(sinks, q, kk, vv, u, w_gate, w_up, w_down, batch, seq):
    nb = seq // ROW_BLOCK
    assert batch * nb == N_EXPERTS, "one expert's weights are cast per grid step"
    wspec = lambda a: pl.BlockSpec((1,) + a.shape[1:], lambda b, i: (b * nb + i, 0, 0))
    weights = (w_gate, w_up, w_down)
    sub = ROW_BLOCK // WINDOW
    cur = lambda w: pl.BlockSpec((ROW_BLOCK, w), lambda b, i: (b * nb + i, 0))
    prev_kv = pl.BlockSpec((WINDOW, 2 * LANES),
                           lambda b, i: (jnp.maximum((b * nb + i) * sub - 1, 0), 0))
    prev_u = pl.BlockSpec((16, D_MODEL),
                          lambda b, i: (jnp.maximum((b * nb + i) * (ROW_BLOCK // 16) - 1, 0), 0))
    n = batch * seq
    return pl.pallas_call(
        _mixer_prompt_body,
        grid=(batch, nb),
        in_specs=[pl.BlockSpec(memory_space=pltpu.SMEM), cur(Q_W), cur(2 * LANES), prev_kv,
                  cur(2 * LANES), prev_kv, cur(D_MODEL), prev_u] + [wspec(a) for a in weights],
        out_specs=(cur(Q_W), cur(D_MODEL)) + tuple(wspec(a) for a in weights),
        out_shape=(jax.ShapeDtypeStruct((n, Q_W), BF16), jax.ShapeDtypeStruct((n, D_MODEL), BF16))
        + tuple(jax.ShapeDtypeStruct(a.shape, BF16) for a in weights),
        scratch_shapes=[pltpu.VMEM((POOL_PAD_ROWS + 16 + ROW_BLOCK, D_MODEL), F32)] * 3,
        compiler_params=pltpu.CompilerParams(dimension_semantics=("arbitrary", "arbitrary"),
                                             vmem_limit_bytes=VMEM_LIMIT),
        name="mixer_prompt",
    )(sinks, q, kk, kk, vv, vv, u, u, *weights)


SEQ_PAIR_ROWS = 8
SAMPLE_PAIRS_PER_STEP = 8


def _mixer_sample_body(sink_ref, q_ref, k_ref, v_ref, u_ref, ck_ref, cv_ref, st_ref,
                       attn_ref, pd_ref, nk_ref, nv_ref, nu_ref, ext_ref):
    m = SEQ_PAIR_ROWS
    half = m // 2
    hist = POOL_STATE_LEN
    ext_ref[16:, :] = jnp.zeros((ext_ref.shape[0] - 16, D_MODEL), F32)
    row8 = lax.broadcasted_iota(jnp.int32, (m, LANES), 0)
    n_chunks = Q_W // LANES
    mq = n_chunks * m
    row1 = lax.broadcasted_iota(jnp.int32, (m, 1), 0)
    row = lax.broadcasted_iota(jnp.int32, (mq, LANES), 0)
    col = lax.broadcasted_iota(jnp.int32, (mq, LANES), 1)
    cur_valid = (row % half) >= col
    lane8 = lax.broadcasted_iota(jnp.int32, (m, LANES), 1)
    lo8 = lane8 < HEAD_DIM
    top8 = lax.broadcasted_iota(jnp.int32, (m, LANES), 0) < half
    tail = jnp.zeros((WINDOW - m, LANES), F32)
    top = row1 < half
    sink_col = jnp.concatenate([jnp.where(top, sink_ref[2 * cc], sink_ref[2 * cc + 1])
                                for cc in range(n_chunks)], axis=0)

    def pair(p, carry):
        r0 = pl.multiple_of(p * m, m)
        q8 = q_ref[pl.ds(r0, m), :]
        k8 = k_ref[pl.ds(r0, m), :]
        v8 = v_ref[pl.ds(r0, m), :]
        u8 = u_ref[pl.ds(r0, m), :]
        attn_parts = []
        pd_parts = []
        for s in range(2):
            shift = lambda a: a if s == 0 else pltpu.roll(a, half, 0)
            seq = 2 * p + s
            ck, cv = ck_ref[seq], cv_ref[seq]
            ks, vs = shift(k8), shift(v8)
            kb = jnp.concatenate([ck, ks, tail], axis=0)
            vb = jnp.concatenate([cv, vs, tail], axis=0)
            for cache, new, out in ((ck, ks, nk_ref), (cv, vs, nv_ref)):
                up = pltpu.roll(cache, WINDOW - half, 0)
                out[seq, 0:WINDOW - m, :] = up[0:WINDOW - m]
                out[seq, WINDOW - m:WINDOW, :] = jnp.where(row8 < half, up[WINDOW - m:],
                                                           pltpu.roll(new, half, 0))
            qs = shift(q8)
            qr = pltpu.roll(qs, half, 0)
            q2 = []
            for cc in range(n_chunks):
                qa = qs[:, LANES * cc:LANES * (cc + 1)]
                qb = qr[:, LANES * cc:LANES * (cc + 1)]
                if cc < n_chunks // N_KV_HEADS:
                    first = jnp.where(lo8, qa, 0.0)
                    second = jnp.where(lo8, pltpu.roll(qb, HEAD_DIM, 1), 0.0)
                else:
                    first = jnp.where(lo8, 0.0, pltpu.roll(qa, HEAD_DIM, 1))
                    second = jnp.where(lo8, 0.0, qb)
                q2.append(jnp.where(top8, first, second))
            q2 = jnp.concatenate(q2, axis=0).astype(BF16)
            o = _attend(q2, kb.astype(BF16), vb.astype(BF16), cur_valid, sink_col, None)
            chunks = []
            for cc in range(n_chunks):
                oc = o[m * cc:m * (cc + 1)]
                if cc < n_chunks // N_KV_HEADS:
                    chunks.append(jnp.where(lo8, oc, pltpu.roll(pltpu.roll(oc, half, 0), HEAD_DIM, 1)))
                else:
                    chunks.append(jnp.where(lo8, pltpu.roll(oc, HEAD_DIM, 1), pltpu.roll(oc, half, 0)))
            attn_parts.append(jnp.concatenate(chunks, axis=1))

            ext_ref[0:hist, :] = st_ref[0, seq]
            ext_ref[hist:hist + half, :] = shift(u8)[0:half]
            nu_ref[0, seq] = ext_ref[half:hist + half, :]
            cols = []
            for g, w in enumerate(POOL_WINDOWS):
                c0, c1 = POOL_GROUP_WIDTH * g, POOL_GROUP_WIDTH * (g + 1)
                acc = ext_ref[hist:hist + m, c0:c1]
                for k in range(1, w):
                    acc = acc + ext_ref[hist - k:hist - k + m, c0:c1]
                cols.append(acc / float(w) - ext_ref[hist:hist + m, c0:c1])
            pd_parts.append(jnp.concatenate(cols, axis=1))

        attn_ref[pl.ds(r0, m), :] = jnp.where(top, attn_parts[0], pltpu.roll(attn_parts[1], half, 0))
        pd_ref[pl.ds(r0, m), :] = jnp.where(top, pd_parts[0], pltpu.roll(pd_parts[1], half, 0))
        return carry

    lax.fori_loop(0, SAMPLE_PAIRS_PER_STEP, pair, 0)


def _mixer_sample(sinks, q, k, v, u, cache_k, cache_v, state):
    n = q.shape[0]
    m = SEQ_PAIR_ROWS * SAMPLE_PAIRS_PER_STEP
    rows = lambda w: pl.BlockSpec((m, w), lambda i: (i, 0))
    seqs = lambda a: pl.BlockSpec((2 * SAMPLE_PAIRS_PER_STEP,) + a.shape[1:], lambda i: (i, 0, 0))
    st_spec = pl.BlockSpec((1, 2 * SAMPLE_PAIRS_PER_STEP) + state.shape[2:], lambda i: (0, i, 0, 0))
    like = lambda a: jax.ShapeDtypeStruct(a.shape, F32)
    return pl.pallas_call(
        _mixer_sample_body,
        grid=(n // m,),
        in_specs=[pl.BlockSpec(memory_space=pltpu.SMEM), rows(Q_W), rows(KV_W), rows(KV_W),
                  rows(D_MODEL), seqs(cache_k), seqs(cache_v), st_spec],
        out_specs=(rows(Q_W), rows(D_MODEL), seqs(cache_k), seqs(cache_v), st_spec),
        out_shape=(jax.ShapeDtypeStruct((n, Q_W), F32), jax.ShapeDtypeStruct((n, D_MODEL), F32),
                   like(cache_k), like(cache_v), like(state)),
        scratch_shapes=[pltpu.VMEM((16 + SEQ_PAIR_ROWS, D_MODEL), F32)],
        compiler_params=pltpu.CompilerParams(dimension_semantics=("arbitrary",),
                                             vmem_limit_bytes=VMEM_LIMIT),
        name="mixer_sample",
    )(sinks, q, k, v, u, cache_k, cache_v, state)


def _dense_body(x_ref, attn_ref, pd_ref, sga_ref, sgp_ref, mix_ref, ps_ref, wa_ref, wp_ref,
                wo_ref, gf_ref, wrh_ref, wrl_ref, br_ref, ltri_ref, ssel_ref, crow_ref,
                x1_ref, hn_ref, rt_ref, cnt_ref, seg_ref, carry_ref):
    rows = x_ref.shape[0]
    step = pl.program_id(0)

    @pl.when(step % TILE_BLOCKS == 0)
    def _():
        carry_ref[...] = jnp.zeros_like(carry_ref)

    pd = pd_ref[...].astype(BF16)
    pooled = []
    for g in range(len(POOL_WINDOWS)):
        c0, c1 = POOL_GROUP_WIDTH * g, POOL_GROUP_WIDTH * (g + 1)
        pooled.append((_bdot(pd[:, c0:c1], mix_ref[g]) * ps_ref[:, c0:c1]).astype(BF16))
    pooled = jnp.concatenate(pooled, axis=1)
    merged = (sga_ref[...] * _bdot(attn_ref[...].astype(BF16), wa_ref[...])
              + sgp_ref[...] * _bdot(pooled, wp_ref[...]))
    x1 = x_ref[...] + _bdot(merged.astype(BF16), wo_ref[...])
    x1_ref[...] = _to_tile_order(x1)
    hn = _rms(x1, gf_ref[...])
    hn_ref[...] = _to_tile_order(hn)
    hi = hn.astype(BF16)
    lo = (hn - hi.astype(F32)).astype(BF16)
    both = _bdot(hi, wrl_ref[...])
    logits = both[:, :LANES] + both[:, LANES:] + _bdot(lo, wrh_ref[...]) + br_ref[...]

    lane = lax.broadcasted_iota(jnp.int32, (rows, LANES), 1)
    big = jnp.int32(LANES)
    gl = jnp.where(lane < N_EXPERT_GROUPS, logits, NEG_INF)
    gmax = jnp.max(gl, axis=-1, keepdims=True)
    gidx = jnp.min(jnp.where(gl == gmax, lane, big), axis=-1, keepdims=True)
    g_w = 1.0 / jnp.sum(jnp.exp(gl - gmax), axis=-1, keepdims=True)
    e0 = EXPERT_LANE0 + gidx * EXPERTS_PER_GROUP
    el = jnp.where((lane >= e0) & (lane < e0 + EXPERTS_PER_GROUP), logits, NEG_INF)
    l1 = jnp.max(el, axis=-1, keepdims=True)
    i1 = jnp.min(jnp.where(el == l1, lane, big), axis=-1, keepdims=True)
    el2 = jnp.where(lane == i1, NEG_INF, el)
    l2 = jnp.max(el2, axis=-1, keepdims=True)
    i2 = jnp.min(jnp.where(el2 == l2, lane, big), axis=-1, keepdims=True)
    e = jnp.exp(l2 - l1)
    w1 = 1.0 / (1.0 + e)
    w2 = e * w1

    sel = (lane == i1) | (lane == i2)
    onehot = jnp.where(sel, 1.0, 0.0).astype(BF16)
    rank = _bdot(ltri_ref[...], onehot) + carry_ref[0:1, :] + crow_ref[...]
    segsum = _bdot(ssel_ref[...], onehot)
    seg_ref[0] = segsum
    carry_ref[...] = carry_ref[...] + segsum[0:1, :]
    cnt_ref[0] = carry_ref[...]
    r1 = jnp.sum(jnp.where(lane == i1, rank, 0.0), axis=-1, keepdims=True)
    r2 = jnp.sum(jnp.where(lane == i2, rank, 0.0), axis=-1, keepdims=True)
    cols = ((i1 - EXPERT_LANE0).astype(F32), (i2 - EXPERT_LANE0).astype(F32), r1, r2,
            g_w * w1, g_w * w2)
    tile = jnp.zeros((rows, LANES), F32)
    for c, val in enumerate(cols):
        tile = jnp.where(lane == c, val, tile)
    rt_ref[...] = tile


def _dense(x, attn, pd, sga, sgp, mix_b, pool_scale, wa_b, wp_b, wo_b, gf, wr_hi, wr_lo, br,
           ltri, ssel, crow):
    n = x.shape[0]
    nb = n // ROW_BLOCK
    row = lambda w: pl.BlockSpec((ROW_BLOCK, w), lambda i: (i, 0))
    full = lambda a: pl.BlockSpec(a.shape, lambda i: (0,) * a.ndim)
    stat = pl.BlockSpec((1, 8, LANES), lambda i: (i, 0, 0))
    tiled = pl.BlockSpec((ROW_BLOCK * TOKEN_ROWS, LANES), lambda i: (i, 0))
    consts = (mix_b, pool_scale, wa_b, wp_b, wo_b, gf, wr_hi, wr_lo, br, ltri, ssel, crow)
    return pl.pallas_call(
        _dense_body,
        grid=(nb,),
        in_specs=[row(D_MODEL)] * 5 + [full(a) for a in consts],
        out_specs=(tiled, tiled, row(LANES), stat, stat),
        out_shape=(jax.ShapeDtypeStruct((n * TOKEN_ROWS, LANES), F32),
                   jax.ShapeDtypeStruct((n * TOKEN_ROWS, LANES), F32),
                   jax.ShapeDtypeStruct((n, LANES), F32),
                   jax.ShapeDtypeStruct((nb, 8, LANES), F32),
                   jax.ShapeDtypeStruct((nb, 8, LANES), F32)),
        scratch_shapes=[pltpu.VMEM((8, LANES), F32)],
        compiler_params=pltpu.CompilerParams(dimension_semantics=("arbitrary",),
                                             vmem_limit_bytes=VMEM_LIMIT),
        name="dense",
    )(x, attn, pd, sga, sgp, *consts)


def _row_offset(r):
    return (r // 8) * (8 * TOKEN_ROWS) + r % 8


def _to_tile_order(val):
    rows = val.shape[0]
    parts = [val[:, LANES * c:LANES * (c + 1)].reshape(rows // 8, 8, LANES) for c in range(TOKEN_ROWS)]
    return jnp.stack(parts, axis=1).reshape(rows * TOKEN_ROWS, LANES)


def _from_tile_order(flat):
    rows = flat.shape[0] // TOKEN_ROWS
    t = flat.reshape(rows // 8, TOKEN_ROWS, 8, LANES)
    return jnp.concatenate([t[:, c].reshape(rows, LANES) for c in range(TOKEN_ROWS)], axis=1)


def _experts_body(nblk_ref, blk0_ref, cnt_ref, slots_ref, wts_ref, hn_p, hn_s, x1_p, x1_s,
                  wg_ref, wu_ref, wd_ref, y_p, y_s,
                  hn_t, acc, xg0, xg1, yb0, yb1, stage, off_list, w_list, sem_in, sem_out):
    tau = pl.program_id(0)
    e = pl.program_id(1)
    tr = TOKEN_ROWS
    dummy_off = TILE_TOKENS * tr
    p_src = pl.ds(pl.multiple_of(tau * (TILE_PROMPT * tr), TILE_PROMPT * tr), TILE_PROMPT * tr)
    s_src = pl.ds(pl.multiple_of(tau * (TILE_SAMPLE * tr), TILE_SAMPLE * tr), TILE_SAMPLE * tr)
    tile_p = pl.ds(0, TILE_PROMPT * tr)
    tile_s = pl.ds(TILE_PROMPT * tr, TILE_SAMPLE * tr)

    def in_copies():
        return [pltpu.make_async_copy(hn_p.at[p_src], hn_t.at[tile_p], sem_in.at[0]),
                pltpu.make_async_copy(hn_s.at[s_src], hn_t.at[tile_s], sem_in.at[1]),
                pltpu.make_async_copy(x1_p.at[p_src], acc.at[tile_p], sem_in.at[2]),
                pltpu.make_async_copy(x1_s.at[s_src], acc.at[tile_s], sem_in.at[3])]

    xbufs = (xg0, xg1)
    ybufs = (yb0, yb1)
    end_block = blk0_ref[tau, N_EXPERTS - 1] + nblk_ref[tau, N_EXPERTS - 1]
    end_slot = end_block * MOE_ROWS
    token = lambda off: pl.ds(off, tr, stride=8)

    def gather_block(b, xdst):
        base = b * MOE_ROWS
        for j in range(MOE_ROWS):
            xdst[token(_row_offset(j)), :] = hn_t[token(off_list[base + j]), :]

    def scatter_block(b, ysrc):
        base = b * MOE_ROWS
        for j0 in range(0, MOE_ROWS, SCATTER_BATCH):
            offs = [off_list[base + j0 + i] for i in range(SCATTER_BATCH)]
            vals = [acc[token(offs[i]), :] + w_list[base + j0 + i] * ysrc[token(_row_offset(j0 + i)), :]
                    for i in range(SCATTER_BATCH)]
            for i in range(SCATTER_BATCH):
                acc[token(offs[i]), :] = vals[i]

    @pl.when(e == 0)
    def _load_tile():
        for cp in in_copies():
            cp.start()

        hn_t[dummy_off:, :] = jnp.zeros((hn_t.shape[0] - dummy_off, LANES), F32)
        yb0[...] = jnp.zeros_like(yb0)

        def pad_range(lo, hi):
            def fill(g, c):
                for i in range(8):
                    off_list[lo + g * 8 + i] = dummy_off
                    w_list[lo + g * 8 + i] = 0.0
                return c
            lax.fori_loop(0, lax.shift_right_logical(hi - lo + 7, 3), fill, 0)
        pad_range(0, MOE_ROWS)
        pad_range(end_slot, end_slot + MOE_ROWS)

        def pad_fill(ei, c):
            s0 = blk0_ref[tau, ei] * MOE_ROWS
            pad_range(s0 + cnt_ref[tau, ei], s0 + nblk_ref[tau, ei] * MOE_ROWS)
            return c
        lax.fori_loop(0, N_EXPERTS, pad_fill, 0)

        def invert(g, c):
            for i in range(8):
                for k in range(2):
                    slot = slots_ref[g * 16 + 2 * i + k]
                    off_list[slot] = g * (8 * tr) + i
                    w_list[slot] = wts_ref[g * 16 + 2 * i + k]
            return c
        lax.fori_loop(0, TILE_TOKENS // 8, invert, 0)

        for cp in in_copies():
            cp.wait()
        acc[dummy_off:, :] = jnp.zeros((acc.shape[0] - dummy_off, LANES), F32)
        gather_block(1, xbufs[1])

    first_block = blk0_ref[tau, e]

    def block(jb, c):
        b = first_block + jb
        for par in range(2):
            @pl.when((b & 1) == par)
            def _():
                gather_block(b + 1, xbufs[1 - par])
                xb = _from_tile_order(xbufs[par][...]).astype(BF16)
                act = (jax.nn.silu(_bdot(xb, wg_ref[0])) * _bdot(xb, wu_ref[0])).astype(BF16)
                ybufs[par][...] = _to_tile_order(_bdot(act, wd_ref[0]))
                scatter_block(b - 1, ybufs[1 - par])
        return c
    lax.fori_loop(0, nblk_ref[tau, e], block, 0)

    @pl.when(e == N_EXPERTS - 1)
    def _store_tile():
        last = end_block - 1
        for par in range(2):
            @pl.when((last & 1) == par)
            def _():
                scatter_block(last, ybufs[par])

        def write_back(dst, dst_row0, tile_row0, rows):
            def chunk(ci, c):
                r0 = pl.multiple_of((tile_row0 + ci * STAGE_ROWS) * tr, STAGE_ROWS * tr)
                for rg in range(min(rows, STAGE_ROWS) // 8):
                    for cc in range(tr):
                        stage[8 * rg:8 * rg + 8, LANES * cc:LANES * (cc + 1)] = acc[
                            pl.ds(r0 + (rg * tr + cc) * 8, 8), :]
                n = min(rows, STAGE_ROWS)
                cp = pltpu.make_async_copy(
                    stage.at[pl.ds(0, n)],
                    dst.at[pl.ds(pl.multiple_of(dst_row0 + ci * STAGE_ROWS, 8), n)], sem_out.at[0])
                cp.start()
                cp.wait()
                return c
            lax.fori_loop(0, max(rows // STAGE_ROWS, 1), chunk, 0)
        write_back(y_p, tau * TILE_PROMPT, 0, TILE_PROMPT)
        write_back(y_s, tau * TILE_SAMPLE, TILE_PROMPT, TILE_SAMPLE)


def _experts(nblk, blk0, cnt, slots, wts, hn_p, hn_s, x1_p, x1_s, w_gate, w_up, w_down):
    any_spec = pl.BlockSpec(memory_space=pl.ANY)
    wspec = lambda a: pl.BlockSpec((1,) + a.shape[1:], lambda t, e, *_: (e, 0, 0))
    smem = lambda a: pl.BlockSpec((a.shape[0] // N_TILES,), lambda t, e, *_: (t,),
                                  memory_space=pltpu.SMEM)
    tile_rows = (TILE_TOKENS + 8) * TOKEN_ROWS
    grid_spec = pltpu.PrefetchScalarGridSpec(
        num_scalar_prefetch=3,
        grid=(N_TILES, N_EXPERTS),
        in_specs=[smem(slots), smem(wts), any_spec, any_spec, any_spec, any_spec,
                  wspec(w_gate), wspec(w_up), wspec(w_down)],
        out_specs=(any_spec, any_spec),
        scratch_shapes=[
            pltpu.VMEM((tile_rows, LANES), F32),
            pltpu.VMEM((tile_rows, LANES), F32),
            pltpu.VMEM((MOE_ROWS * TOKEN_ROWS, LANES), F32),
            pltpu.VMEM((MOE_ROWS * TOKEN_ROWS, LANES), F32),
            pltpu.VMEM((MOE_ROWS * TOKEN_ROWS, LANES), F32),
            pltpu.VMEM((MOE_ROWS * TOKEN_ROWS, LANES), F32),
            pltpu.VMEM((STAGE_ROWS, D_MODEL), F32),
            pltpu.SMEM((LIST_CAP,), jnp.int32),
            pltpu.SMEM((LIST_CAP,), F32),
            pltpu.SemaphoreType.DMA((4,)),
            pltpu.SemaphoreType.DMA((1,)),
        ],
    )
    n_p, n_s = x1_p.shape[0] // TOKEN_ROWS, x1_s.shape[0] // TOKEN_ROWS
    return pl.pallas_call(
        _experts_body,
        grid_spec=grid_spec,
        out_shape=(jax.ShapeDtypeStruct((n_p, D_MODEL), F32), jax.ShapeDtypeStruct((n_s, D_MODEL), F32)),
        compiler_params=pltpu.CompilerParams(dimension_semantics=("arbitrary", "arbitrary"),
                                             vmem_limit_bytes=EXPERTS_VMEM_LIMIT),
        name="experts",
    )(nblk, blk0, cnt, slots, wts, hn_p, hn_s, x1_p, x1_s, w_gate, w_up, w_down)


def _rope_tables(pos):
    half = HEAD_DIM // 2
    inv = ROPE_THETA ** (-jnp.arange(half, dtype=F32) * (2.0 / HEAD_DIM))
    ang = pos.astype(F32)[:, None] * inv[None, :]
    cos = jnp.tile(jnp.cos(ang), (1, LANES // half))
    sin = jnp.sin(ang)
    sin = jnp.tile(jnp.concatenate([-sin, sin], axis=1), (1, LANES // HEAD_DIM))
    return cos, sin


def _split_bf16(w):
    hi = w.astype(BF16)
    return hi, (w - hi.astype(F32)).astype(BF16)


def kernel(x_prompt, x_sample, cache_k, cache_v, state_pool, norm_mix_g, w_in, q_norm_g, k_norm_g,
           attn_sinks, w_attn_branch, pool_mix_w, pool_scale, w_pool_branch, w_out, norm_ffn_g,
           w_route_group, b_route_group, w_route_expert, b_route_expert, w_expert_gate,
           w_expert_up, w_expert_down):
    batch, seq, d = x_prompt.shape
    dec_batch, dec_seq, _ = x_sample.shape
    past_len = 16384
    assert x_prompt.shape == (4, 4096, D_MODEL) and x_sample.shape == (128, 4, D_MODEL)
    assert w_in.shape[0] == 1, "single layer"
    n_p, n_s = batch * seq, dec_batch * dec_seq

    g_mix = norm_mix_g[0][None, :]
    w_in_b = w_in[0].astype(BF16)
    gq = jnp.tile(q_norm_g[0], LANES // HEAD_DIM)[None, :]
    gk = jnp.tile(k_norm_g[0], LANES // HEAD_DIM)[None, :]
    sinks = attn_sinks[0]
    mix_b = pool_mix_w[0].astype(BF16)
    ps = pool_scale[0][None, :]
    wa_b = w_attn_branch[0].astype(BF16)
    wp_b = w_pool_branch[0].astype(BF16)
    wo_b = w_out[0].astype(BF16)
    gf = norm_ffn_g[0][None, :]
    wr = jnp.zeros((D_MODEL, LANES), F32)
    wr = wr.at[:, :N_EXPERT_GROUPS].set(w_route_group[0])
    wr = wr.at[:, EXPERT_LANE0:EXPERT_LANE0 + N_EXPERTS].set(w_route_expert[0])
    wr_hi, wr_lo = _split_bf16(wr)
    br = jnp.zeros((1, LANES), F32)
    br = br.at[0, :N_EXPERT_GROUPS].set(b_route_group[0])
    br = br.at[0, EXPERT_LANE0:EXPERT_LANE0 + N_EXPERTS].set(b_route_expert[0])

    cos_p, sin_p = _rope_tables(jnp.arange(seq, dtype=jnp.int32))
    pos_s = past_len + (jnp.arange(n_s, dtype=jnp.int32) % dec_seq)
    cos_s, sin_s = _rope_tables(pos_s)

    dense_consts = (mix_b, ps, wa_b, wp_b, wo_b, gf, wr_hi, jnp.concatenate([wr_hi, wr_lo], axis=1), br)
    assert n_s == ROW_BLOCK == N_TILES * TILE_SAMPLE and n_p == N_TILES * TILE_PROMPT
    ridx = np.arange(ROW_BLOCK)
    lower = ridx[:, None] > ridx[None, :]
    seg_of = ridx // TILE_SAMPLE
    as_bf16 = lambda a: jnp.asarray(a.astype(np.float32), BF16)
    ltri_p = as_bf16(lower)
    ltri_s = as_bf16(lower & (seg_of[:, None] == seg_of[None, :]))
    ssel_p = as_bf16(np.arange(8)[:, None] == np.zeros_like(ridx)[None, :])
    ssel_s = as_bf16(np.arange(8)[:, None] == seg_of[None, :])

    xp = x_prompt.reshape(n_p, d)
    q, k, v, kk, vv, u, sga, sgp = _inproj(xp, g_mix, w_in_b, cos_p, sin_p, gq, gk)
    attn, pd, wg_b, wu_b, wd_b = _mixer_prompt(sinks, q, kk, vv, u, w_expert_gate[0], w_expert_up[0],
                                               w_expert_down[0], batch, seq)
    x1_p, hn_p, rt_p, cnt_p, _ = _dense(xp, attn, pd, sga, sgp, *dense_consts, ltri_p,
                                        ssel_p, jnp.zeros((ROW_BLOCK, LANES), F32))
    last_window = lambda a: a.reshape(batch, seq, KV_W)[:, -WINDOW:].reshape(
        1, batch, WINDOW, N_KV_HEADS, HEAD_DIM)
    new_k_p = last_window(k)
    new_v_p = last_window(v)
    new_u_p = u.reshape(batch, seq, d)[:, -POOL_STATE_LEN:][None]

    xs = x_sample.reshape(n_s, d)
    q, k, v, kk, vv, u, sga, sgp = _inproj(xs, g_mix, w_in_b, cos_s, sin_s, gq, gk)
    ck = cache_k[0].reshape(dec_batch, WINDOW, KV_W)
    cv = cache_v[0].reshape(dec_batch, WINDOW, KV_W)
    attn, pd, nk_s, nv_s, nu_s = _mixer_sample(sinks, q.astype(F32), k, v, u, ck, cv, state_pool)
    cnt_tiles_p = cnt_p[TILE_BLOCKS - 1::TILE_BLOCKS, 0, :]
    crow = jnp.repeat(cnt_tiles_p, TILE_SAMPLE, axis=0)
    x1_s, hn_s, rt_s, _, seg_s = _dense(xs, attn, pd, sga, sgp, *dense_consts, ltri_s,
                                        ssel_s, crow)
    new_k_s = nk_s.reshape(1, dec_batch, WINDOW, N_KV_HEADS, HEAD_DIM)
    new_v_s = nv_s.reshape(1, dec_batch, WINDOW, N_KV_HEADS, HEAD_DIM)
    new_u_s = nu_s

    ex = slice(EXPERT_LANE0, EXPERT_LANE0 + N_EXPERTS)
    cnt = (cnt_tiles_p[:, ex] + seg_s[0, :N_TILES, ex]).astype(jnp.int32)
    nblk = (cnt + (MOE_ROWS - 1)) // MOE_ROWS
    blk0 = 1 + jnp.cumsum(nblk, axis=1) - nblk
    seg = MOE_ROWS * blk0
    rt = jnp.concatenate([rt_p[:, :6].reshape(N_TILES, TILE_PROMPT, 6),
                          rt_s[:, :6].reshape(N_TILES, TILE_SAMPLE, 6)], axis=1)
    expert_hit = rt[:, :, 0:2].astype(jnp.int32)[..., None] == jnp.arange(N_EXPERTS, dtype=jnp.int32)
    slot = (rt[:, :, 2:4].astype(jnp.int32)
            + jnp.sum(jnp.where(expert_hit, seg[:, None, None, :], 0), axis=-1))
    per_tile = lambda a: jnp.pad(a.reshape(N_TILES, 2 * TILE_TOKENS),
                                 ((0, 0), (0, TILE_LIST - 2 * TILE_TOKENS))).reshape(-1)
    slots = per_tile(slot)
    wts = per_tile(rt[:, :, 4:6])
    y_p, y_s = _experts(nblk, blk0, cnt, slots, wts, hn_p, hn_s, x1_p, x1_s, wg_b, wu_b, wd_b)
    y_prompt = y_p.reshape(batch, seq, d)
    y_sample = y_s.reshape(dec_batch, dec_seq, d)

    return (y_prompt, y_sample, new_k_p, new_v_p, new_u_p, new_k_s, new_v_s, new_u_s)
```

```python
import jax
import jax.numpy as jnp
from jax import lax
from jax.experimental import pallas as pl
from jax.experimental.pallas import tpu as pltpu

F32 = jnp.float32
BF16 = jnp.bfloat16

D_MODEL = 1024
HEAD_DIM = 64
N_Q_HEADS = 16
N_KV_HEADS = 2
WINDOW = 128
ROPE_THETA = 10000.0
POOL_WINDOWS = (2, 4, 8, 16)
POOL_GROUP_WIDTH = D_MODEL // len(POOL_WINDOWS)
POOL_STATE_LEN = max(POOL_WINDOWS) - 1
N_EXPERT_GROUPS = 4
EXPERTS_PER_GROUP = 8
N_EXPERTS = N_EXPERT_GROUPS * EXPERTS_PER_GROUP
D_EXPERT = 512
RMS_EPS = 1e-6
Q_W = N_Q_HEADS * HEAD_DIM
KV_W = N_KV_HEADS * HEAD_DIM
OFF_K = Q_W
OFF_V = OFF_K + KV_W
OFF_U = OFF_V + KV_W
OFF_GA = OFF_U + D_MODEL
OFF_GP = OFF_GA + D_MODEL
IN_W = OFF_GP + D_MODEL

LANES = 128
ROW_BLOCK = 512
EXPERT_LANE0 = 32
VMEM_LIMIT = 56 * 1024 * 1024
EXPERTS_VMEM_LIMIT = 60 * 1024 * 1024
NEG_INF = float("-inf")

N_TILES = 4
TILE_BLOCKS = 8
TILE_PROMPT = TILE_BLOCKS * ROW_BLOCK
TILE_SAMPLE = 128
TILE_TOKENS = TILE_PROMPT + TILE_SAMPLE
MOE_ROWS = 304
DUMMY_ROWS = 8
SCATTER_BATCH = 4
TILE_LIST = -(-(2 * TILE_TOKENS + 1) // 1024) * 1024
LIST_CAP = 2 * TILE_TOKENS + N_EXPERTS * (MOE_ROWS - 1) + 2 * MOE_ROWS + 8


def _rms(x, g):
    return x * lax.rsqrt(jnp.mean(x * x, axis=-1, keepdims=True) + RMS_EPS) * g


def _bdot(a, b):
    return jnp.dot(a, b, preferred_element_type=F32)


def _inproj_body(x_ref, g_ref, w_ref, cos_ref, sin_ref, gq_ref, gk_ref,
                 q_ref, k_ref, v_ref, kk_ref, vv_ref, u_ref, sga_ref, sgp_ref):
    rows = x_ref.shape[0]
    hb = _rms(x_ref[...], g_ref[...]).astype(BF16)
    cos = cos_ref[...]
    sin = sin_ref[...]
    lane = lax.broadcasted_iota(jnp.int32, (rows, LANES), 1)
    lo = lane < HEAD_DIM
    first_half = (lane % HEAD_DIM) < (HEAD_DIM // 2)

    def head_norm_rope(zc, gain):
        sq = zc * zc
        ss_lo = jnp.sum(jnp.where(lo, sq, 0.0), axis=-1, keepdims=True)
        ss_hi = jnp.sum(jnp.where(lo, 0.0, sq), axis=-1, keepdims=True)
        r = lax.rsqrt(jnp.where(lo, ss_lo, ss_hi) * (1.0 / HEAD_DIM) + RMS_EPS)
        y = zc * r * gain
        partner = jnp.where(first_half, pltpu.roll(y, LANES - HEAD_DIM // 2, 1),
                            pltpu.roll(y, HEAD_DIM // 2, 1))
        return y * cos + partner * sin

    gq = gq_ref[...]
    for j in range(Q_W // 256):
        z = _bdot(hb, w_ref[:, 256 * j:256 * (j + 1)])
        for c in range(2):
            qn = head_norm_rope(z[:, LANES * c:LANES * (c + 1)], gq) * (HEAD_DIM ** -0.5)
            q_ref[:, 256 * j + LANES * c:256 * j + LANES * (c + 1)] = qn.astype(BF16)

    z = _bdot(hb, w_ref[:, OFF_K:OFF_U])
    kn = head_norm_rope(z[:, :KV_W], gk_ref[...])
    vr = z[:, KV_W:]
    k_ref[...] = kn
    v_ref[...] = vr
    kr = pltpu.roll(kn, HEAD_DIM, 1)
    vrr = pltpu.roll(vr, HEAD_DIM, 1)
    kk_ref[:, :LANES] = jnp.where(lo, kn, kr).astype(BF16)
    kk_ref[:, LANES:] = jnp.where(lo, kr, kn).astype(BF16)
    vv_ref[:, :LANES] = jnp.where(lo, vr, vrr).astype(BF16)
    vv_ref[:, LANES:] = jnp.where(lo, vrr, vr).astype(BF16)

    for j in range(D_MODEL // 256):
        u_ref[:, 256 * j:256 * (j + 1)] = _bdot(hb, w_ref[:, OFF_U + 256 * j:OFF_U + 256 * (j + 1)])
        sga_ref[:, 256 * j:256 * (j + 1)] = jax.nn.sigmoid(
            _bdot(hb, w_ref[:, OFF_GA + 256 * j:OFF_GA + 256 * (j + 1)]))
        sgp_ref[:, 256 * j:256 * (j + 1)] = jax.nn.sigmoid(
            _bdot(hb, w_ref[:, OFF_GP + 256 * j:OFF_GP + 256 * (j + 1)]))


def _inproj(x, g, w_in_b, cos, sin, gq, gk):
    n = x.shape[0]
    nb = n // ROW_BLOCK
    ncos = cos.shape[0] // ROW_BLOCK
    row = lambda w: pl.BlockSpec((ROW_BLOCK, w), lambda i: (i, 0))
    full = lambda a: pl.BlockSpec(a.shape, lambda i: (0,) * a.ndim)
    tab = pl.BlockSpec((ROW_BLOCK, LANES), lambda i: (i % ncos, 0))
    out_shapes = (
        jax.ShapeDtypeStruct((n, Q_W), BF16),
        jax.ShapeDtypeStruct((n, KV_W), F32),
        jax.ShapeDtypeStruct((n, KV_W), F32),
        jax.ShapeDtypeStruct((n, 2 * LANES), BF16),
        jax.ShapeDtypeStruct((n, 2 * LANES), BF16),
        jax.ShapeDtypeStruct((n, D_MODEL), F32),
        jax.ShapeDtypeStruct((n, D_MODEL), F32),
        jax.ShapeDtypeStruct((n, D_MODEL), F32),
    )
    return pl.pallas_call(
        _inproj_body,
        grid=(nb,),
        in_specs=[row(D_MODEL), full(g), full(w_in_b), tab, tab, full(gq), full(gk)],
        out_specs=tuple(row(s.shape[1]) for s in out_shapes),
        out_shape=out_shapes,
        compiler_params=pltpu.CompilerParams(dimension_semantics=("arbitrary",),
                                             vmem_limit_bytes=VMEM_LIMIT),
        name="inproj",
    )(x, g, w_in_b, cos, sin, gq, gk)


def _attend(q2, kg, vg, cur_valid, sink_col, prev_dead):
    s = lax.dot_general(q2, kg, (((1,), (1,)), ((), ())), preferred_element_type=F32)
    s_prev = s[:, :WINDOW]
    if prev_dead is not None:
        s_prev = jnp.where(prev_dead, NEG_INF, s_prev)
    sc = jnp.where(cur_valid, s[:, WINDOW:], s_prev)
    m = jnp.maximum(jnp.max(sc, axis=-1, keepdims=True), sink_col)
    p = jnp.exp(sc - m)
    den = jnp.sum(p, axis=-1, keepdims=True) + jnp.exp(sink_col - m)
    pn = p * (1.0 / den)
    p2 = jnp.concatenate([jnp.where(cur_valid, 0.0, pn), jnp.where(cur_valid, pn, 0.0)], axis=1)
    return _bdot(p2.astype(BF16), vg)


POOL_PAD_ROWS = 8


def _mixer_prompt_body(sink_ref, q_ref, kkc_ref, kkp_ref, vvc_ref, vvp_ref, uc_ref, up_ref,
                       wg_ref, wu_ref, wd_ref,
                       attn_ref, pd_ref, wgb_ref, wub_ref, wdb_ref, ext_ref, s_a, s_b):
    wgb_ref[...] = wg_ref[...].astype(BF16)
    wub_ref[...] = wu_ref[...].astype(BF16)
    wdb_ref[...] = wd_ref[...].astype(BF16)
    i = pl.program_id(1)
    first = i == 0
    m2 = 2 * WINDOW
    row = lax.broadcasted_iota(jnp.int32, (m2, LANES), 0)
    col = lax.broadcasted_iota(jnp.int32, (m2, LANES), 1)
    cur_valid = (row % WINDOW) >= col
    upper = lax.broadcasted_iota(jnp.int32, (m2, 1), 0) < WINDOW
    lo = lax.broadcasted_iota(jnp.int32, (WINDOW, LANES), 1) < HEAD_DIM

    for j in range(ROW_BLOCK // WINDOW):
        r0 = WINDOW * j
        if j == 0:
            kprev, vprev = kkp_ref[...], vvp_ref[...]
            prev_dead = first
        else:
            kprev, vprev = kkc_ref[r0 - WINDOW:r0, :], vvc_ref[r0 - WINDOW:r0, :]
            prev_dead = None
        kband = jnp.concatenate([kprev, kkc_ref[r0:r0 + WINDOW, :]], axis=0)
        vband = jnp.concatenate([vprev, vvc_ref[r0:r0 + WINDOW, :]], axis=0)
        for g in range(N_KV_HEADS):
            kg = kband[:, LANES * g:LANES * (g + 1)]
            vg = vband[:, LANES * g:LANES * (g + 1)]
            for c in range(4):
                cc = 4 * g + c
                qc = q_ref[r0:r0 + WINDOW, LANES * cc:LANES * (cc + 1)]
                zero = jnp.zeros_like(qc)
                q2 = jnp.concatenate([jnp.where(lo, qc, zero), jnp.where(lo, zero, qc)], axis=0)
                sink_col = jnp.where(upper, sink_ref[2 * cc], sink_ref[2 * cc + 1])
                o = _attend(q2, kg, vg, cur_valid, sink_col, prev_dead)
                attn_ref[r0:r0 + WINDOW, LANES * cc:LANES * (cc + 1)] = jnp.where(
                    lo, o[:WINDOW], o[WINDOW:]).astype(BF16)

    pad, top = POOL_PAD_ROWS, POOL_PAD_ROWS + 16
    end = top + ROW_BLOCK
    for ref in (ext_ref, s_a, s_b):
        ref[0:pad, :] = jnp.zeros((pad, D_MODEL), F32)
    ext_ref[pad:top, :] = jnp.where(first, 0.0, up_ref[...])
    ext_ref[top:, :] = uc_ref[...]
    pos = i * ROW_BLOCK + lax.broadcasted_iota(jnp.int32, (ROW_BLOCK, 1), 0)
    src = ext_ref
    for g, w in enumerate(POOL_WINDOWS):
        c0 = POOL_GROUP_WIDTH * g
        dst = (s_a, s_b)[g % 2]
        shift = w // 2
        dst[pad:end, c0:] = src[pad:end, c0:] + src[pad - shift:end - shift, c0:]
        c1 = c0 + POOL_GROUP_WIDTH
        cnt = jnp.minimum(w, pos + 1).astype(F32)
        pd_ref[:, c0:c1] = (dst[top:end, c0:c1] / cnt - uc_ref[:, c0:c1]).astype(BF16)
        src = dst


def _mixer_prompt(sinks, q, kk, vv, u, w_gate, w_up, w_down, batch, seq):
    nb = seq // ROW_BLOCK
    assert batch * nb == N_EXPERTS, "one expert's weights are cast per grid step"
    sub = ROW_BLOCK // WINDOW
    cur = lambda w: pl.BlockSpec((ROW_BLOCK, w), lambda b, i: (b * nb + i, 0))
    wspec = lambda a: pl.BlockSpec((1,) + a.shape[1:], lambda b, i: (b * nb + i, 0, 0))
    weights = (w_gate, w_up, w_down)
    prev_kv = pl.BlockSpec((WINDOW, 2 * LANES),
                           lambda b, i: (jnp.maximum((b * nb + i) * sub - 1, 0), 0))
    prev_u = pl.BlockSpec((16, D_MODEL),
                          lambda b, i: (jnp.maximum((b * nb + i) * (ROW_BLOCK // 16) - 1, 0), 0))
    n = batch * seq
    return pl.pallas_call(
        _mixer_prompt_body,
        grid=(batch, nb),
        in_specs=[pl.BlockSpec(memory_space=pltpu.SMEM), cur(Q_W), cur(2 * LANES), prev_kv,
                  cur(2 * LANES), prev_kv, cur(D_MODEL), prev_u] + [wspec(a) for a in weights],
        out_specs=(cur(Q_W), cur(D_MODEL)) + tuple(wspec(a) for a in weights),
        out_shape=(jax.ShapeDtypeStruct((n, Q_W), BF16), jax.ShapeDtypeStruct((n, D_MODEL), BF16))
        + tuple(jax.ShapeDtypeStruct(a.shape, BF16) for a in weights),
        scratch_shapes=[pltpu.VMEM((POOL_PAD_ROWS + 16 + ROW_BLOCK, D_MODEL), F32)] * 3,
        compiler_params=pltpu.CompilerParams(dimension_semantics=("arbitrary", "arbitrary"),
                                             vmem_limit_bytes=VMEM_LIMIT),
        name="mixer_prompt",
    )(sinks, q, kk, kk, vv, vv, u, u, *weights)


SEQ_PAIR_ROWS = 8
SAMPLE_PAIRS_PER_STEP = 8


def _mixer_sample_body(sink_ref, q_ref, k_ref, v_ref, u_ref, ck_ref, cv_ref, st_ref,
                       attn_ref, pd_ref, nk_ref, nv_ref, nu_ref, ext_ref):
    m = SEQ_PAIR_ROWS
    half = m // 2
    hist = POOL_STATE_LEN
    ext_ref[16:, :] = jnp.zeros((ext_ref.shape[0] - 16, D_MODEL), F32)
    row8 = lax.broadcasted_iota(jnp.int32, (m, LANES), 0)
    n_chunks = Q_W // LANES
    mq = n_chunks * m
    row1 = lax.broadcasted_iota(jnp.int32, (m, 1), 0)
    row = lax.broadcasted_iota(jnp.int32, (mq, LANES), 0)
    col = lax.broadcasted_iota(jnp.int32, (mq, LANES), 1)
    cur_valid = (row % half) >= col
    lane8 = lax.broadcasted_iota(jnp.int32, (m, LANES), 1)
    lo8 = lane8 < HEAD_DIM
    top8 = lax.broadcasted_iota(jnp.int32, (m, LANES), 0) < half
    tail = jnp.zeros((WINDOW - m, LANES), F32)
    top = row1 < half
    sink_col = jnp.concatenate([jnp.where(top, sink_ref[2 * cc], sink_ref[2 * cc + 1])
                                for cc in range(n_chunks)], axis=0)

    def pair(p, carry):
        r0 = pl.multiple_of(p * m, m)
        q8 = q_ref[pl.ds(r0, m), :]
        k8 = k_ref[pl.ds(r0, m), :]
        v8 = v_ref[pl.ds(r0, m), :]
        u8 = u_ref[pl.ds(r0, m), :]
        attn_parts = []
        pd_parts = []
        for s in range(2):
            shift = lambda a: a if s == 0 else pltpu.roll(a, half, 0)
            seq = 2 * p + s
            ck, cv = ck_ref[seq], cv_ref[seq]
            ks, vs = shift(k8), shift(v8)
            kb = jnp.concatenate([ck, ks, tail], axis=0)
            vb = jnp.concatenate([cv, vs, tail], axis=0)
            for cache, new, out in ((ck, ks, nk_ref), (cv, vs, nv_ref)):
                up = pltpu.roll(cache, WINDOW - half, 0)
                out[seq, 0:WINDOW - m, :] = up[0:WINDOW - m]
                out[seq, WINDOW - m:WINDOW, :] = jnp.where(row8 < half, up[WINDOW - m:],
                                                           pltpu.roll(new, half, 0))
            qs = shift(q8)
            qr = pltpu.roll(qs, half, 0)
            q2 = []
            for cc in range(n_chunks):
                qa = qs[:, LANES * cc:LANES * (cc + 1)]
                qb = qr[:, LANES * cc:LANES * (cc + 1)]
                if cc < n_chunks // N_KV_HEADS:
                    first = jnp.where(lo8, qa, 0.0)
                    second = jnp.where(lo8, pltpu.roll(qb, HEAD_DIM, 1), 0.0)
                else:
                    first = jnp.where(lo8, 0.0, pltpu.roll(qa, HEAD_DIM, 1))
                    second = jnp.where(lo8, 0.0, qb)
                q2.append(jnp.where(top8, first, second))
            q2 = jnp.concatenate(q2, axis=0).astype(BF16)
            o = _attend(q2, kb.astype(BF16), vb.astype(BF16), cur_valid, sink_col, None)
            chunks = []
            for cc in range(n_chunks):
                oc = o[m * cc:m * (cc + 1)]
                if cc < n_chunks // N_KV_HEADS:
                    chunks.append(jnp.where(lo8, oc, pltpu.roll(pltpu.roll(oc, half, 0), HEAD_DIM, 1)))
                else:
                    chunks.append(jnp.where(lo8, pltpu.roll(oc, HEAD_DIM, 1), pltpu.roll(oc, half, 0)))
            attn_parts.append(jnp.concatenate(chunks, axis=1))

            ext_ref[0:hist, :] = st_ref[0, seq]
            ext_ref[hist:hist + half, :] = shift(u8)[0:half]
            nu_ref[0, seq] = ext_ref[half:hist + half, :]
            cols = []
            for g, w in enumerate(POOL_WINDOWS):
                c0, c1 = POOL_GROUP_WIDTH * g, POOL_GROUP_WIDTH * (g + 1)
                acc = ext_ref[hist:hist + m, c0:c1]
                for k in range(1, w):
                    acc = acc + ext_ref[hist - k:hist - k + m, c0:c1]
                cols.append(acc / float(w) - ext_ref[hist:hist + m, c0:c1])
            pd_parts.append(jnp.concatenate(cols, axis=1))

        attn_ref[pl.ds(r0, m), :] = jnp.where(top, attn_parts[0], pltpu.roll(attn_parts[1], half, 0))
        pd_ref[pl.ds(r0, m), :] = jnp.where(top, pd_parts[0], pltpu.roll(pd_parts[1], half, 0))
        return carry

    lax.fori_loop(0, SAMPLE_PAIRS_PER_STEP, pair, 0)


def _mixer_sample(sinks, q, k, v, u, cache_k, cache_v, state):
    n = q.shape[0]
    m = SEQ_PAIR_ROWS * SAMPLE_PAIRS_PER_STEP
    rows = lambda w: pl.BlockSpec((m, w), lambda i: (i, 0))
    seqs = lambda a: pl.BlockSpec((2 * SAMPLE_PAIRS_PER_STEP,) + a.shape[1:], lambda i: (i, 0, 0))
    st_spec = pl.BlockSpec((1, 2 * SAMPLE_PAIRS_PER_STEP) + state.shape[2:], lambda i: (0, i, 0, 0))
    like = lambda a: jax.ShapeDtypeStruct(a.shape, F32)
    return pl.pallas_call(
        _mixer_sample_body,
        grid=(n // m,),
        in_specs=[pl.BlockSpec(memory_space=pltpu.SMEM), rows(Q_W), rows(KV_W), rows(KV_W),
                  rows(D_MODEL), seqs(cache_k), seqs(cache_v), st_spec],
        out_specs=(rows(Q_W), rows(D_MODEL), seqs(cache_k), seqs(cache_v), st_spec),
        out_shape=(jax.ShapeDtypeStruct((n, Q_W), F32), jax.ShapeDtypeStruct((n, D_MODEL), F32),
                   like(cache_k), like(cache_v), like(state)),
        scratch_shapes=[pltpu.VMEM((16 + SEQ_PAIR_ROWS, D_MODEL), F32)],
        compiler_params=pltpu.CompilerParams(dimension_semantics=("arbitrary",),
                                             vmem_limit_bytes=VMEM_LIMIT),
        name="mixer_sample",
    )(sinks, q, k, v, u, cache_k, cache_v, state)


def _dense_body(x_ref, attn_ref, pd_ref, sga_ref, sgp_ref, mix_ref, ps_ref, wa_ref, wp_ref,
                wo_ref, gf_ref, wrh_ref, wrl_ref, br_ref, ltri_ref, ssel_ref, crow_ref,
                x1_ref, hn_ref, rt_ref, cnt_ref, seg_ref, carry_ref):
    rows = x_ref.shape[0]
    step = pl.program_id(0)

    @pl.when(step % TILE_BLOCKS == 0)
    def _():
        carry_ref[...] = jnp.zeros_like(carry_ref)

    pd = pd_ref[...].astype(BF16)
    pooled = []
    for g in range(len(POOL_WINDOWS)):
        c0, c1 = POOL_GROUP_WIDTH * g, POOL_GROUP_WIDTH * (g + 1)
        pooled.append((_bdot(pd[:, c0:c1], mix_ref[g]) * ps_ref[:, c0:c1]).astype(BF16))
    pooled = jnp.concatenate(pooled, axis=1)
    merged = (sga_ref[...] * _bdot(attn_ref[...].astype(BF16), wa_ref[...])
              + sgp_ref[...] * _bdot(pooled, wp_ref[...]))
    x1 = x_ref[...] + _bdot(merged.astype(BF16), wo_ref[...])
    x1_ref[...] = x1
    hn = _rms(x1, gf_ref[...])
    hn_ref[...] = hn
    hi = hn.astype(BF16)
    lo = (hn - hi.astype(F32)).astype(BF16)
    both = _bdot(hi, wrl_ref[...])
    logits = both[:, :LANES] + both[:, LANES:] + _bdot(lo, wrh_ref[...]) + br_ref[...]

    lane = lax.broadcasted_iota(jnp.int32, (rows, LANES), 1)
    big = jnp.int32(LANES)
    gl = jnp.where(lane < N_EXPERT_GROUPS, logits, NEG_INF)
    gmax = jnp.max(gl, axis=-1, keepdims=True)
    gidx = jnp.min(jnp.where(gl == gmax, lane, big), axis=-1, keepdims=True)
    g_w = 1.0 / jnp.sum(jnp.exp(gl - gmax), axis=-1, keepdims=True)
    e0 = EXPERT_LANE0 + gidx * EXPERTS_PER_GROUP
    el = jnp.where((lane >= e0) & (lane < e0 + EXPERTS_PER_GROUP), logits, NEG_INF)
    l1 = jnp.max(el, axis=-1, keepdims=True)
    i1 = jnp.min(jnp.where(el == l1, lane, big), axis=-1, keepdims=True)
    el2 = jnp.where(lane == i1, NEG_INF, el)
    l2 = jnp.max(el2, axis=-1, keepdims=True)
    i2 = jnp.min(jnp.where(el2 == l2, lane, big), axis=-1, keepdims=True)
    e = jnp.exp(l2 - l1)
    w1 = 1.0 / (1.0 + e)
    w2 = e * w1

    sel = (lane == i1) | (lane == i2)
    onehot = jnp.where(sel, 1.0, 0.0).astype(BF16)
    rank = _bdot(ltri_ref[...], onehot) + carry_ref[0:1, :] + crow_ref[...]
    segsum = _bdot(ssel_ref[...], onehot)
    seg_ref[0] = segsum
    carry_ref[...] = carry_ref[...] + segsum[0:1, :]
    cnt_ref[0] = carry_ref[...]
    r1 = jnp.sum(jnp.where(lane == i1, rank, 0.0), axis=-1, keepdims=True)
    r2 = jnp.sum(jnp.where(lane == i2, rank, 0.0), axis=-1, keepdims=True)
    cols = ((i1 - EXPERT_LANE0).astype(F32), (i2 - EXPERT_LANE0).astype(F32), r1, r2,
            g_w * w1, g_w * w2)
    tile = jnp.zeros((rows, LANES), F32)
    for c, val in enumerate(cols):
        tile = jnp.where(lane == c, val, tile)
    rt_ref[...] = tile


def _dense(x, attn, pd, sga, sgp, mix_b, pool_scale, wa_b, wp_b, wo_b, gf, wr_hi, wr_lo, br,
           ltri, ssel, crow):
    n = x.shape[0]
    nb = n // ROW_BLOCK
    row = lambda w: pl.BlockSpec((ROW_BLOCK, w), lambda i: (i, 0))
    full = lambda a: pl.BlockSpec(a.shape, lambda i: (0,) * a.ndim)
    stat = pl.BlockSpec((1, 8, LANES), lambda i: (i, 0, 0))
    consts = (mix_b, pool_scale, wa_b, wp_b, wo_b, gf, wr_hi, wr_lo, br, ltri, ssel, crow)
    return pl.pallas_call(
        _dense_body,
        grid=(nb,),
        in_specs=[row(D_MODEL)] * 5 + [full(a) for a in consts],
        out_specs=(row(D_MODEL), row(D_MODEL), row(LANES), stat, stat),
        out_shape=(jax.ShapeDtypeStruct((n, D_MODEL), F32),
                   jax.ShapeDtypeStruct((n, D_MODEL), F32),
                   jax.ShapeDtypeStruct((n, LANES), F32),
                   jax.ShapeDtypeStruct((nb, 8, LANES), F32),
                   jax.ShapeDtypeStruct((nb, 8, LANES), F32)),
        scratch_shapes=[pltpu.VMEM((8, LANES), F32)],
        compiler_params=pltpu.CompilerParams(dimension_semantics=("arbitrary",),
                                             vmem_limit_bytes=VMEM_LIMIT),
        name="dense",
    )(x, attn, pd, sga, sgp, *consts)


def _experts_body(nblk_ref, blk0_ref, cnt_ref, slots_ref, wts_ref, hn_p, hn_s, x1_p, x1_s,
                  wg_ref, wu_ref, wd_ref, y_p, y_s,
                  hn_t, acc, xg0, xg1, yb0, yb1, tok, sem_in, sem_out):
    tau = pl.program_id(0)
    e = pl.program_id(1)
    t_rows = TILE_TOKENS
    dummy = 2 * t_rows
    p_rows = pl.ds(pl.multiple_of(tau * TILE_PROMPT, TILE_PROMPT), TILE_PROMPT)
    s_rows = pl.ds(pl.multiple_of(tau * TILE_SAMPLE, TILE_SAMPLE), TILE_SAMPLE)
    tile_p = pl.ds(0, TILE_PROMPT)
    tile_s = pl.ds(TILE_PROMPT, TILE_SAMPLE)

    def in_copies():
        return [pltpu.make_async_copy(hn_p.at[p_rows], hn_t.at[tile_p], sem_in.at[0]),
                pltpu.make_async_copy(hn_s.at[s_rows], hn_t.at[tile_s], sem_in.at[1]),
                pltpu.make_async_copy(x1_p.at[p_rows], acc.at[tile_p], sem_in.at[2]),
                pltpu.make_async_copy(x1_s.at[s_rows], acc.at[tile_s], sem_in.at[3])]

    def out_copies():
        return [pltpu.make_async_copy(acc.at[tile_p], y_p.at[p_rows], sem_out.at[0]),
                pltpu.make_async_copy(acc.at[tile_s], y_s.at[s_rows], sem_out.at[1])]

    xbufs = (xg0, xg1)
    ybufs = (yb0, yb1)
    end_block = blk0_ref[tau, N_EXPERTS - 1] + nblk_ref[tau, N_EXPERTS - 1]
    end_slot = end_block * MOE_ROWS

    def gather_block(b, xdst):
        base = b * MOE_ROWS
        for j in range(MOE_ROWS):
            t = lax.shift_right_logical(tok[base + j], 1)
            xdst[j // 8, pl.ds(j % 8, 1), :] = hn_t[pl.ds(t, 1), :]

    def scatter_block(b, ysrc):
        base = b * MOE_ROWS
        for j0 in range(0, MOE_ROWS, SCATTER_BATCH):
            ents = [tok[base + j0 + i] for i in range(SCATTER_BATCH)]
            rows = [lax.shift_right_logical(en, 1) for en in ents]
            vals = [acc[pl.ds(rows[i], 1), :]
                    + wts_ref[ents[i]] * ysrc[(j0 + i) // 8, pl.ds((j0 + i) % 8, 1), :]
                    for i in range(SCATTER_BATCH)]
            for i in range(SCATTER_BATCH):
                acc[pl.ds(rows[i], 1), :] = vals[i]

    @pl.when(e == 0)
    def _load_tile():
        for cp in in_copies():
            cp.start()

        hn_t[t_rows:, :] = jnp.zeros((DUMMY_ROWS, D_MODEL), F32)
        yb0[...] = jnp.zeros_like(yb0)

        def pad_range(lo, hi):
            def fill(g, c):
                for i in range(8):
                    tok[lo + g * 8 + i] = dummy
                return c
            lax.fori_loop(0, lax.shift_right_logical(hi - lo + 7, 3), fill, 0)
        pad_range(0, MOE_ROWS)
        pad_range(end_slot, end_slot + MOE_ROWS)

        def pad_fill(ei, c):
            s0 = blk0_ref[tau, ei] * MOE_ROWS
            pad_range(s0 + cnt_ref[tau, ei], s0 + nblk_ref[tau, ei] * MOE_ROWS)
            return c
        lax.fori_loop(0, N_EXPERTS, pad_fill, 0)

        def invert(g, c):
            a0 = g * 16
            for i in range(16):
                tok[slots_ref[a0 + i]] = a0 + i
            return c
        lax.fori_loop(0, t_rows // 8, invert, 0)

        for cp in in_copies():
            cp.wait()
        acc[t_rows:, :] = jnp.zeros((DUMMY_ROWS, D_MODEL), F32)
        gather_block(1, xbufs[1])

    first_block = blk0_ref[tau, e]

    def block(jb, c):
        b = first_block + jb
        for par in range(2):
            @pl.when((b & 1) == par)
            def _():
                gather_block(b + 1, xbufs[1 - par])
                xb = xbufs[par][...].reshape(MOE_ROWS, D_MODEL).astype(BF16)
                act = (jax.nn.silu(_bdot(xb, wg_ref[0])) * _bdot(xb, wu_ref[0])).astype(BF16)
                ybufs[par][...] = _bdot(act, wd_ref[0]).reshape(MOE_ROWS // 8, 8, D_MODEL)
                scatter_block(b - 1, ybufs[1 - par])
        return c
    lax.fori_loop(0, nblk_ref[tau, e], block, 0)

    @pl.when(e == N_EXPERTS - 1)
    def _store_tile():
        last = end_block - 1
        for par in range(2):
            @pl.when((last & 1) == par)
            def _():
                scatter_block(last, ybufs[par])
        cps = out_copies()
        for cp in cps:
            cp.start()
        for cp in cps:
            cp.wait()


def _experts(nblk, blk0, cnt, slots, wts, hn_p, hn_s, x1_p, x1_s, w_gate, w_up, w_down):
    any_spec = pl.BlockSpec(memory_space=pl.ANY)
    wspec = lambda a: pl.BlockSpec((1,) + a.shape[1:], lambda t, e, *_: (e, 0, 0))
    smem = lambda a: pl.BlockSpec((a.shape[0] // N_TILES,), lambda t, e, *_: (t,),
                                  memory_space=pltpu.SMEM)
    grid_spec = pltpu.PrefetchScalarGridSpec(
        num_scalar_prefetch=3,
        grid=(N_TILES, N_EXPERTS),
        in_specs=[smem(slots), smem(wts), any_spec, any_spec, any_spec, any_spec,
                  wspec(w_gate), wspec(w_up), wspec(w_down)],
        out_specs=(any_spec, any_spec),
        scratch_shapes=[
            pltpu.VMEM((TILE_TOKENS + DUMMY_ROWS, D_MODEL), F32),
            pltpu.VMEM((TILE_TOKENS + DUMMY_ROWS, D_MODEL), F32),
            pltpu.VMEM((MOE_ROWS // 8, 8, D_MODEL), F32),
            pltpu.VMEM((MOE_ROWS // 8, 8, D_MODEL), F32),
            pltpu.VMEM((MOE_ROWS // 8, 8, D_MODEL), F32),
            pltpu.VMEM((MOE_ROWS // 8, 8, D_MODEL), F32),
            pltpu.SMEM((LIST_CAP,), jnp.int32),
            pltpu.SemaphoreType.DMA((4,)),
            pltpu.SemaphoreType.DMA((2,)),
        ],
    )
    return pl.pallas_call(
        _experts_body,
        grid_spec=grid_spec,
        out_shape=(jax.ShapeDtypeStruct(x1_p.shape, F32), jax.ShapeDtypeStruct(x1_s.shape, F32)),
        compiler_params=pltpu.CompilerParams(dimension_semantics=("arbitrary", "arbitrary"),
                                             vmem_limit_bytes=EXPERTS_VMEM_LIMIT),
        name="experts",
    )(nblk, blk0, cnt, slots, wts, hn_p, hn_s, x1_p, x1_s, w_gate, w_up, w_down)


def _rope_tables(pos):
    half = HEAD_DIM // 2
    inv = ROPE_THETA ** (-jnp.arange(half, dtype=F32) * (2.0 / HEAD_DIM))
    ang = pos.astype(F32)[:, None] * inv[None, :]
    cos = jnp.tile(jnp.cos(ang), (1, LANES // half))
    sin = jnp.sin(ang)
    sin = jnp.tile(jnp.concatenate([-sin, sin], axis=1), (1, LANES // HEAD_DIM))
    return cos, sin


def _split_bf16(w):
    hi = w.astype(BF16)
    return hi, (w - hi.astype(F32)).astype(BF16)


def kernel(x_prompt, x_sample, cache_k, cache_v, state_pool, norm_mix_g, w_in, q_norm_g, k_norm_g,
           attn_sinks, w_attn_branch, pool_mix_w, pool_scale, w_pool_branch, w_out, norm_ffn_g,
           w_route_group, b_route_group, w_route_expert, b_route_expert, w_expert_gate,
           w_expert_up, w_expert_down):
    batch, seq, d = x_prompt.shape
    dec_batch, dec_seq, _ = x_sample.shape
    past_len = 16384
    assert x_prompt.shape == (4, 4096, D_MODEL) and x_sample.shape == (128, 4, D_MODEL)
    assert w_in.shape[0] == 1, "single layer"
    n_p, n_s = batch * seq, dec_batch * dec_seq

    g_mix = norm_mix_g[0][None, :]
    w_in_b = w_in[0].astype(BF16)
    gq = jnp.tile(q_norm_g[0], LANES // HEAD_DIM)[None, :]
    gk = jnp.tile(k_norm_g[0], LANES // HEAD_DIM)[None, :]
    sinks = attn_sinks[0]
    mix_b = pool_mix_w[0].astype(BF16)
    ps = pool_scale[0][None, :]
    wa_b = w_attn_branch[0].astype(BF16)
    wp_b = w_pool_branch[0].astype(BF16)
    wo_b = w_out[0].astype(BF16)
    gf = norm_ffn_g[0][None, :]
    wr = jnp.zeros((D_MODEL, LANES), F32)
    wr = wr.at[:, :N_EXPERT_GROUPS].set(w_route_group[0])
    wr = wr.at[:, EXPERT_LANE0:EXPERT_LANE0 + N_EXPERTS].set(w_route_expert[0])
    wr_hi, wr_lo = _split_bf16(wr)
    br = jnp.zeros((1, LANES), F32)
    br = br.at[0, :N_EXPERT_GROUPS].set(b_route_group[0])
    br = br.at[0, EXPERT_LANE0:EXPERT_LANE0 + N_EXPERTS].set(b_route_expert[0])

    cos_p, sin_p = _rope_tables(jnp.arange(seq, dtype=jnp.int32))
    pos_s = past_len + (jnp.arange(n_s, dtype=jnp.int32) % dec_seq)
    cos_s, sin_s = _rope_tables(pos_s)

    dense_consts = (mix_b, ps, wa_b, wp_b, wo_b, gf, wr_hi, jnp.concatenate([wr_hi, wr_lo], axis=1), br)
    assert n_s == ROW_BLOCK == N_TILES * TILE_SAMPLE and n_p == N_TILES * TILE_PROMPT
    ridx = jnp.arange(ROW_BLOCK, dtype=jnp.int32)
    lower = ridx[:, None] > ridx[None, :]
    seg_of = ridx // TILE_SAMPLE
    ltri_p = lower.astype(BF16)
    ltri_s = (lower & (seg_of[:, None] == seg_of[None, :])).astype(BF16)
    ssel_p = (jnp.arange(8, dtype=jnp.int32)[:, None] == 0) & (ridx[None, :] >= 0)
    ssel_s = jnp.arange(8, dtype=jnp.int32)[:, None] == seg_of[None, :]

    xp = x_prompt.reshape(n_p, d)
    q, k, v, kk, vv, u, sga, sgp = _inproj(xp, g_mix, w_in_b, cos_p, sin_p, gq, gk)
    attn, pd, wg_b, wu_b, wd_b = _mixer_prompt(sinks, q, kk, vv, u, w_expert_gate[0], w_expert_up[0],
                                               w_expert_down[0], batch, seq)
    x1_p, hn_p, rt_p, cnt_p, _ = _dense(xp, attn, pd, sga, sgp, *dense_consts, ltri_p,
                                        ssel_p.astype(BF16), jnp.zeros((ROW_BLOCK, LANES), F32))
    last_window = lambda a: a.reshape(batch, seq, KV_W)[:, -WINDOW:].reshape(
        1, batch, WINDOW, N_KV_HEADS, HEAD_DIM)
    new_k_p = last_window(k)
    new_v_p = last_window(v)
    new_u_p = u.reshape(batch, seq, d)[:, -POOL_STATE_LEN:][None]

    xs = x_sample.reshape(n_s, d)
    q, k, v, kk, vv, u, sga, sgp = _inproj(xs, g_mix, w_in_b, cos_s, sin_s, gq, gk)
    ck = cache_k[0].reshape(dec_batch, WINDOW, KV_W)
    cv = cache_v[0].reshape(dec_batch, WINDOW, KV_W)
    attn, pd, nk_s, nv_s, nu_s = _mixer_sample(sinks, q.astype(F32), k, v, u, ck, cv, state_pool)
    cnt_tiles_p = cnt_p[TILE_BLOCKS - 1::TILE_BLOCKS, 0, :]
    crow = jnp.repeat(cnt_tiles_p, TILE_SAMPLE, axis=0)
    x1_s, hn_s, rt_s, _, seg_s = _dense(xs, attn, pd, sga, sgp, *dense_consts, ltri_s,
                                        ssel_s.astype(BF16), crow)
    new_k_s = nk_s.reshape(1, dec_batch, WINDOW, N_KV_HEADS, HEAD_DIM)
    new_v_s = nv_s.reshape(1, dec_batch, WINDOW, N_KV_HEADS, HEAD_DIM)
    new_u_s = nu_s

    ex = slice(EXPERT_LANE0, EXPERT_LANE0 + N_EXPERTS)
    cnt = (cnt_tiles_p[:, ex] + seg_s[0, :N_TILES, ex]).astype(jnp.int32)
    nblk = (cnt + (MOE_ROWS - 1)) // MOE_ROWS
    blk0 = 1 + jnp.cumsum(nblk, axis=1) - nblk
    seg = MOE_ROWS * blk0
    rt = jnp.concatenate([rt_p[:, :6].reshape(N_TILES, TILE_PROMPT, 6),
                          rt_s[:, :6].reshape(N_TILES, TILE_SAMPLE, 6)], axis=1)
    expert_hit = rt[:, :, 0:2].astype(jnp.int32)[..., None] == jnp.arange(N_EXPERTS, dtype=jnp.int32)
    slot = (rt[:, :, 2:4].astype(jnp.int32)
            + jnp.sum(jnp.where(expert_hit, seg[:, None, None, :], 0), axis=-1))
    per_tile = lambda a: jnp.pad(a.reshape(N_TILES, 2 * TILE_TOKENS),
                                 ((0, 0), (0, TILE_LIST - 2 * TILE_TOKENS))).reshape(-1)
    slots = per_tile(slot)
    wts = per_tile(rt[:, :, 4:6])
    y_p, y_s = _experts(nblk, blk0, cnt, slots, wts, hn_p, hn_s, x1_p, x1_s, wg_b, wu_b, wd_b)
    y_prompt = y_p.reshape(batch, seq, d)
    y_sample = y_s.reshape(dec_batch, dec_seq, d)

    return (y_prompt, y_sample, new_k_p, new_v_p, new_u_p, new_k_s, new_v_s, new_u_s)
```

```python
import jax
import jax.numpy as jnp
from jax import lax
from jax.experimental import pallas as pl
from jax.experimental.pallas import tpu as pltpu

F32 = jnp.float32
BF16 = jnp.bfloat16

D_MODEL = 1024
HEAD_DIM = 64
N_Q_HEADS = 16
N_KV_HEADS = 2
WINDOW = 128
ROPE_THETA = 10000.0
POOL_WINDOWS = (2, 4, 8, 16)
POOL_GROUP_WIDTH = D_MODEL // len(POOL_WINDOWS)
POOL_STATE_LEN = max(POOL_WINDOWS) - 1
N_EXPERT_GROUPS = 4
EXPERTS_PER_GROUP = 8
N_EXPERTS = N_EXPERT_GROUPS * EXPERTS_PER_GROUP
D_EXPERT = 512
RMS_EPS = 1e-6
Q_W = N_Q_HEADS * HEAD_DIM
KV_W = N_KV_HEADS * HEAD_DIM
OFF_K = Q_W
OFF_V = OFF_K + KV_W
OFF_U = OFF_V + KV_W
OFF_GA = OFF_U + D_MODEL
OFF_GP = OFF_GA + D_MODEL
IN_W = OFF_GP + D_MODEL

LANES = 128
ROW_BLOCK = 512
EXPERT_LANE0 = 32
VMEM_LIMIT = 56 * 1024 * 1024
EXPERTS_VMEM_LIMIT = 60 * 1024 * 1024
NEG_INF = float("-inf")

N_TILES = 4
TILE_BLOCKS = 8
TILE_PROMPT = TILE_BLOCKS * ROW_BLOCK
TILE_SAMPLE = 128
TILE_TOKENS = TILE_PROMPT + TILE_SAMPLE
MOE_ROWS = 304
TOKEN_ROWS = D_MODEL // LANES
STAGE_ROWS = 512
SCATTER_BATCH = 4
TILE_LIST = -(-(2 * TILE_TOKENS + 1) // 1024) * 1024
LIST_CAP = 2 * TILE_TOKENS + N_EXPERTS * (MOE_ROWS - 1) + 2 * MOE_ROWS + 8


def _rms(x, g):
    return x * lax.rsqrt(jnp.mean(x * x, axis=-1, keepdims=True) + RMS_EPS) * g


def _bdot(a, b):
    return jnp.dot(a, b, preferred_element_type=F32)


def _inproj_body(x_ref, g_ref, w_ref, cos_ref, sin_ref, gq_ref, gk_ref,
                 q_ref, k_ref, v_ref, kk_ref, vv_ref, u_ref, sga_ref, sgp_ref):
    rows = x_ref.shape[0]
    hb = _rms(x_ref[...], g_ref[...]).astype(BF16)
    cos = cos_ref[...]
    sin = sin_ref[...]
    lane = lax.broadcasted_iota(jnp.int32, (rows, LANES), 1)
    lo = lane < HEAD_DIM
    first_half = (lane % HEAD_DIM) < (HEAD_DIM // 2)

    def head_norm_rope(zc, gain):
        sq = zc * zc
        ss_lo = jnp.sum(jnp.where(lo, sq, 0.0), axis=-1, keepdims=True)
        ss_hi = jnp.sum(jnp.where(lo, 0.0, sq), axis=-1, keepdims=True)
        r = lax.rsqrt(jnp.where(lo, ss_lo, ss_hi) * (1.0 / HEAD_DIM) + RMS_EPS)
        y = zc * r * gain
        partner = jnp.where(first_half, pltpu.roll(y, LANES - HEAD_DIM // 2, 1),
                            pltpu.roll(y, HEAD_DIM // 2, 1))
        return y * cos + partner * sin

    gq = gq_ref[...]
    for j in range(Q_W // 256):
        z = _bdot(hb, w_ref[:, 256 * j:256 * (j + 1)])
        for c in range(2):
            qn = head_norm_rope(z[:, LANES * c:LANES * (c + 1)], gq) * (HEAD_DIM ** -0.5)
            q_ref[:, 256 * j + LANES * c:256 * j + LANES * (c + 1)] = qn.astype(BF16)

    z = _bdot(hb, w_ref[:, OFF_K:OFF_U])
    kn = head_norm_rope(z[:, :KV_W], gk_ref[...])
    vr = z[:, KV_W:]
    k_ref[...] = kn
    v_ref[...] = vr
    kr = pltpu.roll(kn, HEAD_DIM, 1)
    vrr = pltpu.roll(vr, HEAD_DIM, 1)
    kk_ref[:, :LANES] = jnp.where(lo, kn, kr).astype(BF16)
    kk_ref[:, LANES:] = jnp.where(lo, kr, kn).astype(BF16)
    vv_ref[:, :LANES] = jnp.where(lo, vr, vrr).astype(BF16)
    vv_ref[:, LANES:] = jnp.where(lo, vrr, vr).astype(BF16)

    for j in range(D_MODEL // 256):
        u_ref[:, 256 * j:256 * (j + 1)] = _bdot(hb, w_ref[:, OFF_U + 256 * j:OFF_U + 256 * (j + 1)])
        sga_ref[:, 256 * j:256 * (j + 1)] = jax.nn.sigmoid(
            _bdot(hb, w_ref[:, OFF_GA + 256 * j:OFF_GA + 256 * (j + 1)]))
        sgp_ref[:, 256 * j:256 * (j + 1)] = jax.nn.sigmoid(
            _bdot(hb, w_ref[:, OFF_GP + 256 * j:OFF_GP + 256 * (j + 1)]))


def _inproj(x, g, w_in_b, cos, sin, gq, gk):
    n = x.shape[0]
    nb = n // ROW_BLOCK
    ncos = cos.shape[0] // ROW_BLOCK
    row = lambda w: pl.BlockSpec((ROW_BLOCK, w), lambda i: (i, 0))
    full = lambda a: pl.BlockSpec(a.shape, lambda i: (0,) * a.ndim)
    tab = pl.BlockSpec((ROW_BLOCK, LANES), lambda i: (i % ncos, 0))
    out_shapes = (
        jax.ShapeDtypeStruct((n, Q_W), BF16),
        jax.ShapeDtypeStruct((n, KV_W), F32),
        jax.ShapeDtypeStruct((n, KV_W), F32),
        jax.ShapeDtypeStruct((n, 2 * LANES), BF16),
        jax.ShapeDtypeStruct((n, 2 * LANES), BF16),
        jax.ShapeDtypeStruct((n, D_MODEL), F32),
        jax.ShapeDtypeStruct((n, D_MODEL), F32),
        jax.ShapeDtypeStruct((n, D_MODEL), F32),
    )
    return pl.pallas_call(
        _inproj_body,
        grid=(nb,),
        in_specs=[row(D_MODEL), full(g), full(w_in_b), tab, tab, full(gq), full(gk)],
        out_specs=tuple(row(s.shape[1]) for s in out_shapes),
        out_shape=out_shapes,
        compiler_params=pltpu.CompilerParams(dimension_semantics=("arbitrary",),
                                             vmem_limit_bytes=VMEM_LIMIT),
        name="inproj",
    )(x, g, w_in_b, cos, sin, gq, gk)


def _attend(q2, kg, vg, cur_valid, sink_col, prev_dead):
    s = lax.dot_general(q2, kg, (((1,), (1,)), ((), ())), preferred_element_type=F32)
    s_prev = s[:, :WINDOW]
    if prev_dead is not None:
        s_prev = jnp.where(prev_dead, NEG_INF, s_prev)
    sc = jnp.where(cur_valid, s[:, WINDOW:], s_prev)
    m = jnp.maximum(jnp.max(sc, axis=-1, keepdims=True), sink_col)
    p = jnp.exp(sc - m)
    den = jnp.sum(p, axis=-1, keepdims=True) + jnp.exp(sink_col - m)
    pn = p * (1.0 / den)
    p2 = jnp.concatenate([jnp.where(cur_valid, 0.0, pn), jnp.where(cur_valid, pn, 0.0)], axis=1)
    return _bdot(p2.astype(BF16), vg)


POOL_PAD_ROWS = 8


def _mixer_prompt_body(sink_ref, q_ref, kkc_ref, kkp_ref, vvc_ref, vvp_ref, uc_ref, up_ref,
                       wg_ref, wu_ref, wd_ref,
                       attn_ref, pd_ref, wgb_ref, wub_ref, wdb_ref, ext_ref, s_a, s_b):
    wgb_ref[...] = wg_ref[...].astype(BF16)
    wub_ref[...] = wu_ref[...].astype(BF16)
    wdb_ref[...] = wd_ref[...].astype(BF16)
    i = pl.program_id(1)
    first = i == 0
    m2 = 2 * WINDOW
    row = lax.broadcasted_iota(jnp.int32, (m2, LANES), 0)
    col = lax.broadcasted_iota(jnp.int32, (m2, LANES), 1)
    cur_valid = (row % WINDOW) >= col
    upper = lax.broadcasted_iota(jnp.int32, (m2, 1), 0) < WINDOW
    lo = lax.broadcasted_iota(jnp.int32, (WINDOW, LANES), 1) < HEAD_DIM

    for j in range(ROW_BLOCK // WINDOW):
        r0 = WINDOW * j
        if j == 0:
            kprev, vprev = kkp_ref[...], vvp_ref[...]
            prev_dead = first
        else:
            kprev, vprev = kkc_ref[r0 - WINDOW:r0, :], vvc_ref[r0 - WINDOW:r0, :]
            prev_dead = None
        kband = jnp.concatenate([kprev, kkc_ref[r0:r0 + WINDOW, :]], axis=0)
        vband = jnp.concatenate([vprev, vvc_ref[r0:r0 + WINDOW, :]], axis=0)
        for g in range(N_KV_HEADS):
            kg = kband[:, LANES * g:LANES * (g + 1)]
            vg = vband[:, LANES * g:LANES * (g + 1)]
            for c in range(4):
                cc = 4 * g + c
                qc = q_ref[r0:r0 + WINDOW, LANES * cc:LANES * (cc + 1)]
                zero = jnp.zeros_like(qc)
                q2 = jnp.concatenate([jnp.where(lo, qc, zero), jnp.where(lo, zero, qc)], axis=0)
                sink_col = jnp.where(upper, sink_ref[2 * cc], sink_ref[2 * cc + 1])
                o = _attend(q2, kg, vg, cur_valid, sink_col, prev_dead)
                attn_ref[r0:r0 + WINDOW, LANES * cc:LANES * (cc + 1)] = jnp.where(
                    lo, o[:WINDOW], o[WINDOW:]).astype(BF16)

    pad, top = POOL_PAD_ROWS, POOL_PAD_ROWS + 16
    end = top + ROW_BLOCK
    for ref in (ext_ref, s_a, s_b):
        ref[0:pad, :] = jnp.zeros((pad, D_MODEL), F32)
    ext_ref[pad:top, :] = jnp.where(first, 0.0, up_ref[...])
    ext_ref[top:, :] = uc_ref[...]
    pos = i * ROW_BLOCK + lax.broadcasted_iota(jnp.int32, (ROW_BLOCK, 1), 0)
    src = ext_ref
    for g, w in enumerate(POOL_WINDOWS):
        c0 = POOL_GROUP_WIDTH * g
        dst = (s_a, s_b)[g % 2]
        shift = w // 2
        dst[pad:end, c0:] = src[pad:end, c0:] + src[pad - shift:end - shift, c0:]
        c1 = c0 + POOL_GROUP_WIDTH
        cnt = jnp.minimum(w, pos + 1).astype(F32)
        pd_ref[:, c0:c1] = (dst[top:end, c0:c1] / cnt - uc_ref[:, c0:c1]).astype(BF16)
        src = dst


def _mixer_prompt(sinks, q, kk, vv, u, w_gate, w_up, w_down, batch, seq):
    nb = seq // ROW_BLOCK
    assert batch * nb == N_EXPERTS, "one expert's weights are cast per grid step"
    sub = ROW_BLOCK // WINDOW
    cur = lambda w: pl.BlockSpec((ROW_BLOCK, w), lambda b, i: (b * nb + i, 0))
    wspec = lambda a: pl.BlockSpec((1,) + a.shape[1:], lambda b, i: (b * nb + i, 0, 0))
    weights = (w_gate, w_up, w_down)
    prev_kv = pl.BlockSpec((WINDOW, 2 * LANES),
                           lambda b, i: (jnp.maximum((b * nb + i) * sub - 1, 0), 0))
    prev_u = pl.BlockSpec((16, D_MODEL),
                          lambda b, i: (jnp.maximum((b * nb + i) * (ROW_BLOCK // 16) - 1, 0), 0))
    n = batch * seq
    return pl.pallas_call(
        _mixer_prompt_body,
        grid=(batch, nb),
        in_specs=[pl.BlockSpec(memory_space=pltpu.SMEM), cur(Q_W), cur(2 * LANES), prev_kv,
                  cur(2 * LANES), prev_kv, cur(D_MODEL), prev_u] + [wspec(a) for a in weights],
        out_specs=(cur(Q_W), cur(D_MODEL)) + tuple(wspec(a) for a in weights),
        out_shape=(jax.ShapeDtypeStruct((n, Q_W), BF16), jax.ShapeDtypeStruct((n, D_MODEL), BF16))
        + tuple(jax.ShapeDtypeStruct(a.shape, BF16) for a in weights),
        scratch_shapes=[pltpu.VMEM((POOL_PAD_ROWS + 16 + ROW_BLOCK, D_MODEL), F32)] * 3,
        compiler_params=pltpu.CompilerParams(dimension_semantics=("arbitrary", "arbitrary"),
                                             vmem_limit_bytes=VMEM_LIMIT),
        name="mixer_prompt",
    )(sinks, q, kk, kk, vv, vv, u, u, *weights)


SEQ_PAIR_ROWS = 8
SAMPLE_PAIRS_PER_STEP = 8


def _mixer_sample_body(sink_ref, q_ref, k_ref, v_ref, u_ref, ck_ref, cv_ref, st_ref,
                       attn_ref, pd_ref, nk_ref, nv_ref, nu_ref, ext_ref):
    m = SEQ_PAIR_ROWS
    half = m // 2
    hist = POOL_STATE_LEN
    ext_ref[16:, :] = jnp.zeros((ext_ref.shape[0] - 16, D_MODEL), F32)
    row8 = lax.broadcasted_iota(jnp.int32, (m, LANES), 0)
    n_chunks = Q_W // LANES
    mq = n_chunks * m
    row1 = lax.broadcasted_iota(jnp.int32, (m, 1), 0)
    row = lax.broadcasted_iota(jnp.int32, (mq, LANES), 0)
    col = lax.broadcasted_iota(jnp.int32, (mq, LANES), 1)
    cur_valid = (row % half) >= col
    lane8 = lax.broadcasted_iota(jnp.int32, (m, LANES), 1)
    lo8 = lane8 < HEAD_DIM
    top8 = lax.broadcasted_iota(jnp.int32, (m, LANES), 0) < half
    tail = jnp.zeros((WINDOW - m, LANES), F32)
    top = row1 < half
    sink_col = jnp.concatenate([jnp.where(top, sink_ref[2 * cc], sink_ref[2 * cc + 1])
                                for cc in range(n_chunks)], axis=0)

    def pair(p, carry):
        r0 = pl.multiple_of(p * m, m)
        q8 = q_ref[pl.ds(r0, m), :]
        k8 = k_ref[pl.ds(r0, m), :]
        v8 = v_ref[pl.ds(r0, m), :]
        u8 = u_ref[pl.ds(r0, m), :]
        attn_parts = []
        pd_parts = []
        for s in range(2):
            shift = lambda a: a if s == 0 else pltpu.roll(a, half, 0)
            seq = 2 * p + s
            ck, cv = ck_ref[seq], cv_ref[seq]
            ks, vs = shift(k8), shift(v8)
            kb = jnp.concatenate([ck, ks, tail], axis=0)
            vb = jnp.concatenate([cv, vs, tail], axis=0)
            for cache, new, out in ((ck, ks, nk_ref), (cv, vs, nv_ref)):
                up = pltpu.roll(cache, WINDOW - half, 0)
                out[seq, 0:WINDOW - m, :] = up[0:WINDOW - m]
                out[seq, WINDOW - m:WINDOW, :] = jnp.where(row8 < half, up[WINDOW - m:],
                                                           pltpu.roll(new, half, 0))
            qs = shift(q8)
            qr = pltpu.roll(qs, half, 0)
            q2 = []
            for cc in range(n_chunks):
                qa = qs[:, LANES * cc:LANES * (cc + 1)]
                qb = qr[:, LANES * cc:LANES * (cc + 1)]
                if cc < n_chunks // N_KV_HEADS:
                    first = jnp.where(lo8, qa, 0.0)
                    second = jnp.where(lo8, pltpu.roll(qb, HEAD_DIM, 1), 0.0)
                else:
                    first = jnp.where(lo8, 0.0, pltpu.roll(qa, HEAD_DIM, 1))
                    second = jnp.where(lo8, 0.0, qb)
                q2.append(jnp.where(top8, first, second))
            q2 = jnp.concatenate(q2, axis=0).astype(BF16)
            o = _attend(q2, kb.astype(BF16), vb.astype(BF16), cur_valid, sink_col, None)
            chunks = []
            for cc in range(n_chunks):
                oc = o[m * cc:m * (cc + 1)]
                if cc < n_chunks // N_KV_HEADS:
                    chunks.append(jnp.where(lo8, oc, pltpu.roll(pltpu.roll(oc, half, 0), HEAD_DIM, 1)))
                else:
                    chunks.append(jnp.where(lo8, pltpu.roll(oc, HEAD_DIM, 1), pltpu.roll(oc, half, 0)))
            attn_parts.append(jnp.concatenate(chunks, axis=1))

            ext_ref[0:hist, :] = st_ref[0, seq]
            ext_ref[hist:hist + half, :] = shift(u8)[0:half]
            nu_ref[0, seq] = ext_ref[half:hist + half, :]
            cols = []
            for g, w in enumerate(POOL_WINDOWS):
                c0, c1 = POOL_GROUP_WIDTH * g, POOL_GROUP_WIDTH * (g + 1)
                acc = ext_ref[hist:hist + m, c0:c1]
                for k in range(1, w):
                    acc = acc + ext_ref[hist - k:hist - k + m, c0:c1]
                cols.append(acc / float(w) - ext_ref[hist:hist + m, c0:c1])
            pd_parts.append(jnp.concatenate(cols, axis=1))

        attn_ref[pl.ds(r0, m), :] = jnp.where(top, attn_parts[0], pltpu.roll(attn_parts[1], half, 0))
        pd_ref[pl.ds(r0, m), :] = jnp.where(top, pd_parts[0], pltpu.roll(pd_parts[1], half, 0))
        return carry

    lax.fori_loop(0, SAMPLE_PAIRS_PER_STEP, pair, 0)


def _mixer_sample(sinks, q, k, v, u, cache_k, cache_v, state):
    n = q.shape[0]
    m = SEQ_PAIR_ROWS * SAMPLE_PAIRS_PER_STEP
    rows = lambda w: pl.BlockSpec((m, w), lambda i: (i, 0))
    seqs = lambda a: pl.BlockSpec((2 * SAMPLE_PAIRS_PER_STEP,) + a.shape[1:], lambda i: (i, 0, 0))
    st_spec = pl.BlockSpec((1, 2 * SAMPLE_PAIRS_PER_STEP) + state.shape[2:], lambda i: (0, i, 0, 0))
    like = lambda a: jax.ShapeDtypeStruct(a.shape, F32)
    return pl.pallas_call(
        _mixer_sample_body,
        grid=(n // m,),
        in_specs=[pl.BlockSpec(memory_space=pltpu.SMEM), rows(Q_W), rows(KV_W), rows(KV_W),
                  rows(D_MODEL), seqs(cache_k), seqs(cache_v), st_spec],
        out_specs=(rows(Q_W), rows(D_MODEL), seqs(cache_k), seqs(cache_v), st_spec),
        out_shape=(jax.ShapeDtypeStruct((n, Q_W), F32), jax.ShapeDtypeStruct((n, D_MODEL), F32),
                   like(cache_k), like(cache_v), like(state)),
        scratch_shapes=[pltpu.VMEM((16 + SEQ_PAIR_ROWS, D_MODEL), F32)],
        compiler_params=pltpu.CompilerParams(dimension_semantics=("arbitrary",),
                                             vmem_limit_bytes=VMEM_LIMIT),
        name="mixer_sample",
    )(sinks, q, k, v, u, cache_k, cache_v, state)


def _dense_body(x_ref, attn_ref, pd_ref, sga_ref, sgp_ref, mix_ref, ps_ref, wa_ref, wp_ref,
                wo_ref, gf_ref, wrh_ref, wrl_ref, br_ref, ltri_ref, ssel_ref, crow_ref,
                x1_ref, hn_ref, rt_ref, cnt_ref, seg_ref, carry_ref):
    rows = x_ref.shape[0]
    step = pl.program_id(0)

    @pl.when(step % TILE_BLOCKS == 0)
    def _():
        carry_ref[...] = jnp.zeros_like(carry_ref)

    pd = pd_ref[...].astype(BF16)
    pooled = []
    for g in range(len(POOL_WINDOWS)):
        c0, c1 = POOL_GROUP_WIDTH * g, POOL_GROUP_WIDTH * (g + 1)
        pooled.append((_bdot(pd[:, c0:c1], mix_ref[g]) * ps_ref[:, c0:c1]).astype(BF16))
    pooled = jnp.concatenate(pooled, axis=1)
    merged = (sga_ref[...] * _bdot(attn_ref[...].astype(BF16), wa_ref[...])
              + sgp_ref[...] * _bdot(pooled, wp_ref[...]))
    x1 = x_ref[...] + _bdot(merged.astype(BF16), wo_ref[...])
    x1_ref[...] = _to_token_rows(x1)
    hn = _rms(x1, gf_ref[...])
    hn_ref[...] = _to_token_rows(hn)
    hi = hn.astype(BF16)
    lo = (hn - hi.astype(F32)).astype(BF16)
    both = _bdot(hi, wrl_ref[...])
    logits = both[:, :LANES] + both[:, LANES:] + _bdot(lo, wrh_ref[...]) + br_ref[...]

    lane = lax.broadcasted_iota(jnp.int32, (rows, LANES), 1)
    big = jnp.int32(LANES)
    gl = jnp.where(lane < N_EXPERT_GROUPS, logits, NEG_INF)
    gmax = jnp.max(gl, axis=-1, keepdims=True)
    gidx = jnp.min(jnp.where(gl == gmax, lane, big), axis=-1, keepdims=True)
    g_w = 1.0 / jnp.sum(jnp.exp(gl - gmax), axis=-1, keepdims=True)
    e0 = EXPERT_LANE0 + gidx * EXPERTS_PER_GROUP
    el = jnp.where((lane >= e0) & (lane < e0 + EXPERTS_PER_GROUP), logits, NEG_INF)
    l1 = jnp.max(el, axis=-1, keepdims=True)
    i1 = jnp.min(jnp.where(el == l1, lane, big), axis=-1, keepdims=True)
    el2 = jnp.where(lane == i1, NEG_INF, el)
    l2 = jnp.max(el2, axis=-1, keepdims=True)
    i2 = jnp.min(jnp.where(el2 == l2, lane, big), axis=-1, keepdims=True)
    e = jnp.exp(l2 - l1)
    w1 = 1.0 / (1.0 + e)
    w2 = e * w1

    sel = (lane == i1) | (lane == i2)
    onehot = jnp.where(sel, 1.0, 0.0).astype(BF16)
    rank = _bdot(ltri_ref[...], onehot) + carry_ref[0:1, :] + crow_ref[...]
    segsum = _bdot(ssel_ref[...], onehot)
    seg_ref[0] = segsum
    carry_ref[...] = carry_ref[...] + segsum[0:1, :]
    cnt_ref[0] = carry_ref[...]
    r1 = jnp.sum(jnp.where(lane == i1, rank, 0.0), axis=-1, keepdims=True)
    r2 = jnp.sum(jnp.where(lane == i2, rank, 0.0), axis=-1, keepdims=True)
    cols = ((i1 - EXPERT_LANE0).astype(F32), (i2 - EXPERT_LANE0).astype(F32), r1, r2,
            g_w * w1, g_w * w2)
    tile = jnp.zeros((rows, LANES), F32)
    for c, val in enumerate(cols):
        tile = jnp.where(lane == c, val, tile)
    rt_ref[...] = tile


def _dense(x, attn, pd, sga, sgp, mix_b, pool_scale, wa_b, wp_b, wo_b, gf, wr_hi, wr_lo, br,
           ltri, ssel, crow):
    n = x.shape[0]
    nb = n // ROW_BLOCK
    row = lambda w: pl.BlockSpec((ROW_BLOCK, w), lambda i: (i, 0))
    full = lambda a: pl.BlockSpec(a.shape, lambda i: (0,) * a.ndim)
    stat = pl.BlockSpec((1, 8, LANES), lambda i: (i, 0, 0))
    tokens = pl.BlockSpec((ROW_BLOCK * TOKEN_ROWS, LANES), lambda i: (i, 0))
    consts = (mix_b, pool_scale, wa_b, wp_b, wo_b, gf, wr_hi, wr_lo, br, ltri, ssel, crow)
    return pl.pallas_call(
        _dense_body,
        grid=(nb,),
        in_specs=[row(D_MODEL)] * 5 + [full(a) for a in consts],
        out_specs=(tokens, tokens, row(LANES), stat, stat),
        out_shape=(jax.ShapeDtypeStruct((n * TOKEN_ROWS, LANES), F32),
                   jax.ShapeDtypeStruct((n * TOKEN_ROWS, LANES), F32),
                   jax.ShapeDtypeStruct((n, LANES), F32),
                   jax.ShapeDtypeStruct((nb, 8, LANES), F32),
                   jax.ShapeDtypeStruct((nb, 8, LANES), F32)),
        scratch_shapes=[pltpu.VMEM((8, LANES), F32)],
        compiler_params=pltpu.CompilerParams(dimension_semantics=("arbitrary",),
                                             vmem_limit_bytes=VMEM_LIMIT),
        name="dense",
    )(x, attn, pd, sga, sgp, *consts)


def _to_token_rows(val):
    rows = val.shape[0]
    return val.reshape(rows, TOKEN_ROWS, LANES).reshape(rows * TOKEN_ROWS, LANES)


def _from_token_rows(flat):
    rows = flat.shape[0] // TOKEN_ROWS
    return flat.reshape(rows, TOKEN_ROWS, LANES).reshape(rows, TOKEN_ROWS * LANES)


def _experts_body(nblk_ref, blk0_ref, cnt_ref, slots_ref, wts_ref, hn_p, hn_s, x1_p, x1_s,
                  wg_ref, wu_ref, wd_ref, y_p, y_s,
                  hn_t, acc, xg0, xg1, yb0, yb1, stage, tok, sem_in, sem_out):
    tau = pl.program_id(0)
    e = pl.program_id(1)
    tr = TOKEN_ROWS
    t_rows = TILE_TOKENS
    dummy = 2 * t_rows
    p_src = pl.ds(pl.multiple_of(tau * (TILE_PROMPT * tr), TILE_PROMPT * tr), TILE_PROMPT * tr)
    s_src = pl.ds(pl.multiple_of(tau * (TILE_SAMPLE * tr), TILE_SAMPLE * tr), TILE_SAMPLE * tr)
    tile_p = pl.ds(0, TILE_PROMPT * tr)
    tile_s = pl.ds(TILE_PROMPT * tr, TILE_SAMPLE * tr)

    def in_copies():
        return [pltpu.make_async_copy(hn_p.at[p_src], hn_t.at[tile_p], sem_in.at[0]),
                pltpu.make_async_copy(hn_s.at[s_src], hn_t.at[tile_s], sem_in.at[1]),
                pltpu.make_async_copy(x1_p.at[p_src], acc.at[tile_p], sem_in.at[2]),
                pltpu.make_async_copy(x1_s.at[s_src], acc.at[tile_s], sem_in.at[3])]

    xbufs = (xg0, xg1)
    ybufs = (yb0, yb1)
    end_block = blk0_ref[tau, N_EXPERTS - 1] + nblk_ref[tau, N_EXPERTS - 1]
    end_slot = end_block * MOE_ROWS

    def token(ent):
        return pl.ds(pl.multiple_of(lax.shift_left(lax.shift_right_logical(ent, 1), 3), tr), tr)

    def gather_block(b, xdst):
        base = b * MOE_ROWS
        for j in range(MOE_ROWS):
            xdst[tr * j:tr * (j + 1), :] = hn_t[token(tok[base + j]), :]

    def scatter_block(b, ysrc):
        base = b * MOE_ROWS
        for j0 in range(0, MOE_ROWS, SCATTER_BATCH):
            ents = [tok[base + j0 + i] for i in range(SCATTER_BATCH)]
            vals = [acc[token(ents[i]), :] + wts_ref[ents[i]] * ysrc[tr * (j0 + i):tr * (j0 + i + 1), :]
                    for i in range(SCATTER_BATCH)]
            for i in range(SCATTER_BATCH):
                acc[token(ents[i]), :] = vals[i]

    @pl.when(e == 0)
    def _load_tile():
        for cp in in_copies():
            cp.start()

        hn_t[t_rows * tr:, :] = jnp.zeros((hn_t.shape[0] - t_rows * tr, LANES), F32)
        yb0[...] = jnp.zeros_like(yb0)

        def pad_range(lo, hi):
            def fill(g, c):
                for i in range(8):
                    tok[lo + g * 8 + i] = dummy
                return c
            lax.fori_loop(0, lax.shift_right_logical(hi - lo + 7, 3), fill, 0)
        pad_range(0, MOE_ROWS)
        pad_range(end_slot, end_slot + MOE_ROWS)

        def pad_fill(ei, c):
            s0 = blk0_ref[tau, ei] * MOE_ROWS
            pad_range(s0 + cnt_ref[tau, ei], s0 + nblk_ref[tau, ei] * MOE_ROWS)
            return c
        lax.fori_loop(0, N_EXPERTS, pad_fill, 0)

        def invert(g, c):
            a0 = g * 16
            for i in range(16):
                tok[slots_ref[a0 + i]] = a0 + i
            return c
        lax.fori_loop(0, t_rows // 8, invert, 0)

        for cp in in_copies():
            cp.wait()
        acc[t_rows * tr:, :] = jnp.zeros((acc.shape[0] - t_rows * tr, LANES), F32)
        gather_block(1, xbufs[1])

    first_block = blk0_ref[tau, e]

    def block(jb, c):
        b = first_block + jb
        for par in range(2):
            @pl.when((b & 1) == par)
            def _():
                gather_block(b + 1, xbufs[1 - par])
                xb = _from_token_rows(xbufs[par][...]).astype(BF16)
                act = (jax.nn.silu(_bdot(xb, wg_ref[0])) * _bdot(xb, wu_ref[0])).astype(BF16)
                ybufs[par][...] = _to_token_rows(_bdot(act, wd_ref[0]))
                scatter_block(b - 1, ybufs[1 - par])
        return c
    lax.fori_loop(0, nblk_ref[tau, e], block, 0)

    @pl.when(e == N_EXPERTS - 1)
    def _store_tile():
        last = end_block - 1
        for par in range(2):
            @pl.when((last & 1) == par)
            def _():
                scatter_block(last, ybufs[par])

        def write_back(dst, dst_row0, tile_row0, rows):
            n = min(rows, STAGE_ROWS)

            def chunk(ci, c):
                r0 = pl.multiple_of((tile_row0 + ci * n) * tr, n * tr)
                stage[0:n, :] = _from_token_rows(acc[pl.ds(r0, n * tr), :])
                cp = pltpu.make_async_copy(
                    stage.at[pl.ds(0, n)], dst.at[pl.ds(pl.multiple_of(dst_row0 + ci * n, 8), n)],
                    sem_out.at[0])
                cp.start()
                cp.wait()
                return c
            lax.fori_loop(0, rows // n, chunk, 0)
        write_back(y_p, tau * TILE_PROMPT, 0, TILE_PROMPT)
        write_back(y_s, tau * TILE_SAMPLE, TILE_PROMPT, TILE_SAMPLE)


def _experts(nblk, blk0, cnt, slots, wts, hn_p, hn_s, x1_p, x1_s, w_gate, w_up, w_down):
    any_spec = pl.BlockSpec(memory_space=pl.ANY)
    wspec = lambda a: pl.BlockSpec((1,) + a.shape[1:], lambda t, e, *_: (e, 0, 0))
    smem = lambda a: pl.BlockSpec((a.shape[0] // N_TILES,), lambda t, e, *_: (t,),
                                  memory_space=pltpu.SMEM)
    tile_rows = (TILE_TOKENS + 8) * TOKEN_ROWS
    block_rows = MOE_ROWS * TOKEN_ROWS
    grid_spec = pltpu.PrefetchScalarGridSpec(
        num_scalar_prefetch=3,
        grid=(N_TILES, N_EXPERTS),
        in_specs=[smem(slots), smem(wts), any_spec, any_spec, any_spec, any_spec,
                  wspec(w_gate), wspec(w_up), wspec(w_down)],
        out_specs=(any_spec, any_spec),
        scratch_shapes=[
            pltpu.VMEM((tile_rows, LANES), F32),
            pltpu.VMEM((tile_rows, LANES), F32),
            pltpu.VMEM((block_rows, LANES), F32),
            pltpu.VMEM((block_rows, LANES), F32),
            pltpu.VMEM((block_rows, LANES), F32),
            pltpu.VMEM((block_rows, LANES), F32),
            pltpu.VMEM((STAGE_ROWS, D_MODEL), F32),
            pltpu.SMEM((LIST_CAP,), jnp.int32),
            pltpu.SemaphoreType.DMA((4,)),
            pltpu.SemaphoreType.DMA((1,)),
        ],
    )
    n_p, n_s = x1_p.shape[0] // TOKEN_ROWS, x1_s.shape[0] // TOKEN_ROWS
    return pl.pallas_call(
        _experts_body,
        grid_spec=grid_spec,
        out_shape=(jax.ShapeDtypeStruct((n_p, D_MODEL), F32), jax.ShapeDtypeStruct((n_s, D_MODEL), F32)),
        compiler_params=pltpu.CompilerParams(dimension_semantics=("arbitrary", "arbitrary"),
                                             vmem_limit_bytes=EXPERTS_VMEM_LIMIT),
        name="experts",
    )(nblk, blk0, cnt, slots, wts, hn_p, hn_s, x1_p, x1_s, w_gate, w_up, w_down)


def _rope_tables(pos):
    half = HEAD_DIM // 2
    inv = ROPE_THETA ** (-jnp.arange(half, dtype=F32) * (2.0 / HEAD_DIM))
    ang = pos.astype(F32)[:, None] * inv[None, :]
    cos = jnp.tile(jnp.cos(ang), (1, LANES // half))
    sin = jnp.sin(ang)
    sin = jnp.tile(jnp.concatenate([-sin, sin], axis=1), (1, LANES // HEAD_DIM))
    return cos, sin


def _split_bf16(w):
    hi = w.astype(BF16)
    return hi, (w - hi.astype(F32)).astype(BF16)


def kernel(x_prompt, x_sample, cache_k, cache_v, state_pool, norm_mix_g, w_in, q_norm_g, k_norm_g,
           attn_sinks, w_attn_branch, pool_mix_w, pool_scale, w_pool_branch, w_out, norm_ffn_g,
           w_route_group, b_route_group, w_route_expert, b_route_expert, w_expert_gate,
           w_expert_up, w_expert_down):
    batch, seq, d = x_prompt.shape
    dec_batch, dec_seq, _ = x_sample.shape
    past_len = 16384
    assert x_prompt.shape == (4, 4096, D_MODEL) and x_sample.shape == (128, 4, D_MODEL)
    assert w_in.shape[0] == 1, "single layer"
    n_p, n_s = batch * seq, dec_batch * dec_seq

    g_mix = norm_mix_g[0][None, :]
    w_in_b = w_in[0].astype(BF16)
    gq = jnp.tile(q_norm_g[0], LANES // HEAD_DIM)[None, :]
    gk = jnp.tile(k_norm_g[0], LANES // HEAD_DIM)[None, :]
    sinks = attn_sinks[0]
    mix_b = pool_mix_w[0].astype(BF16)
    ps = pool_scale[0][None, :]
    wa_b = w_attn_branch[0].astype(BF16)
    wp_b = w_pool_branch[0].astype(BF16)
    wo_b = w_out[0].astype(BF16)
    gf = norm_ffn_g[0][None, :]
    wr = jnp.zeros((D_MODEL, LANES), F32)
    wr = wr.at[:, :N_EXPERT_GROUPS].set(w_route_group[0])
    wr = wr.at[:, EXPERT_LANE0:EXPERT_LANE0 + N_EXPERTS].set(w_route_expert[0])
    wr_hi, wr_lo = _split_bf16(wr)
    br = jnp.zeros((1, LANES), F32)
    br = br.at[0, :N_EXPERT_GROUPS].set(b_route_group[0])
    br = br.at[0, EXPERT_LANE0:EXPERT_LANE0 + N_EXPERTS].set(b_route_expert[0])

    cos_p, sin_p = _rope_tables(jnp.arange(seq, dtype=jnp.int32))
    pos_s = past_len + (jnp.arange(n_s, dtype=jnp.int32) % dec_seq)
    cos_s, sin_s = _rope_tables(pos_s)

    dense_consts = (mix_b, ps, wa_b, wp_b, wo_b, gf, wr_hi, jnp.concatenate([wr_hi, wr_lo], axis=1), br)
    assert n_s == ROW_BLOCK == N_TILES * TILE_SAMPLE and n_p == N_TILES * TILE_PROMPT
    ridx = jnp.arange(ROW_BLOCK, dtype=jnp.int32)
    lower = ridx[:, None] > ridx[None, :]
    seg_of = ridx // TILE_SAMPLE
    ltri_p = lower.astype(BF16)
    ltri_s = (lower & (seg_of[:, None] == seg_of[None, :])).astype(BF16)
    ssel_p = (jnp.arange(8, dtype=jnp.int32)[:, None] == 0) & (ridx[None, :] >= 0)
    ssel_s = jnp.arange(8, dtype=jnp.int32)[:, None] == seg_of[None, :]

    xp = x_prompt.reshape(n_p, d)
    q, k, v, kk, vv, u, sga, sgp = _inproj(xp, g_mix, w_in_b, cos_p, sin_p, gq, gk)
    attn, pd, wg_b, wu_b, wd_b = _mixer_prompt(sinks, q, kk, vv, u, w_expert_gate[0], w_expert_up[0],
                                               w_expert_down[0], batch, seq)
    x1_p, hn_p, rt_p, cnt_p, _ = _dense(xp, attn, pd, sga, sgp, *dense_consts, ltri_p,
                                        ssel_p.astype(BF16), jnp.zeros((ROW_BLOCK, LANES), F32))
    last_window = lambda a: a.reshape(batch, seq, KV_W)[:, -WINDOW:].reshape(
        1, batch, WINDOW, N_KV_HEADS, HEAD_DIM)
    new_k_p = last_window(k)
    new_v_p = last_window(v)
    new_u_p = u.reshape(batch, seq, d)[:, -POOL_STATE_LEN:][None]

    xs = x_sample.reshape(n_s, d)
    q, k, v, kk, vv, u, sga, sgp = _inproj(xs, g_mix, w_in_b, cos_s, sin_s, gq, gk)
    ck = cache_k[0].reshape(dec_batch, WINDOW, KV_W)
    cv = cache_v[0].reshape(dec_batch, WINDOW, KV_W)
    attn, pd, nk_s, nv_s, nu_s = _mixer_sample(sinks, q.astype(F32), k, v, u, ck, cv, state_pool)
    cnt_tiles_p = cnt_p[TILE_BLOCKS - 1::TILE_BLOCKS, 0, :]
    crow = jnp.repeat(cnt_tiles_p, TILE_SAMPLE, axis=0)
    x1_s, hn_s, rt_s, _, seg_s = _dense(xs, attn, pd, sga, sgp, *dense_consts, ltri_s,
                                        ssel_s.astype(BF16), crow)
    new_k_s = nk_s.reshape(1, dec_batch, WINDOW, N_KV_HEADS, HEAD_DIM)
    new_v_s = nv_s.reshape(1, dec_batch, WINDOW, N_KV_HEADS, HEAD_DIM)
    new_u_s = nu_s

    ex = slice(EXPERT_LANE0, EXPERT_LANE0 + N_EXPERTS)
    cnt = (cnt_tiles_p[:, ex] + seg_s[0, :N_TILES, ex]).astype(jnp.int32)
    nblk = (cnt + (MOE_ROWS - 1)) // MOE_ROWS
    blk0 = 1 + jnp.cumsum(nblk, axis=1) - nblk
    seg = MOE_ROWS * blk0
    rt = jnp.concatenate([rt_p[:, :6].reshape(N_TILES, TILE_PROMPT, 6),
                          rt_s[:, :6].reshape(N_TILES, TILE_SAMPLE, 6)], axis=1)
    expert_hit = rt[:, :, 0:2].astype(jnp.int32)[..., None] == jnp.arange(N_EXPERTS, dtype=jnp.int32)
    slot = (rt[:, :, 2:4].astype(jnp.int32)
            + jnp.sum(jnp.where(expert_hit, seg[:, None, None, :], 0), axis=-1))
    per_tile = lambda a: jnp.pad(a.reshape(N_TILES, 2 * TILE_TOKENS),
                                 ((0, 0), (0, TILE_LIST - 2 * TILE_TOKENS))).reshape(-1)
    slots = per_tile(slot)
    wts = per_tile(rt[:, :, 4:6])
    y_p, y_s = _experts(nblk, blk0, cnt, slots, wts, hn_p, hn_s, x1_p, x1_s, wg_b, wu_b, wd_b)
    y_prompt = y_p.reshape(batch, seq, d)
    y_sample = y_s.reshape(dec_batch, dec_seq, d)

    return (y_prompt, y_sample, new_k_p, new_v_p, new_u_p, new_k_s, new_v_s, new_u_s)
```

```python
import jax
import jax.numpy as jnp
from jax import lax
from jax.experimental import pallas as pl
from jax.experimental.pallas import tpu as pltpu

F32 = jnp.float32
BF16 = jnp.bfloat16

D_MODEL = 1024
HEAD_DIM = 64
N_Q_HEADS = 16
N_KV_HEADS = 2
WINDOW = 128
ROPE_THETA = 10000.0
POOL_WINDOWS = (2, 4, 8, 16)
POOL_GROUP_WIDTH = D_MODEL // len(POOL_WINDOWS)
POOL_STATE_LEN = max(POOL_WINDOWS) - 1
N_EXPERT_GROUPS = 4
EXPERTS_PER_GROUP = 8
N_EXPERTS = N_EXPERT_GROUPS * EXPERTS_PER_GROUP
D_EXPERT = 512
RMS_EPS = 1e-6
Q_W = N_Q_HEADS * HEAD_DIM
KV_W = N_KV_HEADS * HEAD_DIM
OFF_K = Q_W
OFF_V = OFF_K + KV_W
OFF_U = OFF_V + KV_W
OFF_GA = OFF_U + D_MODEL
OFF_GP = OFF_GA + D_MODEL
IN_W = OFF_GP + D_MODEL

LANES = 128
ROW_BLOCK = 512
EXPERT_LANE0 = 32
VMEM_LIMIT = 56 * 1024 * 1024
EXPERTS_VMEM_LIMIT = 60 * 1024 * 1024
NEG_INF = float("-inf")

N_TILES = 4
TILE_BLOCKS = 8
TILE_PROMPT = TILE_BLOCKS * ROW_BLOCK
TILE_SAMPLE = 128
TILE_TOKENS = TILE_PROMPT + TILE_SAMPLE
MOE_ROWS = 304
DUMMY_ROWS = 8
SCATTER_BATCH = 8
TILE_LIST = -(-(2 * TILE_TOKENS + 1) // 1024) * 1024
LIST_CAP = 2 * TILE_TOKENS + N_EXPERTS * (MOE_ROWS - 1) + 2 * MOE_ROWS + 8


def _rms(x, g):
    return x * lax.rsqrt(jnp.mean(x * x, axis=-1, keepdims=True) + RMS_EPS) * g


def _bdot(a, b):
    return jnp.dot(a, b, preferred_element_type=F32)


def _inproj_body(x_ref, g_ref, w_ref, cos_ref, sin_ref, gq_ref, gk_ref,
                 q_ref, k_ref, v_ref, kk_ref, vv_ref, u_ref, sga_ref, sgp_ref):
    rows = x_ref.shape[0]
    hb = _rms(x_ref[...], g_ref[...]).astype(BF16)
    cos = cos_ref[...]
    sin = sin_ref[...]
    lane = lax.broadcasted_iota(jnp.int32, (rows, LANES), 1)
    lo = lane < HEAD_DIM
    first_half = (lane % HEAD_DIM) < (HEAD_DIM // 2)

    def head_norm_rope(zc, gain):
        sq = zc * zc
        ss_lo = jnp.sum(jnp.where(lo, sq, 0.0), axis=-1, keepdims=True)
        ss_hi = jnp.sum(jnp.where(lo, 0.0, sq), axis=-1, keepdims=True)
        r = lax.rsqrt(jnp.where(lo, ss_lo, ss_hi) * (1.0 / HEAD_DIM) + RMS_EPS)
        y = zc * r * gain
        partner = jnp.where(first_half, pltpu.roll(y, LANES - HEAD_DIM // 2, 1),
                            pltpu.roll(y, HEAD_DIM // 2, 1))
        return y * cos + partner * sin

    gq = gq_ref[...]
    for j in range(Q_W // 256):
        z = _bdot(hb, w_ref[:, 256 * j:256 * (j + 1)])
        for c in range(2):
            qn = head_norm_rope(z[:, LANES * c:LANES * (c + 1)], gq) * (HEAD_DIM ** -0.5)
            q_ref[:, 256 * j + LANES * c:256 * j + LANES * (c + 1)] = qn.astype(BF16)

    z = _bdot(hb, w_ref[:, OFF_K:OFF_U])
    kn = head_norm_rope(z[:, :KV_W], gk_ref[...])
    vr = z[:, KV_W:]
    k_ref[...] = kn
    v_ref[...] = vr
    kr = pltpu.roll(kn, HEAD_DIM, 1)
    vrr = pltpu.roll(vr, HEAD_DIM, 1)
    kk_ref[:, :LANES] = jnp.where(lo, kn, kr).astype(BF16)
    kk_ref[:, LANES:] = jnp.where(lo, kr, kn).astype(BF16)
    vv_ref[:, :LANES] = jnp.where(lo, vr, vrr).astype(BF16)
    vv_ref[:, LANES:] = jnp.where(lo, vrr, vr).astype(BF16)

    for j in range(D_MODEL // 256):
        u_ref[:, 256 * j:256 * (j + 1)] = _bdot(hb, w_ref[:, OFF_U + 256 * j:OFF_U + 256 * (j + 1)])
        sga_ref[:, 256 * j:256 * (j + 1)] = jax.nn.sigmoid(
            _bdot(hb, w_ref[:, OFF_GA + 256 * j:OFF_GA + 256 * (j + 1)]))
        sgp_ref[:, 256 * j:256 * (j + 1)] = jax.nn.sigmoid(
            _bdot(hb, w_ref[:, OFF_GP + 256 * j:OFF_GP + 256 * (j + 1)]))


def _inproj(x, g, w_in_b, cos, sin, gq, gk):
    n = x.shape[0]
    nb = n // ROW_BLOCK
    ncos = cos.shape[0] // ROW_BLOCK
    row = lambda w: pl.BlockSpec((ROW_BLOCK, w), lambda i: (i, 0))
    full = lambda a: pl.BlockSpec(a.shape, lambda i: (0,) * a.ndim)
    tab = pl.BlockSpec((ROW_BLOCK, LANES), lambda i: (i % ncos, 0))
    out_shapes = (
        jax.ShapeDtypeStruct((n, Q_W), BF16),
        jax.ShapeDtypeStruct((n, KV_W), F32),
        jax.ShapeDtypeStruct((n, KV_W), F32),
        jax.ShapeDtypeStruct((n, 2 * LANES), BF16),
        jax.ShapeDtypeStruct((n, 2 * LANES), BF16),
        jax.ShapeDtypeStruct((n, D_MODEL), F32),
        jax.ShapeDtypeStruct((n, D_MODEL), F32),
        jax.ShapeDtypeStruct((n, D_MODEL), F32),
    )
    return pl.pallas_call(
        _inproj_body,
        grid=(nb,),
        in_specs=[row(D_MODEL), full(g), full(w_in_b), tab, tab, full(gq), full(gk)],
        out_specs=tuple(row(s.shape[1]) for s in out_shapes),
        out_shape=out_shapes,
        compiler_params=pltpu.CompilerParams(dimension_semantics=("arbitrary",),
                                             vmem_limit_bytes=VMEM_LIMIT),
        name="inproj",
    )(x, g, w_in_b, cos, sin, gq, gk)


def _attend(q2, kg, vg, cur_valid, sink_col, prev_dead):
    s = lax.dot_general(q2, kg, (((1,), (1,)), ((), ())), preferred_element_type=F32)
    s_prev = s[:, :WINDOW]
    if prev_dead is not None:
        s_prev = jnp.where(prev_dead, NEG_INF, s_prev)
    sc = jnp.where(cur_valid, s[:, WINDOW:], s_prev)
    m = jnp.maximum(jnp.max(sc, axis=-1, keepdims=True), sink_col)
    p = jnp.exp(sc - m)
    den = jnp.sum(p, axis=-1, keepdims=True) + jnp.exp(sink_col - m)
    pn = p * (1.0 / den)
    p2 = jnp.concatenate([jnp.where(cur_valid, 0.0, pn), jnp.where(cur_valid, pn, 0.0)], axis=1)
    return _bdot(p2.astype(BF16), vg)


POOL_PAD_ROWS = 8


def _mixer_prompt_body(sink_ref, q_ref, kkc_ref, kkp_ref, vvc_ref, vvp_ref, uc_ref, up_ref,
                       wg_ref, wu_ref, wd_ref,
                       attn_ref, pd_ref, wgb_ref, wub_ref, wdb_ref, ext_ref, s_a, s_b):
    wgb_ref[...] = wg_ref[...].astype(BF16)
    wub_ref[...] = wu_ref[...].astype(BF16)
    wdb_ref[...] = wd_ref[...].astype(BF16)
    i = pl.program_id(1)
    first = i == 0
    m2 = 2 * WINDOW
    row = lax.broadcasted_iota(jnp.int32, (m2, LANES), 0)
    col = lax.broadcasted_iota(jnp.int32, (m2, LANES), 1)
    cur_valid = (row % WINDOW) >= col
    upper = lax.broadcasted_iota(jnp.int32, (m2, 1), 0) < WINDOW
    lo = lax.broadcasted_iota(jnp.int32, (WINDOW, LANES), 1) < HEAD_DIM

    for j in range(ROW_BLOCK // WINDOW):
        r0 = WINDOW * j
        if j == 0:
            kprev, vprev = kkp_ref[...], vvp_ref[...]
            prev_dead = first
        else:
            kprev, vprev = kkc_ref[r0 - WINDOW:r0, :], vvc_ref[r0 - WINDOW:r0, :]
            prev_dead = None
        kband = jnp.concatenate([kprev, kkc_ref[r0:r0 + WINDOW, :]], axis=0)
        vband = jnp.concatenate([vprev, vvc_ref[r0:r0 + WINDOW, :]], axis=0)
        for g in range(N_KV_HEADS):
            kg = kband[:, LANES * g:LANES * (g + 1)]
            vg = vband[:, LANES * g:LANES * (g + 1)]
            for c in range(4):
                cc = 4 * g + c
                qc = q_ref[r0:r0 + WINDOW, LANES * cc:LANES * (cc + 1)]
                zero = jnp.zeros_like(qc)
                q2 = jnp.concatenate([jnp.where(lo, qc, zero), jnp.where(lo, zero, qc)], axis=0)
                sink_col = jnp.where(upper, sink_ref[2 * cc], sink_ref[2 * cc + 1])
                o = _attend(q2, kg, vg, cur_valid, sink_col, prev_dead)
                attn_ref[r0:r0 + WINDOW, LANES * cc:LANES * (cc + 1)] = jnp.where(
                    lo, o[:WINDOW], o[WINDOW:]).astype(BF16)

    pad, top = POOL_PAD_ROWS, POOL_PAD_ROWS + 16
    end = top + ROW_BLOCK
    for ref in (ext_ref, s_a, s_b):
        ref[0:pad, :] = jnp.zeros((pad, D_MODEL), F32)
    ext_ref[pad:top, :] = jnp.where(first, 0.0, up_ref[...])
    ext_ref[top:, :] = uc_ref[...]
    pos = i * ROW_BLOCK + lax.broadcasted_iota(jnp.int32, (ROW_BLOCK, 1), 0)
    src = ext_ref
    for g, w in enumerate(POOL_WINDOWS):
        c0 = POOL_GROUP_WIDTH * g
        dst = (s_a, s_b)[g % 2]
        shift = w // 2
        dst[pad:end, c0:] = src[pad:end, c0:] + src[pad - shift:end - shift, c0:]
        c1 = c0 + POOL_GROUP_WIDTH
        cnt = jnp.minimum(w, pos + 1).astype(F32)
        pd_ref[:, c0:c1] = (dst[top:end, c0:c1] / cnt - uc_ref[:, c0:c1]).astype(BF16)
        src = dst


def _mixer_prompt(sinks, q, kk, vv, u, w_gate, w_up, w_down, batch, seq):
    nb = seq // ROW_BLOCK
    assert batch * nb == N_EXPERTS, "one expert's weights are cast per grid step"
    sub = ROW_BLOCK // WINDOW
    cur = lambda w: pl.BlockSpec((ROW_BLOCK, w), lambda b, i: (b * nb + i, 0))
    wspec = lambda a: pl.BlockSpec((1,) + a.shape[1:], lambda b, i: (b * nb + i, 0, 0))
    weights = (w_gate, w_up, w_down)
    prev_kv = pl.BlockSpec((WINDOW, 2 * LANES),
                           lambda b, i: (jnp.maximum((b * nb + i) * sub - 1, 0), 0))
    prev_u = pl.BlockSpec((16, D_MODEL),
                          lambda b, i: (jnp.maximum((b * nb + i) * (ROW_BLOCK // 16) - 1, 0), 0))
    n = batch * seq
    return pl.pallas_call(
        _mixer_prompt_body,
        grid=(batch, nb),
        in_specs=[pl.BlockSpec(memory_space=pltpu.SMEM), cur(Q_W), cur(2 * LANES), prev_kv,
                  cur(2 * LANES), prev_kv, cur(D_MODEL), prev_u] + [wspec(a) for a in weights],
        out_specs=(cur(Q_W), cur(D_MODEL)) + tuple(wspec(a) for a in weights),
        out_shape=(jax.ShapeDtypeStruct((n, Q_W), BF16), jax.ShapeDtypeStruct((n, D_MODEL), BF16))
        + tuple(jax.ShapeDtypeStruct(a.shape, BF16) for a in weights),
        scratch_shapes=[pltpu.VMEM((POOL_PAD_ROWS + 16 + ROW_BLOCK, D_MODEL), F32)] * 3,
        compiler_params=pltpu.CompilerParams(dimension_semantics=("arbitrary", "arbitrary"),
                                             vmem_limit_bytes=VMEM_LIMIT),
        name="mixer_prompt",
    )(sinks, q, kk, kk, vv, vv, u, u, *weights)


SEQ_PAIR_ROWS = 8
SAMPLE_PAIRS_PER_STEP = 8


def _mixer_sample_body(sink_ref, q_ref, k_ref, v_ref, u_ref, ck_ref, cv_ref, st_ref,
                       attn_ref, pd_ref, nk_ref, nv_ref, nu_ref, ext_ref):
    m = SEQ_PAIR_ROWS
    half = m // 2
    hist = POOL_STATE_LEN
    ext_ref[16:, :] = jnp.zeros((ext_ref.shape[0] - 16, D_MODEL), F32)
    row8 = lax.broadcasted_iota(jnp.int32, (m, LANES), 0)
    n_chunks = Q_W // LANES
    mq = n_chunks * m
    row1 = lax.broadcasted_iota(jnp.int32, (m, 1), 0)
    row = lax.broadcasted_iota(jnp.int32, (mq, LANES), 0)
    col = lax.broadcasted_iota(jnp.int32, (mq, LANES), 1)
    cur_valid = (row % half) >= col
    lane8 = lax.broadcasted_iota(jnp.int32, (m, LANES), 1)
    lo8 = lane8 < HEAD_DIM
    top8 = lax.broadcasted_iota(jnp.int32, (m, LANES), 0) < half
    tail = jnp.zeros((WINDOW - m, LANES), F32)
    top = row1 < half
    sink_col = jnp.concatenate([jnp.where(top, sink_ref[2 * cc], sink_ref[2 * cc + 1])
                                for cc in range(n_chunks)], axis=0)

    def pair(p, carry):
        r0 = pl.multiple_of(p * m, m)
        q8 = q_ref[pl.ds(r0, m), :]
        k8 = k_ref[pl.ds(r0, m), :]
        v8 = v_ref[pl.ds(r0, m), :]
        u8 = u_ref[pl.ds(r0, m), :]
        attn_parts = []
        pd_parts = []
        for s in range(2):
            shift = lambda a: a if s == 0 else pltpu.roll(a, half, 0)
            seq = 2 * p + s
            ck, cv = ck_ref[seq], cv_ref[seq]
            ks, vs = shift(k8), shift(v8)
            kb = jnp.concatenate([ck, ks, tail], axis=0)
            vb = jnp.concatenate([cv, vs, tail], axis=0)
            for cache, new, out in ((ck, ks, nk_ref), (cv, vs, nv_ref)):
                up = pltpu.roll(cache, WINDOW - half, 0)
                out[seq, 0:WINDOW - m, :] = up[0:WINDOW - m]
                out[seq, WINDOW - m:WINDOW, :] = jnp.where(row8 < half, up[WINDOW - m:],
                                                           pltpu.roll(new, half, 0))
            qs = shift(q8)
            qr = pltpu.roll(qs, half, 0)
            q2 = []
            for cc in range(n_chunks):
                qa = qs[:, LANES * cc:LANES * (cc + 1)]
                qb = qr[:, LANES * cc:LANES * (cc + 1)]
                if cc < n_chunks // N_KV_HEADS:
                    first = jnp.where(lo8, qa, 0.0)
                    second = jnp.where(lo8, pltpu.roll(qb, HEAD_DIM, 1), 0.0)
                else:
                    first = jnp.where(lo8, 0.0, pltpu.roll(qa, HEAD_DIM, 1))
                    second = jnp.where(lo8, 0.0, qb)
                q2.append(jnp.where(top8, first, second))
            q2 = jnp.concatenate(q2, axis=0).astype(BF16)
            o = _attend(q2, kb.astype(BF16), vb.astype(BF16), cur_valid, sink_col, None)
            chunks = []
            for cc in range(n_chunks):
                oc = o[m * cc:m * (cc + 1)]
                if cc < n_chunks // N_KV_HEADS:
                    chunks.append(jnp.where(lo8, oc, pltpu.roll(pltpu.roll(oc, half, 0), HEAD_DIM, 1)))
                else:
                    chunks.append(jnp.where(lo8, pltpu.roll(oc, HEAD_DIM, 1), pltpu.roll(oc, half, 0)))
            attn_parts.append(jnp.concatenate(chunks, axis=1))

            ext_ref[0:hist, :] = st_ref[0, seq]
            ext_ref[hist:hist + half, :] = shift(u8)[0:half]
            nu_ref[0, seq] = ext_ref[half:hist + half, :]
            cols = []
            for g, w in enumerate(POOL_WINDOWS):
                c0, c1 = POOL_GROUP_WIDTH * g, POOL_GROUP_WIDTH * (g + 1)
                acc = ext_ref[hist:hist + m, c0:c1]
                for k in range(1, w):
                    acc = acc + ext_ref[hist - k:hist - k + m, c0:c1]
                cols.append(acc / float(w) - ext_ref[hist:hist + m, c0:c1])
            pd_parts.append(jnp.concatenate(cols, axis=1))

        attn_ref[pl.ds(r0, m), :] = jnp.where(top, attn_parts[0], pltpu.roll(attn_parts[1], half, 0))
        pd_ref[pl.ds(r0, m), :] = jnp.where(top, pd_parts[0], pltpu.roll(pd_parts[1], half, 0))
        return carry

    lax.fori_loop(0, SAMPLE_PAIRS_PER_STEP, pair, 0)


def _mixer_sample(sinks, q, k, v, u, cache_k, cache_v, state):
    n = q.shape[0]
    m = SEQ_PAIR_ROWS * SAMPLE_PAIRS_PER_STEP
    rows = lambda w: pl.BlockSpec((m, w), lambda i: (i, 0))
    seqs = lambda a: pl.BlockSpec((2 * SAMPLE_PAIRS_PER_STEP,) + a.shape[1:], lambda i: (i, 0, 0))
    st_spec = pl.BlockSpec((1, 2 * SAMPLE_PAIRS_PER_STEP) + state.shape[2:], lambda i: (0, i, 0, 0))
    like = lambda a: jax.ShapeDtypeStruct(a.shape, F32)
    return pl.pallas_call(
        _mixer_sample_body,
        grid=(n // m,),
        in_specs=[pl.BlockSpec(memory_space=pltpu.SMEM), rows(Q_W), rows(KV_W), rows(KV_W),
                  rows(D_MODEL), seqs(cache_k), seqs(cache_v), st_spec],
        out_specs=(rows(Q_W), rows(D_MODEL), seqs(cache_k), seqs(cache_v), st_spec),
        out_shape=(jax.ShapeDtypeStruct((n, Q_W), F32), jax.ShapeDtypeStruct((n, D_MODEL), F32),
                   like(cache_k), like(cache_v), like(state)),
        scratch_shapes=[pltpu.VMEM((16 + SEQ_PAIR_ROWS, D_MODEL), F32)],
        compiler_params=pltpu.CompilerParams(dimension_semantics=("arbitrary",),
                                             vmem_limit_bytes=VMEM_LIMIT),
        name="mixer_sample",
    )(sinks, q, k, v, u, cache_k, cache_v, state)


def _dense_body(x_ref, attn_ref, pd_ref, sga_ref, sgp_ref, mix_ref, ps_ref, wa_ref, wp_ref,
                wo_ref, gf_ref, wrh_ref, wrl_ref, br_ref, ltri_ref, ssel_ref, crow_ref,
                x1_ref, hn_ref, rt_ref, cnt_ref, seg_ref, carry_ref):
    rows = x_ref.shape[0]
    step = pl.program_id(0)

    @pl.when(step % TILE_BLOCKS == 0)
    def _():
        carry_ref[...] = jnp.zeros_like(carry_ref)

    pd = pd_ref[...].astype(BF16)
    pooled = []
    for g in range(len(POOL_WINDOWS)):
        c0, c1 = POOL_GROUP_WIDTH * g, POOL_GROUP_WIDTH * (g + 1)
        pooled.append((_bdot(pd[:, c0:c1], mix_ref[g]) * ps_ref[:, c0:c1]).astype(BF16))
    pooled = jnp.concatenate(pooled, axis=1)
    merged = (sga_ref[...] * _bdot(attn_ref[...].astype(BF16), wa_ref[...])
              + sgp_ref[...] * _bdot(pooled, wp_ref[...]))
    x1 = x_ref[...] + _bdot(merged.astype(BF16), wo_ref[...])
    x1_ref[...] = x1
    hn = _rms(x1, gf_ref[...])
    hn_ref[...] = hn
    hi = hn.astype(BF16)
    lo = (hn - hi.astype(F32)).astype(BF16)
    both = _bdot(hi, wrl_ref[...])
    logits = both[:, :LANES] + both[:, LANES:] + _bdot(lo, wrh_ref[...]) + br_ref[...]

    lane = lax.broadcasted_iota(jnp.int32, (rows, LANES), 1)
    big = jnp.int32(LANES)
    gl = jnp.where(lane < N_EXPERT_GROUPS, logits, NEG_INF)
    gmax = jnp.max(gl, axis=-1, keepdims=True)
    gidx = jnp.min(jnp.where(gl == gmax, lane, big), axis=-1, keepdims=True)
    g_w = 1.0 / jnp.sum(jnp.exp(gl - gmax), axis=-1, keepdims=True)
    e0 = EXPERT_LANE0 + gidx * EXPERTS_PER_GROUP
    el = jnp.where((lane >= e0) & (lane < e0 + EXPERTS_PER_GROUP), logits, NEG_INF)
    l1 = jnp.max(el, axis=-1, keepdims=True)
    i1 = jnp.min(jnp.where(el == l1, lane, big), axis=-1, keepdims=True)
    el2 = jnp.where(lane == i1, NEG_INF, el)
    l2 = jnp.max(el2, axis=-1, keepdims=True)
    i2 = jnp.min(jnp.where(el2 == l2, lane, big), axis=-1, keepdims=True)
    e = jnp.exp(l2 - l1)
    w1 = 1.0 / (1.0 + e)
    w2 = e * w1

    sel = (lane == i1) | (lane == i2)
    onehot = jnp.where(sel, 1.0, 0.0).astype(BF16)
    rank = _bdot(ltri_ref[...], onehot) + carry_ref[0:1, :] + crow_ref[...]
    segsum = _bdot(ssel_ref[...], onehot)
    seg_ref[0] = segsum
    carry_ref[...] = carry_ref[...] + segsum[0:1, :]
    cnt_ref[0] = carry_ref[...]
    r1 = jnp.sum(jnp.where(lane == i1, rank, 0.0), axis=-1, keepdims=True)
    r2 = jnp.sum(jnp.where(lane == i2, rank, 0.0), axis=-1, keepdims=True)
    cols = ((i1 - EXPERT_LANE0).astype(F32), (i2 - EXPERT_LANE0).astype(F32), r1, r2,
            g_w * w1, g_w * w2)
    tile = jnp.zeros((rows, LANES), F32)
    for c, val in enumerate(cols):
        tile = jnp.where(lane == c, val, tile)
    rt_ref[...] = tile


def _dense(x, attn, pd, sga, sgp, mix_b, pool_scale, wa_b, wp_b, wo_b, gf, wr_hi, wr_lo, br,
           ltri, ssel, crow):
    n = x.shape[0]
    nb = n // ROW_BLOCK
    row = lambda w: pl.BlockSpec((ROW_BLOCK, w), lambda i: (i, 0))
    full = lambda a: pl.BlockSpec(a.shape, lambda i: (0,) * a.ndim)
    stat = pl.BlockSpec((1, 8, LANES), lambda i: (i, 0, 0))
    consts = (mix_b, pool_scale, wa_b, wp_b, wo_b, gf, wr_hi, wr_lo, br, ltri, ssel, crow)
    return pl.pallas_call(
        _dense_body,
        grid=(nb,),
        in_specs=[row(D_MODEL)] * 5 + [full(a) for a in consts],
        out_specs=(row(D_MODEL), row(D_MODEL), row(LANES), stat, stat),
        out_shape=(jax.ShapeDtypeStruct((n, D_MODEL), F32),
                   jax.ShapeDtypeStruct((n, D_MODEL), F32),
                   jax.ShapeDtypeStruct((n, LANES), F32),
                   jax.ShapeDtypeStruct((nb, 8, LANES), F32),
                   jax.ShapeDtypeStruct((nb, 8, LANES), F32)),
        scratch_shapes=[pltpu.VMEM((8, LANES), F32)],
        compiler_params=pltpu.CompilerParams(dimension_semantics=("arbitrary",),
                                             vmem_limit_bytes=VMEM_LIMIT),
        name="dense",
    )(x, attn, pd, sga, sgp, *consts)


def _experts_body(nblk_ref, blk0_ref, cnt_ref, slots_ref, wts_ref, hn_p, hn_s, x1_p, x1_s,
                  wg_ref, wu_ref, wd_ref, y_p, y_s,
                  hn_t, acc, xg0, xg1, yb0, yb1, tok, sem_in, sem_out):
    tau = pl.program_id(0)
    e = pl.program_id(1)
    t_rows = TILE_TOKENS
    dummy = 2 * t_rows
    p_rows = pl.ds(pl.multiple_of(tau * TILE_PROMPT, TILE_PROMPT), TILE_PROMPT)
    s_rows = pl.ds(pl.multiple_of(tau * TILE_SAMPLE, TILE_SAMPLE), TILE_SAMPLE)
    tile_p = pl.ds(0, TILE_PROMPT)
    tile_s = pl.ds(TILE_PROMPT, TILE_SAMPLE)

    def in_copies():
        return [pltpu.make_async_copy(hn_p.at[p_rows], hn_t.at[tile_p], sem_in.at[0]),
                pltpu.make_async_copy(hn_s.at[s_rows], hn_t.at[tile_s], sem_in.at[1]),
                pltpu.make_async_copy(x1_p.at[p_rows], acc.at[tile_p], sem_in.at[2]),
                pltpu.make_async_copy(x1_s.at[s_rows], acc.at[tile_s], sem_in.at[3])]

    def out_copies():
        return [pltpu.make_async_copy(acc.at[tile_p], y_p.at[p_rows], sem_out.at[0]),
                pltpu.make_async_copy(acc.at[tile_s], y_s.at[s_rows], sem_out.at[1])]

    xbufs = (xg0, xg1)
    ybufs = (yb0, yb1)
    end_block = blk0_ref[tau, N_EXPERTS - 1] + nblk_ref[tau, N_EXPERTS - 1]
    end_slot = end_block * MOE_ROWS

    def gather_block(b, xdst):
        base = b * MOE_ROWS
        for j in range(MOE_ROWS):
            t = lax.shift_right_logical(tok[base + j], 1)
            xdst[j // 8, pl.ds(j % 8, 1), :] = hn_t[pl.ds(t, 1), :]

    def scatter_block(b, ysrc):
        base = b * MOE_ROWS
        for j0 in range(0, MOE_ROWS, SCATTER_BATCH):
            ents = [tok[base + j0 + i] for i in range(SCATTER_BATCH)]
            rows = [lax.shift_right_logical(en, 1) for en in ents]
            vals = [acc[pl.ds(rows[i], 1), :]
                    + wts_ref[ents[i]] * ysrc[(j0 + i) // 8, pl.ds((j0 + i) % 8, 1), :]
                    for i in range(SCATTER_BATCH)]
            for i in range(SCATTER_BATCH):
                acc[pl.ds(rows[i], 1), :] = vals[i]

    @pl.when(e == 0)
    def _load_tile():
        for cp in in_copies():
            cp.start()

        hn_t[t_rows:, :] = jnp.zeros((DUMMY_ROWS, D_MODEL), F32)
        yb0[...] = jnp.zeros_like(yb0)

        def pad_range(lo, hi):
            def fill(g, c):
                for i in range(8):
                    tok[lo + g * 8 + i] = dummy
                return c
            lax.fori_loop(0, lax.shift_right_logical(hi - lo + 7, 3), fill, 0)
        pad_range(0, MOE_ROWS)
        pad_range(end_slot, end_slot + MOE_ROWS)

        def pad_fill(ei, c):
            s0 = blk0_ref[tau, ei] * MOE_ROWS
            pad_range(s0 + cnt_ref[tau, ei], s0 + nblk_ref[tau, ei] * MOE_ROWS)
            return c
        lax.fori_loop(0, N_EXPERTS, pad_fill, 0)

        def invert(g, c):
            a0 = g * 16
            for i in range(16):
                tok[slots_ref[a0 + i]] = a0 + i
            return c
        lax.fori_loop(0, t_rows // 8, invert, 0)

        for cp in in_copies():
            cp.wait()
        acc[t_rows:, :] = jnp.zeros((DUMMY_ROWS, D_MODEL), F32)
        gather_block(1, xbufs[1])

    first_block = blk0_ref[tau, e]

    def block(jb, c):
        b = first_block + jb
        for par in range(2):
            @pl.when((b & 1) == par)
            def _():
                gather_block(b + 1, xbufs[1 - par])
                xb = xbufs[par][...].reshape(MOE_ROWS, D_MODEL).astype(BF16)
                act = (jax.nn.silu(_bdot(xb, wg_ref[0])) * _bdot(xb, wu_ref[0])).astype(BF16)
                ybufs[par][...] = _bdot(act, wd_ref[0]).reshape(MOE_ROWS // 8, 8, D_MODEL)
                scatter_block(b - 1, ybufs[1 - par])
        return c
    lax.fori_loop(0, nblk_ref[tau, e], block, 0)

    @pl.when(e == N_EXPERTS - 1)
    def _store_tile():
        last = end_block - 1
        for par in range(2):
            @pl.when((last & 1) == par)
            def _():
                scatter_block(last, ybufs[par])
        cps = out_copies()
        for cp in cps:
            cp.start()
        for cp in cps:
            cp.wait()


def _experts(nblk, blk0, cnt, slots, wts, hn_p, hn_s, x1_p, x1_s, w_gate, w_up, w_down):
    any_spec = pl.BlockSpec(memory_space=pl.ANY)
    wspec = lambda a: pl.BlockSpec((1,) + a.shape[1:], lambda t, e, *_: (e, 0, 0))
    smem = lambda a: pl.BlockSpec((a.shape[0] // N_TILES,), lambda t, e, *_: (t,),
                                  memory_space=pltpu.SMEM)
    grid_spec = pltpu.PrefetchScalarGridSpec(
        num_scalar_prefetch=3,
        grid=(N_TILES, N_EXPERTS),
        in_specs=[smem(slots), smem(wts), any_spec, any_spec, any_spec, any_spec,
                  wspec(w_gate), wspec(w_up), wspec(w_down)],
        out_specs=(any_spec, any_spec),
        scratch_shapes=[
            pltpu.VMEM((TILE_TOKENS + DUMMY_ROWS, D_MODEL), F32),
            pltpu.VMEM((TILE_TOKENS + DUMMY_ROWS, D_MODEL), F32),
            pltpu.VMEM((MOE_ROWS // 8, 8, D_MODEL), F32),
            pltpu.VMEM((MOE_ROWS // 8, 8, D_MODEL), F32),
            pltpu.VMEM((MOE_ROWS // 8, 8, D_MODEL), F32),
            pltpu.VMEM((MOE_ROWS // 8, 8, D_MODEL), F32),
            pltpu.SMEM((LIST_CAP,), jnp.int32),
            pltpu.SemaphoreType.DMA((4,)),
            pltpu.SemaphoreType.DMA((2,)),
        ],
    )
    return pl.pallas_call(
        _experts_body,
        grid_spec=grid_spec,
        out_shape=(jax.ShapeDtypeStruct(x1_p.shape, F32), jax.ShapeDtypeStruct(x1_s.shape, F32)),
        compiler_params=pltpu.CompilerParams(dimension_semantics=("arbitrary", "arbitrary"),
                                             vmem_limit_bytes=EXPERTS_VMEM_LIMIT),
        name="experts",
    )(nblk, blk0, cnt, slots, wts, hn_p, hn_s, x1_p, x1_s, w_gate, w_up, w_down)


def _rope_tables(pos):
    half = HEAD_DIM // 2
    inv = ROPE_THETA ** (-jnp.arange(half, dtype=F32) * (2.0 / HEAD_DIM))
    ang = pos.astype(F32)[:, None] * inv[None, :]
    cos = jnp.tile(jnp.cos(ang), (1, LANES // half))
    sin = jnp.sin(ang)
    sin = jnp.tile(jnp.concatenate([-sin, sin], axis=1), (1, LANES // HEAD_DIM))
    return cos, sin


def _split_bf16(w):
    hi = w.astype(BF16)
    return hi, (w - hi.astype(F32)).astype(BF16)


def kernel(x_prompt, x_sample, cache_k, cache_v, state_pool, norm_mix_g, w_in, q_norm_g, k_norm_g,
           attn_sinks, w_attn_branch, pool_mix_w, pool_scale, w_pool_branch, w_out, norm_ffn_g,
           w_route_group, b_route_group, w_route_expert, b_route_expert, w_expert_gate,
           w_expert_up, w_expert_down):
    batch, seq, d = x_prompt.shape
    dec_batch, dec_seq, _ = x_sample.shape
    past_len = 16384
    assert x_prompt.shape == (4, 4096, D_MODEL) and x_sample.shape == (128, 4, D_MODEL)
    assert w_in.shape[0] == 1, "single layer"
    n_p, n_s = batch * seq, dec_batch * dec_seq

    g_mix = norm_mix_g[0][None, :]
    w_in_b = w_in[0].astype(BF16)
    gq = jnp.tile(q_norm_g[0], LANES // HEAD_DIM)[None, :]
    gk = jnp.tile(k_norm_g[0], LANES // HEAD_DIM)[None, :]
    sinks = attn_sinks[0]
    mix_b = pool_mix_w[0].astype(BF16)
    ps = pool_scale[0][None, :]
    wa_b = w_attn_branch[0].astype(BF16)
    wp_b = w_pool_branch[0].astype(BF16)
    wo_b = w_out[0].astype(BF16)
    gf = norm_ffn_g[0][None, :]
    wr = jnp.zeros((D_MODEL, LANES), F32)
    wr = wr.at[:, :N_EXPERT_GROUPS].set(w_route_group[0])
    wr = wr.at[:, EXPERT_LANE0:EXPERT_LANE0 + N_EXPERTS].set(w_route_expert[0])
    wr_hi, wr_lo = _split_bf16(wr)
    br = jnp.zeros((1, LANES), F32)
    br = br.at[0, :N_EXPERT_GROUPS].set(b_route_group[0])
    br = br.at[0, EXPERT_LANE0:EXPERT_LANE0 + N_EXPERTS].set(b_route_expert[0])

    cos_p, sin_p = _rope_tables(jnp.arange(seq, dtype=jnp.int32))
    pos_s = past_len + (jnp.arange(n_s, dtype=jnp.int32) % dec_seq)
    cos_s, sin_s = _rope_tables(pos_s)

    dense_consts = (mix_b, ps, wa_b, wp_b, wo_b, gf, wr_hi, jnp.concatenate([wr_hi, wr_lo], axis=1), br)
    assert n_s == ROW_BLOCK == N_TILES * TILE_SAMPLE and n_p == N_TILES * TILE_PROMPT
    ridx = jnp.arange(ROW_BLOCK, dtype=jnp.int32)
    lower = ridx[:, None] > ridx[None, :]
    seg_of = ridx // TILE_SAMPLE
    ltri_p = lower.astype(BF16)
    ltri_s = (lower & (seg_of[:, None] == seg_of[None, :])).astype(BF16)
    ssel_p = (jnp.arange(8, dtype=jnp.int32)[:, None] == 0) & (ridx[None, :] >= 0)
    ssel_s = jnp.arange(8, dtype=jnp.int32)[:, None] == seg_of[None, :]

    xp = x_prompt.reshape(n_p, d)
    q, k, v, kk, vv, u, sga, sgp = _inproj(xp, g_mix, w_in_b, cos_p, sin_p, gq, gk)
    attn, pd, wg_b, wu_b, wd_b = _mixer_prompt(sinks, q, kk, vv, u, w_expert_gate[0], w_expert_up[0],
                                               w_expert_down[0], batch, seq)
    x1_p, hn_p, rt_p, cnt_p, _ = _dense(xp, attn, pd, sga, sgp, *dense_consts, ltri_p,
                                        ssel_p.astype(BF16), jnp.zeros((ROW_BLOCK, LANES), F32))
    last_window = lambda a: a.reshape(batch, seq, KV_W)[:, -WINDOW:].reshape(
        1, batch, WINDOW, N_KV_HEADS, HEAD_DIM)
    new_k_p = last_window(k)
    new_v_p = last_window(v)
    new_u_p = u.reshape(batch, seq, d)[:, -POOL_STATE_LEN:][None]

    xs = x_sample.reshape(n_s, d)
    q, k, v, kk, vv, u, sga, sgp = _inproj(xs, g_mix, w_in_b, cos_s, sin_s, gq, gk)
    ck = cache_k[0].reshape(dec_batch, WINDOW, KV_W)
    cv = cache_v[0].reshape(dec_batch, WINDOW, KV_W)
    attn, pd, nk_s, nv_s, nu_s = _mixer_sample(sinks, q.astype(F32), k, v, u, ck, cv, state_pool)
    cnt_tiles_p = cnt_p[TILE_BLOCKS - 1::TILE_BLOCKS, 0, :]
    crow = jnp.repeat(cnt_tiles_p, TILE_SAMPLE, axis=0)
    x1_s, hn_s, rt_s, _, seg_s = _dense(xs, attn, pd, sga, sgp, *dense_consts, ltri_s,
                                        ssel_s.astype(BF16), crow)
    new_k_s = nk_s.reshape(1, dec_batch, WINDOW, N_KV_HEADS, HEAD_DIM)
    new_v_s = nv_s.reshape(1, dec_batch, WINDOW, N_KV_HEADS, HEAD_DIM)
    new_u_s = nu_s

    ex = slice(EXPERT_LANE0, EXPERT_LANE0 + N_EXPERTS)
    cnt = (cnt_tiles_p[:, ex] + seg_s[0, :N_TILES, ex]).astype(jnp.int32)
    nblk = (cnt + (MOE_ROWS - 1)) // MOE_ROWS
    blk0 = 1 + jnp.cumsum(nblk, axis=1) - nblk
    seg = MOE_ROWS * blk0
    rt = jnp.concatenate([rt_p[:, :6].reshape(N_TILES, TILE_PROMPT, 6),
                          rt_s[:, :6].reshape(N_TILES, TILE_SAMPLE, 6)], axis=1)
    expert_hit = rt[:, :, 0:2].astype(jnp.int32)[..., None] == jnp.arange(N_EXPERTS, dtype=jnp.int32)
    slot = (rt[:, :, 2:4].astype(jnp.int32)
            + jnp.sum(jnp.where(expert_hit, seg[:, None, None, :], 0), axis=-1))
    per_tile = lambda a: jnp.pad(a.reshape(N_TILES, 2 * TILE_TOKENS),
                                 ((0, 0), (0, TILE_LIST - 2 * TILE_TOKENS))).reshape(-1)
    slots = per_tile(slot)
    wts = per_tile(rt[:, :, 4:6])
    y_p, y_s = _experts(nblk, blk0, cnt, slots, wts, hn_p, hn_s, x1_p, x1_s, wg_b, wu_b, wd_b)
    y_prompt = y_p.reshape(batch, seq, d)
    y_sample = y_s.reshape(dec_batch, dec_seq, d)

    return (y_prompt, y_sample, new_k_p, new_v_p, new_u_p, new_k_s, new_v_s, new_u_s)
```

```python
import jax
import jax.numpy as jnp
from jax import lax
from jax.experimental import pallas as pl
from jax.experimental.pallas import tpu as pltpu

F32 = jnp.float32
BF16 = jnp.bfloat16

D_MODEL = 1024
HEAD_DIM = 64
N_Q_HEADS = 16
N_KV_HEADS = 2
WINDOW = 128
ROPE_THETA = 10000.0
POOL_WINDOWS = (2, 4, 8, 16)
POOL_GROUP_WIDTH = D_MODEL // len(POOL_WINDOWS)
POOL_STATE_LEN = max(POOL_WINDOWS) - 1
N_EXPERT_GROUPS = 4
EXPERTS_PER_GROUP = 8
N_EXPERTS = N_EXPERT_GROUPS * EXPERTS_PER_GROUP
D_EXPERT = 512
RMS_EPS = 1e-6
Q_W = N_Q_HEADS * HEAD_DIM
KV_W = N_KV_HEADS * HEAD_DIM
OFF_K = Q_W
OFF_V = OFF_K + KV_W
OFF_U = OFF_V + KV_W
OFF_GA = OFF_U + D_MODEL
OFF_GP = OFF_GA + D_MODEL
IN_W = OFF_GP + D_MODEL

LANES = 128
ROW_BLOCK = 512
EXPERT_LANE0 = 32
VMEM_LIMIT = 56 * 1024 * 1024
EXPERTS_VMEM_LIMIT = 60 * 1024 * 1024
NEG_INF = float("-inf")

N_TILES = 4
TILE_BLOCKS = 8
TILE_PROMPT = TILE_BLOCKS * ROW_BLOCK
TILE_SAMPLE = 128
TILE_TOKENS = TILE_PROMPT + TILE_SAMPLE
MOE_ROWS = 304
DUMMY_ROWS = 8
SCATTER_BATCH = 4
TILE_LIST = -(-(2 * TILE_TOKENS + 1) // 1024) * 1024
LIST_CAP = 2 * TILE_TOKENS + N_EXPERTS * (MOE_ROWS - 1) + 2 * MOE_ROWS + 8


def _rms(x, g):
    return x * lax.rsqrt(jnp.mean(x * x, axis=-1, keepdims=True) + RMS_EPS) * g


def _bdot(a, b):
    return jnp.dot(a, b, preferred_element_type=F32)


def _inproj_body(x_ref, g_ref, w_ref, cos_ref, sin_ref, gq_ref, gk_ref,
                 q_ref, k_ref, v_ref, kk_ref, vv_ref, u_ref, sga_ref, sgp_ref):
    rows = x_ref.shape[0]
    hb = _rms(x_ref[...], g_ref[...]).astype(BF16)
    cos = cos_ref[...]
    sin = sin_ref[...]
    lane = lax.broadcasted_iota(jnp.int32, (rows, LANES), 1)
    lo = lane < HEAD_DIM
    first_half = (lane % HEAD_DIM) < (HEAD_DIM // 2)

    def head_norm_rope(zc, gain):
        sq = zc * zc
        ss_lo = jnp.sum(jnp.where(lo, sq, 0.0), axis=-1, keepdims=True)
        ss_hi = jnp.sum(jnp.where(lo, 0.0, sq), axis=-1, keepdims=True)
        r = lax.rsqrt(jnp.where(lo, ss_lo, ss_hi) * (1.0 / HEAD_DIM) + RMS_EPS)
        y = zc * r * gain
        partner = jnp.where(first_half, pltpu.roll(y, LANES - HEAD_DIM // 2, 1),
                            pltpu.roll(y, HEAD_DIM // 2, 1))
        return y * cos + partner * sin

    gq = gq_ref[...]
    for j in range(Q_W // 256):
        z = _bdot(hb, w_ref[:, 256 * j:256 * (j + 1)])
        for c in range(2):
            qn = head_norm_rope(z[:, LANES * c:LANES * (c + 1)], gq) * (HEAD_DIM ** -0.5)
            q_ref[:, 256 * j + LANES * c:256 * j + LANES * (c + 1)] = qn.astype(BF16)

    z = _bdot(hb, w_ref[:, OFF_K:OFF_U])
    kn = head_norm_rope(z[:, :KV_W], gk_ref[...])
    vr = z[:, KV_W:]
    k_ref[...] = kn
    v_ref[...] = vr
    kr = pltpu.roll(kn, HEAD_DIM, 1)
    vrr = pltpu.roll(vr, HEAD_DIM, 1)
    kk_ref[:, :LANES] = jnp.where(lo, kn, kr).astype(BF16)
    kk_ref[:, LANES:] = jnp.where(lo, kr, kn).astype(BF16)
    vv_ref[:, :LANES] = jnp.where(lo, vr, vrr).astype(BF16)
    vv_ref[:, LANES:] = jnp.where(lo, vrr, vr).astype(BF16)

    for j in range(D_MODEL // 256):
        u_ref[:, 256 * j:256 * (j + 1)] = _bdot(hb, w_ref[:, OFF_U + 256 * j:OFF_U + 256 * (j + 1)])
        sga_ref[:, 256 * j:256 * (j + 1)] = jax.nn.sigmoid(
            _bdot(hb, w_ref[:, OFF_GA + 256 * j:OFF_GA + 256 * (j + 1)]))
        sgp_ref[:, 256 * j:256 * (j + 1)] = jax.nn.sigmoid(
            _bdot(hb, w_ref[:, OFF_GP + 256 * j:OFF_GP + 256 * (j + 1)]))


def _inproj(x, g, w_in_b, cos, sin, gq, gk):
    n = x.shape[0]
    nb = n // ROW_BLOCK
    ncos = cos.shape[0] // ROW_BLOCK
    row = lambda w: pl.BlockSpec((ROW_BLOCK, w), lambda i: (i, 0))
    full = lambda a: pl.BlockSpec(a.shape, lambda i: (0,) * a.ndim)
    tab = pl.BlockSpec((ROW_BLOCK, LANES), lambda i: (i % ncos, 0))
    out_shapes = (
        jax.ShapeDtypeStruct((n, Q_W), BF16),
        jax.ShapeDtypeStruct((n, KV_W), F32),
        jax.ShapeDtypeStruct((n, KV_W), F32),
        jax.ShapeDtypeStruct((n, 2 * LANES), BF16),
        jax.ShapeDtypeStruct((n, 2 * LANES), BF16),
        jax.ShapeDtypeStruct((n, D_MODEL), F32),
        jax.ShapeDtypeStruct((n, D_MODEL), F32),
        jax.ShapeDtypeStruct((n, D_MODEL), F32),
    )
    return pl.pallas_call(
        _inproj_body,
        grid=(nb,),
        in_specs=[row(D_MODEL), full(g), full(w_in_b), tab, tab, full(gq), full(gk)],
        out_specs=tuple(row(s.shape[1]) for s in out_shapes),
        out_shape=out_shapes,
        compiler_params=pltpu.CompilerParams(dimension_semantics=("arbitrary",),
                                             vmem_limit_bytes=VMEM_LIMIT),
        name="inproj",
    )(x, g, w_in_b, cos, sin, gq, gk)


def _attend(q2, kg, vg, cur_valid, sink_col, prev_dead):
    s = lax.dot_general(q2, kg, (((1,), (1,)), ((), ())), preferred_element_type=F32)
    s_prev = s[:, :WINDOW]
    if prev_dead is not None:
        s_prev = jnp.where(prev_dead, NEG_INF, s_prev)
    sc = jnp.where(cur_valid, s[:, WINDOW:], s_prev)
    m = jnp.maximum(jnp.max(sc, axis=-1, keepdims=True), sink_col)
    p = jnp.exp(sc - m)
    den = jnp.sum(p, axis=-1, keepdims=True) + jnp.exp(sink_col - m)
    pn = p * (1.0 / den)
    p2 = jnp.concatenate([jnp.where(cur_valid, 0.0, pn), jnp.where(cur_valid, pn, 0.0)], axis=1)
    return _bdot(p2.astype(BF16), vg)


def _attend_two(q2, kab, vab, cur_valid, sink_col):
    band = 2 * WINDOW
    s = lax.dot_general(q2, kab, (((1,), (1,)), ((), ())), preferred_element_type=F32)
    is_a = lax.broadcasted_iota(jnp.int32, (q2.shape[0], 1), 0) < q2.shape[0] // 2
    own = jnp.where(is_a, s[:, :band], s[:, band:])
    sc = jnp.where(cur_valid, own[:, WINDOW:], own[:, :WINDOW])
    m = jnp.maximum(jnp.max(sc, axis=-1, keepdims=True), sink_col)
    p = jnp.exp(sc - m)
    den = jnp.sum(p, axis=-1, keepdims=True) + jnp.exp(sink_col - m)
    pn = p * (1.0 / den)
    p2 = jnp.concatenate([jnp.where(cur_valid, 0.0, pn), jnp.where(cur_valid, pn, 0.0)], axis=1)
    pab = jnp.concatenate([jnp.where(is_a, p2, 0.0), jnp.where(is_a, 0.0, p2)], axis=1)
    return _bdot(pab.astype(BF16), vab)


POOL_PAD_ROWS = 8


def _mixer_prompt_body(sink_ref, q_ref, kkc_ref, kkp_ref, vvc_ref, vvp_ref, uc_ref, up_ref,
                       wg_ref, wu_ref, wd_ref,
                       attn_ref, pd_ref, wgb_ref, wub_ref, wdb_ref, ext_ref, s_a, s_b):
    wgb_ref[...] = wg_ref[...].astype(BF16)
    wub_ref[...] = wu_ref[...].astype(BF16)
    wdb_ref[...] = wd_ref[...].astype(BF16)
    i = pl.program_id(1)
    first = i == 0
    m2 = 2 * WINDOW
    row = lax.broadcasted_iota(jnp.int32, (m2, LANES), 0)
    col = lax.broadcasted_iota(jnp.int32, (m2, LANES), 1)
    cur_valid = (row % WINDOW) >= col
    upper = lax.broadcasted_iota(jnp.int32, (m2, 1), 0) < WINDOW
    lo = lax.broadcasted_iota(jnp.int32, (WINDOW, LANES), 1) < HEAD_DIM

    for j in range(ROW_BLOCK // WINDOW):
        r0 = WINDOW * j
        if j == 0:
            kprev, vprev = kkp_ref[...], vvp_ref[...]
            prev_dead = first
        else:
            kprev, vprev = kkc_ref[r0 - WINDOW:r0, :], vvc_ref[r0 - WINDOW:r0, :]
            prev_dead = None
        kband = jnp.concatenate([kprev, kkc_ref[r0:r0 + WINDOW, :]], axis=0)
        vband = jnp.concatenate([vprev, vvc_ref[r0:r0 + WINDOW, :]], axis=0)
        for g in range(N_KV_HEADS):
            kg = kband[:, LANES * g:LANES * (g + 1)]
            vg = vband[:, LANES * g:LANES * (g + 1)]
            for c in range(4):
                cc = 4 * g + c
                qc = q_ref[r0:r0 + WINDOW, LANES * cc:LANES * (cc + 1)]
                zero = jnp.zeros_like(qc)
                q2 = jnp.concatenate([jnp.where(lo, qc, zero), jnp.where(lo, zero, qc)], axis=0)
                sink_col = jnp.where(upper, sink_ref[2 * cc], sink_ref[2 * cc + 1])
                o = _attend(q2, kg, vg, cur_valid, sink_col, prev_dead)
                attn_ref[r0:r0 + WINDOW, LANES * cc:LANES * (cc + 1)] = jnp.where(
                    lo, o[:WINDOW], o[WINDOW:]).astype(BF16)

    pad, top = POOL_PAD_ROWS, POOL_PAD_ROWS + 16
    end = top + ROW_BLOCK
    for ref in (ext_ref, s_a, s_b):
        ref[0:pad, :] = jnp.zeros((pad, D_MODEL), F32)
    ext_ref[pad:top, :] = jnp.where(first, 0.0, up_ref[...])
    ext_ref[top:, :] = uc_ref[...]
    pos = i * ROW_BLOCK + lax.broadcasted_iota(jnp.int32, (ROW_BLOCK, 1), 0)
    src = ext_ref
    for g, w in enumerate(POOL_WINDOWS):
        c0 = POOL_GROUP_WIDTH * g
        dst = (s_a, s_b)[g % 2]
        shift = w // 2
        dst[pad:end, c0:] = src[pad:end, c0:] + src[pad - shift:end - shift, c0:]
        c1 = c0 + POOL_GROUP_WIDTH
        cnt = jnp.minimum(w, pos + 1).astype(F32)
        pd_ref[:, c0:c1] = (dst[top:end, c0:c1] / cnt - uc_ref[:, c0:c1]).astype(BF16)
        src = dst


def _mixer_prompt(sinks, q, kk, vv, u, w_gate, w_up, w_down, batch, seq):
    nb = seq // ROW_BLOCK
    assert batch * nb == N_EXPERTS, "one expert's weights are cast per grid step"
    sub = ROW_BLOCK // WINDOW
    cur = lambda w: pl.BlockSpec((ROW_BLOCK, w), lambda b, i: (b * nb + i, 0))
    wspec = lambda a: pl.BlockSpec((1,) + a.shape[1:], lambda b, i: (b * nb + i, 0, 0))
    weights = (w_gate, w_up, w_down)
    prev_kv = pl.BlockSpec((WINDOW, 2 * LANES),
                           lambda b, i: (jnp.maximum((b * nb + i) * sub - 1, 0), 0))
    prev_u = pl.BlockSpec((16, D_MODEL),
                          lambda b, i: (jnp.maximum((b * nb + i) * (ROW_BLOCK // 16) - 1, 0), 0))
    n = batch * seq
    return pl.pallas_call(
        _mixer_prompt_body,
        grid=(batch, nb),
        in_specs=[pl.BlockSpec(memory_space=pltpu.SMEM), cur(Q_W), cur(2 * LANES), prev_kv,
                  cur(2 * LANES), prev_kv, cur(D_MODEL), prev_u] + [wspec(a) for a in weights],
        out_specs=(cur(Q_W), cur(D_MODEL)) + tuple(wspec(a) for a in weights),
        out_shape=(jax.ShapeDtypeStruct((n, Q_W), BF16), jax.ShapeDtypeStruct((n, D_MODEL), BF16))
        + tuple(jax.ShapeDtypeStruct(a.shape, BF16) for a in weights),
        scratch_shapes=[pltpu.VMEM((POOL_PAD_ROWS + 16 + ROW_BLOCK, D_MODEL), F32)] * 3,
        compiler_params=pltpu.CompilerParams(dimension_semantics=("arbitrary", "arbitrary"),
                                             vmem_limit_bytes=VMEM_LIMIT),
        name="mixer_prompt",
    )(sinks, q, kk, kk, vv, vv, u, u, *weights)


SEQ_PAIR_ROWS = 8
SAMPLE_PAIRS_PER_STEP = 8


def _mixer_sample_body(sink_ref, q_ref, k_ref, v_ref, u_ref, ck_ref, cv_ref, st_ref,
                       attn_ref, pd_ref, nk_ref, nv_ref, nu_ref, ext_ref):
    m = SEQ_PAIR_ROWS
    half = m // 2
    hist = POOL_STATE_LEN
    ext_ref[16:, :] = jnp.zeros((ext_ref.shape[0] - 16, D_MODEL), F32)
    row8 = lax.broadcasted_iota(jnp.int32, (m, LANES), 0)
    n_chunks = Q_W // LANES
    mq = n_chunks * m
    row1 = lax.broadcasted_iota(jnp.int32, (m, 1), 0)
    row = lax.broadcasted_iota(jnp.int32, (2 * mq, LANES), 0)
    col = lax.broadcasted_iota(jnp.int32, (2 * mq, LANES), 1)
    cur_valid = (row % half) >= col
    lane8 = lax.broadcasted_iota(jnp.int32, (m, LANES), 1)
    lo8 = lane8 < HEAD_DIM
    top8 = lax.broadcasted_iota(jnp.int32, (m, LANES), 0) < half
    tail = jnp.zeros((WINDOW - m, LANES), F32)
    top = row1 < half
    sink_col = jnp.concatenate([jnp.where(top, sink_ref[2 * cc], sink_ref[2 * cc + 1])
                                for cc in range(n_chunks)] * 2, axis=0)

    def pair(p, carry):
        r0 = pl.multiple_of(p * m, m)
        q8 = q_ref[pl.ds(r0, m), :]
        k8 = k_ref[pl.ds(r0, m), :]
        v8 = v_ref[pl.ds(r0, m), :]
        u8 = u_ref[pl.ds(r0, m), :]
        q2s, kbs, vbs = [], [], []
        pd_parts = []
        for s in range(2):
            shift = lambda a: a if s == 0 else pltpu.roll(a, half, 0)
            seq = 2 * p + s
            ck, cv = ck_ref[seq], cv_ref[seq]
            ks, vs = shift(k8), shift(v8)
            kb = jnp.concatenate([ck, ks, tail], axis=0)
            vb = jnp.concatenate([cv, vs, tail], axis=0)
            for cache, new, out in ((ck, ks, nk_ref), (cv, vs, nv_ref)):
                up = pltpu.roll(cache, WINDOW - half, 0)
                out[seq, 0:WINDOW - m, :] = up[0:WINDOW - m]
                out[seq, WINDOW - m:WINDOW, :] = jnp.where(row8 < half, up[WINDOW - m:],
                                                           pltpu.roll(new, half, 0))
            qs = shift(q8)
            qr = pltpu.roll(qs, half, 0)
            q2 = []
            for cc in range(n_chunks):
                qa = qs[:, LANES * cc:LANES * (cc + 1)]
                qb = qr[:, LANES * cc:LANES * (cc + 1)]
                if cc < n_chunks // N_KV_HEADS:
                    first = jnp.where(lo8, qa, 0.0)
                    second = jnp.where(lo8, pltpu.roll(qb, HEAD_DIM, 1), 0.0)
                else:
                    first = jnp.where(lo8, 0.0, pltpu.roll(qa, HEAD_DIM, 1))
                    second = jnp.where(lo8, 0.0, qb)
                q2.append(jnp.where(top8, first, second))
            q2s += q2
            kbs.append(kb)
            vbs.append(vb)

            ext_ref[0:hist, :] = st_ref[0, seq]
            ext_ref[hist:hist + half, :] = shift(u8)[0:half]
            nu_ref[0, seq] = ext_ref[half:hist + half, :]
            cols = []
            for g, w in enumerate(POOL_WINDOWS):
                c0, c1 = POOL_GROUP_WIDTH * g, POOL_GROUP_WIDTH * (g + 1)
                acc = ext_ref[hist:hist + m, c0:c1]
                for k in range(1, w):
                    acc = acc + ext_ref[hist - k:hist - k + m, c0:c1]
                cols.append(acc / float(w) - ext_ref[hist:hist + m, c0:c1])
            pd_parts.append(jnp.concatenate(cols, axis=1))

        o2 = _attend_two(jnp.concatenate(q2s, axis=0).astype(BF16),
                         jnp.concatenate(kbs, axis=0).astype(BF16),
                         jnp.concatenate(vbs, axis=0).astype(BF16), cur_valid, sink_col)
        attn_parts = []
        for s in range(2):
            chunks = []
            for cc in range(n_chunks):
                oc = o2[mq * s + m * cc:mq * s + m * (cc + 1)]
                if cc < n_chunks // N_KV_HEADS:
                    chunks.append(jnp.where(lo8, oc, pltpu.roll(pltpu.roll(oc, half, 0), HEAD_DIM, 1)))
                else:
                    chunks.append(jnp.where(lo8, pltpu.roll(oc, HEAD_DIM, 1), pltpu.roll(oc, half, 0)))
            attn_parts.append(jnp.concatenate(chunks, axis=1))
        attn_ref[pl.ds(r0, m), :] = jnp.where(top, attn_parts[0], pltpu.roll(attn_parts[1], half, 0))
        pd_ref[pl.ds(r0, m), :] = jnp.where(top, pd_parts[0], pltpu.roll(pd_parts[1], half, 0))
        return carry

    lax.fori_loop(0, SAMPLE_PAIRS_PER_STEP, pair, 0)


def _mixer_sample(sinks, q, k, v, u, cache_k, cache_v, state):
    n = q.shape[0]
    m = SEQ_PAIR_ROWS * SAMPLE_PAIRS_PER_STEP
    rows = lambda w: pl.BlockSpec((m, w), lambda i: (i, 0))
    seqs = lambda a: pl.BlockSpec((2 * SAMPLE_PAIRS_PER_STEP,) + a.shape[1:], lambda i: (i, 0, 0))
    st_spec = pl.BlockSpec((1, 2 * SAMPLE_PAIRS_PER_STEP) + state.shape[2:], lambda i: (0, i, 0, 0))
    like = lambda a: jax.ShapeDtypeStruct(a.shape, F32)
    return pl.pallas_call(
        _mixer_sample_body,
        grid=(n // m,),
        in_specs=[pl.BlockSpec(memory_space=pltpu.SMEM), rows(Q_W), rows(KV_W), rows(KV_W),
                  rows(D_MODEL), seqs(cache_k), seqs(cache_v), st_spec],
        out_specs=(rows(Q_W), rows(D_MODEL), seqs(cache_k), seqs(cache_v), st_spec),
        out_shape=(jax.ShapeDtypeStruct((n, Q_W), F32), jax.ShapeDtypeStruct((n, D_MODEL), F32),
                   like(cache_k), like(cache_v), like(state)),
        scratch_shapes=[pltpu.VMEM((16 + SEQ_PAIR_ROWS, D_MODEL), F32)],
        compiler_params=pltpu.CompilerParams(dimension_semantics=("arbitrary",),
                                             vmem_limit_bytes=VMEM_LIMIT),
        name="mixer_sample",
    )(sinks, q, k, v, u, cache_k, cache_v, state)


def _dense_body(x_ref, attn_ref, pd_ref, sga_ref, sgp_ref, mix_ref, ps_ref, wa_ref, wp_ref,
                wo_ref, gf_ref, wrh_ref, wrl_ref, br_ref, ltri_ref, ssel_ref, crow_ref,
                x1_ref, hn_ref, rt_ref, cnt_ref, seg_ref, carry_ref):
    rows = x_ref.shape[0]
    step = pl.program_id(0)

    @pl.when(step % TILE_BLOCKS == 0)
    def _():
        carry_ref[...] = jnp.zeros_like(carry_ref)

    pd = pd_ref[...].astype(BF16)
    pooled = []
    for g in range(len(POOL_WINDOWS)):
        c0, c1 = POOL_GROUP_WIDTH * g, POOL_GROUP_WIDTH * (g + 1)
        pooled.append((_bdot(pd[:, c0:c1], mix_ref[g]) * ps_ref[:, c0:c1]).astype(BF16))
    pooled = jnp.concatenate(pooled, axis=1)
    merged = (sga_ref[...] * _bdot(attn_ref[...].astype(BF16), wa_ref[...])
              + sgp_ref[...] * _bdot(pooled, wp_ref[...]))
    x1 = x_ref[...] + _bdot(merged.astype(BF16), wo_ref[...])
    x1_ref[...] = x1
    hn = _rms(x1, gf_ref[...])
    hn_ref[...] = hn
    hi = hn.astype(BF16)
    lo = (hn - hi.astype(F32)).astype(BF16)
    both = _bdot(hi, wrl_ref[...])
    logits = both[:, :LANES] + both[:, LANES:] + _bdot(lo, wrh_ref[...]) + br_ref[...]

    lane = lax.broadcasted_iota(jnp.int32, (rows, LANES), 1)
    big = jnp.int32(LANES)
    gl = jnp.where(lane < N_EXPERT_GROUPS, logits, NEG_INF)
    gmax = jnp.max(gl, axis=-1, keepdims=True)
    gidx = jnp.min(jnp.where(gl == gmax, lane, big), axis=-1, keepdims=True)
    g_w = 1.0 / jnp.sum(jnp.exp(gl - gmax), axis=-1, keepdims=True)
    e0 = EXPERT_LANE0 + gidx * EXPERTS_PER_GROUP
    el = jnp.where((lane >= e0) & (lane < e0 + EXPERTS_PER_GROUP), logits, NEG_INF)
    l1 = jnp.max(el, axis=-1, keepdims=True)
    i1 = jnp.min(jnp.where(el == l1, lane, big), axis=-1, keepdims=True)
    el2 = jnp.where(lane == i1, NEG_INF, el)
    l2 = jnp.max(el2, axis=-1, keepdims=True)
    i2 = jnp.min(jnp.where(el2 == l2, lane, big), axis=-1, keepdims=True)
    e = jnp.exp(l2 - l1)
    w1 = 1.0 / (1.0 + e)
    w2 = e * w1

    sel = (lane == i1) | (lane == i2)
    onehot = jnp.where(sel, 1.0, 0.0).astype(BF16)
    rank = _bdot(ltri_ref[...], onehot) + carry_ref[0:1, :] + crow_ref[...]
    segsum = _bdot(ssel_ref[...], onehot)
    seg_ref[0] = segsum
    carry_ref[...] = carry_ref[...] + segsum[0:1, :]
    cnt_ref[0] = carry_ref[...]
    r1 = jnp.sum(jnp.where(lane == i1, rank, 0.0), axis=-1, keepdims=True)
    r2 = jnp.sum(jnp.where(lane == i2, rank, 0.0), axis=-1, keepdims=True)
    cols = ((i1 - EXPERT_LANE0).astype(F32), (i2 - EXPERT_LANE0).astype(F32), r1, r2,
            g_w * w1, g_w * w2)
    tile = jnp.zeros((rows, LANES), F32)
    for c, val in enumerate(cols):
        tile = jnp.where(lane == c, val, tile)
    rt_ref[...] = tile


def _dense(x, attn, pd, sga, sgp, mix_b, pool_scale, wa_b, wp_b, wo_b, gf, wr_hi, wr_lo, br,
           ltri, ssel, crow):
    n = x.shape[0]
    nb = n // ROW_BLOCK
    row = lambda w: pl.BlockSpec((ROW_BLOCK, w), lambda i: (i, 0))
    full = lambda a: pl.BlockSpec(a.shape, lambda i: (0,) * a.ndim)
    stat = pl.BlockSpec((1, 8, LANES), lambda i: (i, 0, 0))
    consts = (mix_b, pool_scale, wa_b, wp_b, wo_b, gf, wr_hi, wr_lo, br, ltri, ssel, crow)
    return pl.pallas_call(
        _dense_body,
        grid=(nb,),
        in_specs=[row(D_MODEL)] * 5 + [full(a) for a in consts],
        out_specs=(row(D_MODEL), row(D_MODEL), row(LANES), stat, stat),
        out_shape=(jax.ShapeDtypeStruct((n, D_MODEL), F32),
                   jax.ShapeDtypeStruct((n, D_MODEL), F32),
                   jax.ShapeDtypeStruct((n, LANES), F32),
                   jax.ShapeDtypeStruct((nb, 8, LANES), F32),
                   jax.ShapeDtypeStruct((nb, 8, LANES), F32)),
        scratch_shapes=[pltpu.VMEM((8, LANES), F32)],
        compiler_params=pltpu.CompilerParams(dimension_semantics=("arbitrary",),
                                             vmem_limit_bytes=VMEM_LIMIT),
        name="dense",
    )(x, attn, pd, sga, sgp, *consts)


def _experts_body(nblk_ref, blk0_ref, cnt_ref, slots_ref, wts_ref, hn_p, hn_s, x1_p, x1_s,
                  wg_ref, wu_ref, wd_ref, y_p, y_s,
                  hn_t, acc, xg0, xg1, yb0, yb1, tok, sem_in, sem_out):
    tau = pl.program_id(0)
    e = pl.program_id(1)
    t_rows = TILE_TOKENS
    dummy = 2 * t_rows
    p_rows = pl.ds(pl.multiple_of(tau * TILE_PROMPT, TILE_PROMPT), TILE_PROMPT)
    s_rows = pl.ds(pl.multiple_of(tau * TILE_SAMPLE, TILE_SAMPLE), TILE_SAMPLE)
    tile_p = pl.ds(0, TILE_PROMPT)
    tile_s = pl.ds(TILE_PROMPT, TILE_SAMPLE)

    def in_copies():
        return [pltpu.make_async_copy(hn_p.at[p_rows], hn_t.at[tile_p], sem_in.at[0]),
                pltpu.make_async_copy(hn_s.at[s_rows], hn_t.at[tile_s], sem_in.at[1]),
                pltpu.make_async_copy(x1_p.at[p_rows], acc.at[tile_p], sem_in.at[2]),
                pltpu.make_async_copy(x1_s.at[s_rows], acc.at[tile_s], sem_in.at[3])]

    def out_copies():
        return [pltpu.make_async_copy(acc.at[tile_p], y_p.at[p_rows], sem_out.at[0]),
                pltpu.make_async_copy(acc.at[tile_s], y_s.at[s_rows], sem_out.at[1])]

    xbufs = (xg0, xg1)
    ybufs = (yb0, yb1)
    end_block = blk0_ref[tau, N_EXPERTS - 1] + nblk_ref[tau, N_EXPERTS - 1]
    end_slot = end_block * MOE_ROWS

    def gather_block(b, xdst):
        base = b * MOE_ROWS
        for j in range(MOE_ROWS):
            t = lax.shift_right_logical(tok[base + j], 1)
            xdst[j // 8, pl.ds(j % 8, 1), :] = hn_t[pl.ds(t, 1), :]

    def scatter_block(b, ysrc):
        base = b * MOE_ROWS
        for j0 in range(0, MOE_ROWS, SCATTER_BATCH):
            ents = [tok[base + j0 + i] for i in range(SCATTER_BATCH)]
            rows = [lax.shift_right_logical(en, 1) for en in ents]
            vals = [acc[pl.ds(rows[i], 1), :]
                    + wts_ref[ents[i]] * ysrc[(j0 + i) // 8, pl.ds((j0 + i) % 8, 1), :]
                    for i in range(SCATTER_BATCH)]
            for i in range(SCATTER_BATCH):
                acc[pl.ds(rows[i], 1), :] = vals[i]

    @pl.when(e == 0)
    def _load_tile():
        for cp in in_copies():
            cp.start()

        hn_t[t_rows:, :] = jnp.zeros((DUMMY_ROWS, D_MODEL), F32)
        yb0[...] = jnp.zeros_like(yb0)

        def pad_range(lo, hi):
            def fill(g, c):
                for i in range(8):
                    tok[lo + g * 8 + i] = dummy
                return c
            lax.fori_loop(0, lax.shift_right_logical(hi - lo + 7, 3), fill, 0)
        pad_range(0, MOE_ROWS)
        pad_range(end_slot, end_slot + MOE_ROWS)

        def pad_fill(ei, c):
            s0 = blk0_ref[tau, ei] * MOE_ROWS
            pad_range(s0 + cnt_ref[tau, ei], s0 + nblk_ref[tau, ei] * MOE_ROWS)
            return c
        lax.fori_loop(0, N_EXPERTS, pad_fill, 0)

        def invert(g, c):
            a0 = g * 16
            for i in range(16):
                tok[slots_ref[a0 + i]] = a0 + i
            return c
        lax.fori_loop(0, t_rows // 8, invert, 0)

        for cp in in_copies():
            cp.wait()
        acc[t_rows:, :] = jnp.zeros((DUMMY_ROWS, D_MODEL), F32)
        gather_block(1, xbufs[1])

    first_block = blk0_ref[tau, e]

    def block(jb, c):
        b = first_block + jb
        for par in range(2):
            @pl.when((b & 1) == par)
            def _():
                gather_block(b + 1, xbufs[1 - par])
                xb = xbufs[par][...].reshape(MOE_ROWS, D_MODEL).astype(BF16)
                act = (jax.nn.silu(_bdot(xb, wg_ref[0])) * _bdot(xb, wu_ref[0])).astype(BF16)
                ybufs[par][...] = _bdot(act, wd_ref[0]).reshape(MOE_ROWS // 8, 8, D_MODEL)
                scatter_block(b - 1, ybufs[1 - par])
        return c
    lax.fori_loop(0, nblk_ref[tau, e], block, 0)

    @pl.when(e == N_EXPERTS - 1)
    def _store_tile():
        last = end_block - 1
        for par in range(2):
            @pl.when((last & 1) == par)
            def _():
                scatter_block(last, ybufs[par])
        cps = out_copies()
        for cp in cps:
            cp.start()
        for cp in cps:
            cp.wait()


def _experts(nblk, blk0, cnt, slots, wts, hn_p, hn_s, x1_p, x1_s, w_gate, w_up, w_down):
    any_spec = pl.BlockSpec(memory_space=pl.ANY)
    wspec = lambda a: pl.BlockSpec((1,) + a.shape[1:], lambda t, e, *_: (e, 0, 0))
    smem = lambda a: pl.BlockSpec((a.shape[0] // N_TILES,), lambda t, e, *_: (t,),
                                  memory_space=pltpu.SMEM)
    grid_spec = pltpu.PrefetchScalarGridSpec(
        num_scalar_prefetch=3,
        grid=(N_TILES, N_EXPERTS),
        in_specs=[smem(slots), smem(wts), any_spec, any_spec, any_spec, any_spec,
                  wspec(w_gate), wspec(w_up), wspec(w_down)],
        out_specs=(any_spec, any_spec),
        scratch_shapes=[
            pltpu.VMEM((TILE_TOKENS + DUMMY_ROWS, D_MODEL), F32),
            pltpu.VMEM((TILE_TOKENS + DUMMY_ROWS, D_MODEL), F32),
            pltpu.VMEM((MOE_ROWS // 8, 8, D_MODEL), F32),
            pltpu.VMEM((MOE_ROWS // 8, 8, D_MODEL), F32),
            pltpu.VMEM((MOE_ROWS // 8, 8, D_MODEL), F32),
            pltpu.VMEM((MOE_ROWS // 8, 8, D_MODEL), F32),
            pltpu.SMEM((LIST_CAP,), jnp.int32),
            pltpu.SemaphoreType.DMA((4,)),
            pltpu.SemaphoreType.DMA((2,)),
        ],
    )
    return pl.pallas_call(
        _experts_body,
        grid_spec=grid_spec,
        out_shape=(jax.ShapeDtypeStruct(x1_p.shape, F32), jax.ShapeDtypeStruct(x1_s.shape, F32)),
        compiler_params=pltpu.CompilerParams(dimension_semantics=("arbitrary", "arbitrary"),
                                             vmem_limit_bytes=EXPERTS_VMEM_LIMIT),
        name="experts",
    )(nblk, blk0, cnt, slots, wts, hn_p, hn_s, x1_p, x1_s, w_gate, w_up, w_down)


def _rope_tables(pos):
    half = HEAD_DIM // 2
    inv = ROPE_THETA ** (-jnp.arange(half, dtype=F32) * (2.0 / HEAD_DIM))
    ang = pos.astype(F32)[:, None] * inv[None, :]
    cos = jnp.tile(jnp.cos(ang), (1, LANES // half))
    sin = jnp.sin(ang)
    sin = jnp.tile(jnp.concatenate([-sin, sin], axis=1), (1, LANES // HEAD_DIM))
    return cos, sin


def _split_bf16(w):
    hi = w.astype(BF16)
    return hi, (w - hi.astype(F32)).astype(BF16)


def kernel(x_prompt, x_sample, cache_k, cache_v, state_pool, norm_mix_g, w_in, q_norm_g, k_norm_g,
           attn_sinks, w_attn_branch, pool_mix_w, pool_scale, w_pool_branch, w_out, norm_ffn_g,
           w_route_group, b_route_group, w_route_expert, b_route_expert, w_expert_gate,
           w_expert_up, w_expert_down):
    batch, seq, d = x_prompt.shape
    dec_batch, dec_seq, _ = x_sample.shape
    past_len = 16384
    assert x_prompt.shape == (4, 4096, D_MODEL) and x_sample.shape == (128, 4, D_MODEL)
    assert w_in.shape[0] == 1, "single layer"
    n_p, n_s = batch * seq, dec_batch * dec_seq

    g_mix = norm_mix_g[0][None, :]
    w_in_b = w_in[0].astype(BF16)
    gq = jnp.tile(q_norm_g[0], LANES // HEAD_DIM)[None, :]
    gk = jnp.tile(k_norm_g[0], LANES // HEAD_DIM)[None, :]
    sinks = attn_sinks[0]
    mix_b = pool_mix_w[0].astype(BF16)
    ps = pool_scale[0][None, :]
    wa_b = w_attn_branch[0].astype(BF16)
    wp_b = w_pool_branch[0].astype(BF16)
    wo_b = w_out[0].astype(BF16)
    gf = norm_ffn_g[0][None, :]
    wr = jnp.zeros((D_MODEL, LANES), F32)
    wr = wr.at[:, :N_EXPERT_GROUPS].set(w_route_group[0])
    wr = wr.at[:, EXPERT_LANE0:EXPERT_LANE0 + N_EXPERTS].set(w_route_expert[0])
    wr_hi, wr_lo = _split_bf16(wr)
    br = jnp.zeros((1, LANES), F32)
    br = br.at[0, :N_EXPERT_GROUPS].set(b_route_group[0])
    br = br.at[0, EXPERT_LANE0:EXPERT_LANE0 + N_EXPERTS].set(b_route_expert[0])

    cos_p, sin_p = _rope_tables(jnp.arange(seq, dtype=jnp.int32))
    pos_s = past_len + (jnp.arange(n_s, dtype=jnp.int32) % dec_seq)
    cos_s, sin_s = _rope_tables(pos_s)

    dense_consts = (mix_b, ps, wa_b, wp_b, wo_b, gf, wr_hi, jnp.concatenate([wr_hi, wr_lo], axis=1), br)
    assert n_s == ROW_BLOCK == N_TILES * TILE_SAMPLE and n_p == N_TILES * TILE_PROMPT
    ridx = jnp.arange(ROW_BLOCK, dtype=jnp.int32)
    lower = ridx[:, None] > ridx[None, :]
    seg_of = ridx // TILE_SAMPLE
    ltri_p = lower.astype(BF16)
    ltri_s = (lower & (seg_of[:, None] == seg_of[None, :])).astype(BF16)
    ssel_p = (jnp.arange(8, dtype=jnp.int32)[:, None] == 0) & (ridx[None, :] >= 0)
    ssel_s = jnp.arange(8, dtype=jnp.int32)[:, None] == seg_of[None, :]

    xp = x_prompt.reshape(n_p, d)
    q, k, v, kk, vv, u, sga, sgp = _inproj(xp, g_mix, w_in_b, cos_p, sin_p, gq, gk)
    attn, pd, wg_b, wu_b, wd_b = _mixer_prompt(sinks, q, kk, vv, u, w_expert_gate[0], w_expert_up[0],
                                               w_expert_down[0], batch, seq)
    x1_p, hn_p, rt_p, cnt_p, _ = _dense(xp, attn, pd, sga, sgp, *dense_consts, ltri_p,
                                        ssel_p.astype(BF16), jnp.zeros((ROW_BLOCK, LANES), F32))
    last_window = lambda a: a.reshape(batch, seq, KV_W)[:, -WINDOW:].reshape(
        1, batch, WINDOW, N_KV_HEADS, HEAD_DIM)
    new_k_p = last_window(k)
    new_v_p = last_window(v)
    new_u_p = u.reshape(batch, seq, d)[:, -POOL_STATE_LEN:][None]

    xs = x_sample.reshape(n_s, d)
    q, k, v, kk, vv, u, sga, sgp = _inproj(xs, g_mix, w_in_b, cos_s, sin_s, gq, gk)
    ck = cache_k[0].reshape(dec_batch, WINDOW, KV_W)
    cv = cache_v[0].reshape(dec_batch, WINDOW, KV_W)
    attn, pd, nk_s, nv_s, nu_s = _mixer_sample(sinks, q.astype(F32), k, v, u, ck, cv, state_pool)
    cnt_tiles_p = cnt_p[TILE_BLOCKS - 1::TILE_BLOCKS, 0, :]
    crow = jnp.repeat(cnt_tiles_p, TILE_SAMPLE, axis=0)
    x1_s, hn_s, rt_s, _, seg_s = _dense(xs, attn, pd, sga, sgp, *dense_consts, ltri_s,
                                        ssel_s.astype(BF16), crow)
    new_k_s = nk_s.reshape(1, dec_batch, WINDOW, N_KV_HEADS, HEAD_DIM)
    new_v_s = nv_s.reshape(1, dec_batch, WINDOW, N_KV_HEADS, HEAD_DIM)
    new_u_s = nu_s

    ex = slice(EXPERT_LANE0, EXPERT_LANE0 + N_EXPERTS)
    cnt = (cnt_tiles_p[:, ex] + seg_s[0, :N_TILES, ex]).astype(jnp.int32)
    nblk = (cnt + (MOE_ROWS - 1)) // MOE_ROWS
    blk0 = 1 + jnp.cumsum(nblk, axis=1) - nblk
    seg = MOE_ROWS * blk0
    rt = jnp.concatenate([rt_p[:, :6].reshape(N_TILES, TILE_PROMPT, 6),
                          rt_s[:, :6].reshape(N_TILES, TILE_SAMPLE, 6)], axis=1)
    expert_hit = rt[:, :, 0:2].astype(jnp.int32)[..., None] == jnp.arange(N_EXPERTS, dtype=jnp.int32)
    slot = (rt[:, :, 2:4].astype(jnp.int32)
            + jnp.sum(jnp.where(expert_hit, seg[:, None, None, :], 0), axis=-1))
    per_tile = lambda a: jnp.pad(a.reshape(N_TILES, 2 * TILE_TOKENS),
                                 ((0, 0), (0, TILE_LIST - 2 * TILE_TOKENS))).reshape(-1)
    slots = per_tile(slot)
    wts = per_tile(rt[:, :, 4:6])
    y_p, y_s = _experts(nblk, blk0, cnt, slots, wts, hn_p, hn_s, x1_p, x1_s, wg_b, wu_b, wd_b)
    y_prompt = y_p.reshape(batch, seq, d)
    y_sample = y_s.reshape(dec_batch, dec_seq, d)

    return (y_prompt, y_sample, new_k_p, new_v_p, new_u_p, new_k_s, new_v_s, new_u_s)
```

```python
import jax
import jax.numpy as jnp
from jax import lax
from jax.experimental import pallas as pl
from jax.experimental.pallas import tpu as pltpu

F32 = jnp.float32
BF16 = jnp.bfloat16

D_MODEL = 1024
HEAD_DIM = 64
N_Q_HEADS = 16
N_KV_HEADS = 2
WINDOW = 128
ROPE_THETA = 10000.0
POOL_WINDOWS = (2, 4, 8, 16)
POOL_GROUP_WIDTH = D_MODEL // len(POOL_WINDOWS)
POOL_STATE_LEN = max(POOL_WINDOWS) - 1
N_EXPERT_GROUPS = 4
EXPERTS_PER_GROUP = 8
N_EXPERTS = N_EXPERT_GROUPS * EXPERTS_PER_GROUP
D_EXPERT = 512
RMS_EPS = 1e-6
Q_W = N_Q_HEADS * HEAD_DIM
KV_W = N_KV_HEADS * HEAD_DIM
OFF_K = Q_W
OFF_V = OFF_K + KV_W
OFF_U = OFF_V + KV_W
OFF_GA = OFF_U + D_MODEL
OFF_GP = OFF_GA + D_MODEL
IN_W = OFF_GP + D_MODEL

LANES = 128
ROW_BLOCK = 512
EXPERT_LANE0 = 32
VMEM_LIMIT = 56 * 1024 * 1024
EXPERTS_VMEM_LIMIT = 60 * 1024 * 1024
NEG_INF = float("-inf")

N_TILES = 4
TILE_BLOCKS = 8
TILE_PROMPT = TILE_BLOCKS * ROW_BLOCK
TILE_SAMPLE = 128
TILE_TOKENS = TILE_PROMPT + TILE_SAMPLE
MOE_ROWS = 304
DUMMY_ROWS = 8
SCATTER_BATCH = 4
TILE_LIST = -(-(2 * TILE_TOKENS + 1) // 1024) * 1024
LIST_CAP = 2 * TILE_TOKENS + N_EXPERTS * (MOE_ROWS - 1) + 2 * MOE_ROWS + 8


def _rms(x, g):
    return x * lax.rsqrt(jnp.mean(x * x, axis=-1, keepdims=True) + RMS_EPS) * g


def _bdot(a, b):
    return jnp.dot(a, b, preferred_element_type=F32)


def _inproj_body(x_ref, g_ref, w_ref, cos_ref, sin_ref, gq_ref, gk_ref,
                 q_ref, k_ref, v_ref, kk_ref, vv_ref, u_ref, sga_ref, sgp_ref):
    rows = x_ref.shape[0]
    hb = _rms(x_ref[...], g_ref[...]).astype(BF16)
    cos = cos_ref[...]
    sin = sin_ref[...]
    lane = lax.broadcasted_iota(jnp.int32, (rows, LANES), 1)
    lo = lane < HEAD_DIM
    first_half = (lane % HEAD_DIM) < (HEAD_DIM // 2)

    def head_norm_rope(zc, gain):
        sq = zc * zc
        ss_lo = jnp.sum(jnp.where(lo, sq, 0.0), axis=-1, keepdims=True)
        ss_hi = jnp.sum(jnp.where(lo, 0.0, sq), axis=-1, keepdims=True)
        r = lax.rsqrt(jnp.where(lo, ss_lo, ss_hi) * (1.0 / HEAD_DIM) + RMS_EPS)
        y = zc * r * gain
        partner = jnp.where(first_half, pltpu.roll(y, LANES - HEAD_DIM // 2, 1),
                            pltpu.roll(y, HEAD_DIM // 2, 1))
        return y * cos + partner * sin

    gq = gq_ref[...]
    for j in range(Q_W // 256):
        z = _bdot(hb, w_ref[:, 256 * j:256 * (j + 1)])
        for c in range(2):
            qn = head_norm_rope(z[:, LANES * c:LANES * (c + 1)], gq) * (HEAD_DIM ** -0.5)
            q_ref[:, 256 * j + LANES * c:256 * j + LANES * (c + 1)] = qn.astype(BF16)

    z = _bdot(hb, w_ref[:, OFF_K:OFF_U])
    kn = head_norm_rope(z[:, :KV_W], gk_ref[...])
    vr = z[:, KV_W:]
    k_ref[...] = kn
    v_ref[...] = vr
    kr = pltpu.roll(kn, HEAD_DIM, 1)
    vrr = pltpu.roll(vr, HEAD_DIM, 1)
    kk_ref[:, :LANES] = jnp.where(lo, kn, kr).astype(BF16)
    kk_ref[:, LANES:] = jnp.where(lo, kr, kn).astype(BF16)
    vv_ref[:, :LANES] = jnp.where(lo, vr, vrr).astype(BF16)
    vv_ref[:, LANES:] = jnp.where(lo, vrr, vr).astype(BF16)

    for j in range(D_MODEL // 256):
        u_ref[:, 256 * j:256 * (j + 1)] = _bdot(hb, w_ref[:, OFF_U + 256 * j:OFF_U + 256 * (j + 1)])
        sga_ref[:, 256 * j:256 * (j + 1)] = jax.nn.sigmoid(
            _bdot(hb, w_ref[:, OFF_GA + 256 * j:OFF_GA + 256 * (j + 1)]))
        sgp_ref[:, 256 * j:256 * (j + 1)] = jax.nn.sigmoid(
            _bdot(hb, w_ref[:, OFF_GP + 256 * j:OFF_GP + 256 * (j + 1)]))


def _inproj(x, g, w_in_b, cos, sin, gq, gk):
    n = x.shape[0]
    nb = n // ROW_BLOCK
    ncos = cos.shape[0] // ROW_BLOCK
    row = lambda w: pl.BlockSpec((ROW_BLOCK, w), lambda i: (i, 0))
    full = lambda a: pl.BlockSpec(a.shape, lambda i: (0,) * a.ndim)
    tab = pl.BlockSpec((ROW_BLOCK, LANES), lambda i: (i % ncos, 0))
    out_shapes = (
        jax.ShapeDtypeStruct((n, Q_W), BF16),
        jax.ShapeDtypeStruct((n, KV_W), F32),
        jax.ShapeDtypeStruct((n, KV_W), F32),
        jax.ShapeDtypeStruct((n, 2 * LANES), BF16),
        jax.ShapeDtypeStruct((n, 2 * LANES), BF16),
        jax.ShapeDtypeStruct((n, D_MODEL), F32),
        jax.ShapeDtypeStruct((n, D_MODEL), F32),
        jax.ShapeDtypeStruct((n, D_MODEL), F32),
    )
    return pl.pallas_call(
        _inproj_body,
        grid=(nb,),
        in_specs=[row(D_MODEL), full(g), full(w_in_b), tab, tab, full(gq), full(gk)],
        out_specs=tuple(row(s.shape[1]) for s in out_shapes),
        out_shape=out_shapes,
        compiler_params=pltpu.CompilerParams(dimension_semantics=("arbitrary",),
                                             vmem_limit_bytes=VMEM_LIMIT),
        name="inproj",
    )(x, g, w_in_b, cos, sin, gq, gk)


def _attend(q2, kg, vg, cur_valid, sink_col, prev_dead):
    s = lax.dot_general(q2, kg, (((1,), (1,)), ((), ())), preferred_element_type=F32)
    s_prev = s[:, :WINDOW]
    if prev_dead is not None:
        s_prev = jnp.where(prev_dead, NEG_INF, s_prev)
    sc = jnp.where(cur_valid, s[:, WINDOW:], s_prev)
    m = jnp.maximum(jnp.max(sc, axis=-1, keepdims=True), sink_col)
    p = jnp.exp(sc - m)
    den = jnp.sum(p, axis=-1, keepdims=True) + jnp.exp(sink_col - m)
    pn = p * (1.0 / den)
    p2 = jnp.concatenate([jnp.where(cur_valid, 0.0, pn), jnp.where(cur_valid, pn, 0.0)], axis=1)
    return _bdot(p2.astype(BF16), vg)


def _attend_two(q2, kab, vab, cur_valid, sink_col):
    band = 2 * WINDOW
    s = lax.dot_general(q2, kab, (((1,), (1,)), ((), ())), preferred_element_type=F32)
    is_a = lax.broadcasted_iota(jnp.int32, (q2.shape[0], 1), 0) < q2.shape[0] // 2
    own = jnp.where(is_a, s[:, :band], s[:, band:])
    sc = jnp.where(cur_valid, own[:, WINDOW:], own[:, :WINDOW])
    m = jnp.maximum(jnp.max(sc, axis=-1, keepdims=True), sink_col)
    p = jnp.exp(sc - m)
    den = jnp.sum(p, axis=-1, keepdims=True) + jnp.exp(sink_col - m)
    pn = p * (1.0 / den)
    p2 = jnp.concatenate([jnp.where(cur_valid, 0.0, pn), jnp.where(cur_valid, pn, 0.0)], axis=1)
    pab = jnp.concatenate([jnp.where(is_a, p2, 0.0), jnp.where(is_a, 0.0, p2)], axis=1)
    return _bdot(pab.astype(BF16), vab)


POOL_PAD_ROWS = 8


def _mixer_prompt_body(sink_ref, q_ref, kkc_ref, kkp_ref, vvc_ref, vvp_ref, uc_ref, up_ref,
                       wg_ref, wu_ref, wd_ref,
                       attn_ref, pd_ref, wgb_ref, wub_ref, wdb_ref, ext_ref, s_a, s_b):
    wgb_ref[...] = wg_ref[...].astype(BF16)
    wub_ref[...] = wu_ref[...].astype(BF16)
    wdb_ref[...] = wd_ref[...].astype(BF16)
    i = pl.program_id(1)
    first = i == 0
    m2 = 2 * WINDOW
    row = lax.broadcasted_iota(jnp.int32, (m2, LANES), 0)
    col = lax.broadcasted_iota(jnp.int32, (m2, LANES), 1)
    cur_valid = (row % WINDOW) >= col
    upper = lax.broadcasted_iota(jnp.int32, (m2, 1), 0) < WINDOW
    lo = lax.broadcasted_iota(jnp.int32, (WINDOW, LANES), 1) < HEAD_DIM

    for j in range(ROW_BLOCK // WINDOW):
        r0 = WINDOW * j
        if j == 0:
            kprev, vprev = kkp_ref[...], vvp_ref[...]
            prev_dead = first
        else:
            kprev, vprev = kkc_ref[r0 - WINDOW:r0, :], vvc_ref[r0 - WINDOW:r0, :]
            prev_dead = None
        kband = jnp.concatenate([kprev, kkc_ref[r0:r0 + WINDOW, :]], axis=0)
        vband = jnp.concatenate([vprev, vvc_ref[r0:r0 + WINDOW, :]], axis=0)
        for g in range(N_KV_HEADS):
            kg = kband[:, LANES * g:LANES * (g + 1)]
            vg = vband[:, LANES * g:LANES * (g + 1)]
            for c in range(4):
                cc = 4 * g + c
                qc = q_ref[r0:r0 + WINDOW, LANES * cc:LANES * (cc + 1)]
                zero = jnp.zeros_like(qc)
                q2 = jnp.concatenate([jnp.where(lo, qc, zero), jnp.where(lo, zero, qc)], axis=0)
                sink_col = jnp.where(upper, sink_ref[2 * cc], sink_ref[2 * cc + 1])
                o = _attend(q2, kg, vg, cur_valid, sink_col, prev_dead)
                attn_ref[r0:r0 + WINDOW, LANES * cc:LANES * (cc + 1)] = jnp.where(
                    lo, o[:WINDOW], o[WINDOW:]).astype(BF16)

    pad, top = POOL_PAD_ROWS, POOL_PAD_ROWS + 16
    end = top + ROW_BLOCK
    for ref in (ext_ref, s_a, s_b):
        ref[0:pad, :] = jnp.zeros((pad, D_MODEL), F32)
    ext_ref[pad:top, :] = jnp.where(first, 0.0, up_ref[...])
    ext_ref[top:, :] = uc_ref[...]
    pos = i * ROW_BLOCK + lax.broadcasted_iota(jnp.int32, (ROW_BLOCK, 1), 0)
    src = ext_ref
    for g, w in enumerate(POOL_WINDOWS):
        c0 = POOL_GROUP_WIDTH * g
        dst = (s_a, s_b)[g % 2]
        shift = w // 2
        dst[pad:end, c0:] = src[pad:end, c0:] + src[pad - shift:end - shift, c0:]
        c1 = c0 + POOL_GROUP_WIDTH
        cnt = jnp.minimum(w, pos + 1).astype(F32)
        pd_ref[:, c0:c1] = (dst[top:end, c0:c1] / cnt - uc_ref[:, c0:c1]).astype(BF16)
        src = dst


def _mixer_prompt(sinks, q, kk, vv, u, w_gate, w_up, w_down, batch, seq):
    nb = seq // ROW_BLOCK
    assert batch * nb == N_EXPERTS, "one expert's weights are cast per grid step"
    sub = ROW_BLOCK // WINDOW
    cur = lambda w: pl.BlockSpec((ROW_BLOCK, w), lambda b, i: (b * nb + i, 0))
    wspec = lambda a: pl.BlockSpec((1,) + a.shape[1:], lambda b, i: (b * nb + i, 0, 0))
    weights = (w_gate, w_up, w_down)
    prev_kv = pl.BlockSpec((WINDOW, 2 * LANES),
                           lambda b, i: (jnp.maximum((b * nb + i) * sub - 1, 0), 0))
    prev_u = pl.BlockSpec((16, D_MODEL),
                          lambda b, i: (jnp.maximum((b * nb + i) * (ROW_BLOCK // 16) - 1, 0), 0))
    n = batch * seq
    return pl.pallas_call(
        _mixer_prompt_body,
        grid=(batch, nb),
        in_specs=[pl.BlockSpec(memory_space=pltpu.SMEM), cur(Q_W), cur(2 * LANES), prev_kv,
                  cur(2 * LANES), prev_kv, cur(D_MODEL), prev_u] + [wspec(a) for a in weights],
        out_specs=(cur(Q_W), cur(D_MODEL)) + tuple(wspec(a) for a in weights),
        out_shape=(jax.ShapeDtypeStruct((n, Q_W), BF16), jax.ShapeDtypeStruct((n, D_MODEL), BF16))
        + tuple(jax.ShapeDtypeStruct(a.shape, BF16) for a in weights),
        scratch_shapes=[pltpu.VMEM((POOL_PAD_ROWS + 16 + ROW_BLOCK, D_MODEL), F32)] * 3,
        compiler_params=pltpu.CompilerParams(dimension_semantics=("arbitrary", "arbitrary"),
                                             vmem_limit_bytes=VMEM_LIMIT),
        name="mixer_prompt",
    )(sinks, q, kk, kk, vv, vv, u, u, *weights)


SEQ_PAIR_ROWS = 8
SAMPLE_PAIRS_PER_STEP = 8


def _mixer_sample_body(sink_ref, q_ref, k_ref, v_ref, u_ref, ck_ref, cv_ref, st_ref,
                       attn_ref, pd_ref, nk_ref, nv_ref, nu_ref, ext_ref):
    m = SEQ_PAIR_ROWS
    half = m // 2
    hist = POOL_STATE_LEN
    ext_ref[16:, :] = jnp.zeros((ext_ref.shape[0] - 16, D_MODEL), F32)
    row8 = lax.broadcasted_iota(jnp.int32, (m, LANES), 0)
    n_chunks = Q_W // LANES
    mq = n_chunks * m
    row1 = lax.broadcasted_iota(jnp.int32, (m, 1), 0)
    row = lax.broadcasted_iota(jnp.int32, (2 * mq, LANES), 0)
    col = lax.broadcasted_iota(jnp.int32, (2 * mq, LANES), 1)
    cur_valid = (row % half) >= col
    lane8 = lax.broadcasted_iota(jnp.int32, (m, LANES), 1)
    lo8 = lane8 < HEAD_DIM
    top8 = lax.broadcasted_iota(jnp.int32, (m, LANES), 0) < half
    tail = jnp.zeros((WINDOW - m, LANES), F32)
    top = row1 < half
    sink_col = jnp.concatenate([jnp.where(top, sink_ref[2 * cc], sink_ref[2 * cc + 1])
                                for cc in range(n_chunks)] * 2, axis=0)

    def pair(p, carry):
        r0 = pl.multiple_of(p * m, m)
        q8 = q_ref[pl.ds(r0, m), :]
        k8 = k_ref[pl.ds(r0, m), :]
        v8 = v_ref[pl.ds(r0, m), :]
        u8 = u_ref[pl.ds(r0, m), :]
        q2s, kbs, vbs = [], [], []
        pd_parts = []
        for s in range(2):
            shift = lambda a: a if s == 0 else pltpu.roll(a, half, 0)
            seq = 2 * p + s
            ck, cv = ck_ref[seq], cv_ref[seq]
            ks, vs = shift(k8), shift(v8)
            kb = jnp.concatenate([ck, ks, tail], axis=0)
            vb = jnp.concatenate([cv, vs, tail], axis=0)
            for cache, new, out in ((ck, ks, nk_ref), (cv, vs, nv_ref)):
                up = pltpu.roll(cache, WINDOW - half, 0)
                out[seq, 0:WINDOW - m, :] = up[0:WINDOW - m]
                out[seq, WINDOW - m:WINDOW, :] = jnp.where(row8 < half, up[WINDOW - m:],
                                                           pltpu.roll(new, half, 0))
            qs = shift(q8)
            qr = pltpu.roll(qs, half, 0)
            q2 = []
            for cc in range(n_chunks):
                qa = qs[:, LANES * cc:LANES * (cc + 1)]
                qb = qr[:, LANES * cc:LANES * (cc + 1)]
                if cc < n_chunks // N_KV_HEADS:
                    first = jnp.where(lo8, qa, 0.0)
                    second = jnp.where(lo8, pltpu.roll(qb, HEAD_DIM, 1), 0.0)
                else:
                    first = jnp.where(lo8, 0.0, pltpu.roll(qa, HEAD_DIM, 1))
                    second = jnp.where(lo8, 0.0, qb)
                q2.append(jnp.where(top8, first, second))
            q2s += q2
            kbs.append(kb)
            vbs.append(vb)

            ext_ref[0:hist, :] = st_ref[0, seq]
            ext_ref[hist:hist + half, :] = shift(u8)[0:half]
            nu_ref[0, seq] = ext_ref[half:hist + half, :]
            cols = []
            for g, w in enumerate(POOL_WINDOWS):
                c0, c1 = POOL_GROUP_WIDTH * g, POOL_GROUP_WIDTH * (g + 1)
                acc = ext_ref[hist:hist + m, c0:c1]
                for k in range(1, w):
                    acc = acc + ext_ref[hist - k:hist - k + m, c0:c1]
                cols.append(acc / float(w) - ext_ref[hist:hist + m, c0:c1])
            pd_parts.append(jnp.concatenate(cols, axis=1))

        o2 = _attend_two(jnp.concatenate(q2s, axis=0).astype(BF16),
                         jnp.concatenate(kbs, axis=0).astype(BF16),
                         jnp.concatenate(vbs, axis=0).astype(BF16), cur_valid, sink_col)
        attn_parts = []
        for s in range(2):
            chunks = []
            for cc in range(n_chunks):
                oc = o2[mq * s + m * cc:mq * s + m * (cc + 1)]
                if cc < n_chunks // N_KV_HEADS:
                    chunks.append(jnp.where(lo8, oc, pltpu.roll(pltpu.roll(oc, half, 0), HEAD_DIM, 1)))
                else:
                    chunks.append(jnp.where(lo8, pltpu.roll(oc, HEAD_DIM, 1), pltpu.roll(oc, half, 0)))
            attn_parts.append(jnp.concatenate(chunks, axis=1))
        attn_ref[pl.ds(r0, m), :] = jnp.where(top, attn_parts[0], pltpu.roll(attn_parts[1], half, 0))
        pd_ref[pl.ds(r0, m), :] = jnp.where(top, pd_parts[0], pltpu.roll(pd_parts[1], half, 0))
        return carry

    lax.fori_loop(0, SAMPLE_PAIRS_PER_STEP, pair, 0)


def _mixer_sample(sinks, q, k, v, u, cache_k, cache_v, state):
    n = q.shape[0]
    m = SEQ_PAIR_ROWS * SAMPLE_PAIRS_PER_STEP
    rows = lambda w: pl.BlockSpec((m, w), lambda i: (i, 0))
    seqs = lambda a: pl.BlockSpec((2 * SAMPLE_PAIRS_PER_STEP,) + a.shape[1:], lambda i: (i, 0, 0))
    st_spec = pl.BlockSpec((1, 2 * SAMPLE_PAIRS_PER_STEP) + state.shape[2:], lambda i: (0, i, 0, 0))
    like = lambda a: jax.ShapeDtypeStruct(a.shape, F32)
    return pl.pallas_call(
        _mixer_sample_body,
        grid=(n // m,),
        in_specs=[pl.BlockSpec(memory_space=pltpu.SMEM), rows(Q_W), rows(KV_W), rows(KV_W),
                  rows(D_MODEL), seqs(cache_k), seqs(cache_v), st_spec],
        out_specs=(rows(Q_W), rows(D_MODEL), seqs(cache_k), seqs(cache_v), st_spec),
        out_shape=(jax.ShapeDtypeStruct((n, Q_W), F32), jax.ShapeDtypeStruct((n, D_MODEL), F32),
                   like(cache_k), like(cache_v), like(state)),
        scratch_shapes=[pltpu.VMEM((16 + SEQ_PAIR_ROWS, D_MODEL), F32)],
        compiler_params=pltpu.CompilerParams(dimension_semantics=("arbitrary",),
                                             vmem_limit_bytes=VMEM_LIMIT),
        name="mixer_sample",
    )(sinks, q, k, v, u, cache_k, cache_v, state)


def _dense_body(x_ref, attn_ref, pd_ref, sga_ref, sgp_ref, mix_ref, ps_ref, wa_ref, wp_ref,
                wo_ref, gf_ref, wrh_ref, wrl_ref, br_ref, ltri_ref, ssel_ref, crow_ref,
                x1_ref, hn_ref, rt_ref, cnt_ref, seg_ref, carry_ref, *lag_ref):
    rows = x_ref.shape[0]
    step = pl.program_id(0) - len(lag_ref)

    @pl.when((step % TILE_BLOCKS == 0) | (step < 0))
    def _():
        carry_ref[...] = jnp.zeros_like(carry_ref)

    if lag_ref:
        @pl.when(step < 0)
        def _():
            lag_ref[0][...] = jnp.zeros_like(lag_ref[0])
        prev_logits = lag_ref[0][...]

    pd = pd_ref[...].astype(BF16)
    pooled = []
    for g in range(len(POOL_WINDOWS)):
        c0, c1 = POOL_GROUP_WIDTH * g, POOL_GROUP_WIDTH * (g + 1)
        pooled.append((_bdot(pd[:, c0:c1], mix_ref[g]) * ps_ref[:, c0:c1]).astype(BF16))
    pooled = jnp.concatenate(pooled, axis=1)
    merged = (sga_ref[...] * _bdot(attn_ref[...].astype(BF16), wa_ref[...])
              + sgp_ref[...] * _bdot(pooled, wp_ref[...]))
    x1 = x_ref[...] + _bdot(merged.astype(BF16), wo_ref[...])
    x1_ref[...] = x1
    hn = _rms(x1, gf_ref[...])
    hn_ref[...] = hn
    hi = hn.astype(BF16)
    lo = (hn - hi.astype(F32)).astype(BF16)
    both = _bdot(hi, wrl_ref[...])
    logits = both[:, :LANES] + both[:, LANES:] + _bdot(lo, wrh_ref[...]) + br_ref[...]
    if lag_ref:
        lag_ref[0][...] = logits
        logits = prev_logits

    lane = lax.broadcasted_iota(jnp.int32, (rows, LANES), 1)
    big = jnp.int32(LANES)
    gl = jnp.where(lane < N_EXPERT_GROUPS, logits, NEG_INF)
    gmax = jnp.max(gl, axis=-1, keepdims=True)
    gidx = jnp.min(jnp.where(gl == gmax, lane, big), axis=-1, keepdims=True)
    g_w = 1.0 / jnp.sum(jnp.exp(gl - gmax), axis=-1, keepdims=True)
    e0 = EXPERT_LANE0 + gidx * EXPERTS_PER_GROUP
    el = jnp.where((lane >= e0) & (lane < e0 + EXPERTS_PER_GROUP), logits, NEG_INF)
    l1 = jnp.max(el, axis=-1, keepdims=True)
    i1 = jnp.min(jnp.where(el == l1, lane, big), axis=-1, keepdims=True)
    el2 = jnp.where(lane == i1, NEG_INF, el)
    l2 = jnp.max(el2, axis=-1, keepdims=True)
    i2 = jnp.min(jnp.where(el2 == l2, lane, big), axis=-1, keepdims=True)
    e = jnp.exp(l2 - l1)
    w1 = 1.0 / (1.0 + e)
    w2 = e * w1

    sel = (lane == i1) | (lane == i2)
    onehot = jnp.where(sel, 1.0, 0.0).astype(BF16)
    rank = _bdot(ltri_ref[...], onehot) + carry_ref[0:1, :] + crow_ref[...]
    segsum = _bdot(ssel_ref[...], onehot)
    seg_ref[0] = segsum
    carry_ref[...] = carry_ref[...] + segsum[0:1, :]
    cnt_ref[0] = carry_ref[...]
    r1 = jnp.sum(jnp.where(lane == i1, rank, 0.0), axis=-1, keepdims=True)
    r2 = jnp.sum(jnp.where(lane == i2, rank, 0.0), axis=-1, keepdims=True)
    cols = ((i1 - EXPERT_LANE0).astype(F32), (i2 - EXPERT_LANE0).astype(F32), r1, r2,
            g_w * w1, g_w * w2)
    tile = jnp.zeros((rows, LANES), F32)
    for c, val in enumerate(cols):
        tile = jnp.where(lane == c, val, tile)
    rt_ref[...] = tile


def _dense(x, attn, pd, sga, sgp, mix_b, pool_scale, wa_b, wp_b, wo_b, gf, wr_hi, wr_lo, br,
           ltri, ssel, crow):
    n = x.shape[0]
    nb = n // ROW_BLOCK
    lag = 1 if nb > 1 else 0
    cur = lambda i: jnp.minimum(i, nb - 1)
    late = lambda i: jnp.maximum(i - lag, 0)
    row = lambda w: pl.BlockSpec((ROW_BLOCK, w), lambda i: (cur(i), 0))
    full = lambda a: pl.BlockSpec(a.shape, lambda i: (0,) * a.ndim)
    route = pl.BlockSpec((ROW_BLOCK, LANES), lambda i: (late(i), 0))
    stat = pl.BlockSpec((1, 8, LANES), lambda i: (late(i), 0, 0))
    consts = (mix_b, pool_scale, wa_b, wp_b, wo_b, gf, wr_hi, wr_lo, br, ltri, ssel, crow)
    return pl.pallas_call(
        _dense_body,
        grid=(nb + lag,),
        in_specs=[row(D_MODEL)] * 5 + [full(a) for a in consts],
        out_specs=(row(D_MODEL), row(D_MODEL), route, stat, stat),
        out_shape=(jax.ShapeDtypeStruct((n, D_MODEL), F32),
                   jax.ShapeDtypeStruct((n, D_MODEL), F32),
                   jax.ShapeDtypeStruct((n, LANES), F32),
                   jax.ShapeDtypeStruct((nb, 8, LANES), F32),
                   jax.ShapeDtypeStruct((nb, 8, LANES), F32)),
        scratch_shapes=[pltpu.VMEM((8, LANES), F32)] + [pltpu.VMEM((ROW_BLOCK, LANES), F32)] * lag,
        compiler_params=pltpu.CompilerParams(dimension_semantics=("arbitrary",),
                                             vmem_limit_bytes=VMEM_LIMIT),
        name="dense",
    )(x, attn, pd, sga, sgp, *consts)


def _experts_body(nblk_ref, blk0_ref, cnt_ref, slots_ref, wts_ref, hn_p, hn_s, x1_p, x1_s,
                  wg_ref, wu_ref, wd_ref, y_p, y_s,
                  hn_t, acc, xg0, xg1, yb0, yb1, tok, sem_in, sem_out):
    tau = pl.program_id(0)
    e = pl.program_id(1)
    t_rows = TILE_TOKENS
    dummy = 2 * t_rows
    p_rows = pl.ds(pl.multiple_of(tau * TILE_PROMPT, TILE_PROMPT), TILE_PROMPT)
    s_rows = pl.ds(pl.multiple_of(tau * TILE_SAMPLE, TILE_SAMPLE), TILE_SAMPLE)
    tile_p = pl.ds(0, TILE_PROMPT)
    tile_s = pl.ds(TILE_PROMPT, TILE_SAMPLE)

    def in_copies():
        return [pltpu.make_async_copy(hn_p.at[p_rows], hn_t.at[tile_p], sem_in.at[0]),
                pltpu.make_async_copy(hn_s.at[s_rows], hn_t.at[tile_s], sem_in.at[1]),
                pltpu.make_async_copy(x1_p.at[p_rows], acc.at[tile_p], sem_in.at[2]),
                pltpu.make_async_copy(x1_s.at[s_rows], acc.at[tile_s], sem_in.at[3])]

    def out_copies():
        return [pltpu.make_async_copy(acc.at[tile_p], y_p.at[p_rows], sem_out.at[0]),
                pltpu.make_async_copy(acc.at[tile_s], y_s.at[s_rows], sem_out.at[1])]

    xbufs = (xg0, xg1)
    ybufs = (yb0, yb1)
    end_block = blk0_ref[tau, N_EXPERTS - 1] + nblk_ref[tau, N_EXPERTS - 1]
    end_slot = end_block * MOE_ROWS

    def gather_block(b, xdst):
        base = b * MOE_ROWS
        for j in range(MOE_ROWS):
            t = lax.shift_right_logical(tok[base + j], 1)
            xdst[j // 8, pl.ds(j % 8, 1), :] = hn_t[pl.ds(t, 1), :]

    def scatter_block(b, ysrc):
        base = b * MOE_ROWS
        for j0 in range(0, MOE_ROWS, SCATTER_BATCH):
            ents = [tok[base + j0 + i] for i in range(SCATTER_BATCH)]
            rows = [lax.shift_right_logical(en, 1) for en in ents]
            vals = [acc[pl.ds(rows[i], 1), :]
                    + wts_ref[ents[i]] * ysrc[(j0 + i) // 8, pl.ds((j0 + i) % 8, 1), :]
                    for i in range(SCATTER_BATCH)]
            for i in range(SCATTER_BATCH):
                acc[pl.ds(rows[i], 1), :] = vals[i]

    @pl.when(e == 0)
    def _load_tile():
        for cp in in_copies():
            cp.start()

        hn_t[t_rows:, :] = jnp.zeros((DUMMY_ROWS, D_MODEL), F32)
        yb0[...] = jnp.zeros_like(yb0)

        def pad_range(lo, hi):
            def fill(g, c):
                for i in range(8):
                    tok[lo + g * 8 + i] = dummy
                return c
            lax.fori_loop(0, lax.shift_right_logical(hi - lo + 7, 3), fill, 0)
        pad_range(0, MOE_ROWS)
        pad_range(end_slot, end_slot + MOE_ROWS)

        def pad_fill(ei, c):
            s0 = blk0_ref[tau, ei] * MOE_ROWS
            pad_range(s0 + cnt_ref[tau, ei], s0 + nblk_ref[tau, ei] * MOE_ROWS)
            return c
        lax.fori_loop(0, N_EXPERTS, pad_fill, 0)

        def invert(g, c):
            a0 = g * 16
            for i in range(16):
                tok[slots_ref[a0 + i]] = a0 + i
            return c
        lax.fori_loop(0, t_rows // 8, invert, 0)

        for cp in in_copies():
            cp.wait()
        acc[t_rows:, :] = jnp.zeros((DUMMY_ROWS, D_MODEL), F32)
        gather_block(1, xbufs[1])

    first_block = blk0_ref[tau, e]

    def block(jb, c):
        b = first_block + jb
        for par in range(2):
            @pl.when((b & 1) == par)
            def _():
                gather_block(b + 1, xbufs[1 - par])
                xb = xbufs[par][...].reshape(MOE_ROWS, D_MODEL).astype(BF16)
                act = (jax.nn.silu(_bdot(xb, wg_ref[0])) * _bdot(xb, wu_ref[0])).astype(BF16)
                ybufs[par][...] = _bdot(act, wd_ref[0]).reshape(MOE_ROWS // 8, 8, D_MODEL)
                scatter_block(b - 1, ybufs[1 - par])
        return c
    lax.fori_loop(0, nblk_ref[tau, e], block, 0)

    @pl.when(e == N_EXPERTS - 1)
    def _store_tile():
        last = end_block - 1
        for par in range(2):
            @pl.when((last & 1) == par)
            def _():
                scatter_block(last, ybufs[par])
        cps = out_copies()
        for cp in cps:
            cp.start()
        for cp in cps:
            cp.wait()


def _experts(nblk, blk0, cnt, slots, wts, hn_p, hn_s, x1_p, x1_s, w_gate, w_up, w_down):
    any_spec = pl.BlockSpec(memory_space=pl.ANY)
    wspec = lambda a: pl.BlockSpec((1,) + a.shape[1:], lambda t, e, *_: (e, 0, 0))
    smem = lambda a: pl.BlockSpec((a.shape[0] // N_TILES,), lambda t, e, *_: (t,),
                                  memory_space=pltpu.SMEM)
    grid_spec = pltpu.PrefetchScalarGridSpec(
        num_scalar_prefetch=3,
        grid=(N_TILES, N_EXPERTS),
        in_specs=[smem(slots), smem(wts), any_spec, any_spec, any_spec, any_spec,
                  wspec(w_gate), wspec(w_up), wspec(w_down)],
        out_specs=(any_spec, any_spec),
        scratch_shapes=[
            pltpu.VMEM((TILE_TOKENS + DUMMY_ROWS, D_MODEL), F32),
            pltpu.VMEM((TILE_TOKENS + DUMMY_ROWS, D_MODEL), F32),
            pltpu.VMEM((MOE_ROWS // 8, 8, D_MODEL), F32),
            pltpu.VMEM((MOE_ROWS // 8, 8, D_MODEL), F32),
            pltpu.VMEM((MOE_ROWS // 8, 8, D_MODEL), F32),
            pltpu.VMEM((MOE_ROWS // 8, 8, D_MODEL), F32),
            pltpu.SMEM((LIST_CAP,), jnp.int32),
            pltpu.SemaphoreType.DMA((4,)),
            pltpu.SemaphoreType.DMA((2,)),
        ],
    )
    return pl.pallas_call(
        _experts_body,
        grid_spec=grid_spec,
        out_shape=(jax.ShapeDtypeStruct(x1_p.shape, F32), jax.ShapeDtypeStruct(x1_s.shape, F32)),
        compiler_params=pltpu.CompilerParams(dimension_semantics=("arbitrary", "arbitrary"),
                                             vmem_limit_bytes=EXPERTS_VMEM_LIMIT),
        name="experts",
    )(nblk, blk0, cnt, slots, wts, hn_p, hn_s, x1_p, x1_s, w_gate, w_up, w_down)


def _rope_tables(pos):
    half = HEAD_DIM // 2
    inv = ROPE_THETA ** (-jnp.arange(half, dtype=F32) * (2.0 / HEAD_DIM))
    ang = pos.astype(F32)[:, None] * inv[None, :]
    cos = jnp.tile(jnp.cos(ang), (1, LANES // half))
    sin = jnp.sin(ang)
    sin = jnp.tile(jnp.concatenate([-sin, sin], axis=1), (1, LANES // HEAD_DIM))
    return cos, sin


def _split_bf16(w):
    hi = w.astype(BF16)
    return hi, (w - hi.astype(F32)).astype(BF16)


def kernel(x_prompt, x_sample, cache_k, cache_v, state_pool, norm_mix_g, w_in, q_norm_g, k_norm_g,
           attn_sinks, w_attn_branch, pool_mix_w, pool_scale, w_pool_branch, w_out, norm_ffn_g,
           w_route_group, b_route_group, w_route_expert, b_route_expert, w_expert_gate,
           w_expert_up, w_expert_down):
    batch, seq, d = x_prompt.shape
    dec_batch, dec_seq, _ = x_sample.shape
    past_len = 16384
    assert x_prompt.shape == (4, 4096, D_MODEL) and x_sample.shape == (128, 4, D_MODEL)
    assert w_in.shape[0] == 1, "single layer"
    n_p, n_s = batch * seq, dec_batch * dec_seq

    g_mix = norm_mix_g[0][None, :]
    w_in_b = w_in[0].astype(BF16)
    gq = jnp.tile(q_norm_g[0], LANES // HEAD_DIM)[None, :]
    gk = jnp.tile(k_norm_g[0], LANES // HEAD_DIM)[None, :]
    sinks = attn_sinks[0]
    mix_b = pool_mix_w[0].astype(BF16)
    ps = pool_scale[0][None, :]
    wa_b = w_attn_branch[0].astype(BF16)
    wp_b = w_pool_branch[0].astype(BF16)
    wo_b = w_out[0].astype(BF16)
    gf = norm_ffn_g[0][None, :]
    wr = jnp.zeros((D_MODEL, LANES), F32)
    wr = wr.at[:, :N_EXPERT_GROUPS].set(w_route_group[0])
    wr = wr.at[:, EXPERT_LANE0:EXPERT_LANE0 + N_EXPERTS].set(w_route_expert[0])
    wr_hi, wr_lo = _split_bf16(wr)
    br = jnp.zeros((1, LANES), F32)
    br = br.at[0, :N_EXPERT_GROUPS].set(b_route_group[0])
    br = br.at[0, EXPERT_LANE0:EXPERT_LANE0 + N_EXPERTS].set(b_route_expert[0])

    cos_p, sin_p = _rope_tables(jnp.arange(seq, dtype=jnp.int32))
    pos_s = past_len + (jnp.arange(n_s, dtype=jnp.int32) % dec_seq)
    cos_s, sin_s = _rope_tables(pos_s)

    dense_consts = (mix_b, ps, wa_b, wp_b, wo_b, gf, wr_hi, jnp.concatenate([wr_hi, wr_lo], axis=1), br)
    assert n_s == ROW_BLOCK == N_TILES * TILE_SAMPLE and n_p == N_TILES * TILE_PROMPT
    ridx = jnp.arange(ROW_BLOCK, dtype=jnp.int32)
    lower = ridx[:, None] > ridx[None, :]
    seg_of = ridx // TILE_SAMPLE
    ltri_p = lower.astype(BF16)
    ltri_s = (lower & (seg_of[:, None] == seg_of[None, :])).astype(BF16)
    ssel_p = (jnp.arange(8, dtype=jnp.int32)[:, None] == 0) & (ridx[None, :] >= 0)
    ssel_s = jnp.arange(8, dtype=jnp.int32)[:, None] == seg_of[None, :]

    xp = x_prompt.reshape(n_p, d)
    q, k, v, kk, vv, u, sga, sgp = _inproj(xp, g_mix, w_in_b, cos_p, sin_p, gq, gk)
    attn, pd, wg_b, wu_b, wd_b = _mixer_prompt(sinks, q, kk, vv, u, w_expert_gate[0], w_expert_up[0],
                                               w_expert_down[0], batch, seq)
    x1_p, hn_p, rt_p, cnt_p, _ = _dense(xp, attn, pd, sga, sgp, *dense_consts, ltri_p,
                                        ssel_p.astype(BF16), jnp.zeros((ROW_BLOCK, LANES), F32))
    last_window = lambda a: a.reshape(batch, seq, KV_W)[:, -WINDOW:].reshape(
        1, batch, WINDOW, N_KV_HEADS, HEAD_DIM)
    new_k_p = last_window(k)
    new_v_p = last_window(v)
    new_u_p = u.reshape(batch, seq, d)[:, -POOL_STATE_LEN:][None]

    xs = x_sample.reshape(n_s, d)
    q, k, v, kk, vv, u, sga, sgp = _inproj(xs, g_mix, w_in_b, cos_s, sin_s, gq, gk)
    ck = cache_k[0].reshape(dec_batch, WINDOW, KV_W)
    cv = cache_v[0].reshape(dec_batch, WINDOW, KV_W)
    attn, pd, nk_s, nv_s, nu_s = _mixer_sample(sinks, q.astype(F32), k, v, u, ck, cv, state_pool)
    cnt_tiles_p = cnt_p[TILE_BLOCKS - 1::TILE_BLOCKS, 0, :]
    crow = jnp.repeat(cnt_tiles_p, TILE_SAMPLE, axis=0)
    x1_s, hn_s, rt_s, _, seg_s = _dense(xs, attn, pd, sga, sgp, *dense_consts, ltri_s,
                                        ssel_s.astype(BF16), crow)
    new_k_s = nk_s.reshape(1, dec_batch, WINDOW, N_KV_HEADS, HEAD_DIM)
    new_v_s = nv_s.reshape(1, dec_batch, WINDOW, N_KV_HEADS, HEAD_DIM)
    new_u_s = nu_s

    ex = slice(EXPERT_LANE0, EXPERT_LANE0 + N_EXPERTS)
    cnt = (cnt_tiles_p[:, ex] + seg_s[0, :N_TILES, ex]).astype(jnp.int32)
    nblk = (cnt + (MOE_ROWS - 1)) // MOE_ROWS
    blk0 = 1 + jnp.cumsum(nblk, axis=1) - nblk
    seg = MOE_ROWS * blk0
    rt = jnp.concatenate([rt_p[:, :6].reshape(N_TILES, TILE_PROMPT, 6),
                          rt_s[:, :6].reshape(N_TILES, TILE_SAMPLE, 6)], axis=1)
    expert_hit = rt[:, :, 0:2].astype(jnp.int32)[..., None] == jnp.arange(N_EXPERTS, dtype=jnp.int32)
    slot = (rt[:, :, 2:4].astype(jnp.int32)
            + jnp.sum(jnp.where(expert_hit, seg[:, None, None, :], 0), axis=-1))
    per_tile = lambda a: jnp.pad(a.reshape(N_TILES, 2 * TILE_TOKENS),
                                 ((0, 0), (0, TILE_LIST - 2 * TILE_TOKENS))).reshape(-1)
    slots = per_tile(slot)
    wts = per_tile(rt[:, :, 4:6])
    y_p, y_s = _experts(nblk, blk0, cnt, slots, wts, hn_p, hn_s, x1_p, x1_s, wg_b, wu_b, wd_b)
    y_prompt = y_p.reshape(batch, seq, d)
    y_sample = y_s.reshape(dec_batch, dec_seq, d)

    return (y_prompt, y_sample, new_k_p, new_v_p, new_u_p, new_k_s, new_v_s, new_u_s)
```
